```python
import jax, jax.numpy as jnp
from jax import lax
import numpy as np

D_MODEL = 1024
BATCH = 1
SEQ = 16384
DEPTH = 1
DEC_BATCH = 16
DEC_SEQ = 64
PAST_LEN = 1024

CHUNK = 64
ATTN_WIDTH = D_MODEL // 2
N_HEADS = 8
HEAD_DIM = ATTN_WIDTH // N_HEADS
POOL_WIDTH = D_MODEL - ATTN_WIDTH
MIX_WIDTH = ATTN_WIDTH + POOL_WIDTH
POOL_WINDOWS = (2, 4, 8, 16)
N_POOL_GROUPS = 4
POOL_GROUP_DIM = POOL_WIDTH // N_POOL_GROUPS
POOL_PAD = 15
IN_COLS = 3 * ATTN_WIDTH + N_HEADS + POOL_WIDTH
Q_BLOCK = 128
N_GROUPS = 4
EXPERTS_PER_GROUP = 4
N_EXPERTS = N_GROUPS * EXPERTS_PER_GROUP
TOP_K_INNER = 2
D_EXPERT = 512
FORGET_BIAS_INIT = 2.0
EPS = 1e-6

kernel_name = 'fox_pool_hier_moe_stream_step'


def rms_norm(x, g):
    xf = x.astype(jnp.float32)
    r = lax.rsqrt(jnp.mean(xf * xf, axis=-1, keepdims=True) + EPS)
    return (xf * r).astype(x.dtype) * g


def project_in(x, norm_g, w_in, b_f, q_g, k_g):
    b_, t_ = x.shape[:2]
    n = rms_norm(x, norm_g)
    z = jnp.einsum('btd,dc->btc', n, w_in)
    q, k, v, f, p = jnp.split(z, [ATTN_WIDTH, 2 * ATTN_WIDTH, 3 * ATTN_WIDTH, 3 * ATTN_WIDTH + N_HEADS], axis=-1)
    q = rms_norm(q.reshape(b_, t_, N_HEADS, HEAD_DIM), q_g)
    k = rms_norm(k.reshape(b_, t_, N_HEADS, HEAD_DIM), k_g)
    v = v.reshape(b_, t_, N_HEADS, HEAD_DIM)
    logf = jax.nn.log_sigmoid((f + b_f).astype(jnp.float32))
    return q, k, v, logf, p


def fox_prompt(q, k, v, logf):
    b_, s_ = q.shape[:2]
    nb = s_ // Q_BLOCK
    c = jnp.cumsum(logf, axis=1)
    cT = c.transpose(0, 2, 1)
    qb = q.reshape(b_, nb, Q_BLOCK, N_HEADS, HEAD_DIM).transpose(1, 0, 2, 3, 4)
    cb = cT.reshape(b_, N_HEADS, nb, Q_BLOCK).transpose(2, 0, 1, 3)
    kpos = jnp.arange(s_)
    scale = HEAD_DIM ** -0.5

    def block(args):
        i, qi, ci = args
        qpos = i * Q_BLOCK + jnp.arange(Q_BLOCK)
        s = jnp.einsum('bqhd,bkhd->bhqk', qi, k).astype(jnp.float32) * scale
        s = s + ci[..., :, None] - cT[..., None, :]
        s = jnp.where(kpos[None, :] <= qpos[:, None], s, -jnp.inf)
        p = jax.nn.softmax(s, axis=-1).astype(v.dtype)
        return jnp.einsum('bhqk,bkhd->bqhd', p, v)

    out = lax.map(block, (jnp.arange(nb), qb, cb))
    return out.transpose(1, 0, 2, 3, 4).reshape(b_, s_, ATTN_WIDTH)


def fox_sample(q, k, v, logf, cache_k, cache_v, cache_logf):
    b_, t_ = q.shape[:2]
    p_ = cache_k.shape[1]
    k_all = jnp.concatenate([cache_k, k], axis=1)
    v_all = jnp.concatenate([cache_v, v], axis=1)
    c = jnp.cumsum(jnp.concatenate([cache_logf.astype(jnp.float32), logf], axis=1), axis=1)
    cT = c.transpose(0, 2, 1)
    s = jnp.einsum('bqhd,bkhd->bhqk', q, k_all).astype(jnp.float32) * (HEAD_DIM ** -0.5)
    s = s + cT[..., p_:, None] - cT[..., None, :]
    qpos = p_ + jnp.arange(t_)
    kpos = jnp.arange(p_ + t_)
    s = jnp.where(kpos[None, :] <= qpos[:, None], s, -jnp.inf)
    pr = jax.nn.softmax(s, axis=-1).astype(v_all.dtype)
    return jnp.einsum('bhqk,bkhd->bqhd', pr, v_all).reshape(b_, t_, ATTN_WIDTH)


def pool_mixer(p_hist, pos, pool_w, pool_scale):
    b_ = p_hist.shape[0]
    t_ = pos.shape[0]
    pf = p_hist.astype(jnp.float32)
    cs = jnp.cumsum(pf, axis=1)
    cs = jnp.concatenate([jnp.zeros_like(cs[:, :1]), cs], axis=1)
    end = cs[:, POOL_PAD + 1:]
    x_tok = pf[:, POOL_PAD:]
    groups = []
    for g, w in enumerate(POOL_WINDOWS):
        lo, hi = g * POOL_GROUP_DIM, (g + 1) * POOL_GROUP_DIM
        start = cs[:, POOL_PAD + 1 - w:POOL_PAD + 1 - w + t_, lo:hi]
        cnt = jnp.minimum(pos + 1, w).astype(jnp.float32)[None, :, None]
        groups.append((end[..., lo:hi] - start) / cnt - x_tok[..., lo:hi])
    d = jnp.stack(groups, axis=2)
    y = jnp.einsum('btgc,gce->btge', d, pool_w.astype(jnp.float32))
    return (y.reshape(b_, t_, POOL_WIDTH) * pool_scale).astype(p_hist.dtype)


def hier_moe(x, w_rg, b_rg, w_re, b_re, w_gate, w_up, w_down):
    n_ = x.shape[0]
    lg = jnp.einsum('nd,dg->ng', x, w_rg).astype(jnp.float32) + b_rg
    pg = jax.nn.softmax(lg, axis=-1)
    top_pg, g_idx = lax.top_k(pg, 1)
    le = (jnp.einsum('nd,de->ne', x, w_re).astype(jnp.float32) + b_re).reshape(n_, N_GROUPS, EXPERTS_PER_GROUP)
    le_sel = jnp.take_along_axis(le, g_idx[:, :, None], axis=1)[:, 0]
    top_le, e_idx = lax.top_k(le_sel, TOP_K_INNER)
    w_sel = jax.nn.softmax(top_le, axis=-1) * top_pg
    expert_id = g_idx * EXPERTS_PER_GROUP + e_idx
    gate = jnp.sum(jax.nn.one_hot(expert_id, N_EXPERTS, dtype=jnp.float32) * w_sel[..., None], axis=1)
    h = jax.nn.silu(jnp.einsum('nd,edf->nef', x, w_gate)) * jnp.einsum('nd,edf->nef', x, w_up)
    h = h * gate[:, :, None].astype(h.dtype)
    return jnp.einsum('nef,efd->nd', h, w_down)


def channel_mixer(x, norm_g, w_rg, b_rg, w_re, b_re, w_gate, w_up, w_down):
    b_, t_, d_ = x.shape
    n = rms_norm(x, norm_g).reshape(b_ * t_, d_)
    return x + hier_moe(n, w_rg, b_rg, w_re, b_re, w_gate, w_up, w_down).reshape(b_, t_, d_)


def setup_inputs(seed: int = 0) -> dict:
    key = jax.random.key(seed)
    ks = jax.random.split(key, 24)

    def nrm(k, shape, s):
        return jax.random.normal(k, shape, jnp.float32) * s

    L = DEPTH
    return {
        'x_prompt': nrm(ks[0], (BATCH, SEQ, D_MODEL), 1.0),
        'x_sample': nrm(ks[1], (DEC_BATCH, DEC_SEQ, D_MODEL), 1.0),
        'cache_k': nrm(ks[2], (L, DEC_BATCH, PAST_LEN, N_HEADS, HEAD_DIM), 1.0),
        'cache_v': nrm(ks[3], (L, DEC_BATCH, PAST_LEN, N_HEADS, HEAD_DIM), 1.0),
        'cache_logf': jax.nn.log_sigmoid(FORGET_BIAS_INIT + nrm(ks[4], (L, DEC_BATCH, PAST_LEN, N_HEADS), 1.0)),
        'state_pool': nrm(ks[5], (L, DEC_BATCH, POOL_PAD, POOL_WIDTH), 1.0),
        'norm1_g': 1.0 + nrm(ks[6], (L, D_MODEL), 0.02),
        'w_in': nrm(ks[7], (L, D_MODEL, IN_COLS), D_MODEL ** -0.5),
        'b_f': FORGET_BIAS_INIT + nrm(ks[8], (L, N_HEADS), 0.5),
        'q_norm_g': 1.0 + nrm(ks[9], (L, HEAD_DIM), 0.02),
        'k_norm_g': 1.0 + nrm(ks[10], (L, HEAD_DIM), 0.02),
        'pool_w': nrm(ks[11], (L, N_POOL_GROUPS, POOL_GROUP_DIM, POOL_GROUP_DIM), POOL_GROUP_DIM ** -0.5),
        'pool_scale': 1.0 + nrm(ks[12], (L, POOL_WIDTH), 0.1),
        'w_out': nrm(ks[13], (L, MIX_WIDTH, D_MODEL), MIX_WIDTH ** -0.5),
        'norm2_g': 1.0 + nrm(ks[14], (L, D_MODEL), 0.02),
        'w_router_group': nrm(ks[15], (L, D_MODEL, N_GROUPS), D_MODEL ** -0.5),
        'b_router_group': nrm(ks[16], (L, N_GROUPS), 0.01),
        'w_router_expert': nrm(ks[17], (L, D_MODEL, N_EXPERTS), D_MODEL ** -0.5),
        'b_router_expert': nrm(ks[18], (L, N_EXPERTS), 0.01),
        'w_gate': nrm(ks[19], (L, N_EXPERTS, D_MODEL, D_EXPERT), D_MODEL ** -0.5),
        'w_up': nrm(ks[20], (L, N_EXPERTS, D_MODEL, D_EXPERT), D_MODEL ** -0.5),
        'w_down': nrm(ks[21], (L, N_EXPERTS, D_EXPERT, D_MODEL), D_EXPERT ** -0.5),
    }


def reference(x_prompt, x_sample, cache_k, cache_v, cache_logf, state_pool,
              norm1_g, w_in, b_f, q_norm_g, k_norm_g, pool_w, pool_scale, w_out,
              norm2_g, w_router_group, b_router_group, w_router_expert, b_router_expert,
              w_gate, w_up, w_down):
    b_p, s_p = x_prompt.shape[:2]
    t_s = x_sample.shape[1]
    p_len = cache_k.shape[2]
    pos_p = jnp.arange(s_p)
    pos_s = p_len + jnp.arange(t_s)
    xp, xs = x_prompt, x_sample
    kp, vp, fp, pp = [], [], [], []
    ksl, vsl, fsl, psl = [], [], [], []
    for l in range(DEPTH):
        q, k, v, logf, pin = project_in(xp, norm1_g[l], w_in[l], b_f[l], q_norm_g[l], k_norm_g[l])
        a = fox_prompt(q, k, v, logf)
        hist = jnp.concatenate([jnp.zeros((b_p, POOL_PAD, POOL_WIDTH), pin.dtype), pin], axis=1)
        pm = pool_mixer(hist, pos_p, pool_w[l], pool_scale[l])
        xp = xp + jnp.einsum('btc,cd->btd', jnp.concatenate([a, pm], axis=-1), w_out[l])
        xp = channel_mixer(xp, norm2_g[l], w_router_group[l], b_router_group[l], w_router_expert[l],
                           b_router_expert[l], w_gate[l], w_up[l], w_down[l])
        kp.append(k)
        vp.append(v)
        fp.append(logf)
        pp.append(hist[:, -POOL_PAD:])
        q, k, v, logf, pin = project_in(xs, norm1_g[l], w_in[l], b_f[l], q_norm_g[l], k_norm_g[l])
        a = fox_sample(q, k, v, logf, cache_k[l], cache_v[l], cache_logf[l])
        hist = jnp.concatenate([state_pool[l].astype(pin.dtype), pin], axis=1)
        pm = pool_mixer(hist, pos_s, pool_w[l], pool_scale[l])
        xs = xs + jnp.einsum('btc,cd->btd', jnp.concatenate([a, pm], axis=-1), w_out[l])
        xs = channel_mixer(xs, norm2_g[l], w_router_group[l], b_router_group[l], w_router_expert[l],
                           b_router_expert[l], w_gate[l], w_up[l], w_down[l])
        ksl.append(k)
        vsl.append(v)
        fsl.append(logf)
        psl.append(hist[:, -POOL_PAD:])
    k_prompt = jnp.stack(kp)
    v_prompt = jnp.stack(vp)
    logf_prompt = jnp.stack(fp)
    pool_prompt = jnp.stack(pp)
    k_sample = jnp.stack(ksl)
    v_sample = jnp.stack(vsl)
    logf_sample = jnp.stack(fsl)
    pool_sample = jnp.stack(psl)
    return (xp, xs, k_prompt, v_prompt, logf_prompt, pool_prompt, k_sample, v_sample, logf_sample, pool_sample)
```

```python
import functools

import jax
import jax.numpy as jnp
from jax import lax
from jax.experimental import pallas as pl
from jax.experimental.pallas import tpu as pltpu

F32 = jnp.float32
BF16 = jnp.bfloat16

D_MODEL = 1024
ATTN_WIDTH = 512
N_HEADS = 8
HEAD_DIM = 64
POOL_WIDTH = 512
POOL_WINDOWS = (2, 4, 8, 16)
POOL_GROUP_DIM = 128
POOL_PAD = 15
HIST_ROWS = 16
N_GROUPS = 4
EXPERTS_PER_GROUP = 4
N_EXPERTS = 16
D_EXPERT = 512
EPS = 1e-6
LANES = 128
PAIR = 2 * HEAD_DIM
N_PAIRS = N_HEADS // 2
VMEM_LIMIT = 56 * 1024 * 1024


def _cparams(sem):
    return pltpu.CompilerParams(dimension_semantics=sem, vmem_limit_bytes=VMEM_LIMIT)


def _split3(x):
    hi = x.astype(BF16)
    r1 = x - hi.astype(F32)
    mid = r1.astype(BF16)
    lo = (r1 - mid.astype(F32)).astype(BF16)
    return hi, mid, lo


def _split2(x):
    hi = x.astype(BF16)
    lo = (x - hi.astype(F32)).astype(BF16)
    return hi, lo


def _inproj_kernel(x_ref, g1_ref, w_ref, bf_ref, qg_ref, kg_ref, hm_ref, tri_ref,
                   q_ref, kf_ref, kb_ref, vf_ref, vb_ref, p_ref, logf_ref, c_ref, carry_ref, *, carry_rows):
    x = x_ref[...]
    r = lax.rsqrt(jnp.mean(x * x, axis=-1, keepdims=True) + EPS)
    n = ((x * r) * g1_ref[...]).astype(BF16)
    z = jnp.dot(n, w_ref[...], preferred_element_type=F32)

    hm = hm_ref[...]

    def head_norm(zz, g):
        hi, lo = _split2(zz * zz)
        ms = jnp.dot(hi, hm, preferred_element_type=F32) + jnp.dot(lo, hm, preferred_element_type=F32)
        return (zz * lax.rsqrt(ms + EPS)) * g

    q = head_norm(z[:, 0:ATTN_WIDTH], qg_ref[...])
    q_ref[...] = (q * (HEAD_DIM ** -0.5)).astype(BF16)
    k = head_norm(z[:, ATTN_WIDTH:2 * ATTN_WIDTH], kg_ref[...])
    kf_ref[...] = k
    kb_ref[...] = k.astype(BF16)
    v = z[:, 2 * ATTN_WIDTH:3 * ATTN_WIDTH]
    vf_ref[...] = v
    vb_ref[...] = v.astype(BF16)
    p_ref[...] = z[:, 3 * ATTN_WIDTH:3 * ATTN_WIDTH + POOL_WIDTH]

    f = z[:, 3 * ATTN_WIDTH + POOL_WIDTH:] + bf_ref[...]
    lane = lax.broadcasted_iota(jnp.int32, f.shape, 1)
    logf = -(jnp.maximum(-f, 0.0) + jnp.log1p(jnp.exp(-jnp.abs(f))))
    logf = jnp.where(lane < N_HEADS, logf, 0.0)
    logf_ref[...] = logf

    tri = tri_ref[...]
    hi, mid, lo = _split3(logf)
    c = (jnp.dot(tri, hi, preferred_element_type=F32) + jnp.dot(tri, mid, preferred_element_type=F32)
         + jnp.dot(tri, lo, preferred_element_type=F32))
    if carry_rows:
        @pl.when(pl.program_id(0) == 0)
        def _():
            carry_ref[...] = jnp.zeros_like(carry_ref)
        c = c + carry_ref[0:1, :]
        carry_ref[...] = jnp.broadcast_to(c[c.shape[0] - 1:, :], carry_ref.shape)
    c_ref[...] = c


def _in_project(x2d, tm, seg, carry_rows, g1, w_all, bf_pad, qg, kg, hm):
    n = x2d.shape[0]
    rows = lax.broadcasted_iota(jnp.int32, (tm, tm), 0)
    cols = lax.broadcasted_iota(jnp.int32, (tm, tm), 1)
    tri = ((cols <= rows) & (rows // seg == cols // seg)).astype(BF16)
    wc = w_all.shape[1]
    row_blk = lambda w: pl.BlockSpec((tm, w), lambda i: (i, 0))
    const = lambda a: pl.BlockSpec(a.shape, lambda i: (0,) * a.ndim)
    out_shape = (
        jax.ShapeDtypeStruct((n, ATTN_WIDTH), BF16),
        jax.ShapeDtypeStruct((n, ATTN_WIDTH), F32),
        jax.ShapeDtypeStruct((n, ATTN_WIDTH), BF16),
        jax.ShapeDtypeStruct((n, ATTN_WIDTH), F32),
        jax.ShapeDtypeStruct((n, ATTN_WIDTH), BF16),
        jax.ShapeDtypeStruct((n, POOL_WIDTH), F32),
        jax.ShapeDtypeStruct((n, LANES), F32),
        jax.ShapeDtypeStruct((n, LANES), F32),
    )
    return pl.pallas_call(
        functools.partial(_inproj_kernel, carry_rows=carry_rows),
        grid=(n // tm,),
        in_specs=[row_blk(D_MODEL), const(g1), const(w_all), const(bf_pad), const(qg), const(kg), const(hm),
                  const(tri)],
        out_specs=[row_blk(ATTN_WIDTH)] * 5 + [row_blk(POOL_WIDTH), row_blk(LANES), row_blk(LANES)],
        out_shape=out_shape,
        scratch_shapes=[pltpu.VMEM((8, LANES), F32)],
        compiler_params=_cparams(("arbitrary",)),
        name="in_project",
    )(x2d, g1, w_all, bf_pad, qg, kg, hm, tri)


def _attn_prompt_kernel(q_ref, k_ref, v_ref, c_ref, ct_ref, o_ref, acc_ref, m_ref, l_ref, *, tq):
    hp = pl.program_id(0)
    i = pl.program_id(1)
    lane = lax.broadcasted_iota(jnp.int32, (1, PAIR), 1)
    first = lane < HEAD_DIM
    q = q_ref[...]
    zero = jnp.zeros_like(q)
    qh = (jnp.where(first, q, zero), jnp.where(first, zero, q))
    cblk = c_ref[...]
    clane = lax.broadcasted_iota(jnp.int32, cblk.shape, 1)
    cq = tuple(jnp.sum(jnp.where(clane == 2 * hp + h, cblk, 0.0), axis=1, keepdims=True) for h in range(2))

    acc_ref[...] = jnp.zeros_like(acc_ref)
    m_ref[...] = jnp.full_like(m_ref, -jnp.inf)
    l_ref[...] = jnp.zeros_like(l_ref)

    def step(j, masked):
        start = pl.multiple_of(j * tq, tq)
        k = k_ref[pl.ds(start, tq), :]
        v = v_ref[pl.ds(start, tq), :]
        alphas, pvs = [], []
        for h in range(2):
            s = lax.dot_general(qh[h], k, (((1,), (1,)), ((), ())), preferred_element_type=F32)
            s = s + cq[h] - ct_ref[h:h + 1, pl.ds(start, tq)]
            if masked:
                row = lax.broadcasted_iota(jnp.int32, s.shape, 0)
                col = lax.broadcasted_iota(jnp.int32, s.shape, 1)
                s = jnp.where(col <= row, s, -jnp.inf)
            m_prev = m_ref[h]
            m_new = jnp.maximum(m_prev, jnp.max(s, axis=1, keepdims=True))
            alpha = jnp.exp(m_prev - m_new)
            p = jnp.exp(s - m_new)
            l_ref[h] = alpha * l_ref[h] + jnp.sum(p, axis=1, keepdims=True)
            m_ref[h] = m_new
            alphas.append(alpha)
            pvs.append(jnp.dot(p.astype(BF16), v, preferred_element_type=F32))
        acc_ref[...] = acc_ref[...] * jnp.where(first, alphas[0], alphas[1]) + jnp.where(first, pvs[0], pvs[1])

    def body(j, carry):
        step(j, False)
        return carry

    lax.fori_loop(0, i, body, 0)
    step(i, True)
    o_ref[...] = (acc_ref[...] / jnp.where(first, l_ref[0], l_ref[1])).astype(o_ref.dtype)


def _attention_prompt(q, kb, vb, c, ct4, tq):
    n = q.shape[0]
    return pl.pallas_call(
        functools.partial(_attn_prompt_kernel, tq=tq),
        grid=(N_PAIRS, n // tq),
        in_specs=[
            pl.BlockSpec((tq, PAIR), lambda hp, i: (i, hp)),
            pl.BlockSpec((n, PAIR), lambda hp, i: (0, hp)),
            pl.BlockSpec((n, PAIR), lambda hp, i: (0, hp)),
            pl.BlockSpec((tq, LANES), lambda hp, i: (i, 0)),
            pl.BlockSpec((None, 2, n), lambda hp, i: (hp, 0, 0)),
        ],
        out_specs=pl.BlockSpec((tq, PAIR), lambda hp, i: (i, hp)),
        out_shape=jax.ShapeDtypeStruct((n, ATTN_WIDTH), BF16),
        scratch_shapes=[pltpu.VMEM((tq, PAIR), F32), pltpu.VMEM((2, tq, 1), F32), pltpu.VMEM((2, tq, 1), F32)],
        compiler_params=_cparams(("arbitrary", "arbitrary")),
        name="attention_prompt",
    )(q, kb, vb, c, ct4)


def _suffix_kernel(x_ref, o_ref):
    x = x_ref[...]
    p_len = x.shape[1]
    rows = lax.broadcasted_iota(jnp.int32, (p_len, p_len), 0)
    cols = lax.broadcasted_iota(jnp.int32, (p_len, p_len), 1)
    u = (rows > cols).astype(BF16)
    hi, mid, lo = _split3(x)
    o_ref[...] = (jnp.dot(hi, u, preferred_element_type=F32) + jnp.dot(mid, u, preferred_element_type=F32)
                  + jnp.dot(lo, u, preferred_element_type=F32))


def _suffix_sums(x):
    return pl.pallas_call(
        _suffix_kernel,
        out_shape=jax.ShapeDtypeStruct(x.shape, F32),
        compiler_params=pltpu.CompilerParams(vmem_limit_bytes=VMEM_LIMIT),
        name="cache_suffix_sums",
    )(x)


def _attn_sample_kernel(q_ref, kn_ref, vn_ref, ck_ref, cv_ref, d_ref, dt_ref, r_ref, o_ref):
    hp = pl.program_id(1)
    lane = lax.broadcasted_iota(jnp.int32, (1, PAIR), 1)
    first = lane < HEAD_DIM
    q = q_ref[...]
    zero = jnp.zeros_like(q)
    qh = (jnp.where(first, q, zero), jnp.where(first, zero, q))
    dblk = d_ref[...]
    dlane = lax.broadcasted_iota(jnp.int32, dblk.shape, 1)
    kn = kn_ref[...]
    vn = vn_ref[...]
    ck = ck_ref[...].astype(BF16)
    cv = cv_ref[...].astype(BF16)
    nt = (((1,), (1,)), ((), ()))
    outs, ls = [], []
    for h in range(2):
        dq = jnp.sum(jnp.where(dlane == 2 * hp + h, dblk, 0.0), axis=1, keepdims=True)
        s1 = lax.dot_general(qh[h], ck, nt, preferred_element_type=F32) + dq + r_ref[h:h + 1, :]
        s2 = lax.dot_general(qh[h], kn, nt, preferred_element_type=F32) + dq - dt_ref[h:h + 1, :]
        row = lax.broadcasted_iota(jnp.int32, s2.shape, 0)
        col = lax.broadcasted_iota(jnp.int32, s2.shape, 1)
        s2 = jnp.where(col <= row, s2, -jnp.inf)
        m = jnp.maximum(jnp.max(s1, axis=1, keepdims=True), jnp.max(s2, axis=1, keepdims=True))
        p1 = jnp.exp(s1 - m)
        p2 = jnp.exp(s2 - m)
        ls.append(jnp.sum(p1, axis=1, keepdims=True) + jnp.sum(p2, axis=1, keepdims=True))
        outs.append(jnp.dot(p1.astype(BF16), cv, preferred_element_type=F32)
                    + jnp.dot(p2.astype(BF16), vn, preferred_element_type=F32))
    o_ref[...] = (jnp.where(first, outs[0], outs[1]) / jnp.where(first, ls[0], ls[1])).astype(o_ref.dtype)


def _attention_sample(q, kb, vb, cache_k, cache_v, d, dt4, r4, t):
    nb, p_len = cache_k.shape[0], cache_k.shape[1]
    return pl.pallas_call(
        _attn_sample_kernel,
        grid=(nb, N_PAIRS),
        in_specs=[
            pl.BlockSpec((t, PAIR), lambda b, hp: (b, hp)),
            pl.BlockSpec((t, PAIR), lambda b, hp: (b, hp)),
            pl.BlockSpec((t, PAIR), lambda b, hp: (b, hp)),
            pl.BlockSpec((None, p_len, PAIR), lambda b, hp: (b, 0, hp)),
            pl.BlockSpec((None, p_len, PAIR), lambda b, hp: (b, 0, hp)),
            pl.BlockSpec((t, LANES), lambda b, hp: (b, 0)),
            pl.BlockSpec((None, None, 2, t), lambda b, hp: (b, hp, 0, 0)),
            pl.BlockSpec((None, None, 2, p_len), lambda b, hp: (b, hp, 0, 0)),
        ],
        out_specs=pl.BlockSpec((t, PAIR), lambda b, hp: (b, hp)),
        out_shape=jax.ShapeDtypeStruct((nb * t, ATTN_WIDTH), BF16),
        compiler_params=_cparams(("arbitrary", "arbitrary")),
        name="attention_sample",
    )(q, kb, vb, cache_k, cache_v, d, dt4, r4)


def _mix_kernel(a_ref, p_ref, hist_ref, x_ref, pw_ref, ps_ref, wo_ref, g2_ref, wrh_ref, wrl_ref, br_ref,
                x1_ref, n2_ref, gate_ref, *, pos0, pos_stride, zero_first):
    i = pl.program_id(0)
    t = p_ref.shape[0]
    p = p_ref[...]
    hist = hist_ref[...]
    if zero_first:
        hist = jnp.where(i == 0, 0.0, hist)
    xh = jnp.concatenate([hist, p], axis=0)
    pos = pos0 + i * pos_stride + lax.broadcasted_iota(jnp.int32, (t, 1), 0)
    ys = []
    for g, w in enumerate(POOL_WINDOWS):
        col = xh[:, g * POOL_GROUP_DIM:(g + 1) * POOL_GROUP_DIM]
        acc = col
        span = 1
        while span < w:
            acc = acc + pltpu.roll(acc, span, axis=0)
            span *= 2
        cnt = jnp.minimum(pos + 1, w).astype(F32)
        dgrp = acc[HIST_ROWS:, :] / cnt - col[HIST_ROWS:, :]
        ys.append(jnp.dot(dgrp.astype(BF16), pw_ref[g], preferred_element_type=F32))
    pm = (jnp.concatenate(ys, axis=1) * ps_ref[...]).astype(BF16)
    mix = (jnp.dot(a_ref[...], wo_ref[0:ATTN_WIDTH, :], preferred_element_type=F32)
           + jnp.dot(pm, wo_ref[ATTN_WIDTH:, :], preferred_element_type=F32))
    x1 = x_ref[...] + mix
    x1_ref[...] = x1

    r = lax.rsqrt(jnp.mean(x1 * x1, axis=-1, keepdims=True) + EPS)
    n2 = (x1 * r) * g2_ref[...]
    n2_ref[...] = n2.astype(BF16)

    nh, nl = _split2(n2)
    wrh = wrh_ref[...]
    logits = (jnp.dot(nh, wrh, preferred_element_type=F32) + jnp.dot(nl, wrh, preferred_element_type=F32)
              + jnp.dot(nh, wrl_ref[...], preferred_element_type=F32)) + br_ref[...]
    lane = lax.broadcasted_iota(jnp.int32, logits.shape, 1).astype(F32)
    neg = -jnp.inf
    big = jnp.float32(1 << 20)
    is_g = (lane >= N_EXPERTS) & (lane < N_EXPERTS + N_GROUPS)
    lg = jnp.where(is_g, logits, neg)
    gmax = jnp.max(lg, axis=1, keepdims=True)
    g_lane = jnp.min(jnp.where(lg == gmax, lane, big), axis=1, keepdims=True)
    top_pg = 1.0 / jnp.sum(jnp.exp(lg - gmax), axis=1, keepdims=True)
    g_idx = g_lane - N_EXPERTS
    sel = (lane >= g_idx * EXPERTS_PER_GROUP) & (lane < (g_idx + 1) * EXPERTS_PER_GROUP)
    le = jnp.where(sel, logits, neg)
    m1 = jnp.max(le, axis=1, keepdims=True)
    i1 = jnp.min(jnp.where(le == m1, lane, big), axis=1, keepdims=True)
    le2 = jnp.where(lane == i1, neg, le)
    m2 = jnp.max(le2, axis=1, keepdims=True)
    i2 = jnp.min(jnp.where(le2 == m2, lane, big), axis=1, keepdims=True)
    e2 = jnp.exp(m2 - m1)
    w1 = top_pg / (1.0 + e2)
    w2 = top_pg * e2 / (1.0 + e2)
    gate_ref[...] = jnp.where(lane == i1, w1, 0.0) + jnp.where(lane == i2, w2, 0.0)


def _mix(a, p, hist_src, hist_map, x2d, t, pos0, pos_stride, zero_first, pw, ps, wo, g2, wrh, wrl, br):
    n = x2d.shape[0]
    row_blk = lambda w: pl.BlockSpec((t, w), lambda i: (i, 0))
    const = lambda arr: pl.BlockSpec(arr.shape, lambda i: (0,) * arr.ndim)
    return pl.pallas_call(
        functools.partial(_mix_kernel, pos0=pos0, pos_stride=pos_stride, zero_first=zero_first),
        grid=(n // t,),
        in_specs=[row_blk(ATTN_WIDTH), row_blk(POOL_WIDTH), pl.BlockSpec((HIST_ROWS, POOL_WIDTH), hist_map),
                  row_blk(D_MODEL), const(pw), const(ps), const(wo), const(g2), const(wrh), const(wrl), const(br)],
        out_specs=[row_blk(D_MODEL), row_blk(D_MODEL), row_blk(LANES)],
        out_shape=(jax.ShapeDtypeStruct((n, D_MODEL), F32), jax.ShapeDtypeStruct((n, D_MODEL), BF16),
                   jax.ShapeDtypeStruct((n, LANES), F32)),
        compiler_params=_cparams(("arbitrary",)),
        name="pool_mix_route",
    )(a, p, hist_src, x2d, pw, ps, wo, g2, wrh, wrl, br)


def _moe_kernel(n2_ref, gate_ref, x1_ref, wg_ref, wu_ref, wd_ref, y_ref, acc_ref):
    e = pl.program_id(1)

    @pl.when(e == 0)
    def _():
        acc_ref[...] = jnp.zeros_like(acc_ref)

    gate = gate_ref[...]
    lane = lax.broadcasted_iota(jnp.int32, gate.shape, 1)
    ge = jnp.sum(jnp.where(lane == e, gate, 0.0), axis=1, keepdims=True)
    n2 = n2_ref[...]
    hg = jnp.dot(n2, wg_ref[...], preferred_element_type=F32)
    hu = jnp.dot(n2, wu_ref[...], preferred_element_type=F32)
    h = (hg * jax.nn.sigmoid(hg)) * hu * ge
    acc_ref[...] += jnp.dot(h.astype(BF16), wd_ref[...], preferred_element_type=F32)

    @pl.when(e == pl.num_programs(1) - 1)
    def _():
        y_ref[...] = x1_ref[...] + acc_ref[...]


def _moe(n2, gate, x1, wg, wu, wd, tm):
    n = n2.shape[0]
    return pl.pallas_call(
        _moe_kernel,
        grid=(n // tm, N_EXPERTS),
        in_specs=[
            pl.BlockSpec((tm, D_MODEL), lambda i, e: (i, 0)),
            pl.BlockSpec((tm, LANES), lambda i, e: (i, 0)),
            pl.BlockSpec((tm, D_MODEL), lambda i, e: (i, 0)),
            pl.BlockSpec((None, D_MODEL, D_EXPERT), lambda i, e: (e, 0, 0)),
            pl.BlockSpec((None, D_MODEL, D_EXPERT), lambda i, e: (e, 0, 0)),
            pl.BlockSpec((None, D_EXPERT, D_MODEL), lambda i, e: (e, 0, 0)),
        ],
        out_specs=pl.BlockSpec((tm, D_MODEL), lambda i, e: (i, 0)),
        out_shape=jax.ShapeDtypeStruct((n, D_MODEL), F32),
        scratch_shapes=[pltpu.VMEM((tm, D_MODEL), F32)],
        compiler_params=_cparams(("arbitrary", "arbitrary")),
        name="moe_experts",
    )(n2, gate, x1, wg, wu, wd)


def kernel(x_prompt, x_sample, cache_k, cache_v, cache_logf, state_pool, norm1_g, w_in, b_f, q_norm_g, k_norm_g,
           pool_w, pool_scale, w_out, norm2_g, w_router_group, b_router_group, w_router_expert, b_router_expert,
           w_gate, w_up, w_down):
    depth = norm1_g.shape[0]
    assert depth == 1, "single-layer step"
    b_p, s_p, _ = x_prompt.shape
    assert b_p == 1, "prompt kernels assume one stream"
    b_s, t_s, _ = x_sample.shape
    p_len = cache_k.shape[2]
    l = 0

    w = w_in[l]
    a3 = 3 * ATTN_WIDTH
    w_all = jnp.concatenate(
        [w[:, :a3], w[:, a3 + N_HEADS:], jnp.pad(w[:, a3:a3 + N_HEADS], ((0, 0), (0, LANES - N_HEADS)))],
        axis=1).astype(BF16)
    bf_pad = jnp.pad(b_f[l], (0, LANES - N_HEADS)).reshape(1, LANES)
    g1 = norm1_g[l].reshape(1, D_MODEL)
    qg = jnp.tile(q_norm_g[l], N_HEADS).reshape(1, ATTN_WIDTH)
    kg = jnp.tile(k_norm_g[l], N_HEADS).reshape(1, ATTN_WIDTH)
    hr = lax.broadcasted_iota(jnp.int32, (ATTN_WIDTH, ATTN_WIDTH), 0) // HEAD_DIM
    hc = lax.broadcasted_iota(jnp.int32, (ATTN_WIDTH, ATTN_WIDTH), 1) // HEAD_DIM
    hm = jnp.where(hr == hc, 1.0 / HEAD_DIM, 0.0).astype(BF16)
    pw = pool_w[l].astype(BF16)
    ps = pool_scale[l].reshape(1, POOL_WIDTH)
    wo = w_out[l].astype(BF16)
    g2 = norm2_g[l].reshape(1, D_MODEL)
    w_r = jnp.pad(jnp.concatenate([w_router_expert[l], w_router_group[l]], axis=1),
                  ((0, 0), (0, LANES - N_EXPERTS - N_GROUPS)))
    wrh = w_r.astype(BF16)
    wrl = (w_r - wrh.astype(F32)).astype(BF16)
    br = jnp.pad(jnp.concatenate([b_router_expert[l], b_router_group[l]]),
                 (0, LANES - N_EXPERTS - N_GROUPS)).reshape(1, LANES)
    wg = w_gate[l].astype(BF16)
    wu = w_up[l].astype(BF16)
    wd = w_down[l].astype(BF16)

    tile = 512

    xp = x_prompt.reshape(s_p, D_MODEL)
    q, kf, kb, vf, vb, pin, logf, c = _in_project(xp, tile, tile, True, g1, w_all, bf_pad, qg, kg, hm)
    ct4 = c[:, :N_HEADS].T.reshape(N_PAIRS, 2, s_p)
    a = _attention_prompt(q, kb, vb, c, ct4, tile)
    hist_map = lambda i: (jnp.maximum(i * (tile // HIST_ROWS) - 1, 0), 0)
    x1, n2, gate = _mix(a, pin, pin, hist_map, xp, tile, 0, tile, True, pw, ps, wo, g2, wrh, wrl, br)
    y_prompt = _moe(n2, gate, x1, wg, wu, wd, 1024).reshape(b_p, s_p, D_MODEL)
    k_prompt = kf.reshape(depth, b_p, s_p, N_HEADS, HEAD_DIM)
    v_prompt = vf.reshape(depth, b_p, s_p, N_HEADS, HEAD_DIM)
    logf_prompt = logf[:, :N_HEADS].reshape(depth, b_p, s_p, N_HEADS)
    pool_prompt = pin[s_p - POOL_PAD:].reshape(depth, b_p, POOL_PAD, POOL_WIDTH)

    n_s = b_s * t_s
    xs = x_sample.reshape(n_s, D_MODEL)
    q, kf, kb, vf, vb, pin, logf, d = _in_project(xs, tile, t_s, False, g1, w_all, bf_pad, qg, kg, hm)
    dt4 = d[:, :N_HEADS].reshape(b_s, t_s, N_PAIRS, 2).transpose(0, 2, 3, 1)
    clf = cache_logf[l].astype(F32).transpose(0, 2, 1).reshape(b_s * N_HEADS, p_len)
    r4 = _suffix_sums(clf).reshape(b_s, N_PAIRS, 2, p_len)
    ck = cache_k[l].reshape(b_s, p_len, ATTN_WIDTH)
    cv = cache_v[l].reshape(b_s, p_len, ATTN_WIDTH)
    a = _attention_sample(q, kb, vb, ck, cv, d, dt4, r4, t_s)
    hist_s = jnp.pad(state_pool[l], ((0, 0), (HIST_ROWS - POOL_PAD, 0), (0, 0))).reshape(b_s * HIST_ROWS, POOL_WIDTH)
    x1, n2, gate = _mix(a, pin, hist_s, lambda i: (i, 0), xs, t_s, p_len, 0, False, pw, ps, wo, g2, wrh, wrl, br)
    y_sample = _moe(n2, gate, x1, wg, wu, wd, n_s).reshape(b_s, t_s, D_MODEL)
    k_sample = kf.reshape(depth, b_s, t_s, N_HEADS, HEAD_DIM)
    v_sample = vf.reshape(depth, b_s, t_s, N_HEADS, HEAD_DIM)
    logf_sample = logf[:, :N_HEADS].reshape(depth, b_s, t_s, N_HEADS)
    pool_sample = pin.reshape(b_s, t_s, POOL_WIDTH)[:, t_s - POOL_PAD:].reshape(depth, b_s, POOL_PAD, POOL_WIDTH)

    return (y_prompt, y_sample, k_prompt, v_prompt, logf_prompt, pool_prompt,
            k_sample, v_sample, logf_sample, pool_sample)
```

```python
import functools

import jax
import jax.numpy as jnp
from jax import lax
from jax.experimental import pallas as pl
from jax.experimental.pallas import tpu as pltpu

F32 = jnp.float32
BF16 = jnp.bfloat16

D_MODEL = 1024
ATTN_WIDTH = 512
N_HEADS = 8
HEAD_DIM = 64
POOL_WIDTH = 512
POOL_WINDOWS = (2, 4, 8, 16)
POOL_GROUP_DIM = 128
POOL_PAD = 15
HIST_ROWS = 16
N_GROUPS = 4
EXPERTS_PER_GROUP = 4
N_EXPERTS = 16
D_EXPERT = 512
EPS = 1e-6
LANES = 128
PAIR = 2 * HEAD_DIM
N_PAIRS = N_HEADS // 2
VMEM_LIMIT = 56 * 1024 * 1024


def _cparams(sem):
    return pltpu.CompilerParams(dimension_semantics=sem, vmem_limit_bytes=VMEM_LIMIT)


def _split3(x):
    hi = x.astype(BF16)
    r1 = x - hi.astype(F32)
    mid = r1.astype(BF16)
    lo = (r1 - mid.astype(F32)).astype(BF16)
    return hi, mid, lo


def _split2(x):
    hi = x.astype(BF16)
    lo = (x - hi.astype(F32)).astype(BF16)
    return hi, lo


def _inproj_kernel(x_ref, g1_ref, w_ref, bf_ref, qg_ref, kg_ref, hm_ref, tri_ref,
                   q_ref, kf_ref, kb_ref, vf_ref, vb_ref, p_ref, logf_ref, c_ref, carry_ref, *, carry_rows):
    x = x_ref[...]
    r = lax.rsqrt(jnp.mean(x * x, axis=-1, keepdims=True) + EPS)
    n = ((x * r) * g1_ref[...]).astype(BF16)
    z = jnp.dot(n, w_ref[...], preferred_element_type=F32)

    hm = hm_ref[...]

    def head_norm(zz, g):
        hi, lo = _split2(zz * zz)
        ms = jnp.dot(hi, hm, preferred_element_type=F32) + jnp.dot(lo, hm, preferred_element_type=F32)
        return (zz * lax.rsqrt(ms + EPS)) * g

    q = head_norm(z[:, 0:ATTN_WIDTH], qg_ref[...])
    q_ref[...] = (q * (HEAD_DIM ** -0.5)).astype(BF16)
    k = head_norm(z[:, ATTN_WIDTH:2 * ATTN_WIDTH], kg_ref[...])
    kf_ref[...] = k
    kb_ref[...] = k.astype(BF16)
    v = z[:, 2 * ATTN_WIDTH:3 * ATTN_WIDTH]
    vf_ref[...] = v
    vb_ref[...] = v.astype(BF16)
    p_ref[...] = z[:, 3 * ATTN_WIDTH:3 * ATTN_WIDTH + POOL_WIDTH]

    f = z[:, 3 * ATTN_WIDTH + POOL_WIDTH:] + bf_ref[...]
    lane = lax.broadcasted_iota(jnp.int32, f.shape, 1)
    logf = -(jnp.maximum(-f, 0.0) + jnp.log1p(jnp.exp(-jnp.abs(f))))
    logf = jnp.where(lane < N_HEADS, logf, 0.0)
    logf_ref[...] = logf

    tri = tri_ref[...]
    hi, mid, lo = _split3(logf)
    c = (jnp.dot(tri, hi, preferred_element_type=F32) + jnp.dot(tri, mid, preferred_element_type=F32)
         + jnp.dot(tri, lo, preferred_element_type=F32))
    if carry_rows:
        @pl.when(pl.program_id(0) == 0)
        def _():
            carry_ref[...] = jnp.zeros_like(carry_ref)
        c = c + carry_ref[0:1, :]
        carry_ref[...] = jnp.broadcast_to(c[c.shape[0] - 1:, :], carry_ref.shape)
    c_ref[...] = c


def _in_project(x2d, tm, seg, carry_rows, g1, w_all, bf_pad, qg, kg, hm):
    n = x2d.shape[0]
    rows = lax.broadcasted_iota(jnp.int32, (tm, tm), 0)
    cols = lax.broadcasted_iota(jnp.int32, (tm, tm), 1)
    tri = ((cols <= rows) & (rows // seg == cols // seg)).astype(BF16)
    wc = w_all.shape[1]
    row_blk = lambda w: pl.BlockSpec((tm, w), lambda i: (i, 0))
    const = lambda a: pl.BlockSpec(a.shape, lambda i: (0,) * a.ndim)
    out_shape = (
        jax.ShapeDtypeStruct((n, ATTN_WIDTH), BF16),
        jax.ShapeDtypeStruct((n, ATTN_WIDTH), F32),
        jax.ShapeDtypeStruct((n, ATTN_WIDTH), BF16),
        jax.ShapeDtypeStruct((n, ATTN_WIDTH), F32),
        jax.ShapeDtypeStruct((n, ATTN_WIDTH), BF16),
        jax.ShapeDtypeStruct((n, POOL_WIDTH), F32),
        jax.ShapeDtypeStruct((n, LANES), F32),
        jax.ShapeDtypeStruct((n, LANES), F32),
    )
    return pl.pallas_call(
        functools.partial(_inproj_kernel, carry_rows=carry_rows),
        grid=(n // tm,),
        in_specs=[row_blk(D_MODEL), const(g1), const(w_all), const(bf_pad), const(qg), const(kg), const(hm),
                  const(tri)],
        out_specs=[row_blk(ATTN_WIDTH)] * 5 + [row_blk(POOL_WIDTH), row_blk(LANES), row_blk(LANES)],
        out_shape=out_shape,
        scratch_shapes=[pltpu.VMEM((8, LANES), F32)],
        compiler_params=_cparams(("arbitrary",)),
        name="in_project",
    )(x2d, g1, w_all, bf_pad, qg, kg, hm, tri)


def _attn_prompt_kernel(j0_ref, q_ref, k_ref, v_ref, c_ref, ct_ref, o_ref, acc_ref, m_ref, l_ref, *, tq):
    hp = pl.program_id(0)
    i = pl.program_id(1)
    lane = lax.broadcasted_iota(jnp.int32, (1, PAIR), 1)
    first = lane < HEAD_DIM
    q = q_ref[...]
    zero = jnp.zeros_like(q)
    qh = (jnp.where(first, q, zero), jnp.where(first, zero, q))
    cblk = c_ref[...]
    clane = lax.broadcasted_iota(jnp.int32, cblk.shape, 1)

    acc_ref[...] = jnp.zeros_like(acc_ref)
    m_ref[...] = jnp.full_like(m_ref, -jnp.inf)
    l_ref[...] = jnp.zeros_like(l_ref)

    for h in range(2):
        cq = jnp.sum(jnp.where(clane == 2 * hp + h, cblk, 0.0), axis=1, keepdims=True)

        def step(j, masked, h=h, cq=cq):
            start = pl.multiple_of(j * tq, tq)
            k = k_ref[pl.ds(start, tq), :]
            v = v_ref[pl.ds(start, tq), :]
            s = lax.dot_general(qh[h], k, (((1,), (1,)), ((), ())), preferred_element_type=F32)
            s = s + cq - ct_ref[h:h + 1, pl.ds(start, tq)]
            if masked:
                row = lax.broadcasted_iota(jnp.int32, s.shape, 0)
                col = lax.broadcasted_iota(jnp.int32, s.shape, 1)
                s = jnp.where(col <= row, s, -jnp.inf)
            m_prev = m_ref[h]
            m_new = jnp.maximum(m_prev, jnp.max(s, axis=1, keepdims=True))
            alpha = jnp.exp(m_prev - m_new)
            p = jnp.exp(s - m_new)
            l_ref[h] = alpha * l_ref[h] + jnp.sum(p, axis=1, keepdims=True)
            m_ref[h] = m_new
            acc_ref[h] = acc_ref[h] * alpha + jnp.dot(p.astype(BF16), v, preferred_element_type=F32)

        def body(j, carry, step=step):
            step(j, False)
            return carry

        lax.fori_loop(j0_ref[i, 2 * hp + h], i, body, 0)
        step(i, True)

    o_ref[...] = (jnp.where(first, acc_ref[0], acc_ref[1]) / jnp.where(first, l_ref[0], l_ref[1])).astype(o_ref.dtype)


def _attention_prompt(j0, q, kb, vb, c, ct4, tq):
    n = q.shape[0]
    grid_spec = pltpu.PrefetchScalarGridSpec(
        num_scalar_prefetch=1,
        grid=(N_PAIRS, n // tq),
        in_specs=[
            pl.BlockSpec((tq, PAIR), lambda hp, i, j0: (i, hp)),
            pl.BlockSpec((n, PAIR), lambda hp, i, j0: (0, hp)),
            pl.BlockSpec((n, PAIR), lambda hp, i, j0: (0, hp)),
            pl.BlockSpec((tq, LANES), lambda hp, i, j0: (i, 0)),
            pl.BlockSpec((None, 2, n), lambda hp, i, j0: (hp, 0, 0)),
        ],
        out_specs=pl.BlockSpec((tq, PAIR), lambda hp, i, j0: (i, hp)),
        scratch_shapes=[pltpu.VMEM((2, tq, PAIR), F32), pltpu.VMEM((2, tq, 1), F32), pltpu.VMEM((2, tq, 1), F32)],
    )
    return pl.pallas_call(
        functools.partial(_attn_prompt_kernel, tq=tq),
        grid_spec=grid_spec,
        out_shape=jax.ShapeDtypeStruct((n, ATTN_WIDTH), BF16),
        compiler_params=_cparams(("arbitrary", "arbitrary")),
        name="attention_prompt",
    )(j0, q, kb, vb, c, ct4)


def _first_key_blocks(c, q_gain, k_gain, blk):
    n = c.shape[0]
    nb = n // blk
    qk_bound = 1.02 * HEAD_DIM ** 0.5 * jnp.max(jnp.abs(q_gain)) * jnp.max(jnp.abs(k_gain))
    threshold = 104.0 + 2.0 * qk_bound + 1.0
    c_start = c[0::blk, :N_HEADS]
    c_end = c[blk - 1::blk, :N_HEADS]
    gap = c_start[:, None, :] - c_end[None, :, :]
    earlier = jnp.arange(nb)[None, :, None] < jnp.arange(nb)[:, None, None]
    return jnp.sum((gap < -threshold) & earlier, axis=1).astype(jnp.int32)


def _suffix_kernel(x_ref, o_ref):
    x = x_ref[...]
    p_len = x.shape[1]
    rows = lax.broadcasted_iota(jnp.int32, (p_len, p_len), 0)
    cols = lax.broadcasted_iota(jnp.int32, (p_len, p_len), 1)
    u = (rows > cols).astype(BF16)
    hi, mid, lo = _split3(x)
    o_ref[...] = (jnp.dot(hi, u, preferred_element_type=F32) + jnp.dot(mid, u, preferred_element_type=F32)
                  + jnp.dot(lo, u, preferred_element_type=F32))


def _suffix_sums(x):
    return pl.pallas_call(
        _suffix_kernel,
        out_shape=jax.ShapeDtypeStruct(x.shape, F32),
        compiler_params=pltpu.CompilerParams(vmem_limit_bytes=VMEM_LIMIT),
        name="cache_suffix_sums",
    )(x)


def _attn_sample_kernel(q_ref, kn_ref, vn_ref, ck_ref, cv_ref, d_ref, dt_ref, r_ref, o_ref):
    hp = pl.program_id(1)
    lane = lax.broadcasted_iota(jnp.int32, (1, PAIR), 1)
    first = lane < HEAD_DIM
    q = q_ref[...]
    zero = jnp.zeros_like(q)
    qh = (jnp.where(first, q, zero), jnp.where(first, zero, q))
    dblk = d_ref[...]
    dlane = lax.broadcasted_iota(jnp.int32, dblk.shape, 1)
    kn = kn_ref[...]
    vn = vn_ref[...]
    ck = ck_ref[...].astype(BF16)
    cv = cv_ref[...].astype(BF16)
    nt = (((1,), (1,)), ((), ()))
    outs, ls = [], []
    for h in range(2):
        dq = jnp.sum(jnp.where(dlane == 2 * hp + h, dblk, 0.0), axis=1, keepdims=True)
        s1 = lax.dot_general(qh[h], ck, nt, preferred_element_type=F32) + dq + r_ref[h:h + 1, :]
        s2 = lax.dot_general(qh[h], kn, nt, preferred_element_type=F32) + dq - dt_ref[h:h + 1, :]
        row = lax.broadcasted_iota(jnp.int32, s2.shape, 0)
        col = lax.broadcasted_iota(jnp.int32, s2.shape, 1)
        s2 = jnp.where(col <= row, s2, -jnp.inf)
        m = jnp.maximum(jnp.max(s1, axis=1, keepdims=True), jnp.max(s2, axis=1, keepdims=True))
        p1 = jnp.exp(s1 - m)
        p2 = jnp.exp(s2 - m)
        ls.append(jnp.sum(p1, axis=1, keepdims=True) + jnp.sum(p2, axis=1, keepdims=True))
        outs.append(jnp.dot(p1.astype(BF16), cv, preferred_element_type=F32)
                    + jnp.dot(p2.astype(BF16), vn, preferred_element_type=F32))
    o_ref[...] = (jnp.where(first, outs[0], outs[1]) / jnp.where(first, ls[0], ls[1])).astype(o_ref.dtype)


def _attention_sample(q, kb, vb, cache_k, cache_v, d, dt4, r4, t):
    nb, p_len = cache_k.shape[0], cache_k.shape[1]
    return pl.pallas_call(
        _attn_sample_kernel,
        grid=(nb, N_PAIRS),
        in_specs=[
            pl.BlockSpec((t, PAIR), lambda b, hp: (b, hp)),
            pl.BlockSpec((t, PAIR), lambda b, hp: (b, hp)),
            pl.BlockSpec((t, PAIR), lambda b, hp: (b, hp)),
            pl.BlockSpec((None, p_len, PAIR), lambda b, hp: (b, 0, hp)),
            pl.BlockSpec((None, p_len, PAIR), lambda b, hp: (b, 0, hp)),
            pl.BlockSpec((t, LANES), lambda b, hp: (b, 0)),
            pl.BlockSpec((None, None, 2, t), lambda b, hp: (b, hp, 0, 0)),
            pl.BlockSpec((None, None, 2, p_len), lambda b, hp: (b, hp, 0, 0)),
        ],
        out_specs=pl.BlockSpec((t, PAIR), lambda b, hp: (b, hp)),
        out_shape=jax.ShapeDtypeStruct((nb * t, ATTN_WIDTH), BF16),
        compiler_params=_cparams(("arbitrary", "arbitrary")),
        name="attention_sample",
    )(q, kb, vb, cache_k, cache_v, d, dt4, r4)


def _mix_kernel(a_ref, p_ref, hist_ref, x_ref, pw_ref, ps_ref, wo_ref, g2_ref, wrh_ref, wrl_ref, br_ref,
                x1_ref, n2_ref, gate_ref, *, pos0, pos_stride, zero_first):
    i = pl.program_id(0)
    t = p_ref.shape[0]
    p = p_ref[...]
    hist = hist_ref[...]
    if zero_first:
        hist = jnp.where(i == 0, 0.0, hist)
    xh = jnp.concatenate([hist, p], axis=0)
    pos = pos0 + i * pos_stride + lax.broadcasted_iota(jnp.int32, (t, 1), 0)
    ys = []
    for g, w in enumerate(POOL_WINDOWS):
        col = xh[:, g * POOL_GROUP_DIM:(g + 1) * POOL_GROUP_DIM]
        acc = col
        span = 1
        while span < w:
            acc = acc + pltpu.roll(acc, span, axis=0)
            span *= 2
        cnt = jnp.minimum(pos + 1, w).astype(F32)
        dgrp = acc[HIST_ROWS:, :] / cnt - col[HIST_ROWS:, :]
        ys.append(jnp.dot(dgrp.astype(BF16), pw_ref[g], preferred_element_type=F32))
    pm = (jnp.concatenate(ys, axis=1) * ps_ref[...]).astype(BF16)
    mix = (jnp.dot(a_ref[...], wo_ref[0:ATTN_WIDTH, :], preferred_element_type=F32)
           + jnp.dot(pm, wo_ref[ATTN_WIDTH:, :], preferred_element_type=F32))
    x1 = x_ref[...] + mix
    x1_ref[...] = x1

    r = lax.rsqrt(jnp.mean(x1 * x1, axis=-1, keepdims=True) + EPS)
    n2 = (x1 * r) * g2_ref[...]
    n2_ref[...] = n2.astype(BF16)

    nh, nl = _split2(n2)
    wrh = wrh_ref[...]
    logits = (jnp.dot(nh, wrh, preferred_element_type=F32) + jnp.dot(nl, wrh, preferred_element_type=F32)
              + jnp.dot(nh, wrl_ref[...], preferred_element_type=F32)) + br_ref[...]
    lane = lax.broadcasted_iota(jnp.int32, logits.shape, 1).astype(F32)
    neg = -jnp.inf
    big = jnp.float32(1 << 20)
    is_g = (lane >= N_EXPERTS) & (lane < N_EXPERTS + N_GROUPS)
    lg = jnp.where(is_g, logits, neg)
    gmax = jnp.max(lg, axis=1, keepdims=True)
    g_lane = jnp.min(jnp.where(lg == gmax, lane, big), axis=1, keepdims=True)
    top_pg = 1.0 / jnp.sum(jnp.exp(lg - gmax), axis=1, keepdims=True)
    g_idx = g_lane - N_EXPERTS
    sel = (lane >= g_idx * EXPERTS_PER_GROUP) & (lane < (g_idx + 1) * EXPERTS_PER_GROUP)
    le = jnp.where(sel, logits, neg)
    m1 = jnp.max(le, axis=1, keepdims=True)
    i1 = jnp.min(jnp.where(le == m1, lane, big), axis=1, keepdims=True)
    le2 = jnp.where(lane == i1, neg, le)
    m2 = jnp.max(le2, axis=1, keepdims=True)
    i2 = jnp.min(jnp.where(le2 == m2, lane, big), axis=1, keepdims=True)
    e2 = jnp.exp(m2 - m1)
    w1 = top_pg / (1.0 + e2)
    w2 = top_pg * e2 / (1.0 + e2)
    gate_ref[...] = jnp.where(lane == i1, w1, 0.0) + jnp.where(lane == i2, w2, 0.0)


def _mix(a, p, hist_src, hist_map, x2d, t, pos0, pos_stride, zero_first, pw, ps, wo, g2, wrh, wrl, br):
    n = x2d.shape[0]
    row_blk = lambda w: pl.BlockSpec((t, w), lambda i: (i, 0))
    const = lambda arr: pl.BlockSpec(arr.shape, lambda i: (0,) * arr.ndim)
    return pl.pallas_call(
        functools.partial(_mix_kernel, pos0=pos0, pos_stride=pos_stride, zero_first=zero_first),
        grid=(n // t,),
        in_specs=[row_blk(ATTN_WIDTH), row_blk(POOL_WIDTH), pl.BlockSpec((HIST_ROWS, POOL_WIDTH), hist_map),
                  row_blk(D_MODEL), const(pw), const(ps), const(wo), const(g2), const(wrh), const(wrl), const(br)],
        out_specs=[row_blk(D_MODEL), row_blk(D_MODEL), row_blk(LANES)],
        out_shape=(jax.ShapeDtypeStruct((n, D_MODEL), F32), jax.ShapeDtypeStruct((n, D_MODEL), BF16),
                   jax.ShapeDtypeStruct((n, LANES), F32)),
        compiler_params=_cparams(("arbitrary",)),
        name="pool_mix_route",
    )(a, p, hist_src, x2d, pw, ps, wo, g2, wrh, wrl, br)


def _moe_kernel(n2_ref, gate_ref, x1_ref, wg_ref, wu_ref, wd_ref, y_ref, acc_ref):
    e = pl.program_id(1)

    @pl.when(e == 0)
    def _():
        acc_ref[...] = jnp.zeros_like(acc_ref)

    gate = gate_ref[...]
    lane = lax.broadcasted_iota(jnp.int32, gate.shape, 1)
    ge = jnp.sum(jnp.where(lane == e, gate, 0.0), axis=1, keepdims=True)
    n2 = n2_ref[...]
    hg = jnp.dot(n2, wg_ref[...], preferred_element_type=F32)
    hu = jnp.dot(n2, wu_ref[...], preferred_element_type=F32)
    h = (hg * jax.nn.sigmoid(hg)) * hu * ge
    acc_ref[...] += jnp.dot(h.astype(BF16), wd_ref[...], preferred_element_type=F32)

    @pl.when(e == pl.num_programs(1) - 1)
    def _():
        y_ref[...] = x1_ref[...] + acc_ref[...]


def _moe(n2, gate, x1, wg, wu, wd, tm):
    n = n2.shape[0]
    return pl.pallas_call(
        _moe_kernel,
        grid=(n // tm, N_EXPERTS),
        in_specs=[
            pl.BlockSpec((tm, D_MODEL), lambda i, e: (i, 0)),
            pl.BlockSpec((tm, LANES), lambda i, e: (i, 0)),
            pl.BlockSpec((tm, D_MODEL), lambda i, e: (i, 0)),
            pl.BlockSpec((None, D_MODEL, D_EXPERT), lambda i, e: (e, 0, 0)),
            pl.BlockSpec((None, D_MODEL, D_EXPERT), lambda i, e: (e, 0, 0)),
            pl.BlockSpec((None, D_EXPERT, D_MODEL), lambda i, e: (e, 0, 0)),
        ],
        out_specs=pl.BlockSpec((tm, D_MODEL), lambda i, e: (i, 0)),
        out_shape=jax.ShapeDtypeStruct((n, D_MODEL), F32),
        scratch_shapes=[pltpu.VMEM((tm, D_MODEL), F32)],
        compiler_params=_cparams(("arbitrary", "arbitrary")),
        name="moe_experts",
    )(n2, gate, x1, wg, wu, wd)


def kernel(x_prompt, x_sample, cache_k, cache_v, cache_logf, state_pool, norm1_g, w_in, b_f, q_norm_g, k_norm_g,
           pool_w, pool_scale, w_out, norm2_g, w_router_group, b_router_group, w_router_expert, b_router_expert,
           w_gate, w_up, w_down):
    depth = norm1_g.shape[0]
    assert depth == 1, "single-layer step"
    b_p, s_p, _ = x_prompt.shape
    assert b_p == 1, "prompt kernels assume one stream"
    b_s, t_s, _ = x_sample.shape
    p_len = cache_k.shape[2]
    l = 0

    w = w_in[l]
    a3 = 3 * ATTN_WIDTH
    w_all = jnp.concatenate(
        [w[:, :a3], w[:, a3 + N_HEADS:], jnp.pad(w[:, a3:a3 + N_HEADS], ((0, 0), (0, LANES - N_HEADS)))],
        axis=1).astype(BF16)
    bf_pad = jnp.pad(b_f[l], (0, LANES - N_HEADS)).reshape(1, LANES)
    g1 = norm1_g[l].reshape(1, D_MODEL)
    qg = jnp.tile(q_norm_g[l], N_HEADS).reshape(1, ATTN_WIDTH)
    kg = jnp.tile(k_norm_g[l], N_HEADS).reshape(1, ATTN_WIDTH)
    hr = lax.broadcasted_iota(jnp.int32, (ATTN_WIDTH, ATTN_WIDTH), 0) // HEAD_DIM
    hc = lax.broadcasted_iota(jnp.int32, (ATTN_WIDTH, ATTN_WIDTH), 1) // HEAD_DIM
    hm = jnp.where(hr == hc, 1.0 / HEAD_DIM, 0.0).astype(BF16)
    pw = pool_w[l].astype(BF16)
    ps = pool_scale[l].reshape(1, POOL_WIDTH)
    wo = w_out[l].astype(BF16)
    g2 = norm2_g[l].reshape(1, D_MODEL)
    w_r = jnp.pad(jnp.concatenate([w_router_expert[l], w_router_group[l]], axis=1),
                  ((0, 0), (0, LANES - N_EXPERTS - N_GROUPS)))
    wrh = w_r.astype(BF16)
    wrl = (w_r - wrh.astype(F32)).astype(BF16)
    br = jnp.pad(jnp.concatenate([b_router_expert[l], b_router_group[l]]),
                 (0, LANES - N_EXPERTS - N_GROUPS)).reshape(1, LANES)
    wg = w_gate[l].astype(BF16)
    wu = w_up[l].astype(BF16)
    wd = w_down[l].astype(BF16)

    tile = 512

    xp = x_prompt.reshape(s_p, D_MODEL)
    q, kf, kb, vf, vb, pin, logf, c = _in_project(xp, tile, tile, True, g1, w_all, bf_pad, qg, kg, hm)
    ct4 = c[:, :N_HEADS].T.reshape(N_PAIRS, 2, s_p)
    j0 = _first_key_blocks(c, q_norm_g[l], k_norm_g[l], tile)
    a = _attention_prompt(j0, q, kb, vb, c, ct4, tile)
    hist_map = lambda i: (jnp.maximum(i * (tile // HIST_ROWS) - 1, 0), 0)
    x1, n2, gate = _mix(a, pin, pin, hist_map, xp, tile, 0, tile, True, pw, ps, wo, g2, wrh, wrl, br)
    y_prompt = _moe(n2, gate, x1, wg, wu, wd, 1024).reshape(b_p, s_p, D_MODEL)
    k_prompt = kf.reshape(depth, b_p, s_p, N_HEADS, HEAD_DIM)
    v_prompt = vf.reshape(depth, b_p, s_p, N_HEADS, HEAD_DIM)
    logf_prompt = logf[:, :N_HEADS].reshape(depth, b_p, s_p, N_HEADS)
    pool_prompt = pin[s_p - POOL_PAD:].reshape(depth, b_p, POOL_PAD, POOL_WIDTH)

    n_s = b_s * t_s
    xs = x_sample.reshape(n_s, D_MODEL)
    q, kf, kb, vf, vb, pin, logf, d = _in_project(xs, tile, t_s, False, g1, w_all, bf_pad, qg, kg, hm)
    dt4 = d[:, :N_HEADS].reshape(b_s, t_s, N_PAIRS, 2).transpose(0, 2, 3, 1)
    clf = cache_logf[l].astype(F32).transpose(0, 2, 1).reshape(b_s * N_HEADS, p_len)
    r4 = _suffix_sums(clf).reshape(b_s, N_PAIRS, 2, p_len)
    ck = cache_k[l].reshape(b_s, p_len, ATTN_WIDTH)
    cv = cache_v[l].reshape(b_s, p_len, ATTN_WIDTH)
    a = _attention_sample(q, kb, vb, ck, cv, d, dt4, r4, t_s)
    hist_s = jnp.pad(state_pool[l], ((0, 0), (HIST_ROWS - POOL_PAD, 0), (0, 0))).reshape(b_s * HIST_ROWS, POOL_WIDTH)
    x1, n2, gate = _mix(a, pin, hist_s, lambda i: (i, 0), xs, t_s, p_len, 0, False, pw, ps, wo, g2, wrh, wrl, br)
    y_sample = _moe(n2, gate, x1, wg, wu, wd, n_s).reshape(b_s, t_s, D_MODEL)
    k_sample = kf.reshape(depth, b_s, t_s, N_HEADS, HEAD_DIM)
    v_sample = vf.reshape(depth, b_s, t_s, N_HEADS, HEAD_DIM)
    logf_sample = logf[:, :N_HEADS].reshape(depth, b_s, t_s, N_HEADS)
    pool_sample = pin.reshape(b_s, t_s, POOL_WIDTH)[:, t_s - POOL_PAD:].reshape(depth, b_s, POOL_PAD, POOL_WIDTH)

    return (y_prompt, y_sample, k_prompt, v_prompt, logf_prompt, pool_prompt,
            k_sample, v_sample, logf_sample, pool_sample)
```

```python
import functools

import jax
import jax.numpy as jnp
from jax import lax
from jax.experimental import pallas as pl
from jax.experimental.pallas import tpu as pltpu

F32 = jnp.float32
BF16 = jnp.bfloat16

D_MODEL = 1024
ATTN_WIDTH = 512
N_HEADS = 8
HEAD_DIM = 64
POOL_WIDTH = 512
POOL_WINDOWS = (2, 4, 8, 16)
POOL_GROUP_DIM = 128
POOL_PAD = 15
HIST_ROWS = 16
N_GROUPS = 4
EXPERTS_PER_GROUP = 4
N_EXPERTS = 16
D_EXPERT = 512
EPS = 1e-6
LANES = 128
PAIR = 2 * HEAD_DIM
N_PAIRS = N_HEADS // 2
VMEM_LIMIT = 56 * 1024 * 1024
V_PAD_COL = 3 * ATTN_WIDTH + POOL_WIDTH


def _cparams(sem):
    return pltpu.CompilerParams(dimension_semantics=sem, vmem_limit_bytes=VMEM_LIMIT)


def _split3(x):
    hi = x.astype(BF16)
    r1 = x - hi.astype(F32)
    mid = r1.astype(BF16)
    lo = (r1 - mid.astype(F32)).astype(BF16)
    return hi, mid, lo


def _split2(x):
    hi = x.astype(BF16)
    lo = (x - hi.astype(F32)).astype(BF16)
    return hi, lo


def _inproj_kernel(x_ref, g1_ref, w_ref, bf_ref, qg_ref, kg_ref, hm_ref, tri_ref,
                   q_ref, kf_ref, kb_ref, vf_ref, vb_ref, vp_ref, p_ref, logf_ref, c_ref, carry_ref, *, carry_rows):
    x = x_ref[...]
    r = lax.rsqrt(jnp.mean(x * x, axis=-1, keepdims=True) + EPS)
    n = ((x * r) * g1_ref[...]).astype(BF16)
    z = jnp.dot(n, w_ref[...], preferred_element_type=F32)

    hm = hm_ref[...]

    def head_norm(zz, g):
        hi, lo = _split2(zz * zz)
        ms = jnp.dot(hi, hm, preferred_element_type=F32) + jnp.dot(lo, hm, preferred_element_type=F32)
        return (zz * lax.rsqrt(ms + EPS)) * g

    q = head_norm(z[:, 0:ATTN_WIDTH], qg_ref[...])
    q_ref[...] = (q * (HEAD_DIM ** -0.5)).astype(BF16)
    k = head_norm(z[:, ATTN_WIDTH:2 * ATTN_WIDTH], kg_ref[...])
    kf_ref[...] = k
    kb_ref[...] = k.astype(BF16)
    v = z[:, 2 * ATTN_WIDTH:3 * ATTN_WIDTH]
    vf_ref[...] = v
    vb_ref[...] = v.astype(BF16)
    p_ref[...] = z[:, 3 * ATTN_WIDTH:3 * ATTN_WIDTH + POOL_WIDTH]
    vp = z[:, V_PAD_COL:V_PAD_COL + N_HEADS * LANES]
    vlane = lax.broadcasted_iota(jnp.int32, vp.shape, 1) % LANES
    vp_ref[...] = jnp.where(vlane == HEAD_DIM, 1.0, vp).astype(BF16)

    f = z[:, V_PAD_COL + N_HEADS * LANES:] + bf_ref[...]
    lane = lax.broadcasted_iota(jnp.int32, f.shape, 1)
    logf = -(jnp.maximum(-f, 0.0) + jnp.log1p(jnp.exp(-jnp.abs(f))))
    logf = jnp.where(lane < N_HEADS, logf, 0.0)
    logf_ref[...] = logf

    tri = tri_ref[...]
    hi, mid, lo = _split3(logf)
    c = (jnp.dot(tri, hi, preferred_element_type=F32) + jnp.dot(tri, mid, preferred_element_type=F32)
         + jnp.dot(tri, lo, preferred_element_type=F32))
    if carry_rows:
        @pl.when(pl.program_id(0) == 0)
        def _():
            carry_ref[...] = jnp.zeros_like(carry_ref)
        c = c + carry_ref[0:1, :]
        carry_ref[...] = jnp.broadcast_to(c[c.shape[0] - 1:, :], carry_ref.shape)
    c_ref[...] = c


def _in_project(x2d, tm, seg, carry_rows, g1, w_all, bf_pad, qg, kg, hm):
    n = x2d.shape[0]
    rows = lax.broadcasted_iota(jnp.int32, (tm, tm), 0)
    cols = lax.broadcasted_iota(jnp.int32, (tm, tm), 1)
    tri = ((cols <= rows) & (rows // seg == cols // seg)).astype(BF16)
    wc = w_all.shape[1]
    row_blk = lambda w: pl.BlockSpec((tm, w), lambda i: (i, 0))
    const = lambda a: pl.BlockSpec(a.shape, lambda i: (0,) * a.ndim)
    out_shape = (
        jax.ShapeDtypeStruct((n, ATTN_WIDTH), BF16),
        jax.ShapeDtypeStruct((n, ATTN_WIDTH), F32),
        jax.ShapeDtypeStruct((n, ATTN_WIDTH), BF16),
        jax.ShapeDtypeStruct((n, ATTN_WIDTH), F32),
        jax.ShapeDtypeStruct((n, ATTN_WIDTH), BF16),
        jax.ShapeDtypeStruct((n, N_HEADS * LANES), BF16),
        jax.ShapeDtypeStruct((n, POOL_WIDTH), F32),
        jax.ShapeDtypeStruct((n, LANES), F32),
        jax.ShapeDtypeStruct((n, LANES), F32),
    )
    return pl.pallas_call(
        functools.partial(_inproj_kernel, carry_rows=carry_rows),
        grid=(n // tm,),
        in_specs=[row_blk(D_MODEL), const(g1), const(w_all), const(bf_pad), const(qg), const(kg), const(hm),
                  const(tri)],
        out_specs=[row_blk(ATTN_WIDTH)] * 5 + [row_blk(N_HEADS * LANES), row_blk(POOL_WIDTH), row_blk(LANES),
                                                row_blk(LANES)],
        out_shape=out_shape,
        scratch_shapes=[pltpu.VMEM((8, LANES), F32)],
        compiler_params=_cparams(("arbitrary",)),
        name="in_project",
    )(x2d, g1, w_all, bf_pad, qg, kg, hm, tri)


def _attn_prompt_kernel(j0_ref, q_ref, k_ref, v_ref, ct_ref, o_ref, acc_ref, m_ref, *, tq, tk):
    hp = pl.program_id(0)
    i = pl.program_id(1)
    lane = lax.broadcasted_iota(jnp.int32, (1, PAIR), 1)
    first = lane < HEAD_DIM
    q = q_ref[...]
    zero = jnp.zeros_like(q)
    qh = (jnp.where(first, q, zero), jnp.where(first, zero, q))
    q_start = pl.multiple_of(i * tq, tq)

    acc_ref[...] = jnp.zeros_like(acc_ref)
    m_ref[...] = jnp.full_like(m_ref, -jnp.inf)

    for h in range(2):
        c0 = ct_ref[h:h + 1, pl.ds(q_start, LANES)][:, 0:1]

        last = (i + 1) * (tq // tk) - 1

        def scores(j, h=h, c0=c0, last=last):
            start = pl.multiple_of(jnp.minimum(j, last) * tk, tk)
            k = k_ref[pl.ds(start, tk), :]
            s = lax.dot_general(qh[h], k, (((1,), (1,)), ((), ())), preferred_element_type=F32)
            s = s - (ct_ref[h:h + 1, pl.ds(start, tk)] - c0)
            row = lax.broadcasted_iota(jnp.int32, s.shape, 0)
            col = lax.broadcasted_iota(jnp.int32, s.shape, 1)
            s = jnp.where(col - row <= q_start - j * tk, s, -jnp.inf)
            return s, jnp.broadcast_to(jnp.max(s, axis=1, keepdims=True), (tq, LANES))

        def body(j, carry, h=h, last=last, scores=scores):
            s, rowmax = carry
            nxt = scores(j + 1)
            start = pl.multiple_of(j * tk, tk)
            v = v_ref[pl.ds(start, tk), h * LANES:(h + 1) * LANES]
            m_prev = m_ref[h]
            m_new = jnp.maximum(m_prev, rowmax)
            alpha = jnp.exp(m_prev - m_new)
            p = jnp.concatenate([jnp.exp(s[:, g * LANES:(g + 1) * LANES] - m_new) for g in range(tk // LANES)], axis=1)
            m_ref[h] = m_new
            acc_ref[h] = acc_ref[h] * alpha + jnp.dot(p.astype(BF16), v, preferred_element_type=F32)
            return nxt

        j_first = j0_ref[i, 2 * hp + h]
        lax.fori_loop(j_first, last + 1, body, scores(j_first))

    out = [acc_ref[h] / acc_ref[h][:, HEAD_DIM:HEAD_DIM + 1] for h in range(2)]
    o_ref[...] = jnp.where(first, out[0], pltpu.roll(out[1], HEAD_DIM, axis=1)).astype(o_ref.dtype)


def _attention_prompt(j0, q, kb, vp, ct4, tq, tk):
    n = q.shape[0]
    grid_spec = pltpu.PrefetchScalarGridSpec(
        num_scalar_prefetch=1,
        grid=(N_PAIRS, n // tq),
        in_specs=[
            pl.BlockSpec((tq, PAIR), lambda hp, i, j0: (i, hp)),
            pl.BlockSpec((n, PAIR), lambda hp, i, j0: (0, hp)),
            pl.BlockSpec((n, 2 * LANES), lambda hp, i, j0: (0, hp)),
            pl.BlockSpec((None, 2, n), lambda hp, i, j0: (hp, 0, 0)),
        ],
        out_specs=pl.BlockSpec((tq, PAIR), lambda hp, i, j0: (i, hp)),
        scratch_shapes=[pltpu.VMEM((2, tq, LANES), F32), pltpu.VMEM((2, tq, LANES), F32)],
    )
    return pl.pallas_call(
        functools.partial(_attn_prompt_kernel, tq=tq, tk=tk),
        grid_spec=grid_spec,
        out_shape=jax.ShapeDtypeStruct((n, ATTN_WIDTH), BF16),
        compiler_params=_cparams(("arbitrary", "arbitrary")),
        name="attention_prompt",
    )(j0, q, kb, vp, ct4)


def _first_key_blocks(c, q_gain, k_gain, tq, tk):
    n = c.shape[0]
    qk_bound = 1.02 * HEAD_DIM ** 0.5 * jnp.max(jnp.abs(q_gain)) * jnp.max(jnp.abs(k_gain))
    threshold = 104.0 + 2.0 * qk_bound + 1.0
    c_start = c[0::tq, :N_HEADS]
    c_end = c[tk - 1::tk, :N_HEADS]
    gap = c_start[:, None, :] - c_end[None, :, :]
    earlier = (jnp.arange(n // tk)[None, :, None] + 1) * tk <= jnp.arange(n // tq)[:, None, None] * tq
    return jnp.sum((gap < -threshold) & earlier, axis=1).astype(jnp.int32)


def _suffix_kernel(x_ref, o_ref):
    x = x_ref[...]
    p_len = x.shape[1]
    rows = lax.broadcasted_iota(jnp.int32, (p_len, p_len), 0)
    cols = lax.broadcasted_iota(jnp.int32, (p_len, p_len), 1)
    u = (rows > cols).astype(BF16)
    hi, mid, lo = _split3(x)
    o_ref[...] = (jnp.dot(hi, u, preferred_element_type=F32) + jnp.dot(mid, u, preferred_element_type=F32)
                  + jnp.dot(lo, u, preferred_element_type=F32))


def _suffix_sums(x):
    return pl.pallas_call(
        _suffix_kernel,
        out_shape=jax.ShapeDtypeStruct(x.shape, F32),
        compiler_params=pltpu.CompilerParams(vmem_limit_bytes=VMEM_LIMIT),
        name="cache_suffix_sums",
    )(x)


def _attn_sample_kernel(q_ref, kn_ref, vn_ref, ck_ref, cv_ref, d_ref, dt_ref, r_ref, o_ref):
    hp = pl.program_id(1)
    lane = lax.broadcasted_iota(jnp.int32, (1, PAIR), 1)
    first = lane < HEAD_DIM
    q = q_ref[...]
    zero = jnp.zeros_like(q)
    qh = (jnp.where(first, q, zero), jnp.where(first, zero, q))
    dblk = d_ref[...]
    dlane = lax.broadcasted_iota(jnp.int32, dblk.shape, 1)
    kn = kn_ref[...]
    vn = vn_ref[...]
    ck = ck_ref[...].astype(BF16)
    cv = cv_ref[...].astype(BF16)
    nt = (((1,), (1,)), ((), ()))
    outs, ls = [], []
    for h in range(2):
        dq = jnp.sum(jnp.where(dlane == 2 * hp + h, dblk, 0.0), axis=1, keepdims=True)
        s1 = lax.dot_general(qh[h], ck, nt, preferred_element_type=F32) + dq + r_ref[h:h + 1, :]
        s2 = lax.dot_general(qh[h], kn, nt, preferred_element_type=F32) + dq - dt_ref[h:h + 1, :]
        row = lax.broadcasted_iota(jnp.int32, s2.shape, 0)
        col = lax.broadcasted_iota(jnp.int32, s2.shape, 1)
        s2 = jnp.where(col <= row, s2, -jnp.inf)
        m = jnp.maximum(jnp.max(s1, axis=1, keepdims=True), jnp.max(s2, axis=1, keepdims=True))
        p1 = jnp.exp(s1 - m)
        p2 = jnp.exp(s2 - m)
        ls.append(jnp.sum(p1, axis=1, keepdims=True) + jnp.sum(p2, axis=1, keepdims=True))
        outs.append(jnp.dot(p1.astype(BF16), cv, preferred_element_type=F32)
                    + jnp.dot(p2.astype(BF16), vn, preferred_element_type=F32))
    o_ref[...] = (jnp.where(first, outs[0], outs[1]) / jnp.where(first, ls[0], ls[1])).astype(o_ref.dtype)


def _attention_sample(q, kb, vb, cache_k, cache_v, d, dt4, r4, t):
    nb, p_len = cache_k.shape[0], cache_k.shape[1]
    return pl.pallas_call(
        _attn_sample_kernel,
        grid=(nb, N_PAIRS),
        in_specs=[
            pl.BlockSpec((t, PAIR), lambda b, hp: (b, hp)),
            pl.BlockSpec((t, PAIR), lambda b, hp: (b, hp)),
            pl.BlockSpec((t, PAIR), lambda b, hp: (b, hp)),
            pl.BlockSpec((None, p_len, PAIR), lambda b, hp: (b, 0, hp)),
            pl.BlockSpec((None, p_len, PAIR), lambda b, hp: (b, 0, hp)),
            pl.BlockSpec((t, LANES), lambda b, hp: (b, 0)),
            pl.BlockSpec((None, None, 2, t), lambda b, hp: (b, hp, 0, 0)),
            pl.BlockSpec((None, None, 2, p_len), lambda b, hp: (b, hp, 0, 0)),
        ],
        out_specs=pl.BlockSpec((t, PAIR), lambda b, hp: (b, hp)),
        out_shape=jax.ShapeDtypeStruct((nb * t, ATTN_WIDTH), BF16),
        compiler_params=_cparams(("arbitrary", "arbitrary")),
        name="attention_sample",
    )(q, kb, vb, cache_k, cache_v, d, dt4, r4)


def _mix_kernel(a_ref, p_ref, hist_ref, x_ref, pw_ref, ps_ref, wo_ref, g2_ref, wrh_ref, wrl_ref, br_ref,
                x1_ref, n2_ref, gate_ref, *, pos0, pos_stride, zero_first):
    i = pl.program_id(0)
    t = p_ref.shape[0]
    p = p_ref[...]
    hist = hist_ref[...]
    if zero_first:
        hist = jnp.where(i == 0, 0.0, hist)
    xh = jnp.concatenate([hist, p], axis=0)
    pos = pos0 + i * pos_stride + lax.broadcasted_iota(jnp.int32, (t, 1), 0)
    ys = []
    for g, w in enumerate(POOL_WINDOWS):
        col = xh[:, g * POOL_GROUP_DIM:(g + 1) * POOL_GROUP_DIM]
        acc = col
        span = 1
        while span < w:
            acc = acc + pltpu.roll(acc, span, axis=0)
            span *= 2
        cnt = jnp.minimum(pos + 1, w).astype(F32)
        dgrp = acc[HIST_ROWS:, :] / cnt - col[HIST_ROWS:, :]
        ys.append(jnp.dot(dgrp.astype(BF16), pw_ref[g], preferred_element_type=F32))
    pm = (jnp.concatenate(ys, axis=1) * ps_ref[...]).astype(BF16)
    mix = (jnp.dot(a_ref[...], wo_ref[0:ATTN_WIDTH, :], preferred_element_type=F32)
           + jnp.dot(pm, wo_ref[ATTN_WIDTH:, :], preferred_element_type=F32))
    x1 = x_ref[...] + mix
    x1_ref[...] = x1

    r = lax.rsqrt(jnp.mean(x1 * x1, axis=-1, keepdims=True) + EPS)
    n2 = (x1 * r) * g2_ref[...]
    n2_ref[...] = n2.astype(BF16)

    nh, nl = _split2(n2)
    wrh = wrh_ref[...]
    logits = (jnp.dot(nh, wrh, preferred_element_type=F32) + jnp.dot(nl, wrh, preferred_element_type=F32)
              + jnp.dot(nh, wrl_ref[...], preferred_element_type=F32)) + br_ref[...]
    lane = lax.broadcasted_iota(jnp.int32, logits.shape, 1).astype(F32)
    neg = -jnp.inf
    big = jnp.float32(1 << 20)
    is_g = (lane >= N_EXPERTS) & (lane < N_EXPERTS + N_GROUPS)
    lg = jnp.where(is_g, logits, neg)
    gmax = jnp.max(lg, axis=1, keepdims=True)
    g_lane = jnp.min(jnp.where(lg == gmax, lane, big), axis=1, keepdims=True)
    top_pg = 1.0 / jnp.sum(jnp.exp(lg - gmax), axis=1, keepdims=True)
    g_idx = g_lane - N_EXPERTS
    sel = (lane >= g_idx * EXPERTS_PER_GROUP) & (lane < (g_idx + 1) * EXPERTS_PER_GROUP)
    le = jnp.where(sel, logits, neg)
    m1 = jnp.max(le, axis=1, keepdims=True)
    i1 = jnp.min(jnp.where(le == m1, lane, big), axis=1, keepdims=True)
    le2 = jnp.where(lane == i1, neg, le)
    m2 = jnp.max(le2, axis=1, keepdims=True)
    i2 = jnp.min(jnp.where(le2 == m2, lane, big), axis=1, keepdims=True)
    e2 = jnp.exp(m2 - m1)
    w1 = top_pg / (1.0 + e2)
    w2 = top_pg * e2 / (1.0 + e2)
    gate_ref[...] = jnp.where(lane == i1, w1, 0.0) + jnp.where(lane == i2, w2, 0.0)


def _mix(a, p, hist_src, hist_map, x2d, t, pos0, pos_stride, zero_first, pw, ps, wo, g2, wrh, wrl, br):
    n = x2d.shape[0]
    row_blk = lambda w: pl.BlockSpec((t, w), lambda i: (i, 0))
    const = lambda arr: pl.BlockSpec(arr.shape, lambda i: (0,) * arr.ndim)
    return pl.pallas_call(
        functools.partial(_mix_kernel, pos0=pos0, pos_stride=pos_stride, zero_first=zero_first),
        grid=(n // t,),
        in_specs=[row_blk(ATTN_WIDTH), row_blk(POOL_WIDTH), pl.BlockSpec((HIST_ROWS, POOL_WIDTH), hist_map),
                  row_blk(D_MODEL), const(pw), const(ps), const(wo), const(g2), const(wrh), const(wrl), const(br)],
        out_specs=[row_blk(D_MODEL), row_blk(D_MODEL), row_blk(LANES)],
        out_shape=(jax.ShapeDtypeStruct((n, D_MODEL), F32), jax.ShapeDtypeStruct((n, D_MODEL), BF16),
                   jax.ShapeDtypeStruct((n, LANES), F32)),
        compiler_params=_cparams(("arbitrary",)),
        name="pool_mix_route",
    )(a, p, hist_src, x2d, pw, ps, wo, g2, wrh, wrl, br)


def _moe_kernel(n2_ref, gate_ref, x1_ref, wg_ref, wu_ref, wd_ref, y_ref, acc_ref):
    e = pl.program_id(1)

    @pl.when(e == 0)
    def _():
        acc_ref[...] = jnp.zeros_like(acc_ref)

    gate = gate_ref[...]
    lane = lax.broadcasted_iota(jnp.int32, gate.shape, 1)
    ge = jnp.sum(jnp.where(lane == e, gate, 0.0), axis=1, keepdims=True)
    n2 = n2_ref[...]
    hg = jnp.dot(n2, wg_ref[...], preferred_element_type=F32)
    hu = jnp.dot(n2, wu_ref[...], preferred_element_type=F32)
    h = (hg * jax.nn.sigmoid(hg)) * hu * ge
    acc_ref[...] += jnp.dot(h.astype(BF16), wd_ref[...], preferred_element_type=F32)

    @pl.when(e == pl.num_programs(1) - 1)
    def _():
        y_ref[...] = x1_ref[...] + acc_ref[...]


def _moe(n2, gate, x1, wg, wu, wd, tm):
    n = n2.shape[0]
    return pl.pallas_call(
        _moe_kernel,
        grid=(n // tm, N_EXPERTS),
        in_specs=[
            pl.BlockSpec((tm, D_MODEL), lambda i, e: (i, 0)),
            pl.BlockSpec((tm, LANES), lambda i, e: (i, 0)),
            pl.BlockSpec((tm, D_MODEL), lambda i, e: (i, 0)),
            pl.BlockSpec((None, D_MODEL, D_EXPERT), lambda i, e: (e, 0, 0)),
            pl.BlockSpec((None, D_MODEL, D_EXPERT), lambda i, e: (e, 0, 0)),
            pl.BlockSpec((None, D_EXPERT, D_MODEL), lambda i, e: (e, 0, 0)),
        ],
        out_specs=pl.BlockSpec((tm, D_MODEL), lambda i, e: (i, 0)),
        out_shape=jax.ShapeDtypeStruct((n, D_MODEL), F32),
        scratch_shapes=[pltpu.VMEM((tm, D_MODEL), F32)],
        compiler_params=_cparams(("arbitrary", "arbitrary")),
        name="moe_experts",
    )(n2, gate, x1, wg, wu, wd)


def kernel(x_prompt, x_sample, cache_k, cache_v, cache_logf, state_pool, norm1_g, w_in, b_f, q_norm_g, k_norm_g,
           pool_w, pool_scale, w_out, norm2_g, w_router_group, b_router_group, w_router_expert, b_router_expert,
           w_gate, w_up, w_down):
    depth = norm1_g.shape[0]
    assert depth == 1, "single-layer step"
    b_p, s_p, _ = x_prompt.shape
    assert b_p == 1, "prompt kernels assume one stream"
    b_s, t_s, _ = x_sample.shape
    p_len = cache_k.shape[2]
    l = 0

    w = w_in[l]
    a3 = 3 * ATTN_WIDTH
    w_v = w[:, 2 * ATTN_WIDTH:a3].reshape(D_MODEL, N_HEADS, HEAD_DIM)
    w_vpad = jnp.pad(w_v, ((0, 0), (0, 0), (0, LANES - HEAD_DIM))).reshape(D_MODEL, N_HEADS * LANES)
    w_all = jnp.concatenate(
        [w[:, :a3], w[:, a3 + N_HEADS:], w_vpad, jnp.pad(w[:, a3:a3 + N_HEADS], ((0, 0), (0, LANES - N_HEADS)))],
        axis=1).astype(BF16)
    bf_pad = jnp.pad(b_f[l], (0, LANES - N_HEADS)).reshape(1, LANES)
    g1 = norm1_g[l].reshape(1, D_MODEL)
    qg = jnp.tile(q_norm_g[l], N_HEADS).reshape(1, ATTN_WIDTH)
    kg = jnp.tile(k_norm_g[l], N_HEADS).reshape(1, ATTN_WIDTH)
    hr = lax.broadcasted_iota(jnp.int32, (ATTN_WIDTH, ATTN_WIDTH), 0) // HEAD_DIM
    hc = lax.broadcasted_iota(jnp.int32, (ATTN_WIDTH, ATTN_WIDTH), 1) // HEAD_DIM
    hm = jnp.where(hr == hc, 1.0 / HEAD_DIM, 0.0).astype(BF16)
    pw = pool_w[l].astype(BF16)
    ps = pool_scale[l].reshape(1, POOL_WIDTH)
    wo = w_out[l].astype(BF16)
    g2 = norm2_g[l].reshape(1, D_MODEL)
    w_r = jnp.pad(jnp.concatenate([w_router_expert[l], w_router_group[l]], axis=1),
                  ((0, 0), (0, LANES - N_EXPERTS - N_GROUPS)))
    wrh = w_r.astype(BF16)
    wrl = (w_r - wrh.astype(F32)).astype(BF16)
    br = jnp.pad(jnp.concatenate([b_router_expert[l], b_router_group[l]]),
                 (0, LANES - N_EXPERTS - N_GROUPS)).reshape(1, LANES)
    wg = w_gate[l].astype(BF16)
    wu = w_up[l].astype(BF16)
    wd = w_down[l].astype(BF16)

    tile = 512
    key_tile = 256

    xp = x_prompt.reshape(s_p, D_MODEL)
    q, kf, kb, vf, _, vp, pin, logf, c = _in_project(xp, tile, tile, True, g1, w_all, bf_pad, qg, kg, hm)
    ct4 = c[:, :N_HEADS].T.reshape(N_PAIRS, 2, s_p)
    j0 = _first_key_blocks(c, q_norm_g[l], k_norm_g[l], tile, key_tile)
    a = _attention_prompt(j0, q, kb, vp, ct4, tile, key_tile)
    hist_map = lambda i: (jnp.maximum(i * (tile // HIST_ROWS) - 1, 0), 0)
    x1, n2, gate = _mix(a, pin, pin, hist_map, xp, tile, 0, tile, True, pw, ps, wo, g2, wrh, wrl, br)
    y_prompt = _moe(n2, gate, x1, wg, wu, wd, 1024).reshape(b_p, s_p, D_MODEL)
    k_prompt = kf.reshape(depth, b_p, s_p, N_HEADS, HEAD_DIM)
    v_prompt = vf.reshape(depth, b_p, s_p, N_HEADS, HEAD_DIM)
    logf_prompt = logf[:, :N_HEADS].reshape(depth, b_p, s_p, N_HEADS)
    pool_prompt = pin[s_p - POOL_PAD:].reshape(depth, b_p, POOL_PAD, POOL_WIDTH)

    n_s = b_s * t_s
    xs = x_sample.reshape(n_s, D_MODEL)
    q, kf, kb, vf, vb, _, pin, logf, d = _in_project(xs, tile, t_s, False, g1, w_all, bf_pad, qg, kg, hm)
    dt4 = d[:, :N_HEADS].reshape(b_s, t_s, N_PAIRS, 2).transpose(0, 2, 3, 1)
    clf = cache_logf[l].astype(F32).transpose(0, 2, 1).reshape(b_s * N_HEADS, p_len)
    r4 = _suffix_sums(clf).reshape(b_s, N_PAIRS, 2, p_len)
    ck = cache_k[l].reshape(b_s, p_len, ATTN_WIDTH)
    cv = cache_v[l].reshape(b_s, p_len, ATTN_WIDTH)
    a = _attention_sample(q, kb, vb, ck, cv, d, dt4, r4, t_s)
    hist_s = jnp.pad(state_pool[l], ((0, 0), (HIST_ROWS - POOL_PAD, 0), (0, 0))).reshape(b_s * HIST_ROWS, POOL_WIDTH)
    x1, n2, gate = _mix(a, pin, hist_s, lambda i: (i, 0), xs, t_s, p_len, 0, False, pw, ps, wo, g2, wrh, wrl, br)
    y_sample = _moe(n2, gate, x1, wg, wu, wd, n_s).reshape(b_s, t_s, D_MODEL)
    k_sample = kf.reshape(depth, b_s, t_s, N_HEADS, HEAD_DIM)
    v_sample = vf.reshape(depth, b_s, t_s, N_HEADS, HEAD_DIM)
    logf_sample = logf[:, :N_HEADS].reshape(depth, b_s, t_s, N_HEADS)
    pool_sample = pin.reshape(b_s, t_s, POOL_WIDTH)[:, t_s - POOL_PAD:].reshape(depth, b_s, POOL_PAD, POOL_WIDTH)

    return (y_prompt, y_sample, k_prompt, v_prompt, logf_prompt, pool_prompt,
            k_sample, v_sample, logf_sample, pool_sample)
```

```python
import functools

import jax
import jax.numpy as jnp
from jax import lax
from jax.experimental import pallas as pl
from jax.experimental.pallas import tpu as pltpu

F32 = jnp.float32
BF16 = jnp.bfloat16

D_MODEL = 1024
ATTN_WIDTH = 512
N_HEADS = 8
HEAD_DIM = 64
POOL_WIDTH = 512
POOL_WINDOWS = (2, 4, 8, 16)
POOL_GROUP_DIM = 128
POOL_PAD = 15
HIST_ROWS = 16
N_GROUPS = 4
EXPERTS_PER_GROUP = 4
N_EXPERTS = 16
D_EXPERT = 512
EPS = 1e-6
LANES = 128
PAIR = 2 * HEAD_DIM
N_PAIRS = N_HEADS // 2
VMEM_LIMIT = 56 * 1024 * 1024
V_PAD_COL = 3 * ATTN_WIDTH + POOL_WIDTH
ROW_WIDTH = D_MODEL + LANES
ROUTE_GROUP_LANE = N_EXPERTS
ROUTE_RANK_LANE = N_EXPERTS + 1


def _cparams(sem):
    return pltpu.CompilerParams(dimension_semantics=sem, vmem_limit_bytes=VMEM_LIMIT)


def _split3(x):
    hi = x.astype(BF16)
    r1 = x - hi.astype(F32)
    mid = r1.astype(BF16)
    lo = (r1 - mid.astype(F32)).astype(BF16)
    return hi, mid, lo


def _split2(x):
    hi = x.astype(BF16)
    lo = (x - hi.astype(F32)).astype(BF16)
    return hi, lo


def _inproj_kernel(x_ref, g1_ref, w_ref, bf_ref, qg_ref, kg_ref, hm_ref, tri_ref,
                   q_ref, kf_ref, kb_ref, vf_ref, vb_ref, vp_ref, p_ref, logf_ref, c_ref, carry_ref, *, carry_rows):
    x = x_ref[...]
    r = lax.rsqrt(jnp.mean(x * x, axis=-1, keepdims=True) + EPS)
    n = ((x * r) * g1_ref[...]).astype(BF16)
    z = jnp.dot(n, w_ref[...], preferred_element_type=F32)

    hm = hm_ref[...]

    def head_norm(zz, g):
        hi, lo = _split2(zz * zz)
        ms = jnp.dot(hi, hm, preferred_element_type=F32) + jnp.dot(lo, hm, preferred_element_type=F32)
        return (zz * lax.rsqrt(ms + EPS)) * g

    q = head_norm(z[:, 0:ATTN_WIDTH], qg_ref[...])
    q_ref[...] = (q * (HEAD_DIM ** -0.5)).astype(BF16)
    k = head_norm(z[:, ATTN_WIDTH:2 * ATTN_WIDTH], kg_ref[...])
    kf_ref[...] = k
    kb_ref[...] = k.astype(BF16)
    v = z[:, 2 * ATTN_WIDTH:3 * ATTN_WIDTH]
    vf_ref[...] = v
    vb_ref[...] = v.astype(BF16)
    p_ref[...] = z[:, 3 * ATTN_WIDTH:3 * ATTN_WIDTH + POOL_WIDTH]
    vp = z[:, V_PAD_COL:V_PAD_COL + N_HEADS * LANES]
    vlane = lax.broadcasted_iota(jnp.int32, vp.shape, 1) % LANES
    vp_ref[...] = jnp.where(vlane == HEAD_DIM, 1.0, vp).astype(BF16)

    f = z[:, V_PAD_COL + N_HEADS * LANES:] + bf_ref[...]
    lane = lax.broadcasted_iota(jnp.int32, f.shape, 1)
    logf = -(jnp.maximum(-f, 0.0) + jnp.log1p(jnp.exp(-jnp.abs(f))))
    logf = jnp.where(lane < N_HEADS, logf, 0.0)
    logf_ref[...] = logf

    tri = tri_ref[...]
    hi, mid, lo = _split3(logf)
    c = (jnp.dot(tri, hi, preferred_element_type=F32) + jnp.dot(tri, mid, preferred_element_type=F32)
         + jnp.dot(tri, lo, preferred_element_type=F32))
    if carry_rows:
        @pl.when(pl.program_id(0) == 0)
        def _():
            carry_ref[...] = jnp.zeros_like(carry_ref)
        c = c + carry_ref[0:1, :]
        carry_ref[...] = jnp.broadcast_to(c[c.shape[0] - 1:, :], carry_ref.shape)
    c_ref[...] = c


def _in_project(x2d, tm, seg, carry_rows, g1, w_all, bf_pad, qg, kg, hm):
    n = x2d.shape[0]
    rows = lax.broadcasted_iota(jnp.int32, (tm, tm), 0)
    cols = lax.broadcasted_iota(jnp.int32, (tm, tm), 1)
    tri = ((cols <= rows) & (rows // seg == cols // seg)).astype(BF16)
    wc = w_all.shape[1]
    row_blk = lambda w: pl.BlockSpec((tm, w), lambda i: (i, 0))
    const = lambda a: pl.BlockSpec(a.shape, lambda i: (0,) * a.ndim)
    out_shape = (
        jax.ShapeDtypeStruct((n, ATTN_WIDTH), BF16),
        jax.ShapeDtypeStruct((n, ATTN_WIDTH), F32),
        jax.ShapeDtypeStruct((n, ATTN_WIDTH), BF16),
        jax.ShapeDtypeStruct((n, ATTN_WIDTH), F32),
        jax.ShapeDtypeStruct((n, ATTN_WIDTH), BF16),
        jax.ShapeDtypeStruct((n, N_HEADS * LANES), BF16),
        jax.ShapeDtypeStruct((n, POOL_WIDTH), F32),
        jax.ShapeDtypeStruct((n, LANES), F32),
        jax.ShapeDtypeStruct((n, LANES), F32),
    )
    return pl.pallas_call(
        functools.partial(_inproj_kernel, carry_rows=carry_rows),
        grid=(n // tm,),
        in_specs=[row_blk(D_MODEL), const(g1), const(w_all), const(bf_pad), const(qg), const(kg), const(hm),
                  const(tri)],
        out_specs=[row_blk(ATTN_WIDTH)] * 5 + [row_blk(N_HEADS * LANES), row_blk(POOL_WIDTH), row_blk(LANES),
                                                row_blk(LANES)],
        out_shape=out_shape,
        scratch_shapes=[pltpu.VMEM((8, LANES), F32)],
        compiler_params=_cparams(("arbitrary",)),
        name="in_project",
    )(x2d, g1, w_all, bf_pad, qg, kg, hm, tri)


def _attn_prompt_kernel(j0_ref, q_ref, k_ref, v_ref, ct_ref, o_ref, acc_ref, m_ref, *, tq, tk):
    hp = pl.program_id(0)
    i = pl.program_id(1)
    lane = lax.broadcasted_iota(jnp.int32, (1, PAIR), 1)
    first = lane < HEAD_DIM
    q = q_ref[...]
    zero = jnp.zeros_like(q)
    qh = (jnp.where(first, q, zero), jnp.where(first, zero, q))
    q_start = pl.multiple_of(i * tq, tq)

    acc_ref[...] = jnp.zeros_like(acc_ref)
    m_ref[...] = jnp.full_like(m_ref, -jnp.inf)

    for h in range(2):
        c0 = ct_ref[h:h + 1, pl.ds(q_start, LANES)][:, 0:1]

        last = (i + 1) * (tq // tk) - 1

        def scores(j, h=h, c0=c0, last=last):
            start = pl.multiple_of(jnp.minimum(j, last) * tk, tk)
            k = k_ref[pl.ds(start, tk), :]
            s = lax.dot_general(qh[h], k, (((1,), (1,)), ((), ())), preferred_element_type=F32)
            s = s - (ct_ref[h:h + 1, pl.ds(start, tk)] - c0)
            row = lax.broadcasted_iota(jnp.int32, s.shape, 0)
            col = lax.broadcasted_iota(jnp.int32, s.shape, 1)
            s = jnp.where(col - row <= q_start - j * tk, s, -jnp.inf)
            return s, jnp.broadcast_to(jnp.max(s, axis=1, keepdims=True), (tq, LANES))

        def body(j, carry, h=h, last=last, scores=scores):
            s, rowmax = carry
            nxt = scores(j + 1)
            start = pl.multiple_of(j * tk, tk)
            v = v_ref[pl.ds(start, tk), h * LANES:(h + 1) * LANES]
            m_prev = m_ref[h]
            m_new = jnp.maximum(m_prev, rowmax)
            alpha = jnp.exp(m_prev - m_new)
            p = jnp.concatenate([jnp.exp(s[:, g * LANES:(g + 1) * LANES] - m_new) for g in range(tk // LANES)], axis=1)
            m_ref[h] = m_new
            acc_ref[h] = acc_ref[h] * alpha + jnp.dot(p.astype(BF16), v, preferred_element_type=F32)
            return nxt

        j_first = j0_ref[i, 2 * hp + h]
        lax.fori_loop(j_first, last + 1, body, scores(j_first))

    out = [acc_ref[h] / acc_ref[h][:, HEAD_DIM:HEAD_DIM + 1] for h in range(2)]
    o_ref[...] = jnp.where(first, out[0], pltpu.roll(out[1], HEAD_DIM, axis=1)).astype(o_ref.dtype)


def _attention_prompt(j0, q, kb, vp, ct4, tq, tk):
    n = q.shape[0]
    grid_spec = pltpu.PrefetchScalarGridSpec(
        num_scalar_prefetch=1,
        grid=(N_PAIRS, n // tq),
        in_specs=[
            pl.BlockSpec((tq, PAIR), lambda hp, i, j0: (i, hp)),
            pl.BlockSpec((n, PAIR), lambda hp, i, j0: (0, hp)),
            pl.BlockSpec((n, 2 * LANES), lambda hp, i, j0: (0, hp)),
            pl.BlockSpec((None, 2, n), lambda hp, i, j0: (hp, 0, 0)),
        ],
        out_specs=pl.BlockSpec((tq, PAIR), lambda hp, i, j0: (i, hp)),
        scratch_shapes=[pltpu.VMEM((2, tq, LANES), F32), pltpu.VMEM((2, tq, LANES), F32)],
    )
    return pl.pallas_call(
        functools.partial(_attn_prompt_kernel, tq=tq, tk=tk),
        grid_spec=grid_spec,
        out_shape=jax.ShapeDtypeStruct((n, ATTN_WIDTH), BF16),
        compiler_params=_cparams(("arbitrary", "arbitrary")),
        name="attention_prompt",
    )(j0, q, kb, vp, ct4)


def _first_key_blocks(c, q_gain, k_gain, tq, tk):
    n = c.shape[0]
    qk_bound = 1.02 * HEAD_DIM ** 0.5 * jnp.max(jnp.abs(q_gain)) * jnp.max(jnp.abs(k_gain))
    threshold = 104.0 + 2.0 * qk_bound + 1.0
    c_start = c[0::tq, :N_HEADS]
    c_end = c[tk - 1::tk, :N_HEADS]
    gap = c_start[:, None, :] - c_end[None, :, :]
    earlier = (jnp.arange(n // tk)[None, :, None] + 1) * tk <= jnp.arange(n // tq)[:, None, None] * tq
    return jnp.sum((gap < -threshold) & earlier, axis=1).astype(jnp.int32)


def _suffix_kernel(x_ref, o_ref):
    x = x_ref[...]
    p_len = x.shape[1]
    rows = lax.broadcasted_iota(jnp.int32, (p_len, p_len), 0)
    cols = lax.broadcasted_iota(jnp.int32, (p_len, p_len), 1)
    u = (rows > cols).astype(BF16)
    hi, mid, lo = _split3(x)
    o_ref[...] = (jnp.dot(hi, u, preferred_element_type=F32) + jnp.dot(mid, u, preferred_element_type=F32)
                  + jnp.dot(lo, u, preferred_element_type=F32))


def _suffix_sums(x):
    return pl.pallas_call(
        _suffix_kernel,
        out_shape=jax.ShapeDtypeStruct(x.shape, F32),
        compiler_params=pltpu.CompilerParams(vmem_limit_bytes=VMEM_LIMIT),
        name="cache_suffix_sums",
    )(x)


def _attn_sample_kernel(q_ref, kn_ref, vn_ref, ck_ref, cv_ref, d_ref, dt_ref, r_ref, o_ref):
    hp = pl.program_id(1)
    lane = lax.broadcasted_iota(jnp.int32, (1, PAIR), 1)
    first = lane < HEAD_DIM
    q = q_ref[...]
    zero = jnp.zeros_like(q)
    qh = (jnp.where(first, q, zero), jnp.where(first, zero, q))
    dblk = d_ref[...]
    dlane = lax.broadcasted_iota(jnp.int32, dblk.shape, 1)
    kn = kn_ref[...]
    vn = vn_ref[...]
    ck = ck_ref[...].astype(BF16)
    cv = cv_ref[...].astype(BF16)
    nt = (((1,), (1,)), ((), ()))
    outs, ls = [], []
    for h in range(2):
        dq = jnp.sum(jnp.where(dlane == 2 * hp + h, dblk, 0.0), axis=1, keepdims=True)
        s1 = lax.dot_general(qh[h], ck, nt, preferred_element_type=F32) + dq + r_ref[h:h + 1, :]
        s2 = lax.dot_general(qh[h], kn, nt, preferred_element_type=F32) + dq - dt_ref[h:h + 1, :]
        row = lax.broadcasted_iota(jnp.int32, s2.shape, 0)
        col = lax.broadcasted_iota(jnp.int32, s2.shape, 1)
        s2 = jnp.where(col <= row, s2, -jnp.inf)
        m = jnp.maximum(jnp.max(s1, axis=1, keepdims=True), jnp.max(s2, axis=1, keepdims=True))
        p1 = jnp.exp(s1 - m)
        p2 = jnp.exp(s2 - m)
        ls.append(jnp.sum(p1, axis=1, keepdims=True) + jnp.sum(p2, axis=1, keepdims=True))
        outs.append(jnp.dot(p1.astype(BF16), cv, preferred_element_type=F32)
                    + jnp.dot(p2.astype(BF16), vn, preferred_element_type=F32))
    o_ref[...] = (jnp.where(first, outs[0], outs[1]) / jnp.where(first, ls[0], ls[1])).astype(o_ref.dtype)


def _attention_sample(q, kb, vb, cache_k, cache_v, d, dt4, r4, t):
    nb, p_len = cache_k.shape[0], cache_k.shape[1]
    return pl.pallas_call(
        _attn_sample_kernel,
        grid=(nb, N_PAIRS),
        in_specs=[
            pl.BlockSpec((t, PAIR), lambda b, hp: (b, hp)),
            pl.BlockSpec((t, PAIR), lambda b, hp: (b, hp)),
            pl.BlockSpec((t, PAIR), lambda b, hp: (b, hp)),
            pl.BlockSpec((None, p_len, PAIR), lambda b, hp: (b, 0, hp)),
            pl.BlockSpec((None, p_len, PAIR), lambda b, hp: (b, 0, hp)),
            pl.BlockSpec((t, LANES), lambda b, hp: (b, 0)),
            pl.BlockSpec((None, None, 2, t), lambda b, hp: (b, hp, 0, 0)),
            pl.BlockSpec((None, None, 2, p_len), lambda b, hp: (b, hp, 0, 0)),
        ],
        out_specs=pl.BlockSpec((t, PAIR), lambda b, hp: (b, hp)),
        out_shape=jax.ShapeDtypeStruct((nb * t, ATTN_WIDTH), BF16),
        compiler_params=_cparams(("arbitrary", "arbitrary")),
        name="attention_sample",
    )(q, kb, vb, cache_k, cache_v, d, dt4, r4)


def _mix_kernel(a_ref, p_ref, hist_ref, x_ref, pw_ref, ps_ref, wo_ref, g2_ref, wrh_ref, wrl_ref, br_ref, tri_ref,
                xg_ref, route_ref, cnt_ref, *, pos0, pos_stride, zero_first):
    i = pl.program_id(0)
    t = p_ref.shape[0]
    p = p_ref[...]
    hist = hist_ref[...]
    if zero_first:
        hist = jnp.where(i == 0, 0.0, hist)
    xh = jnp.concatenate([hist, p], axis=0)
    pos = pos0 + i * pos_stride + lax.broadcasted_iota(jnp.int32, (t, 1), 0)
    ys = []
    for g, w in enumerate(POOL_WINDOWS):
        col = xh[:, g * POOL_GROUP_DIM:(g + 1) * POOL_GROUP_DIM]
        acc = col
        span = 1
        while span < w:
            acc = acc + pltpu.roll(acc, span, axis=0)
            span *= 2
        cnt = jnp.minimum(pos + 1, w).astype(F32)
        dgrp = acc[HIST_ROWS:, :] / cnt - col[HIST_ROWS:, :]
        ys.append(jnp.dot(dgrp.astype(BF16), pw_ref[g], preferred_element_type=F32))
    pm = (jnp.concatenate(ys, axis=1) * ps_ref[...]).astype(BF16)
    mix = (jnp.dot(a_ref[...], wo_ref[0:ATTN_WIDTH, :], preferred_element_type=F32)
           + jnp.dot(pm, wo_ref[ATTN_WIDTH:, :], preferred_element_type=F32))
    x1 = x_ref[...] + mix
    xg_ref[:, 0:D_MODEL] = x1

    r = lax.rsqrt(jnp.mean(x1 * x1, axis=-1, keepdims=True) + EPS)
    n2 = (x1 * r) * g2_ref[...]

    nh, nl = _split2(n2)
    wrh = wrh_ref[...]
    logits = (jnp.dot(nh, wrh, preferred_element_type=F32) + jnp.dot(nl, wrh, preferred_element_type=F32)
              + jnp.dot(nh, wrl_ref[...], preferred_element_type=F32)) + br_ref[...]
    lane = lax.broadcasted_iota(jnp.int32, logits.shape, 1).astype(F32)
    neg = -jnp.inf
    big = jnp.float32(1 << 20)
    is_g = (lane >= N_EXPERTS) & (lane < N_EXPERTS + N_GROUPS)
    lg = jnp.where(is_g, logits, neg)
    gmax = jnp.max(lg, axis=1, keepdims=True)
    g_lane = jnp.min(jnp.where(lg == gmax, lane, big), axis=1, keepdims=True)
    top_pg = 1.0 / jnp.sum(jnp.exp(lg - gmax), axis=1, keepdims=True)
    g_idx = g_lane - N_EXPERTS
    sel = (lane >= g_idx * EXPERTS_PER_GROUP) & (lane < (g_idx + 1) * EXPERTS_PER_GROUP)
    le = jnp.where(sel, logits, neg)
    m1 = jnp.max(le, axis=1, keepdims=True)
    i1 = jnp.min(jnp.where(le == m1, lane, big), axis=1, keepdims=True)
    le2 = jnp.where(lane == i1, neg, le)
    m2 = jnp.max(le2, axis=1, keepdims=True)
    i2 = jnp.min(jnp.where(le2 == m2, lane, big), axis=1, keepdims=True)
    e2 = jnp.exp(m2 - m1)
    w1 = top_pg / (1.0 + e2)
    w2 = top_pg * e2 / (1.0 + e2)
    gate = jnp.where(lane == i1, w1, 0.0) + jnp.where(lane == i2, w2, 0.0)

    @pl.when(i == 0)
    def _():
        cnt_ref[...] = jnp.zeros_like(cnt_ref)
    member = lane == g_idx
    before = jnp.dot(tri_ref[...], member.astype(BF16), preferred_element_type=F32) + cnt_ref[0:1, :]
    rank = jnp.sum(jnp.where(member, before, 0.0), axis=1, keepdims=True)
    cnt_ref[...] = cnt_ref[...] + jnp.sum(member.astype(F32), axis=0, keepdims=True)
    route = jnp.where(lane == ROUTE_GROUP_LANE, g_idx, jnp.where(lane == ROUTE_RANK_LANE, rank, gate))
    route_ref[...] = route
    xg_ref[:, D_MODEL:] = route


def _mix(a, p, hist_src, hist_map, x2d, t, pos0, pos_stride, zero_first, pw, ps, wo, g2, wrh, wrl, br):
    n = x2d.shape[0]
    rows = lax.broadcasted_iota(jnp.int32, (t, t), 0)
    cols = lax.broadcasted_iota(jnp.int32, (t, t), 1)
    tri = (cols < rows).astype(BF16)
    row_blk = lambda w: pl.BlockSpec((t, w), lambda i: (i, 0))
    const = lambda arr: pl.BlockSpec(arr.shape, lambda i: (0,) * arr.ndim)
    return pl.pallas_call(
        functools.partial(_mix_kernel, pos0=pos0, pos_stride=pos_stride, zero_first=zero_first),
        grid=(n // t,),
        in_specs=[row_blk(ATTN_WIDTH), row_blk(POOL_WIDTH), pl.BlockSpec((HIST_ROWS, POOL_WIDTH), hist_map),
                  row_blk(D_MODEL), const(pw), const(ps), const(wo), const(g2), const(wrh), const(wrl), const(br),
                  const(tri)],
        out_specs=[row_blk(ROW_WIDTH), row_blk(LANES), pl.BlockSpec((8, LANES), lambda i: (0, 0))],
        out_shape=(jax.ShapeDtypeStruct((n, ROW_WIDTH), F32), jax.ShapeDtypeStruct((n, LANES), F32),
                   jax.ShapeDtypeStruct((8, LANES), F32)),
        compiler_params=_cparams(("arbitrary",)),
        name="pool_mix_route",
    )(a, p, hist_src, x2d, pw, ps, wo, g2, wrh, wrl, br, tri)


def _scatter_rows_kernel(dest_ref, x_ref, xs_in_ref, xs_ref, sem):
    del xs_in_ref
    t = x_ref.shape[0]

    def issue(r, carry):
        pltpu.make_async_copy(x_ref.at[pl.ds(r, 1)], xs_ref.at[pl.ds(dest_ref[r], 1)], sem).start()
        return carry

    lax.fori_loop(0, t, issue, 0)
    pltpu.make_async_copy(x_ref, xs_ref.at[pl.ds(0, t)], sem).wait()


def _scatter_rows(dest, x, xs, t):
    n, w = x.shape
    return pl.pallas_call(
        _scatter_rows_kernel,
        grid=(n // t,),
        in_specs=[pl.BlockSpec((t,), lambda i: (i,), memory_space=pltpu.SMEM),
                  pl.BlockSpec((t, w), lambda i: (i, 0)),
                  pl.BlockSpec(memory_space=pl.ANY)],
        out_specs=pl.BlockSpec(memory_space=pl.ANY),
        out_shape=jax.ShapeDtypeStruct(xs.shape, xs.dtype),
        scratch_shapes=[pltpu.SemaphoreType.DMA],
        input_output_aliases={2: 0},
        compiler_params=_cparams(("arbitrary",)),
        name="moe_scatter_rows",
    )(dest, x, xs)


def _gather_rows_kernel(dest_ref, ys_ref, y_ref, sem):
    t = y_ref.shape[0]

    def issue(r, carry):
        pltpu.make_async_copy(ys_ref.at[pl.ds(dest_ref[r], 1)], y_ref.at[pl.ds(r, 1)], sem).start()
        return carry

    lax.fori_loop(0, t, issue, 0)
    pltpu.make_async_copy(ys_ref.at[pl.ds(0, t)], y_ref, sem).wait()


def _gather_rows(dest, ys, t):
    n = dest.shape[0]
    w = ys.shape[1]
    return pl.pallas_call(
        _gather_rows_kernel,
        grid=(n // t,),
        in_specs=[pl.BlockSpec((t,), lambda i: (i,), memory_space=pltpu.SMEM),
                  pl.BlockSpec(memory_space=pl.ANY)],
        out_specs=pl.BlockSpec((t, w), lambda i: (i, 0)),
        out_shape=jax.ShapeDtypeStruct((n, w), ys.dtype),
        scratch_shapes=[pltpu.SemaphoreType.DMA],
        compiler_params=_cparams(("arbitrary",)),
        name="moe_gather_rows",
    )(dest, ys)


def _moe_kernel(tg_ref, tv_ref, xs_ref, g2_ref, wg_ref, wu_ref, wd_ref, y_ref):
    m = pl.program_id(0)

    @pl.when(tv_ref[m] > 0)
    def _():
        x1 = xs_ref[:, 0:D_MODEL]
        route = xs_ref[:, D_MODEL:]
        lane = lax.broadcasted_iota(jnp.int32, route.shape, 1)
        r = lax.rsqrt(jnp.mean(x1 * x1, axis=-1, keepdims=True) + EPS)
        n2 = ((x1 * r) * g2_ref[...]).astype(BF16)
        first_expert = tg_ref[m] * EXPERTS_PER_GROUP
        acc = x1
        for e in range(EXPERTS_PER_GROUP):
            ge = jnp.sum(jnp.where(lane == first_expert + e, route, 0.0), axis=1, keepdims=True)
            hg = jnp.dot(n2, wg_ref[e], preferred_element_type=F32)
            hu = jnp.dot(n2, wu_ref[e], preferred_element_type=F32)
            h = (hg * jax.nn.sigmoid(hg)) * hu * ge
            acc = acc + jnp.dot(h.astype(BF16), wd_ref[e], preferred_element_type=F32)
        y_ref[...] = acc

    @pl.when(tv_ref[m] == 0)
    def _():
        y_ref[...] = jnp.zeros_like(y_ref)


def _moe_sorted(tile_group, tile_valid, xs, g2, wg4, wu4, wd4, tm):
    m_pad = xs.shape[0]
    grid_spec = pltpu.PrefetchScalarGridSpec(
        num_scalar_prefetch=2,
        grid=(m_pad // tm,),
        in_specs=[
            pl.BlockSpec((tm, ROW_WIDTH), lambda m, tg, tv: (m, 0)),
            pl.BlockSpec(g2.shape, lambda m, tg, tv: (0, 0)),
            pl.BlockSpec((None, EXPERTS_PER_GROUP, D_MODEL, D_EXPERT), lambda m, tg, tv: (tg[m], 0, 0, 0)),
            pl.BlockSpec((None, EXPERTS_PER_GROUP, D_MODEL, D_EXPERT), lambda m, tg, tv: (tg[m], 0, 0, 0)),
            pl.BlockSpec((None, EXPERTS_PER_GROUP, D_EXPERT, D_MODEL), lambda m, tg, tv: (tg[m], 0, 0, 0)),
        ],
        out_specs=pl.BlockSpec((tm, D_MODEL), lambda m, tg, tv: (m, 0)),
    )
    return pl.pallas_call(
        _moe_kernel,
        grid_spec=grid_spec,
        out_shape=jax.ShapeDtypeStruct((m_pad, D_MODEL), F32),
        compiler_params=_cparams(("arbitrary",)),
        name="moe_experts",
    )(tile_group, tile_valid, xs, g2, wg4, wu4, wd4)


def _moe_plan(route_p, cnt_p, route_s, cnt_s, tm):
    n_total = route_p.shape[0] + route_s.shape[0]
    n_tiles = n_total // tm + N_GROUPS
    cp = cnt_p[0, :N_GROUPS].astype(jnp.int32)
    cs = cnt_s[0, :N_GROUPS].astype(jnp.int32)
    total = cp + cs
    tiles = (total + tm - 1) // tm
    tile_end = jnp.cumsum(tiles)
    offset = (tile_end - tiles) * tm
    gp = route_p[:, ROUTE_GROUP_LANE].astype(jnp.int32)
    gs = route_s[:, ROUTE_GROUP_LANE].astype(jnp.int32)
    dest_p = offset[gp] + route_p[:, ROUTE_RANK_LANE].astype(jnp.int32)
    dest_s = offset[gs] + cp[gs] + route_s[:, ROUTE_RANK_LANE].astype(jnp.int32)
    tile_id = jnp.arange(n_tiles)
    tile_group = jnp.minimum(jnp.sum(tile_id[:, None] >= tile_end[None, :], axis=1), N_GROUPS - 1).astype(jnp.int32)
    used = jnp.clip(offset[tile_group] + total[tile_group] - tile_id * tm, 0, tm)
    tile_valid = jnp.where(tile_id < tile_end[N_GROUPS - 1], used, 0).astype(jnp.int32)
    return dest_p, dest_s, tile_group, tile_valid, n_tiles * tm


def kernel(x_prompt, x_sample, cache_k, cache_v, cache_logf, state_pool, norm1_g, w_in, b_f, q_norm_g, k_norm_g,
           pool_w, pool_scale, w_out, norm2_g, w_router_group, b_router_group, w_router_expert, b_router_expert,
           w_gate, w_up, w_down):
    depth = norm1_g.shape[0]
    assert depth == 1, "single-layer step"
    b_p, s_p, _ = x_prompt.shape
    assert b_p == 1, "prompt kernels assume one stream"
    b_s, t_s, _ = x_sample.shape
    p_len = cache_k.shape[2]
    l = 0

    w = w_in[l]
    a3 = 3 * ATTN_WIDTH
    w_v = w[:, 2 * ATTN_WIDTH:a3].reshape(D_MODEL, N_HEADS, HEAD_DIM)
    w_vpad = jnp.pad(w_v, ((0, 0), (0, 0), (0, LANES - HEAD_DIM))).reshape(D_MODEL, N_HEADS * LANES)
    w_all = jnp.concatenate(
        [w[:, :a3], w[:, a3 + N_HEADS:], w_vpad, jnp.pad(w[:, a3:a3 + N_HEADS], ((0, 0), (0, LANES - N_HEADS)))],
        axis=1).astype(BF16)
    bf_pad = jnp.pad(b_f[l], (0, LANES - N_HEADS)).reshape(1, LANES)
    g1 = norm1_g[l].reshape(1, D_MODEL)
    qg = jnp.tile(q_norm_g[l], N_HEADS).reshape(1, ATTN_WIDTH)
    kg = jnp.tile(k_norm_g[l], N_HEADS).reshape(1, ATTN_WIDTH)
    hr = lax.broadcasted_iota(jnp.int32, (ATTN_WIDTH, ATTN_WIDTH), 0) // HEAD_DIM
    hc = lax.broadcasted_iota(jnp.int32, (ATTN_WIDTH, ATTN_WIDTH), 1) // HEAD_DIM
    hm = jnp.where(hr == hc, 1.0 / HEAD_DIM, 0.0).astype(BF16)
    pw = pool_w[l].astype(BF16)
    ps = pool_scale[l].reshape(1, POOL_WIDTH)
    wo = w_out[l].astype(BF16)
    g2 = norm2_g[l].reshape(1, D_MODEL)
    w_r = jnp.pad(jnp.concatenate([w_router_expert[l], w_router_group[l]], axis=1),
                  ((0, 0), (0, LANES - N_EXPERTS - N_GROUPS)))
    wrh = w_r.astype(BF16)
    wrl = (w_r - wrh.astype(F32)).astype(BF16)
    br = jnp.pad(jnp.concatenate([b_router_expert[l], b_router_group[l]]),
                 (0, LANES - N_EXPERTS - N_GROUPS)).reshape(1, LANES)
    wg4 = w_gate[l].astype(BF16).reshape(N_GROUPS, EXPERTS_PER_GROUP, D_MODEL, D_EXPERT)
    wu4 = w_up[l].astype(BF16).reshape(N_GROUPS, EXPERTS_PER_GROUP, D_MODEL, D_EXPERT)
    wd4 = w_down[l].astype(BF16).reshape(N_GROUPS, EXPERTS_PER_GROUP, D_EXPERT, D_MODEL)

    tile = 512
    key_tile = 256

    xp = x_prompt.reshape(s_p, D_MODEL)
    q, kf, kb, vf, _, vp, pin, logf, c = _in_project(xp, tile, tile, True, g1, w_all, bf_pad, qg, kg, hm)
    ct4 = c[:, :N_HEADS].T.reshape(N_PAIRS, 2, s_p)
    j0 = _first_key_blocks(c, q_norm_g[l], k_norm_g[l], tile, key_tile)
    a = _attention_prompt(j0, q, kb, vp, ct4, tile, key_tile)
    hist_map = lambda i: (jnp.maximum(i * (tile // HIST_ROWS) - 1, 0), 0)
    xg_p, route_p, cnt_p = _mix(a, pin, pin, hist_map, xp, tile, 0, tile, True, pw, ps, wo, g2, wrh, wrl, br)
    k_prompt = kf.reshape(depth, b_p, s_p, N_HEADS, HEAD_DIM)
    v_prompt = vf.reshape(depth, b_p, s_p, N_HEADS, HEAD_DIM)
    logf_prompt = logf[:, :N_HEADS].reshape(depth, b_p, s_p, N_HEADS)
    pool_prompt = pin[s_p - POOL_PAD:].reshape(depth, b_p, POOL_PAD, POOL_WIDTH)

    n_s = b_s * t_s
    xs = x_sample.reshape(n_s, D_MODEL)
    q, kf, kb, vf, vb, _, pin, logf, d = _in_project(xs, tile, t_s, False, g1, w_all, bf_pad, qg, kg, hm)
    dt4 = d[:, :N_HEADS].reshape(b_s, t_s, N_PAIRS, 2).transpose(0, 2, 3, 1)
    clf = cache_logf[l].astype(F32).transpose(0, 2, 1).reshape(b_s * N_HEADS, p_len)
    r4 = _suffix_sums(clf).reshape(b_s, N_PAIRS, 2, p_len)
    ck = cache_k[l].reshape(b_s, p_len, ATTN_WIDTH)
    cv = cache_v[l].reshape(b_s, p_len, ATTN_WIDTH)
    a = _attention_sample(q, kb, vb, ck, cv, d, dt4, r4, t_s)
    hist_s = jnp.pad(state_pool[l], ((0, 0), (HIST_ROWS - POOL_PAD, 0), (0, 0))).reshape(b_s * HIST_ROWS, POOL_WIDTH)
    xg_s, route_s, cnt_s = _mix(a, pin, hist_s, lambda i: (i, 0), xs, t_s, p_len, 0, False, pw, ps, wo, g2, wrh, wrl,
                                br)
    k_sample = kf.reshape(depth, b_s, t_s, N_HEADS, HEAD_DIM)
    v_sample = vf.reshape(depth, b_s, t_s, N_HEADS, HEAD_DIM)
    logf_sample = logf[:, :N_HEADS].reshape(depth, b_s, t_s, N_HEADS)
    pool_sample = pin.reshape(b_s, t_s, POOL_WIDTH)[:, t_s - POOL_PAD:].reshape(depth, b_s, POOL_PAD, POOL_WIDTH)

    dest_p, dest_s, tile_group, tile_valid, m_pad = _moe_plan(route_p, cnt_p, route_s, cnt_s, tile)
    rows = jnp.zeros((m_pad, ROW_WIDTH), F32)
    rows = _scatter_rows(dest_p, xg_p, rows, tile)
    rows = _scatter_rows(dest_s, xg_s, rows, tile)
    ys = _moe_sorted(tile_group, tile_valid, rows, g2, wg4, wu4, wd4, tile)
    y_prompt = _gather_rows(dest_p, ys, tile).reshape(b_p, s_p, D_MODEL)
    y_sample = _gather_rows(dest_s, ys, tile).reshape(b_s, t_s, D_MODEL)

    return (y_prompt, y_sample, k_prompt, v_prompt, logf_prompt, pool_prompt,
            k_sample, v_sample, logf_sample, pool_sample)
```

```python
import functools

import jax
import jax.numpy as jnp
from jax import lax
from jax.experimental import pallas as pl
from jax.experimental.pallas import tpu as pltpu

F32 = jnp.float32
BF16 = jnp.bfloat16

D_MODEL = 1024
ATTN_WIDTH = 512
N_HEADS = 8
HEAD_DIM = 64
POOL_WIDTH = 512
POOL_WINDOWS = (2, 4, 8, 16)
POOL_GROUP_DIM = 128
POOL_PAD = 15
HIST_ROWS = 16
N_GROUPS = 4
EXPERTS_PER_GROUP = 4
N_EXPERTS = 16
D_EXPERT = 512
EPS = 1e-6
LANES = 128
SUBLANES = 8
PAIR = 2 * HEAD_DIM
N_PAIRS = N_HEADS // 2
ROWSUM_LANE = (HEAD_DIM, 0)
VMEM_LIMIT = 56 * 1024 * 1024
ROW_WIDTH = D_MODEL + LANES
ROUTE_GROUP_LANE = N_EXPERTS
ROUTE_RANK_LANE = N_EXPERTS + 1


def _cparams(sem):
    return pltpu.CompilerParams(dimension_semantics=sem, vmem_limit_bytes=VMEM_LIMIT)


def _split3(x):
    hi = x.astype(BF16)
    r1 = x - hi.astype(F32)
    mid = r1.astype(BF16)
    lo = (r1 - mid.astype(F32)).astype(BF16)
    return hi, mid, lo


def _split2(x):
    hi = x.astype(BF16)
    lo = (x - hi.astype(F32)).astype(BF16)
    return hi, lo


def _inproj_kernel(x_ref, g1_ref, w_ref, bf_ref, qg_ref, kg_ref, hm_ref, tri_ref,
                   q_ref, kf_ref, kb_ref, vf_ref, vb_ref, vp_ref, p_ref, logf_ref, c_ref, carry_ref, *, carry_rows):
    x = x_ref[...]
    r = lax.rsqrt(jnp.mean(x * x, axis=-1, keepdims=True) + EPS)
    n = ((x * r) * g1_ref[...]).astype(BF16)
    z = jnp.dot(n, w_ref[...], preferred_element_type=F32)

    hm = hm_ref[...]

    def head_norm(zz, g):
        hi, lo = _split2(zz * zz)
        ms = jnp.dot(hi, hm, preferred_element_type=F32) + jnp.dot(lo, hm, preferred_element_type=F32)
        return (zz * lax.rsqrt(ms + EPS)) * g

    q = head_norm(z[:, 0:ATTN_WIDTH], qg_ref[...])
    q_ref[...] = (q * (HEAD_DIM ** -0.5)).astype(BF16)
    k = head_norm(z[:, ATTN_WIDTH:2 * ATTN_WIDTH], kg_ref[...])
    kf_ref[...] = k
    kb_ref[...] = k.astype(BF16)
    v = z[:, 2 * ATTN_WIDTH:3 * ATTN_WIDTH]
    vf_ref[...] = v
    vb_ref[...] = v.astype(BF16)
    p_ref[...] = z[:, 3 * ATTN_WIDTH:3 * ATTN_WIDTH + POOL_WIDTH]
    plane = lax.broadcasted_iota(jnp.int32, (v.shape[0], PAIR), 1)
    for hp in range(N_PAIRS):
        vpair = v[:, hp * PAIR:(hp + 1) * PAIR]
        for h in range(2):
            own = (plane < HEAD_DIM) if h == 0 else (plane >= HEAD_DIM)
            marker = jnp.where(plane == ROWSUM_LANE[h], 1.0, 0.0)
            vp_ref[:, (2 * hp + h) * LANES:(2 * hp + h + 1) * LANES] = jnp.where(own, vpair, marker).astype(BF16)

    f = z[:, 3 * ATTN_WIDTH + POOL_WIDTH:] + bf_ref[...]
    lane = lax.broadcasted_iota(jnp.int32, f.shape, 1)
    logf = -(jnp.maximum(-f, 0.0) + jnp.log1p(jnp.exp(-jnp.abs(f))))
    logf = jnp.where(lane < N_HEADS, logf, 0.0)
    logf_ref[...] = logf

    tri = tri_ref[...]
    hi, mid, lo = _split3(logf)
    c = (jnp.dot(tri, hi, preferred_element_type=F32) + jnp.dot(tri, mid, preferred_element_type=F32)
         + jnp.dot(tri, lo, preferred_element_type=F32))
    if carry_rows:
        @pl.when(pl.program_id(0) == 0)
        def _():
            carry_ref[...] = jnp.zeros_like(carry_ref)
        c = c + carry_ref[0:1, :]
        carry_ref[...] = jnp.broadcast_to(c[c.shape[0] - 1:, :], carry_ref.shape)
    c_ref[...] = c


def _in_project(x2d, tm, seg, carry_rows, g1, w_all, bf_pad, qg, kg, hm):
    n = x2d.shape[0]
    rows = lax.broadcasted_iota(jnp.int32, (tm, tm), 0)
    cols = lax.broadcasted_iota(jnp.int32, (tm, tm), 1)
    tri = ((cols <= rows) & (rows // seg == cols // seg)).astype(BF16)
    wc = w_all.shape[1]
    row_blk = lambda w: pl.BlockSpec((tm, w), lambda i: (i, 0))
    const = lambda a: pl.BlockSpec(a.shape, lambda i: (0,) * a.ndim)
    out_shape = (
        jax.ShapeDtypeStruct((n, ATTN_WIDTH), BF16),
        jax.ShapeDtypeStruct((n, ATTN_WIDTH), F32),
        jax.ShapeDtypeStruct((n, ATTN_WIDTH), BF16),
        jax.ShapeDtypeStruct((n, ATTN_WIDTH), F32),
        jax.ShapeDtypeStruct((n, ATTN_WIDTH), BF16),
        jax.ShapeDtypeStruct((n, N_HEADS * LANES), BF16),
        jax.ShapeDtypeStruct((n, POOL_WIDTH), F32),
        jax.ShapeDtypeStruct((n, LANES), F32),
        jax.ShapeDtypeStruct((n, LANES), F32),
    )
    return pl.pallas_call(
        functools.partial(_inproj_kernel, carry_rows=carry_rows),
        grid=(n // tm,),
        in_specs=[row_blk(D_MODEL), const(g1), const(w_all), const(bf_pad), const(qg), const(kg), const(hm),
                  const(tri)],
        out_specs=[row_blk(ATTN_WIDTH)] * 5 + [row_blk(N_HEADS * LANES), row_blk(POOL_WIDTH), row_blk(LANES),
                                                row_blk(LANES)],
        out_shape=out_shape,
        scratch_shapes=[pltpu.VMEM((8, LANES), F32)],
        compiler_params=_cparams(("arbitrary",)),
        name="in_project",
    )(x2d, g1, w_all, bf_pad, qg, kg, hm, tri)


def _attn_prompt_kernel(j0_ref, q_ref, k_ref, v_ref, ct_ref, o_ref, acc_ref, m_ref, *, tq, tk):
    hp = pl.program_id(0)
    i = pl.program_id(1)
    lane = lax.broadcasted_iota(jnp.int32, (1, PAIR), 1)
    first = lane < HEAD_DIM
    q = q_ref[...]
    zero = jnp.zeros_like(q)
    qh = (jnp.where(first, q, zero), jnp.where(first, zero, q))
    q_start = pl.multiple_of(i * tq, tq)

    acc_ref[...] = jnp.zeros_like(acc_ref)
    m_ref[...] = jnp.full_like(m_ref, -jnp.inf)

    for h in range(2):
        c0 = ct_ref[h:h + 1, pl.ds(q_start, LANES)][:, 0:1]

        last = (i + 1) * (tq // tk) - 1

        def scores(j, h=h, c0=c0, last=last):
            start = pl.multiple_of(jnp.minimum(j, last) * tk, tk)
            k = k_ref[pl.ds(start, tk), :]
            s = lax.dot_general(qh[h], k, (((1,), (1,)), ((), ())), preferred_element_type=F32)
            s = s - (ct_ref[h:h + 1, pl.ds(start, tk)] - c0)
            row = lax.broadcasted_iota(jnp.int32, s.shape, 0)
            col = lax.broadcasted_iota(jnp.int32, s.shape, 1)
            s = jnp.where(col - row <= q_start - j * tk, s, -jnp.inf)
            return s, jnp.broadcast_to(jnp.max(s, axis=1, keepdims=True), (tq, LANES))

        def body(j, carry, h=h, last=last, scores=scores):
            s, rowmax = carry
            nxt = scores(j + 1)
            start = pl.multiple_of(j * tk, tk)
            v = v_ref[pl.ds(start, tk), h * LANES:(h + 1) * LANES]
            m_prev = m_ref[h]
            m_new = jnp.maximum(m_prev, rowmax)
            alpha = jnp.exp(m_prev - m_new)
            p = jnp.concatenate([jnp.exp(s[:, g * LANES:(g + 1) * LANES] - m_new) for g in range(tk // LANES)], axis=1)
            m_ref[h] = m_new
            acc_ref[h] = acc_ref[h] * alpha + jnp.dot(p.astype(BF16), v, preferred_element_type=F32)
            return nxt

        j_first = j0_ref[i, 2 * hp + h]
        lax.fori_loop(j_first, last + 1, body, scores(j_first))

    out = [acc_ref[h] / acc_ref[h][:, ROWSUM_LANE[h]:ROWSUM_LANE[h] + 1] for h in range(2)]
    o_ref[...] = jnp.where(first, out[0], out[1]).astype(o_ref.dtype)


def _attention_prompt(j0, q, kb, vp, ct4, tq, tk):
    n = q.shape[0]
    grid_spec = pltpu.PrefetchScalarGridSpec(
        num_scalar_prefetch=1,
        grid=(N_PAIRS, n // tq),
        in_specs=[
            pl.BlockSpec((tq, PAIR), lambda hp, i, j0: (i, hp)),
            pl.BlockSpec((n, PAIR), lambda hp, i, j0: (0, hp)),
            pl.BlockSpec((n, 2 * LANES), lambda hp, i, j0: (0, hp)),
            pl.BlockSpec((None, 2, n), lambda hp, i, j0: (hp, 0, 0)),
        ],
        out_specs=pl.BlockSpec((tq, PAIR), lambda hp, i, j0: (i, hp)),
        scratch_shapes=[pltpu.VMEM((2, tq, LANES), F32), pltpu.VMEM((2, tq, LANES), F32)],
    )
    return pl.pallas_call(
        functools.partial(_attn_prompt_kernel, tq=tq, tk=tk),
        grid_spec=grid_spec,
        out_shape=jax.ShapeDtypeStruct((n, ATTN_WIDTH), BF16),
        compiler_params=_cparams(("arbitrary", "arbitrary")),
        name="attention_prompt",
    )(j0, q, kb, vp, ct4)


def _first_key_blocks(c, q_gain, k_gain, tq, tk):
    n = c.shape[0]
    qk_bound = 1.02 * HEAD_DIM ** 0.5 * jnp.max(jnp.abs(q_gain)) * jnp.max(jnp.abs(k_gain))
    threshold = 104.0 + 2.0 * qk_bound + 1.0
    c_start = c[0::tq, :N_HEADS]
    c_end = c[tk - 1::tk, :N_HEADS]
    gap = c_start[:, None, :] - c_end[None, :, :]
    earlier = (jnp.arange(n // tk)[None, :, None] + 1) * tk <= jnp.arange(n // tq)[:, None, None] * tq
    return jnp.sum((gap < -threshold) & earlier, axis=1).astype(jnp.int32)


def _suffix_kernel(x_ref, o_ref):
    x = x_ref[...]
    p_len = x.shape[1]
    rows = lax.broadcasted_iota(jnp.int32, (p_len, p_len), 0)
    cols = lax.broadcasted_iota(jnp.int32, (p_len, p_len), 1)
    u = (rows > cols).astype(BF16)
    hi, mid, lo = _split3(x)
    o_ref[...] = (jnp.dot(hi, u, preferred_element_type=F32) + jnp.dot(mid, u, preferred_element_type=F32)
                  + jnp.dot(lo, u, preferred_element_type=F32))


def _suffix_sums(x):
    return pl.pallas_call(
        _suffix_kernel,
        out_shape=jax.ShapeDtypeStruct(x.shape, F32),
        compiler_params=pltpu.CompilerParams(vmem_limit_bytes=VMEM_LIMIT),
        name="cache_suffix_sums",
    )(x)


def _attn_sample_kernel(q_ref, kn_ref, vn_ref, ck_ref, cv_ref, d_ref, dt_ref, r_ref, o_ref):
    hp = pl.program_id(1)
    lane = lax.broadcasted_iota(jnp.int32, (1, PAIR), 1)
    first = lane < HEAD_DIM
    q = q_ref[...]
    zero = jnp.zeros_like(q)
    qh = (jnp.where(first, q, zero), jnp.where(first, zero, q))
    dblk = d_ref[...]
    dlane = lax.broadcasted_iota(jnp.int32, dblk.shape, 1)
    kn = kn_ref[...]
    vn = vn_ref[...]
    ck = ck_ref[...].astype(BF16)
    cv = cv_ref[...].astype(BF16)
    nt = (((1,), (1,)), ((), ()))
    outs, ls = [], []
    for h in range(2):
        dq = jnp.sum(jnp.where(dlane == 2 * hp + h, dblk, 0.0), axis=1, keepdims=True)
        s1 = lax.dot_general(qh[h], ck, nt, preferred_element_type=F32) + dq + r_ref[h:h + 1, :]
        s2 = lax.dot_general(qh[h], kn, nt, preferred_element_type=F32) + dq - dt_ref[h:h + 1, :]
        row = lax.broadcasted_iota(jnp.int32, s2.shape, 0)
        col = lax.broadcasted_iota(jnp.int32, s2.shape, 1)
        s2 = jnp.where(col <= row, s2, -jnp.inf)
        m = jnp.maximum(jnp.max(s1, axis=1, keepdims=True), jnp.max(s2, axis=1, keepdims=True))
        p1 = jnp.exp(s1 - m)
        p2 = jnp.exp(s2 - m)
        ls.append(jnp.sum(p1, axis=1, keepdims=True) + jnp.sum(p2, axis=1, keepdims=True))
        outs.append(jnp.dot(p1.astype(BF16), cv, preferred_element_type=F32)
                    + jnp.dot(p2.astype(BF16), vn, preferred_element_type=F32))
    o_ref[...] = (jnp.where(first, outs[0], outs[1]) / jnp.where(first, ls[0], ls[1])).astype(o_ref.dtype)


def _attention_sample(q, kb, vb, cache_k, cache_v, d, dt4, r4, t):
    nb, p_len = cache_k.shape[0], cache_k.shape[1]
    return pl.pallas_call(
        _attn_sample_kernel,
        grid=(nb, N_PAIRS),
        in_specs=[
            pl.BlockSpec((t, PAIR), lambda b, hp: (b, hp)),
            pl.BlockSpec((t, PAIR), lambda b, hp: (b, hp)),
            pl.BlockSpec((t, PAIR), lambda b, hp: (b, hp)),
            pl.BlockSpec((None, p_len, PAIR), lambda b, hp: (b, 0, hp)),
            pl.BlockSpec((None, p_len, PAIR), lambda b, hp: (b, 0, hp)),
            pl.BlockSpec((t, LANES), lambda b, hp: (b, 0)),
            pl.BlockSpec((None, None, 2, t), lambda b, hp: (b, hp, 0, 0)),
            pl.BlockSpec((None, None, 2, p_len), lambda b, hp: (b, hp, 0, 0)),
        ],
        out_specs=pl.BlockSpec((t, PAIR), lambda b, hp: (b, hp)),
        out_shape=jax.ShapeDtypeStruct((nb * t, ATTN_WIDTH), BF16),
        compiler_params=_cparams(("arbitrary", "arbitrary")),
        name="attention_sample",
    )(q, kb, vb, cache_k, cache_v, d, dt4, r4)


def _mix_kernel(a_ref, p_ref, hist_ref, x_ref, pw_ref, ps_ref, wo_ref, g2_ref, wrh_ref, wrl_ref, br_ref, tri_ref,
                xg_ref, route_ref, cnt_ref, *, pos0, pos_stride, zero_first):
    i = pl.program_id(0)
    t = p_ref.shape[0]
    p = p_ref[...]
    hist = hist_ref[...]
    if zero_first:
        hist = jnp.where(i == 0, 0.0, hist)
    xh = jnp.concatenate([hist, p], axis=0)
    pos = pos0 + i * pos_stride + lax.broadcasted_iota(jnp.int32, (t, 1), 0)
    ys = []
    for g, w in enumerate(POOL_WINDOWS):
        col = xh[:, g * POOL_GROUP_DIM:(g + 1) * POOL_GROUP_DIM]
        acc = col
        span = 1
        while span < w:
            acc = acc + pltpu.roll(acc, span, axis=0)
            span *= 2
        cnt = jnp.minimum(pos + 1, w).astype(F32)
        dgrp = acc[HIST_ROWS:, :] / cnt - col[HIST_ROWS:, :]
        ys.append(jnp.dot(dgrp.astype(BF16), pw_ref[g], preferred_element_type=F32))
    pm = (jnp.concatenate(ys, axis=1) * ps_ref[...]).astype(BF16)
    mix = (jnp.dot(a_ref[...], wo_ref[0:ATTN_WIDTH, :], preferred_element_type=F32)
           + jnp.dot(pm, wo_ref[ATTN_WIDTH:, :], preferred_element_type=F32))
    x1 = x_ref[...] + mix
    xg_ref[:, 0:D_MODEL] = x1

    r = lax.rsqrt(jnp.mean(x1 * x1, axis=-1, keepdims=True) + EPS)
    n2 = (x1 * r) * g2_ref[...]

    nh, nl = _split2(n2)
    wrh = wrh_ref[...]
    logits = (jnp.dot(nh, wrh, preferred_element_type=F32) + jnp.dot(nl, wrh, preferred_element_type=F32)
              + jnp.dot(nh, wrl_ref[...], preferred_element_type=F32)) + br_ref[...]
    lane = lax.broadcasted_iota(jnp.int32, logits.shape, 1).astype(F32)
    neg = -jnp.inf
    big = jnp.float32(1 << 20)
    is_g = (lane >= N_EXPERTS) & (lane < N_EXPERTS + N_GROUPS)
    lg = jnp.where(is_g, logits, neg)
    gmax = jnp.max(lg, axis=1, keepdims=True)
    g_lane = jnp.min(jnp.where(lg == gmax, lane, big), axis=1, keepdims=True)
    top_pg = 1.0 / jnp.sum(jnp.exp(lg - gmax), axis=1, keepdims=True)
    g_idx = g_lane - N_EXPERTS
    sel = (lane >= g_idx * EXPERTS_PER_GROUP) & (lane < (g_idx + 1) * EXPERTS_PER_GROUP)
    le = jnp.where(sel, logits, neg)
    m1 = jnp.max(le, axis=1, keepdims=True)
    i1 = jnp.min(jnp.where(le == m1, lane, big), axis=1, keepdims=True)
    le2 = jnp.where(lane == i1, neg, le)
    m2 = jnp.max(le2, axis=1, keepdims=True)
    i2 = jnp.min(jnp.where(le2 == m2, lane, big), axis=1, keepdims=True)
    e2 = jnp.exp(m2 - m1)
    w1 = top_pg / (1.0 + e2)
    w2 = top_pg * e2 / (1.0 + e2)
    gate = jnp.where(lane == i1, w1, 0.0) + jnp.where(lane == i2, w2, 0.0)

    @pl.when(i == 0)
    def _():
        cnt_ref[...] = jnp.zeros_like(cnt_ref)
    member = lane == g_idx
    before = jnp.dot(tri_ref[...], member.astype(BF16), preferred_element_type=F32) + cnt_ref[0:1, :]
    rank = jnp.sum(jnp.where(member, before, 0.0), axis=1, keepdims=True)
    cnt_ref[...] = cnt_ref[...] + jnp.sum(member.astype(F32), axis=0, keepdims=True)
    route = jnp.where(lane == ROUTE_GROUP_LANE, g_idx, jnp.where(lane == ROUTE_RANK_LANE, rank, gate))
    route_ref[...] = route
    xg_ref[:, D_MODEL:] = route


def _mix(a, p, hist_src, hist_map, x2d, t, pos0, pos_stride, zero_first, pw, ps, wo, g2, wrh, wrl, br):
    n = x2d.shape[0]
    rows = lax.broadcasted_iota(jnp.int32, (t, t), 0)
    cols = lax.broadcasted_iota(jnp.int32, (t, t), 1)
    tri = (cols < rows).astype(BF16)
    row_blk = lambda w: pl.BlockSpec((t, w), lambda i: (i, 0))
    const = lambda arr: pl.BlockSpec(arr.shape, lambda i: (0,) * arr.ndim)
    return pl.pallas_call(
        functools.partial(_mix_kernel, pos0=pos0, pos_stride=pos_stride, zero_first=zero_first),
        grid=(n // t,),
        in_specs=[row_blk(ATTN_WIDTH), row_blk(POOL_WIDTH), pl.BlockSpec((HIST_ROWS, POOL_WIDTH), hist_map),
                  row_blk(D_MODEL), const(pw), const(ps), const(wo), const(g2), const(wrh), const(wrl), const(br),
                  const(tri)],
        out_specs=[row_blk(ROW_WIDTH), row_blk(LANES), pl.BlockSpec((8, LANES), lambda i: (0, 0))],
        out_shape=(jax.ShapeDtypeStruct((n, ROW_WIDTH), F32), jax.ShapeDtypeStruct((n, LANES), F32),
                   jax.ShapeDtypeStruct((8, LANES), F32)),
        compiler_params=_cparams(("arbitrary",)),
        name="pool_mix_route",
    )(a, p, hist_src, x2d, pw, ps, wo, g2, wrh, wrl, br, tri)


def _scatter_rows_kernel(dest_ref, x_ref, xs_in_ref, xs_ref, sem):
    del xs_in_ref
    t = x_ref.shape[0]

    def issue(g, carry):
        base = pl.multiple_of(g * SUBLANES, SUBLANES)
        tile_rows = x_ref.at[pl.ds(base, SUBLANES)]
        for u in range(SUBLANES):
            pltpu.make_async_copy(tile_rows.at[pl.ds(u, 1)], xs_ref.at[pl.ds(dest_ref[base + u], 1)],
                                  sem).start(priority=u % 2)
        return carry

    lax.fori_loop(0, t // SUBLANES, issue, 0)
    pltpu.make_async_copy(x_ref, xs_ref.at[pl.ds(0, t)], sem).wait()


def _scatter_rows(dest, x, xs, t):
    n, w = x.shape
    return pl.pallas_call(
        _scatter_rows_kernel,
        grid=(n // t,),
        in_specs=[pl.BlockSpec((t,), lambda i: (i,), memory_space=pltpu.SMEM),
                  pl.BlockSpec((t, w), lambda i: (i, 0)),
                  pl.BlockSpec(memory_space=pl.ANY)],
        out_specs=pl.BlockSpec(memory_space=pl.ANY),
        out_shape=jax.ShapeDtypeStruct(xs.shape, xs.dtype),
        scratch_shapes=[pltpu.SemaphoreType.DMA],
        input_output_aliases={2: 0},
        compiler_params=_cparams(("arbitrary",)),
        name="moe_scatter_rows",
    )(dest, x, xs)


def _gather_rows_kernel(dest_ref, ys_ref, y_ref, sem):
    t = y_ref.shape[0]

    def issue(g, carry):
        base = pl.multiple_of(g * SUBLANES, SUBLANES)
        tile_rows = y_ref.at[pl.ds(base, SUBLANES)]
        for u in range(SUBLANES):
            pltpu.make_async_copy(ys_ref.at[pl.ds(dest_ref[base + u], 1)], tile_rows.at[pl.ds(u, 1)],
                                  sem).start(priority=u % 2)
        return carry

    lax.fori_loop(0, t // SUBLANES, issue, 0)
    pltpu.make_async_copy(ys_ref.at[pl.ds(0, t)], y_ref, sem).wait()


def _gather_rows(dest, ys, t):
    n = dest.shape[0]
    w = ys.shape[1]
    return pl.pallas_call(
        _gather_rows_kernel,
        grid=(n // t,),
        in_specs=[pl.BlockSpec((t,), lambda i: (i,), memory_space=pltpu.SMEM),
                  pl.BlockSpec(memory_space=pl.ANY)],
        out_specs=pl.BlockSpec((t, w), lambda i: (i, 0)),
        out_shape=jax.ShapeDtypeStruct((n, w), ys.dtype),
        scratch_shapes=[pltpu.SemaphoreType.DMA],
        compiler_params=_cparams(("arbitrary",)),
        name="moe_gather_rows",
    )(dest, ys)


def _moe_kernel(tg_ref, tv_ref, xs_ref, g2_ref, wg_ref, wu_ref, wd_ref, y_ref):
    m = pl.program_id(0)

    @pl.when(tv_ref[m] > 0)
    def _():
        x1 = xs_ref[:, 0:D_MODEL]
        route = xs_ref[:, D_MODEL:]
        lane = lax.broadcasted_iota(jnp.int32, route.shape, 1)
        r = lax.rsqrt(jnp.mean(x1 * x1, axis=-1, keepdims=True) + EPS)
        n2 = ((x1 * r) * g2_ref[...]).astype(BF16)
        first_expert = tg_ref[m] * EXPERTS_PER_GROUP
        acc = x1
        for e in range(EXPERTS_PER_GROUP):
            ge = jnp.sum(jnp.where(lane == first_expert + e, route, 0.0), axis=1, keepdims=True)
            hg = jnp.dot(n2, wg_ref[e], preferred_element_type=F32)
            hu = jnp.dot(n2, wu_ref[e], preferred_element_type=F32)
            h = (hg * jax.nn.sigmoid(hg)) * hu * ge
            acc = acc + jnp.dot(h.astype(BF16), wd_ref[e], preferred_element_type=F32)
        y_ref[...] = acc

    @pl.when(tv_ref[m] == 0)
    def _():
        y_ref[...] = jnp.zeros_like(y_ref)


def _moe_sorted(tile_group, tile_valid, xs, g2, wg4, wu4, wd4, tm):
    m_pad = xs.shape[0]
    grid_spec = pltpu.PrefetchScalarGridSpec(
        num_scalar_prefetch=2,
        grid=(m_pad // tm,),
        in_specs=[
            pl.BlockSpec((tm, ROW_WIDTH), lambda m, tg, tv: (m, 0)),
            pl.BlockSpec(g2.shape, lambda m, tg, tv: (0, 0)),
            pl.BlockSpec((None, EXPERTS_PER_GROUP, D_MODEL, D_EXPERT), lambda m, tg, tv: (tg[m], 0, 0, 0)),
            pl.BlockSpec((None, EXPERTS_PER_GROUP, D_MODEL, D_EXPERT), lambda m, tg, tv: (tg[m], 0, 0, 0)),
            pl.BlockSpec((None, EXPERTS_PER_GROUP, D_EXPERT, D_MODEL), lambda m, tg, tv: (tg[m], 0, 0, 0)),
        ],
        out_specs=pl.BlockSpec((tm, D_MODEL), lambda m, tg, tv: (m, 0)),
    )
    return pl.pallas_call(
        _moe_kernel,
        grid_spec=grid_spec,
        out_shape=jax.ShapeDtypeStruct((m_pad, D_MODEL), F32),
        compiler_params=_cparams(("arbitrary",)),
        name="moe_experts",
    )(tile_group, tile_valid, xs, g2, wg4, wu4, wd4)


def _moe_plan(route_p, cnt_p, route_s, cnt_s, tm):
    n_total = route_p.shape[0] + route_s.shape[0]
    n_tiles = n_total // tm + N_GROUPS
    cp = cnt_p[0, :N_GROUPS].astype(jnp.int32)
    cs = cnt_s[0, :N_GROUPS].astype(jnp.int32)
    total = cp + cs
    tiles = (total + tm - 1) // tm
    tile_end = jnp.cumsum(tiles)
    offset = (tile_end - tiles) * tm
    gp = route_p[:, ROUTE_GROUP_LANE].astype(jnp.int32)
    gs = route_s[:, ROUTE_GROUP_LANE].astype(jnp.int32)
    dest_p = offset[gp] + route_p[:, ROUTE_RANK_LANE].astype(jnp.int32)
    dest_s = offset[gs] + cp[gs] + route_s[:, ROUTE_RANK_LANE].astype(jnp.int32)
    tile_id = jnp.arange(n_tiles)
    tile_group = jnp.minimum(jnp.sum(tile_id[:, None] >= tile_end[None, :], axis=1), N_GROUPS - 1).astype(jnp.int32)
    used = jnp.clip(offset[tile_group] + total[tile_group] - tile_id * tm, 0, tm)
    tile_valid = jnp.where(tile_id < tile_end[N_GROUPS - 1], used, 0).astype(jnp.int32)
    return dest_p, dest_s, tile_group, tile_valid, n_tiles * tm


def kernel(x_prompt, x_sample, cache_k, cache_v, cache_logf, state_pool, norm1_g, w_in, b_f, q_norm_g, k_norm_g,
           pool_w, pool_scale, w_out, norm2_g, w_router_group, b_router_group, w_router_expert, b_router_expert,
           w_gate, w_up, w_down):
    depth = norm1_g.shape[0]
    assert depth == 1, "single-layer step"
    b_p, s_p, _ = x_prompt.shape
    assert b_p == 1, "prompt kernels assume one stream"
    b_s, t_s, _ = x_sample.shape
    p_len = cache_k.shape[2]
    l = 0

    w = w_in[l]
    a3 = 3 * ATTN_WIDTH
    w_all = jnp.concatenate(
        [w[:, :a3], w[:, a3 + N_HEADS:], jnp.pad(w[:, a3:a3 + N_HEADS], ((0, 0), (0, LANES - N_HEADS)))],
        axis=1).astype(BF16)
    bf_pad = jnp.pad(b_f[l], (0, LANES - N_HEADS)).reshape(1, LANES)
    g1 = norm1_g[l].reshape(1, D_MODEL)
    qg = jnp.tile(q_norm_g[l], N_HEADS).reshape(1, ATTN_WIDTH)
    kg = jnp.tile(k_norm_g[l], N_HEADS).reshape(1, ATTN_WIDTH)
    hr = lax.broadcasted_iota(jnp.int32, (ATTN_WIDTH, ATTN_WIDTH), 0) // HEAD_DIM
    hc = lax.broadcasted_iota(jnp.int32, (ATTN_WIDTH, ATTN_WIDTH), 1) // HEAD_DIM
    hm = jnp.where(hr == hc, 1.0 / HEAD_DIM, 0.0).astype(BF16)
    pw = pool_w[l].astype(BF16)
    ps = pool_scale[l].reshape(1, POOL_WIDTH)
    wo = w_out[l].astype(BF16)
    g2 = norm2_g[l].reshape(1, D_MODEL)
    w_r = jnp.pad(jnp.concatenate([w_router_expert[l], w_router_group[l]], axis=1),
                  ((0, 0), (0, LANES - N_EXPERTS - N_GROUPS)))
    wrh = w_r.astype(BF16)
    wrl = (w_r - wrh.astype(F32)).astype(BF16)
    br = jnp.pad(jnp.concatenate([b_router_expert[l], b_router_group[l]]),
                 (0, LANES - N_EXPERTS - N_GROUPS)).reshape(1, LANES)
    wg4 = w_gate[l].astype(BF16).reshape(N_GROUPS, EXPERTS_PER_GROUP, D_MODEL, D_EXPERT)
    wu4 = w_up[l].astype(BF16).reshape(N_GROUPS, EXPERTS_PER_GROUP, D_MODEL, D_EXPERT)
    wd4 = w_down[l].astype(BF16).reshape(N_GROUPS, EXPERTS_PER_GROUP, D_EXPERT, D_MODEL)

    tile = 512
    key_tile = 256

    xp = x_prompt.reshape(s_p, D_MODEL)
    q, kf, kb, vf, _, vp, pin, logf, c = _in_project(xp, tile, tile, True, g1, w_all, bf_pad, qg, kg, hm)
    ct4 = c[:, :N_HEADS].T.reshape(N_PAIRS, 2, s_p)
    j0 = _first_key_blocks(c, q_norm_g[l], k_norm_g[l], tile, key_tile)
    a = _attention_prompt(j0, q, kb, vp, ct4, tile, key_tile)
    hist_map = lambda i: (jnp.maximum(i * (tile // HIST_ROWS) - 1, 0), 0)
    xg_p, route_p, cnt_p = _mix(a, pin, pin, hist_map, xp, tile, 0, tile, True, pw, ps, wo, g2, wrh, wrl, br)
    k_prompt = kf.reshape(depth, b_p, s_p, N_HEADS, HEAD_DIM)
    v_prompt = vf.reshape(depth, b_p, s_p, N_HEADS, HEAD_DIM)
    logf_prompt = logf[:, :N_HEADS].reshape(depth, b_p, s_p, N_HEADS)
    pool_prompt = pin[s_p - POOL_PAD:].reshape(depth, b_p, POOL_PAD, POOL_WIDTH)

    n_s = b_s * t_s
    xs = x_sample.reshape(n_s, D_MODEL)
    q, kf, kb, vf, vb, _, pin, logf, d = _in_project(xs, tile, t_s, False, g1, w_all, bf_pad, qg, kg, hm)
    dt4 = d[:, :N_HEADS].reshape(b_s, t_s, N_PAIRS, 2).transpose(0, 2, 3, 1)
    clf = cache_logf[l].astype(F32).transpose(0, 2, 1).reshape(b_s * N_HEADS, p_len)
    r4 = _suffix_sums(clf).reshape(b_s, N_PAIRS, 2, p_len)
    ck = cache_k[l].reshape(b_s, p_len, ATTN_WIDTH)
    cv = cache_v[l].reshape(b_s, p_len, ATTN_WIDTH)
    a = _attention_sample(q, kb, vb, ck, cv, d, dt4, r4, t_s)
    hist_s = jnp.pad(state_pool[l], ((0, 0), (HIST_ROWS - POOL_PAD, 0), (0, 0))).reshape(b_s * HIST_ROWS, POOL_WIDTH)
    xg_s, route_s, cnt_s = _mix(a, pin, hist_s, lambda i: (i, 0), xs, t_s, p_len, 0, False, pw, ps, wo, g2, wrh, wrl,
                                br)
    k_sample = kf.reshape(depth, b_s, t_s, N_HEADS, HEAD_DIM)
    v_sample = vf.reshape(depth, b_s, t_s, N_HEADS, HEAD_DIM)
    logf_sample = logf[:, :N_HEADS].reshape(depth, b_s, t_s, N_HEADS)
    pool_sample = pin.reshape(b_s, t_s, POOL_WIDTH)[:, t_s - POOL_PAD:].reshape(depth, b_s, POOL_PAD, POOL_WIDTH)

    dest_p, dest_s, tile_group, tile_valid, m_pad = _moe_plan(route_p, cnt_p, route_s, cnt_s, tile)
    rows = jnp.zeros((m_pad, ROW_WIDTH), F32)
    rows = _scatter_rows(dest_p, xg_p, rows, tile)
    rows = _scatter_rows(dest_s, xg_s, rows, tile)
    ys = _moe_sorted(tile_group, tile_valid, rows, g2, wg4, wu4, wd4, tile)
    y_prompt = _gather_rows(dest_p, ys, tile).reshape(b_p, s_p, D_MODEL)
    y_sample = _gather_rows(dest_s, ys, tile).reshape(b_s, t_s, D_MODEL)

    return (y_prompt, y_sample, k_prompt, v_prompt, logf_prompt, pool_prompt,
            k_sample, v_sample, logf_sample, pool_sample)
```

```python
import functools

import jax
import jax.numpy as jnp
from jax import lax
from jax.experimental import pallas as pl
from jax.experimental.pallas import tpu as pltpu

F32 = jnp.float32
BF16 = jnp.bfloat16

D_MODEL = 1024
ATTN_WIDTH = 512
N_HEADS = 8
HEAD_DIM = 64
POOL_WIDTH = 512
POOL_WINDOWS = (2, 4, 8, 16)
POOL_GROUP_DIM = 128
POOL_PAD = 15
HIST_ROWS = 16
N_GROUPS = 4
EXPERTS_PER_GROUP = 4
N_EXPERTS = 16
D_EXPERT = 512
EPS = 1e-6
LANES = 128
SUBLANES = 8
PAIR = 2 * HEAD_DIM
N_PAIRS = N_HEADS // 2
ROWSUM_ROW = (HEAD_DIM, 0)
VMEM_LIMIT = 56 * 1024 * 1024
ROW_WIDTH = D_MODEL + LANES
ROUTE_GROUP_LANE = N_EXPERTS
ROUTE_RANK_LANE = N_EXPERTS + 1


def _cparams(sem):
    return pltpu.CompilerParams(dimension_semantics=sem, vmem_limit_bytes=VMEM_LIMIT)


def _split3(x):
    hi = x.astype(BF16)
    r1 = x - hi.astype(F32)
    mid = r1.astype(BF16)
    lo = (r1 - mid.astype(F32)).astype(BF16)
    return hi, mid, lo


def _split2(x):
    hi = x.astype(BF16)
    lo = (x - hi.astype(F32)).astype(BF16)
    return hi, lo


def _inproj_kernel(x_ref, g1_ref, w_ref, bf_ref, qg_ref, kg_ref, hm_ref, tri_ref, place_ref,
                   q_ref, kf_ref, kb_ref, vf_ref, vb_ref, ka_ref, vt_ref, p_ref, logf_ref, c_ref, carry_ref, *,
                   carry_rows, key_blk):
    x = x_ref[...]
    r = lax.rsqrt(jnp.mean(x * x, axis=-1, keepdims=True) + EPS)
    n = ((x * r) * g1_ref[...]).astype(BF16)
    z = jnp.dot(n, w_ref[...], preferred_element_type=F32)

    hm = hm_ref[...]

    def head_norm(zz, g):
        hi, lo = _split2(zz * zz)
        ms = jnp.dot(hi, hm, preferred_element_type=F32) + jnp.dot(lo, hm, preferred_element_type=F32)
        return (zz * lax.rsqrt(ms + EPS)) * g

    q = head_norm(z[:, 0:ATTN_WIDTH], qg_ref[...])
    q_ref[...] = (q * (HEAD_DIM ** -0.5)).astype(BF16)
    k = head_norm(z[:, ATTN_WIDTH:2 * ATTN_WIDTH], kg_ref[...])
    kf_ref[...] = k
    kb_ref[...] = k.astype(BF16)
    v = z[:, 2 * ATTN_WIDTH:3 * ATTN_WIDTH]
    vf_ref[...] = v
    vb_ref[...] = v.astype(BF16)
    p_ref[...] = z[:, 3 * ATTN_WIDTH:3 * ATTN_WIDTH + POOL_WIDTH]
    prow = lax.broadcasted_iota(jnp.int32, (PAIR, v.shape[0]), 0)
    for hp in range(N_PAIRS):
        vpair_t = v[:, hp * PAIR:(hp + 1) * PAIR].T
        for h in range(2):
            own = (prow < HEAD_DIM) if h == 0 else (prow >= HEAD_DIM)
            marker = jnp.where(prow == ROWSUM_ROW[h], 1.0, 0.0)
            vt_ref[(2 * hp + h) * LANES:(2 * hp + h + 1) * LANES, :] = jnp.where(own, vpair_t, marker).astype(BF16)

    f = z[:, 3 * ATTN_WIDTH + POOL_WIDTH:] + bf_ref[...]
    lane = lax.broadcasted_iota(jnp.int32, f.shape, 1)
    logf = -(jnp.maximum(-f, 0.0) + jnp.log1p(jnp.exp(-jnp.abs(f))))
    logf = jnp.where(lane < N_HEADS, logf, 0.0)
    logf_ref[...] = logf

    tri = tri_ref[...]
    hi, mid, lo = _split3(logf)
    c = (jnp.dot(tri, hi, preferred_element_type=F32) + jnp.dot(tri, mid, preferred_element_type=F32)
         + jnp.dot(tri, lo, preferred_element_type=F32))
    if carry_rows:
        @pl.when(pl.program_id(0) == 0)
        def _():
            carry_ref[...] = jnp.zeros_like(carry_ref)
        c = c + carry_ref[0:1, :]
        carry_ref[...] = jnp.broadcast_to(c[c.shape[0] - 1:, :], carry_ref.shape)
    c_ref[...] = c

    tm = c.shape[0]
    rel = jnp.concatenate([c[r:r + key_blk, :] - c[r:r + 1, :] for r in range(0, tm, key_blk)], axis=0)
    parts = jnp.concatenate(_split3(rel), axis=1)
    decay = jnp.dot(parts, place_ref[...], preferred_element_type=F32).astype(BF16)
    kb = k.astype(BF16)
    for hp in range(N_PAIRS):
        ka_ref[:, 2 * hp * LANES:(2 * hp + 1) * LANES] = kb[:, hp * PAIR:(hp + 1) * PAIR]
        ka_ref[:, (2 * hp + 1) * LANES:(2 * hp + 2) * LANES] = decay[:, hp * LANES:(hp + 1) * LANES]


def _decay_lane(h, part):
    return 3 * h + part


def _in_project(x2d, tm, seg, carry_rows, key_blk, g1, w_all, bf_pad, qg, kg, hm):
    n = x2d.shape[0]
    rows = lax.broadcasted_iota(jnp.int32, (tm, tm), 0)
    cols = lax.broadcasted_iota(jnp.int32, (tm, tm), 1)
    tri = ((cols <= rows) & (rows // seg == cols // seg)).astype(BF16)
    src = jnp.arange(3 * LANES)
    part, head = src // LANES, src % LANES
    dst = (head // 2) * LANES + _decay_lane(head % 2, part)
    place = ((jnp.arange(N_PAIRS * LANES)[None, :] == dst[:, None]) & (head < N_HEADS)[:, None]).astype(BF16)
    row_blk = lambda w: pl.BlockSpec((tm, w), lambda i: (i, 0))
    const = lambda a: pl.BlockSpec(a.shape, lambda i: (0,) * a.ndim)
    out_shape = (
        jax.ShapeDtypeStruct((n, ATTN_WIDTH), BF16),
        jax.ShapeDtypeStruct((n, ATTN_WIDTH), F32),
        jax.ShapeDtypeStruct((n, ATTN_WIDTH), BF16),
        jax.ShapeDtypeStruct((n, ATTN_WIDTH), F32),
        jax.ShapeDtypeStruct((n, ATTN_WIDTH), BF16),
        jax.ShapeDtypeStruct((n, N_PAIRS * 2 * LANES), BF16),
        jax.ShapeDtypeStruct((N_HEADS * LANES, n), BF16),
        jax.ShapeDtypeStruct((n, POOL_WIDTH), F32),
        jax.ShapeDtypeStruct((n, LANES), F32),
        jax.ShapeDtypeStruct((n, LANES), F32),
    )
    return pl.pallas_call(
        functools.partial(_inproj_kernel, carry_rows=carry_rows, key_blk=key_blk),
        grid=(n // tm,),
        in_specs=[row_blk(D_MODEL), const(g1), const(w_all), const(bf_pad), const(qg), const(kg), const(hm),
                  const(tri), const(place)],
        out_specs=[row_blk(ATTN_WIDTH)] * 5 + [row_blk(N_PAIRS * 2 * LANES),
                                                pl.BlockSpec((N_HEADS * LANES, tm), lambda i: (0, i)),
                                                row_blk(POOL_WIDTH), row_blk(LANES), row_blk(LANES)],
        out_shape=out_shape,
        scratch_shapes=[pltpu.VMEM((8, LANES), F32)],
        compiler_params=_cparams(("arbitrary",)),
        name="in_project",
    )(x2d, g1, w_all, bf_pad, qg, kg, hm, tri, place)


def _attn_prompt_kernel(j0_ref, cb_ref, q_ref, k_ref, vt_ref, o_ref, acc_ref, m_ref, *, tq, tk):
    hp = pl.program_id(0)
    i = pl.program_id(1)
    nk = tq // tk
    lane = lax.broadcasted_iota(jnp.int32, (1, PAIR), 1)
    first = lane < HEAD_DIM
    q = q_ref[...]
    zero = jnp.zeros_like(q)
    qw = []
    for h in range(2):
        minus = jnp.zeros((1, LANES), F32)
        for part in range(3):
            minus = jnp.where(lane == _decay_lane(h, part), -1.0, minus)
        qw.append(jnp.concatenate([jnp.where(first == (h == 0), q, zero),
                                   jnp.broadcast_to(minus, (tq, LANES)).astype(BF16)], axis=1))
    q_start = i * tq

    acc_ref[...] = jnp.zeros_like(acc_ref)
    m_ref[...] = jnp.full_like(m_ref, -jnp.inf)

    last = (i + 1) * nk - 1

    def scores(h, j, masked=True):
        start = pl.multiple_of(jnp.minimum(j, last) * tk, tk)
        k = k_ref[pl.ds(start, tk), :]
        s = lax.dot_general(k, qw[h], (((1,), (1,)), ((), ())), preferred_element_type=F32)
        if masked:
            key = lax.broadcasted_iota(jnp.int32, s.shape, 0)
            qry = lax.broadcasted_iota(jnp.int32, s.shape, 1)
            s = jnp.where(key - qry <= q_start - j * tk, s, -jnp.inf)
        return s, jnp.max(s, axis=0, keepdims=True)

    def update(h, j, s, colmax):
        jc = jnp.minimum(j, last)
        shift = cb_ref[jc, 2 * hp + h] - cb_ref[i * nk, 2 * hp + h]
        start = pl.multiple_of(jc * tk, tk)
        vt = vt_ref[h * LANES:(h + 1) * LANES, pl.ds(start, tk)]
        m_prev = m_ref[h]
        m_new = jnp.maximum(m_prev, colmax - shift)
        alpha = jnp.exp(m_prev - m_new)
        p = jnp.exp(s - (m_new + shift))
        m_ref[h] = m_new
        acc_ref[h] = acc_ref[h] * alpha + jnp.dot(vt, p.astype(BF16), preferred_element_type=F32)

    j_first = [j0_ref[i, 2 * hp + h] for h in range(2)]
    j_both = jnp.maximum(j_first[0], j_first[1])
    ahead = []
    for h in range(2):
        def body_one(j, carry, h=h):
            nxt = scores(h, j + 1)
            update(h, j, *carry)
            return nxt

        ahead.append(lax.fori_loop(j_first[h], j_both, body_one, scores(h, j_first[h])))

    def body_both(j, carry, masked):
        nxt = (scores(0, j + 1, masked), scores(1, j + 1, masked))
        update(0, j, *carry[0])
        update(1, j, *carry[1])
        return nxt

    j_mask = jnp.maximum(j_both, i * nk - 1)
    carry = lax.fori_loop(j_both, j_mask, functools.partial(body_both, masked=False), (ahead[0], ahead[1]))
    lax.fori_loop(j_mask, last + 1, functools.partial(body_both, masked=True), carry)

    out = [acc_ref[h] / acc_ref[h][ROWSUM_ROW[h]:ROWSUM_ROW[h] + 1, :] for h in range(2)]
    row = lax.broadcasted_iota(jnp.int32, (PAIR, 1), 0)
    o_ref[...] = jnp.where(row < HEAD_DIM, out[0], out[1]).T.astype(o_ref.dtype)


def _attention_prompt(j0, c_blocks, q, ka, vt, tq, tk):
    n = q.shape[0]
    grid_spec = pltpu.PrefetchScalarGridSpec(
        num_scalar_prefetch=2,
        grid=(N_PAIRS, n // tq),
        in_specs=[
            pl.BlockSpec((tq, PAIR), lambda hp, i, j0, cb: (i, hp)),
            pl.BlockSpec((n, 2 * LANES), lambda hp, i, j0, cb: (0, hp)),
            pl.BlockSpec((2 * LANES, n), lambda hp, i, j0, cb: (hp, 0)),
        ],
        out_specs=pl.BlockSpec((tq, PAIR), lambda hp, i, j0, cb: (i, hp)),
        scratch_shapes=[pltpu.VMEM((2, LANES, tq), F32), pltpu.VMEM((2, 1, tq), F32)],
    )
    return pl.pallas_call(
        functools.partial(_attn_prompt_kernel, tq=tq, tk=tk),
        grid_spec=grid_spec,
        out_shape=jax.ShapeDtypeStruct((n, ATTN_WIDTH), BF16),
        compiler_params=_cparams(("arbitrary", "arbitrary")),
        name="attention_prompt",
    )(j0, c_blocks, q, ka, vt)


def _first_key_blocks(c, q_gain, k_gain, tq, tk):
    n = c.shape[0]
    qk_bound = 1.02 * HEAD_DIM ** 0.5 * jnp.max(jnp.abs(q_gain)) * jnp.max(jnp.abs(k_gain))
    threshold = 104.0 + 2.0 * qk_bound + 1.0
    c_start = c[0::tq, :N_HEADS]
    c_end = c[tk - 1::tk, :N_HEADS]
    gap = c_start[:, None, :] - c_end[None, :, :]
    earlier = (jnp.arange(n // tk)[None, :, None] + 1) * tk <= jnp.arange(n // tq)[:, None, None] * tq
    return jnp.sum((gap < -threshold) & earlier, axis=1).astype(jnp.int32)


def _suffix_kernel(x_ref, o_ref):
    x = x_ref[...]
    p_len = x.shape[1]
    rows = lax.broadcasted_iota(jnp.int32, (p_len, p_len), 0)
    cols = lax.broadcasted_iota(jnp.int32, (p_len, p_len), 1)
    u = (rows > cols).astype(BF16)
    hi, mid, lo = _split3(x)
    o_ref[...] = (jnp.dot(hi, u, preferred_element_type=F32) + jnp.dot(mid, u, preferred_element_type=F32)
                  + jnp.dot(lo, u, preferred_element_type=F32))


def _suffix_sums(x):
    return pl.pallas_call(
        _suffix_kernel,
        out_shape=jax.ShapeDtypeStruct(x.shape, F32),
        compiler_params=pltpu.CompilerParams(vmem_limit_bytes=VMEM_LIMIT),
        name="cache_suffix_sums",
    )(x)


def _attn_sample_kernel(q_ref, kn_ref, vn_ref, ck_ref, cv_ref, d_ref, dt_ref, r_ref, o_ref):
    hp = pl.program_id(1)
    lane = lax.broadcasted_iota(jnp.int32, (1, PAIR), 1)
    first = lane < HEAD_DIM
    q = q_ref[...]
    zero = jnp.zeros_like(q)
    qh = (jnp.where(first, q, zero), jnp.where(first, zero, q))
    dblk = d_ref[...]
    dlane = lax.broadcasted_iota(jnp.int32, dblk.shape, 1)
    kn = kn_ref[...]
    vn = vn_ref[...]
    ck = ck_ref[...].astype(BF16)
    cv = cv_ref[...].astype(BF16)
    nt = (((1,), (1,)), ((), ()))
    outs, ls = [], []
    for h in range(2):
        dq = jnp.sum(jnp.where(dlane == 2 * hp + h, dblk, 0.0), axis=1, keepdims=True)
        s1 = lax.dot_general(qh[h], ck, nt, preferred_element_type=F32) + dq + r_ref[h:h + 1, :]
        s2 = lax.dot_general(qh[h], kn, nt, preferred_element_type=F32) + dq - dt_ref[h:h + 1, :]
        row = lax.broadcasted_iota(jnp.int32, s2.shape, 0)
        col = lax.broadcasted_iota(jnp.int32, s2.shape, 1)
        s2 = jnp.where(col <= row, s2, -jnp.inf)
        m = jnp.maximum(jnp.max(s1, axis=1, keepdims=True), jnp.max(s2, axis=1, keepdims=True))
        p1 = jnp.exp(s1 - m)
        p2 = jnp.exp(s2 - m)
        ls.append(jnp.sum(p1, axis=1, keepdims=True) + jnp.sum(p2, axis=1, keepdims=True))
        outs.append(jnp.dot(p1.astype(BF16), cv, preferred_element_type=F32)
                    + jnp.dot(p2.astype(BF16), vn, preferred_element_type=F32))
    o_ref[...] = (jnp.where(first, outs[0], outs[1]) / jnp.where(first, ls[0], ls[1])).astype(o_ref.dtype)


def _attention_sample(q, kb, vb, cache_k, cache_v, d, dt4, r4, t):
    nb, p_len = cache_k.shape[0], cache_k.shape[1]
    return pl.pallas_call(
        _attn_sample_kernel,
        grid=(nb, N_PAIRS),
        in_specs=[
            pl.BlockSpec((t, PAIR), lambda b, hp: (b, hp)),
            pl.BlockSpec((t, PAIR), lambda b, hp: (b, hp)),
            pl.BlockSpec((t, PAIR), lambda b, hp: (b, hp)),
            pl.BlockSpec((None, p_len, PAIR), lambda b, hp: (b, 0, hp)),
            pl.BlockSpec((None, p_len, PAIR), lambda b, hp: (b, 0, hp)),
            pl.BlockSpec((t, LANES), lambda b, hp: (b, 0)),
            pl.BlockSpec((None, None, 2, t), lambda b, hp: (b, hp, 0, 0)),
            pl.BlockSpec((None, None, 2, p_len), lambda b, hp: (b, hp, 0, 0)),
        ],
        out_specs=pl.BlockSpec((t, PAIR), lambda b, hp: (b, hp)),
        out_shape=jax.ShapeDtypeStruct((nb * t, ATTN_WIDTH), BF16),
        compiler_params=_cparams(("arbitrary", "arbitrary")),
        name="attention_sample",
    )(q, kb, vb, cache_k, cache_v, d, dt4, r4)


def _mix_kernel(a_ref, p_ref, hist_ref, x_ref, pw_ref, ps_ref, wo_ref, g2_ref, wrh_ref, wrl_ref, br_ref, tri_ref,
                xg_ref, route_ref, cnt_ref, *, pos0, pos_stride, zero_first):
    i = pl.program_id(0)
    t = p_ref.shape[0]
    p = p_ref[...]
    hist = hist_ref[...]
    if zero_first:
        hist = jnp.where(i == 0, 0.0, hist)
    xh = jnp.concatenate([hist, p], axis=0)
    pos = pos0 + i * pos_stride + lax.broadcasted_iota(jnp.int32, (t, 1), 0)
    ys = []
    for g, w in enumerate(POOL_WINDOWS):
        col = xh[:, g * POOL_GROUP_DIM:(g + 1) * POOL_GROUP_DIM]
        acc = col
        span = 1
        while span < w:
            acc = acc + pltpu.roll(acc, span, axis=0)
            span *= 2
        cnt = jnp.minimum(pos + 1, w).astype(F32)
        dgrp = acc[HIST_ROWS:, :] / cnt - col[HIST_ROWS:, :]
        ys.append(jnp.dot(dgrp.astype(BF16), pw_ref[g], preferred_element_type=F32))
    pm = (jnp.concatenate(ys, axis=1) * ps_ref[...]).astype(BF16)
    mix = (jnp.dot(a_ref[...], wo_ref[0:ATTN_WIDTH, :], preferred_element_type=F32)
           + jnp.dot(pm, wo_ref[ATTN_WIDTH:, :], preferred_element_type=F32))
    x1 = x_ref[...] + mix
    xg_ref[:, 0:D_MODEL] = x1

    r = lax.rsqrt(jnp.mean(x1 * x1, axis=-1, keepdims=True) + EPS)
    n2 = (x1 * r) * g2_ref[...]

    nh, nl = _split2(n2)
    wrh = wrh_ref[...]
    logits = (jnp.dot(nh, wrh, preferred_element_type=F32) + jnp.dot(nl, wrh, preferred_element_type=F32)
              + jnp.dot(nh, wrl_ref[...], preferred_element_type=F32)) + br_ref[...]
    lane = lax.broadcasted_iota(jnp.int32, logits.shape, 1).astype(F32)
    neg = -jnp.inf
    big = jnp.float32(1 << 20)
    is_g = (lane >= N_EXPERTS) & (lane < N_EXPERTS + N_GROUPS)
    lg = jnp.where(is_g, logits, neg)
    gmax = jnp.max(lg, axis=1, keepdims=True)
    g_lane = jnp.min(jnp.where(lg == gmax, lane, big), axis=1, keepdims=True)
    top_pg = 1.0 / jnp.sum(jnp.exp(lg - gmax), axis=1, keepdims=True)
    g_idx = g_lane - N_EXPERTS
    sel = (lane >= g_idx * EXPERTS_PER_GROUP) & (lane < (g_idx + 1) * EXPERTS_PER_GROUP)
    le = jnp.where(sel, logits, neg)
    m1 = jnp.max(le, axis=1, keepdims=True)
    i1 = jnp.min(jnp.where(le == m1, lane, big), axis=1, keepdims=True)
    le2 = jnp.where(lane == i1, neg, le)
    m2 = jnp.max(le2, axis=1, keepdims=True)
    i2 = jnp.min(jnp.where(le2 == m2, lane, big), axis=1, keepdims=True)
    e2 = jnp.exp(m2 - m1)
    w1 = top_pg / (1.0 + e2)
    w2 = top_pg * e2 / (1.0 + e2)
    gate = jnp.where(lane == i1, w1, 0.0) + jnp.where(lane == i2, w2, 0.0)

    @pl.when(i == 0)
    def _():
        cnt_ref[...] = jnp.zeros_like(cnt_ref)
    member = lane == g_idx
    before = jnp.dot(tri_ref[...], member.astype(BF16), preferred_element_type=F32) + cnt_ref[0:1, :]
    rank = jnp.sum(jnp.where(member, before, 0.0), axis=1, keepdims=True)
    cnt_ref[...] = cnt_ref[...] + jnp.sum(member.astype(F32), axis=0, keepdims=True)
    route = jnp.where(lane == ROUTE_GROUP_LANE, g_idx, jnp.where(lane == ROUTE_RANK_LANE, rank, gate))
    route_ref[...] = route
    xg_ref[:, D_MODEL:] = route


def _mix(a, p, hist_src, hist_map, x2d, t, pos0, pos_stride, zero_first, pw, ps, wo, g2, wrh, wrl, br):
    n = x2d.shape[0]
    rows = lax.broadcasted_iota(jnp.int32, (t, t), 0)
    cols = lax.broadcasted_iota(jnp.int32, (t, t), 1)
    tri = (cols < rows).astype(BF16)
    row_blk = lambda w: pl.BlockSpec((t, w), lambda i: (i, 0))
    const = lambda arr: pl.BlockSpec(arr.shape, lambda i: (0,) * arr.ndim)
    return pl.pallas_call(
        functools.partial(_mix_kernel, pos0=pos0, pos_stride=pos_stride, zero_first=zero_first),
        grid=(n // t,),
        in_specs=[row_blk(ATTN_WIDTH), row_blk(POOL_WIDTH), pl.BlockSpec((HIST_ROWS, POOL_WIDTH), hist_map),
                  row_blk(D_MODEL), const(pw), const(ps), const(wo), const(g2), const(wrh), const(wrl), const(br),
                  const(tri)],
        out_specs=[row_blk(ROW_WIDTH), row_blk(LANES), pl.BlockSpec((8, LANES), lambda i: (0, 0))],
        out_shape=(jax.ShapeDtypeStruct((n, ROW_WIDTH), F32), jax.ShapeDtypeStruct((n, LANES), F32),
                   jax.ShapeDtypeStruct((8, LANES), F32)),
        compiler_params=_cparams(("arbitrary",)),
        name="pool_mix_route",
    )(a, p, hist_src, x2d, pw, ps, wo, g2, wrh, wrl, br, tri)


def _scatter_rows_kernel(dest_ref, x_ref, xs_in_ref, xs_ref, sem):
    del xs_in_ref
    t = x_ref.shape[0]

    def issue(g, carry):
        base = pl.multiple_of(g * SUBLANES, SUBLANES)
        tile_rows = x_ref.at[pl.ds(base, SUBLANES)]
        for u in range(SUBLANES):
            pltpu.make_async_copy(tile_rows.at[pl.ds(u, 1)], xs_ref.at[pl.ds(dest_ref[base + u], 1)],
                                  sem).start(priority=u % 2)
        return carry

    lax.fori_loop(0, t // SUBLANES, issue, 0)
    pltpu.make_async_copy(x_ref, xs_ref.at[pl.ds(0, t)], sem).wait()


def _scatter_rows(dest, x, xs, t):
    n, w = x.shape
    return pl.pallas_call(
        _scatter_rows_kernel,
        grid=(n // t,),
        in_specs=[pl.BlockSpec((t,), lambda i: (i,), memory_space=pltpu.SMEM),
                  pl.BlockSpec((t, w), lambda i: (i, 0)),
                  pl.BlockSpec(memory_space=pl.ANY)],
        out_specs=pl.BlockSpec(memory_space=pl.ANY),
        out_shape=jax.ShapeDtypeStruct(xs.shape, xs.dtype),
        scratch_shapes=[pltpu.SemaphoreType.DMA],
        input_output_aliases={2: 0},
        compiler_params=_cparams(("arbitrary",)),
        name="moe_scatter_rows",
    )(dest, x, xs)


def _gather_rows_kernel(dest_ref, ys_ref, y_ref, sem):
    t = y_ref.shape[0]

    def issue(g, carry):
        base = pl.multiple_of(g * SUBLANES, SUBLANES)
        tile_rows = y_ref.at[pl.ds(base, SUBLANES)]
        for u in range(SUBLANES):
            pltpu.make_async_copy(ys_ref.at[pl.ds(dest_ref[base + u], 1)], tile_rows.at[pl.ds(u, 1)],
                                  sem).start(priority=u % 2)
        return carry

    lax.fori_loop(0, t // SUBLANES, issue, 0)
    pltpu.make_async_copy(ys_ref.at[pl.ds(0, t)], y_ref, sem).wait()


def _gather_rows(dest, ys, t):
    n = dest.shape[0]
    w = ys.shape[1]
    return pl.pallas_call(
        _gather_rows_kernel,
        grid=(n // t,),
        in_specs=[pl.BlockSpec((t,), lambda i: (i,), memory_space=pltpu.SMEM),
                  pl.BlockSpec(memory_space=pl.ANY)],
        out_specs=pl.BlockSpec((t, w), lambda i: (i, 0)),
        out_shape=jax.ShapeDtypeStruct((n, w), ys.dtype),
        scratch_shapes=[pltpu.SemaphoreType.DMA],
        compiler_params=_cparams(("arbitrary",)),
        name="moe_gather_rows",
    )(dest, ys)


def _moe_kernel(tg_ref, tv_ref, xs_ref, g2_ref, wg_ref, wu_ref, wd_ref, y_ref):
    m = pl.program_id(0)

    @pl.when(tv_ref[m] > 0)
    def _():
        x1 = xs_ref[:, 0:D_MODEL]
        route = xs_ref[:, D_MODEL:]
        lane = lax.broadcasted_iota(jnp.int32, route.shape, 1)
        r = lax.rsqrt(jnp.mean(x1 * x1, axis=-1, keepdims=True) + EPS)
        n2 = ((x1 * r) * g2_ref[...]).astype(BF16)
        first_expert = tg_ref[m] * EXPERTS_PER_GROUP
        acc = x1
        for e in range(EXPERTS_PER_GROUP):
            ge = jnp.sum(jnp.where(lane == first_expert + e, route, 0.0), axis=1, keepdims=True)
            hg = jnp.dot(n2, wg_ref[e], preferred_element_type=F32)
            hu = jnp.dot(n2, wu_ref[e], preferred_element_type=F32)
            h = (hg * jax.nn.sigmoid(hg)) * hu * ge
            acc = acc + jnp.dot(h.astype(BF16), wd_ref[e], preferred_element_type=F32)
        y_ref[...] = acc

    @pl.when(tv_ref[m] == 0)
    def _():
        y_ref[...] = jnp.zeros_like(y_ref)


def _moe_sorted(tile_group, tile_valid, xs, g2, wg4, wu4, wd4, tm):
    m_pad = xs.shape[0]
    grid_spec = pltpu.PrefetchScalarGridSpec(
        num_scalar_prefetch=2,
        grid=(m_pad // tm,),
        in_specs=[
            pl.BlockSpec((tm, ROW_WIDTH), lambda m, tg, tv: (m, 0)),
            pl.BlockSpec(g2.shape, lambda m, tg, tv: (0, 0)),
            pl.BlockSpec((None, EXPERTS_PER_GROUP, D_MODEL, D_EXPERT), lambda m, tg, tv: (tg[m], 0, 0, 0)),
            pl.BlockSpec((None, EXPERTS_PER_GROUP, D_MODEL, D_EXPERT), lambda m, tg, tv: (tg[m], 0, 0, 0)),
            pl.BlockSpec((None, EXPERTS_PER_GROUP, D_EXPERT, D_MODEL), lambda m, tg, tv: (tg[m], 0, 0, 0)),
        ],
        out_specs=pl.BlockSpec((tm, D_MODEL), lambda m, tg, tv: (m, 0)),
    )
    return pl.pallas_call(
        _moe_kernel,
        grid_spec=grid_spec,
        out_shape=jax.ShapeDtypeStruct((m_pad, D_MODEL), F32),
        compiler_params=_cparams(("arbitrary",)),
        name="moe_experts",
    )(tile_group, tile_valid, xs, g2, wg4, wu4, wd4)


def _moe_plan(route_p, cnt_p, route_s, cnt_s, tm):
    n_total = route_p.shape[0] + route_s.shape[0]
    n_tiles = n_total // tm + N_GROUPS
    cp = cnt_p[0, :N_GROUPS].astype(jnp.int32)
    cs = cnt_s[0, :N_GROUPS].astype(jnp.int32)
    total = cp + cs
    tiles = (total + tm - 1) // tm
    tile_end = jnp.cumsum(tiles)
    offset = (tile_end - tiles) * tm
    gp = route_p[:, ROUTE_GROUP_LANE].astype(jnp.int32)
    gs = route_s[:, ROUTE_GROUP_LANE].astype(jnp.int32)
    dest_p = offset[gp] + route_p[:, ROUTE_RANK_LANE].astype(jnp.int32)
    dest_s = offset[gs] + cp[gs] + route_s[:, ROUTE_RANK_LANE].astype(jnp.int32)
    tile_id = jnp.arange(n_tiles)
    tile_group = jnp.minimum(jnp.sum(tile_id[:, None] >= tile_end[None, :], axis=1), N_GROUPS - 1).astype(jnp.int32)
    used = jnp.clip(offset[tile_group] + total[tile_group] - tile_id * tm, 0, tm)
    tile_valid = jnp.where(tile_id < tile_end[N_GROUPS - 1], used, 0).astype(jnp.int32)
    return dest_p, dest_s, tile_group, tile_valid, n_tiles * tm


def kernel(x_prompt, x_sample, cache_k, cache_v, cache_logf, state_pool, norm1_g, w_in, b_f, q_norm_g, k_norm_g,
           pool_w, pool_scale, w_out, norm2_g, w_router_group, b_router_group, w_router_expert, b_router_expert,
           w_gate, w_up, w_down):
    depth = norm1_g.shape[0]
    assert depth == 1, "single-layer step"
    b_p, s_p, _ = x_prompt.shape
    assert b_p == 1, "prompt kernels assume one stream"
    b_s, t_s, _ = x_sample.shape
    p_len = cache_k.shape[2]
    l = 0

    w = w_in[l]
    a3 = 3 * ATTN_WIDTH
    w_all = jnp.concatenate(
        [w[:, :a3], w[:, a3 + N_HEADS:], jnp.pad(w[:, a3:a3 + N_HEADS], ((0, 0), (0, LANES - N_HEADS)))],
        axis=1).astype(BF16)
    bf_pad = jnp.pad(b_f[l], (0, LANES - N_HEADS)).reshape(1, LANES)
    g1 = norm1_g[l].reshape(1, D_MODEL)
    qg = jnp.tile(q_norm_g[l], N_HEADS).reshape(1, ATTN_WIDTH)
    kg = jnp.tile(k_norm_g[l], N_HEADS).reshape(1, ATTN_WIDTH)
    hr = lax.broadcasted_iota(jnp.int32, (ATTN_WIDTH, ATTN_WIDTH), 0) // HEAD_DIM
    hc = lax.broadcasted_iota(jnp.int32, (ATTN_WIDTH, ATTN_WIDTH), 1) // HEAD_DIM
    hm = jnp.where(hr == hc, 1.0 / HEAD_DIM, 0.0).astype(BF16)
    pw = pool_w[l].astype(BF16)
    ps = pool_scale[l].reshape(1, POOL_WIDTH)
    wo = w_out[l].astype(BF16)
    g2 = norm2_g[l].reshape(1, D_MODEL)
    w_r = jnp.pad(jnp.concatenate([w_router_expert[l], w_router_group[l]], axis=1),
                  ((0, 0), (0, LANES - N_EXPERTS - N_GROUPS)))
    wrh = w_r.astype(BF16)
    wrl = (w_r - wrh.astype(F32)).astype(BF16)
    br = jnp.pad(jnp.concatenate([b_router_expert[l], b_router_group[l]]),
                 (0, LANES - N_EXPERTS - N_GROUPS)).reshape(1, LANES)
    wg4 = w_gate[l].astype(BF16).reshape(N_GROUPS, EXPERTS_PER_GROUP, D_MODEL, D_EXPERT)
    wu4 = w_up[l].astype(BF16).reshape(N_GROUPS, EXPERTS_PER_GROUP, D_MODEL, D_EXPERT)
    wd4 = w_down[l].astype(BF16).reshape(N_GROUPS, EXPERTS_PER_GROUP, D_EXPERT, D_MODEL)

    tile = 512
    key_tile = 256

    xp = x_prompt.reshape(s_p, D_MODEL)
    q, kf, _, vf, _, ka, vt, pin, logf, c = _in_project(xp, tile, tile, True, key_tile, g1, w_all, bf_pad, qg, kg, hm)
    c_blocks = c[0::key_tile, :N_HEADS]
    j0 = _first_key_blocks(c, q_norm_g[l], k_norm_g[l], tile, key_tile)
    a = _attention_prompt(j0, c_blocks, q, ka, vt, tile, key_tile)
    hist_map = lambda i: (jnp.maximum(i * (tile // HIST_ROWS) - 1, 0), 0)
    xg_p, route_p, cnt_p = _mix(a, pin, pin, hist_map, xp, tile, 0, tile, True, pw, ps, wo, g2, wrh, wrl, br)
    k_prompt = kf.reshape(depth, b_p, s_p, N_HEADS, HEAD_DIM)
    v_prompt = vf.reshape(depth, b_p, s_p, N_HEADS, HEAD_DIM)
    logf_prompt = logf[:, :N_HEADS].reshape(depth, b_p, s_p, N_HEADS)
    pool_prompt = pin[s_p - POOL_PAD:].reshape(depth, b_p, POOL_PAD, POOL_WIDTH)

    n_s = b_s * t_s
    xs = x_sample.reshape(n_s, D_MODEL)
    q, kf, kb, vf, vb, _, _, pin, logf, d = _in_project(xs, tile, t_s, False, key_tile, g1, w_all, bf_pad, qg, kg, hm)
    dt4 = d[:, :N_HEADS].reshape(b_s, t_s, N_PAIRS, 2).transpose(0, 2, 3, 1)
    clf = cache_logf[l].astype(F32).transpose(0, 2, 1).reshape(b_s * N_HEADS, p_len)
    r4 = _suffix_sums(clf).reshape(b_s, N_PAIRS, 2, p_len)
    ck = cache_k[l].reshape(b_s, p_len, ATTN_WIDTH)
    cv = cache_v[l].reshape(b_s, p_len, ATTN_WIDTH)
    a = _attention_sample(q, kb, vb, ck, cv, d, dt4, r4, t_s)
    hist_s = jnp.pad(state_pool[l], ((0, 0), (HIST_ROWS - POOL_PAD, 0), (0, 0))).reshape(b_s * HIST_ROWS, POOL_WIDTH)
    xg_s, route_s, cnt_s = _mix(a, pin, hist_s, lambda i: (i, 0), xs, t_s, p_len, 0, False, pw, ps, wo, g2, wrh, wrl,
                                br)
    k_sample = kf.reshape(depth, b_s, t_s, N_HEADS, HEAD_DIM)
    v_sample = vf.reshape(depth, b_s, t_s, N_HEADS, HEAD_DIM)
    logf_sample = logf[:, :N_HEADS].reshape(depth, b_s, t_s, N_HEADS)
    pool_sample = pin.reshape(b_s, t_s, POOL_WIDTH)[:, t_s - POOL_PAD:].reshape(depth, b_s, POOL_PAD, POOL_WIDTH)

    dest_p, dest_s, tile_group, tile_valid, m_pad = _moe_plan(route_p, cnt_p, route_s, cnt_s, tile)
    rows = jnp.zeros((m_pad, ROW_WIDTH), F32)
    rows = _scatter_rows(dest_p, xg_p, rows, tile)
    rows = _scatter_rows(dest_s, xg_s, rows, tile)
    ys = _moe_sorted(tile_group, tile_valid, rows, g2, wg4, wu4, wd4, tile)
    y_prompt = _gather_rows(dest_p, ys, tile).reshape(b_p, s_p, D_MODEL)
    y_sample = _gather_rows(dest_s, ys, tile).reshape(b_s, t_s, D_MODEL)

    return (y_prompt, y_sample, k_prompt, v_prompt, logf_prompt, pool_prompt,
            k_sample, v_sample, logf_sample, pool_sample)
```

```python
import functools

import jax
import jax.numpy as jnp
from jax import lax
from jax.experimental import pallas as pl
from jax.experimental.pallas import tpu as pltpu

F32 = jnp.float32
BF16 = jnp.bfloat16

D_MODEL = 1024
ATTN_WIDTH = 512
N_HEADS = 8
HEAD_DIM = 64
POOL_WIDTH = 512
POOL_WINDOWS = (2, 4, 8, 16)
POOL_GROUP_DIM = 128
POOL_PAD = 15
HIST_ROWS = 16
N_GROUPS = 4
EXPERTS_PER_GROUP = 4
N_EXPERTS = 16
D_EXPERT = 512
EPS = 1e-6
LANES = 128
SUBLANES = 8
PAIR = 2 * HEAD_DIM
N_PAIRS = N_HEADS // 2
ROWSUM_ROW = (HEAD_DIM, 0)
VMEM_LIMIT = 56 * 1024 * 1024
ROW_WIDTH = D_MODEL + LANES
ROUTE_GROUP_LANE = N_EXPERTS
ROUTE_RANK_LANE = N_EXPERTS + 1


def _cparams(sem):
    return pltpu.CompilerParams(dimension_semantics=sem, vmem_limit_bytes=VMEM_LIMIT)


def _split3(x):
    hi = x.astype(BF16)
    r1 = x - hi.astype(F32)
    mid = r1.astype(BF16)
    lo = (r1 - mid.astype(F32)).astype(BF16)
    return hi, mid, lo


def _split2(x):
    hi = x.astype(BF16)
    lo = (x - hi.astype(F32)).astype(BF16)
    return hi, lo


def _inproj_kernel(x_ref, g1_ref, w_ref, bf_ref, qg_ref, kg_ref, hm_ref, tri_ref, place_ref,
                   q_ref, kf_ref, kb_ref, vf_ref, vb_ref, ka_ref, vt_ref, p_ref, logf_ref, c_ref, carry_ref, *,
                   carry_rows, key_blk):
    x = x_ref[...]
    r = lax.rsqrt(jnp.mean(x * x, axis=-1, keepdims=True) + EPS)
    n = ((x * r) * g1_ref[...]).astype(BF16)
    z = jnp.dot(n, w_ref[...], preferred_element_type=F32)

    hm = hm_ref[...]

    def head_norm(zz, g):
        hi, lo = _split2(zz * zz)
        ms = jnp.dot(hi, hm, preferred_element_type=F32) + jnp.dot(lo, hm, preferred_element_type=F32)
        return (zz * lax.rsqrt(ms + EPS)) * g

    q = head_norm(z[:, 0:ATTN_WIDTH], qg_ref[...])
    q_ref[...] = (q * (HEAD_DIM ** -0.5)).astype(BF16)
    k = head_norm(z[:, ATTN_WIDTH:2 * ATTN_WIDTH], kg_ref[...])
    kf_ref[...] = k
    kb_ref[...] = k.astype(BF16)
    v = z[:, 2 * ATTN_WIDTH:3 * ATTN_WIDTH]
    vf_ref[...] = v
    vb_ref[...] = v.astype(BF16)
    p_ref[...] = z[:, 3 * ATTN_WIDTH:3 * ATTN_WIDTH + POOL_WIDTH]
    prow = lax.broadcasted_iota(jnp.int32, (PAIR, v.shape[0]), 0)
    for hp in range(N_PAIRS):
        vpair_t = v[:, hp * PAIR:(hp + 1) * PAIR].T
        for h in range(2):
            own = (prow < HEAD_DIM) if h == 0 else (prow >= HEAD_DIM)
            marker = jnp.where(prow == ROWSUM_ROW[h], 1.0, 0.0)
            vt_ref[(2 * hp + h) * LANES:(2 * hp + h + 1) * LANES, :] = jnp.where(own, vpair_t, marker).astype(BF16)

    f = z[:, 3 * ATTN_WIDTH + POOL_WIDTH:] + bf_ref[...]
    lane = lax.broadcasted_iota(jnp.int32, f.shape, 1)
    logf = -(jnp.maximum(-f, 0.0) + jnp.log1p(jnp.exp(-jnp.abs(f))))
    logf = jnp.where(lane < N_HEADS, logf, 0.0)
    logf_ref[...] = logf

    tri = tri_ref[...]
    hi, mid, lo = _split3(logf)
    c = (jnp.dot(tri, hi, preferred_element_type=F32) + jnp.dot(tri, mid, preferred_element_type=F32)
         + jnp.dot(tri, lo, preferred_element_type=F32))
    if carry_rows:
        @pl.when(pl.program_id(0) == 0)
        def _():
            carry_ref[...] = jnp.zeros_like(carry_ref)
        c = c + carry_ref[0:1, :]
        carry_ref[...] = jnp.broadcast_to(c[c.shape[0] - 1:, :], carry_ref.shape)
    c_ref[...] = c

    tm = c.shape[0]
    rel = jnp.concatenate([c[r:r + key_blk, :] - c[r:r + 1, :] for r in range(0, tm, key_blk)], axis=0)
    parts = jnp.concatenate(_split3(rel), axis=1)
    decay = jnp.dot(parts, place_ref[...], preferred_element_type=F32).astype(BF16)
    kb = k.astype(BF16)
    for hp in range(N_PAIRS):
        ka_ref[:, 2 * hp * LANES:(2 * hp + 1) * LANES] = kb[:, hp * PAIR:(hp + 1) * PAIR]
        ka_ref[:, (2 * hp + 1) * LANES:(2 * hp + 2) * LANES] = decay[:, hp * LANES:(hp + 1) * LANES]


def _decay_lane(h, part):
    return 3 * h + part


def _in_project(x2d, tm, seg, carry_rows, key_blk, g1, w_all, bf_pad, qg, kg, hm):
    n = x2d.shape[0]
    rows = lax.broadcasted_iota(jnp.int32, (tm, tm), 0)
    cols = lax.broadcasted_iota(jnp.int32, (tm, tm), 1)
    tri = ((cols <= rows) & (rows // seg == cols // seg)).astype(BF16)
    src = jnp.arange(3 * LANES)
    part, head = src // LANES, src % LANES
    dst = (head // 2) * LANES + _decay_lane(head % 2, part)
    place = ((jnp.arange(N_PAIRS * LANES)[None, :] == dst[:, None]) & (head < N_HEADS)[:, None]).astype(BF16)
    row_blk = lambda w: pl.BlockSpec((tm, w), lambda i: (i, 0))
    const = lambda a: pl.BlockSpec(a.shape, lambda i: (0,) * a.ndim)
    out_shape = (
        jax.ShapeDtypeStruct((n, ATTN_WIDTH), BF16),
        jax.ShapeDtypeStruct((n, ATTN_WIDTH), F32),
        jax.ShapeDtypeStruct((n, ATTN_WIDTH), BF16),
        jax.ShapeDtypeStruct((n, ATTN_WIDTH), F32),
        jax.ShapeDtypeStruct((n, ATTN_WIDTH), BF16),
        jax.ShapeDtypeStruct((n, N_PAIRS * 2 * LANES), BF16),
        jax.ShapeDtypeStruct((N_HEADS * LANES, n), BF16),
        jax.ShapeDtypeStruct((n, POOL_WIDTH), F32),
        jax.ShapeDtypeStruct((n, LANES), F32),
        jax.ShapeDtypeStruct((n, LANES), F32),
    )
    return pl.pallas_call(
        functools.partial(_inproj_kernel, carry_rows=carry_rows, key_blk=key_blk),
        grid=(n // tm,),
        in_specs=[row_blk(D_MODEL), const(g1), const(w_all), const(bf_pad), const(qg), const(kg), const(hm),
                  const(tri), const(place)],
        out_specs=[row_blk(ATTN_WIDTH)] * 5 + [row_blk(N_PAIRS * 2 * LANES),
                                                pl.BlockSpec((N_HEADS * LANES, tm), lambda i: (0, i)),
                                                row_blk(POOL_WIDTH), row_blk(LANES), row_blk(LANES)],
        out_shape=out_shape,
        scratch_shapes=[pltpu.VMEM((8, LANES), F32)],
        compiler_params=_cparams(("arbitrary",)),
        name="in_project",
    )(x2d, g1, w_all, bf_pad, qg, kg, hm, tri, place)


def _attn_prompt_kernel(j0_ref, cb_ref, q_ref, k_ref, vt_ref, o_ref, acc_ref, m_ref, *, tq, tk):
    hp = pl.program_id(0)
    i = pl.program_id(1)
    nk = tq // tk
    lane = lax.broadcasted_iota(jnp.int32, (1, PAIR), 1)
    first = lane < HEAD_DIM
    q = q_ref[...]
    zero = jnp.zeros_like(q)
    qw = []
    for h in range(2):
        minus = jnp.zeros((1, LANES), F32)
        for part in range(3):
            minus = jnp.where(lane == _decay_lane(h, part), -1.0, minus)
        qw.append(jnp.concatenate([jnp.where(first == (h == 0), q, zero),
                                   jnp.broadcast_to(minus, (tq, LANES)).astype(BF16)], axis=1))
    q_start = i * tq

    acc_ref[...] = jnp.zeros_like(acc_ref)
    m_ref[...] = jnp.full_like(m_ref, -jnp.inf)

    last = (i + 1) * nk - 1

    def scores(h, j, masked=True):
        start = pl.multiple_of(jnp.minimum(j, last) * tk, tk)
        k = k_ref[pl.ds(start, tk), :]
        s = lax.dot_general(k, qw[h], (((1,), (1,)), ((), ())), preferred_element_type=F32)
        if masked:
            key = lax.broadcasted_iota(jnp.int32, s.shape, 0)
            qry = lax.broadcasted_iota(jnp.int32, s.shape, 1)
            s = jnp.where(key - qry <= q_start - j * tk, s, -jnp.inf)
        return s, jnp.max(s, axis=0, keepdims=True)

    def update(h, j, s, colmax):
        jc = jnp.minimum(j, last)
        shift = cb_ref[jc, 2 * hp + h] - cb_ref[i * nk, 2 * hp + h]
        start = pl.multiple_of(jc * tk, tk)
        vt = vt_ref[h * LANES:(h + 1) * LANES, pl.ds(start, tk)]
        m_prev = m_ref[h]
        m_new = jnp.maximum(m_prev, colmax - shift)
        alpha = jnp.exp(m_prev - m_new)
        p = jnp.exp(s - (m_new + shift))
        m_ref[h] = m_new
        acc_ref[h] = acc_ref[h] * alpha + jnp.dot(vt, p.astype(BF16), preferred_element_type=F32)

    j_first = [j0_ref[i, 2 * hp + h] for h in range(2)]
    j_both = jnp.maximum(j_first[0], j_first[1])
    ahead = []
    for h in range(2):
        def body_one(j, carry, h=h):
            nxt = scores(h, j + 1)
            update(h, j, *carry)
            return nxt

        ahead.append(lax.fori_loop(j_first[h], j_both, body_one, scores(h, j_first[h])))

    def body_both(j, carry, masked):
        nxt = (scores(0, j + 1, masked), scores(1, j + 1, masked))
        update(0, j, *carry[0])
        update(1, j, *carry[1])
        return nxt

    j_mask = jnp.maximum(j_both, i * nk - 1)
    pairs = (j_mask - j_both) // 2

    def body_two(t, carry):
        j = j_both + 2 * t
        return body_both(j + 1, body_both(j, carry, False), False)

    carry = lax.fori_loop(0, pairs, body_two, (ahead[0], ahead[1]))
    lax.fori_loop(j_both + 2 * pairs, last + 1, functools.partial(body_both, masked=True), carry)

    out = [acc_ref[h] / acc_ref[h][ROWSUM_ROW[h]:ROWSUM_ROW[h] + 1, :] for h in range(2)]
    row = lax.broadcasted_iota(jnp.int32, (PAIR, 1), 0)
    o_ref[...] = jnp.where(row < HEAD_DIM, out[0], out[1]).T.astype(o_ref.dtype)


def _attention_prompt(j0, c_blocks, q, ka, vt, tq, tk):
    n = q.shape[0]
    grid_spec = pltpu.PrefetchScalarGridSpec(
        num_scalar_prefetch=2,
        grid=(N_PAIRS, n // tq),
        in_specs=[
            pl.BlockSpec((tq, PAIR), lambda hp, i, j0, cb: (i, hp)),
            pl.BlockSpec((n, 2 * LANES), lambda hp, i, j0, cb: (0, hp)),
            pl.BlockSpec((2 * LANES, n), lambda hp, i, j0, cb: (hp, 0)),
        ],
        out_specs=pl.BlockSpec((tq, PAIR), lambda hp, i, j0, cb: (i, hp)),
        scratch_shapes=[pltpu.VMEM((2, LANES, tq), F32), pltpu.VMEM((2, 1, tq), F32)],
    )
    return pl.pallas_call(
        functools.partial(_attn_prompt_kernel, tq=tq, tk=tk),
        grid_spec=grid_spec,
        out_shape=jax.ShapeDtypeStruct((n, ATTN_WIDTH), BF16),
        compiler_params=_cparams(("arbitrary", "arbitrary")),
        name="attention_prompt",
    )(j0, c_blocks, q, ka, vt)


def _first_key_blocks(c, q_gain, k_gain, tq, tk):
    n = c.shape[0]
    qk_bound = 1.02 * HEAD_DIM ** 0.5 * jnp.max(jnp.abs(q_gain)) * jnp.max(jnp.abs(k_gain))
    threshold = 104.0 + 2.0 * qk_bound + 1.0
    c_start = c[0::tq, :N_HEADS]
    c_end = c[tk - 1::tk, :N_HEADS]
    gap = c_start[:, None, :] - c_end[None, :, :]
    earlier = (jnp.arange(n // tk)[None, :, None] + 1) * tk <= jnp.arange(n // tq)[:, None, None] * tq
    return jnp.sum((gap < -threshold) & earlier, axis=1).astype(jnp.int32)


def _suffix_kernel(x_ref, o_ref):
    x = x_ref[...]
    p_len = x.shape[1]
    rows = lax.broadcasted_iota(jnp.int32, (p_len, p_len), 0)
    cols = lax.broadcasted_iota(jnp.int32, (p_len, p_len), 1)
    u = (rows > cols).astype(BF16)
    hi, mid, lo = _split3(x)
    o_ref[...] = (jnp.dot(hi, u, preferred_element_type=F32) + jnp.dot(mid, u, preferred_element_type=F32)
                  + jnp.dot(lo, u, preferred_element_type=F32))


def _suffix_sums(x):
    return pl.pallas_call(
        _suffix_kernel,
        out_shape=jax.ShapeDtypeStruct(x.shape, F32),
        compiler_params=pltpu.CompilerParams(vmem_limit_bytes=VMEM_LIMIT),
        name="cache_suffix_sums",
    )(x)


def _attn_sample_kernel(q_ref, kn_ref, vn_ref, ck_ref, cv_ref, d_ref, dt_ref, r_ref, o_ref):
    hp = pl.program_id(1)
    lane = lax.broadcasted_iota(jnp.int32, (1, PAIR), 1)
    first = lane < HEAD_DIM
    q = q_ref[...]
    zero = jnp.zeros_like(q)
    qh = (jnp.where(first, q, zero), jnp.where(first, zero, q))
    dblk = d_ref[...]
    dlane = lax.broadcasted_iota(jnp.int32, dblk.shape, 1)
    kn = kn_ref[...]
    vn = vn_ref[...]
    ck = ck_ref[...]
    cv = cv_ref[...]
    nt = (((1,), (1,)), ((), ()))
    outs, ls = [], []
    for h in range(2):
        dq = jnp.sum(jnp.where(dlane == 2 * hp + h, dblk, 0.0), axis=1, keepdims=True)
        s1 = lax.dot_general(qh[h], ck, nt, preferred_element_type=F32) + dq + r_ref[h:h + 1, :]
        s2 = lax.dot_general(qh[h], kn, nt, preferred_element_type=F32) + dq - dt_ref[h:h + 1, :]
        row = lax.broadcasted_iota(jnp.int32, s2.shape, 0)
        col = lax.broadcasted_iota(jnp.int32, s2.shape, 1)
        s2 = jnp.where(col <= row, s2, -jnp.inf)
        m = jnp.maximum(jnp.max(s1, axis=1, keepdims=True), jnp.max(s2, axis=1, keepdims=True))
        p1 = jnp.exp(s1 - m)
        p2 = jnp.exp(s2 - m)
        ls.append(jnp.sum(p1, axis=1, keepdims=True) + jnp.sum(p2, axis=1, keepdims=True))
        outs.append(jnp.dot(p1.astype(BF16), cv, preferred_element_type=F32)
                    + jnp.dot(p2.astype(BF16), vn, preferred_element_type=F32))
    o_ref[...] = (jnp.where(first, outs[0], outs[1]) / jnp.where(first, ls[0], ls[1])).astype(o_ref.dtype)


def _attention_sample(q, kb, vb, cache_k, cache_v, d, dt4, r4, t):
    nb, p_len = cache_k.shape[0], cache_k.shape[1]
    return pl.pallas_call(
        _attn_sample_kernel,
        grid=(nb, N_PAIRS),
        in_specs=[
            pl.BlockSpec((t, PAIR), lambda b, hp: (b, hp)),
            pl.BlockSpec((t, PAIR), lambda b, hp: (b, hp)),
            pl.BlockSpec((t, PAIR), lambda b, hp: (b, hp)),
            pl.BlockSpec((None, p_len, PAIR), lambda b, hp: (b, 0, hp)),
            pl.BlockSpec((None, p_len, PAIR), lambda b, hp: (b, 0, hp)),
            pl.BlockSpec((t, LANES), lambda b, hp: (b, 0)),
            pl.BlockSpec((None, None, 2, t), lambda b, hp: (b, hp, 0, 0)),
            pl.BlockSpec((None, None, 2, p_len), lambda b, hp: (b, hp, 0, 0)),
        ],
        out_specs=pl.BlockSpec((t, PAIR), lambda b, hp: (b, hp)),
        out_shape=jax.ShapeDtypeStruct((nb * t, ATTN_WIDTH), BF16),
        compiler_params=_cparams(("arbitrary", "arbitrary")),
        name="attention_sample",
    )(q, kb, vb, cache_k, cache_v, d, dt4, r4)


def _mix_kernel(a_ref, p_ref, hist_ref, x_ref, pw_ref, ps_ref, wo_ref, g2_ref, wrh_ref, wrl_ref, br_ref, tri_ref,
                xg_ref, route_ref, cnt_ref, *, pos0, pos_stride, zero_first):
    i = pl.program_id(0)
    t = p_ref.shape[0]
    p = p_ref[...]
    hist = hist_ref[...]
    if zero_first:
        hist = jnp.where(i == 0, 0.0, hist)
    xh = jnp.concatenate([hist, p], axis=0)
    pos = pos0 + i * pos_stride + lax.broadcasted_iota(jnp.int32, (t, 1), 0)
    ys = []
    for g, w in enumerate(POOL_WINDOWS):
        col = xh[:, g * POOL_GROUP_DIM:(g + 1) * POOL_GROUP_DIM]
        acc = col
        span = 1
        while span < w:
            acc = acc + pltpu.roll(acc, span, axis=0)
            span *= 2
        cnt = jnp.minimum(pos + 1, w).astype(F32)
        dgrp = acc[HIST_ROWS:, :] / cnt - col[HIST_ROWS:, :]
        ys.append(jnp.dot(dgrp.astype(BF16), pw_ref[g], preferred_element_type=F32))
    pm = (jnp.concatenate(ys, axis=1) * ps_ref[...]).astype(BF16)
    mix = (jnp.dot(a_ref[...], wo_ref[0:ATTN_WIDTH, :], preferred_element_type=F32)
           + jnp.dot(pm, wo_ref[ATTN_WIDTH:, :], preferred_element_type=F32))
    x1 = x_ref[...] + mix
    xg_ref[:, 0:D_MODEL] = x1

    r = lax.rsqrt(jnp.mean(x1 * x1, axis=-1, keepdims=True) + EPS)
    n2 = (x1 * r) * g2_ref[...]

    nh, nl = _split2(n2)
    wrh = wrh_ref[...]
    logits = (jnp.dot(nh, wrh, preferred_element_type=F32) + jnp.dot(nl, wrh, preferred_element_type=F32)
              + jnp.dot(nh, wrl_ref[...], preferred_element_type=F32)) + br_ref[...]
    lane = lax.broadcasted_iota(jnp.int32, logits.shape, 1).astype(F32)
    neg = -jnp.inf
    big = jnp.float32(1 << 20)
    is_g = (lane >= N_EXPERTS) & (lane < N_EXPERTS + N_GROUPS)
    lg = jnp.where(is_g, logits, neg)
    gmax = jnp.max(lg, axis=1, keepdims=True)
    g_lane = jnp.min(jnp.where(lg == gmax, lane, big), axis=1, keepdims=True)
    top_pg = 1.0 / jnp.sum(jnp.exp(lg - gmax), axis=1, keepdims=True)
    g_idx = g_lane - N_EXPERTS
    sel = (lane >= g_idx * EXPERTS_PER_GROUP) & (lane < (g_idx + 1) * EXPERTS_PER_GROUP)
    le = jnp.where(sel, logits, neg)
    m1 = jnp.max(le, axis=1, keepdims=True)
    i1 = jnp.min(jnp.where(le == m1, lane, big), axis=1, keepdims=True)
    le2 = jnp.where(lane == i1, neg, le)
    m2 = jnp.max(le2, axis=1, keepdims=True)
    i2 = jnp.min(jnp.where(le2 == m2, lane, big), axis=1, keepdims=True)
    e2 = jnp.exp(m2 - m1)
    w1 = top_pg / (1.0 + e2)
    w2 = top_pg * e2 / (1.0 + e2)
    gate = jnp.where(lane == i1, w1, 0.0) + jnp.where(lane == i2, w2, 0.0)

    @pl.when(i == 0)
    def _():
        cnt_ref[...] = jnp.zeros_like(cnt_ref)
    member = lane == g_idx
    before = jnp.dot(tri_ref[...], member.astype(BF16), preferred_element_type=F32) + cnt_ref[0:1, :]
    rank = jnp.sum(jnp.where(member, before, 0.0), axis=1, keepdims=True)
    cnt_ref[...] = cnt_ref[...] + jnp.sum(member.astype(F32), axis=0, keepdims=True)
    route = jnp.where(lane == ROUTE_GROUP_LANE, g_idx, jnp.where(lane == ROUTE_RANK_LANE, rank, gate))
    route_ref[...] = route
    xg_ref[:, D_MODEL:] = route


def _mix(a, p, hist_src, hist_map, x2d, t, pos0, pos_stride, zero_first, pw, ps, wo, g2, wrh, wrl, br):
    n = x2d.shape[0]
    rows = lax.broadcasted_iota(jnp.int32, (t, t), 0)
    cols = lax.broadcasted_iota(jnp.int32, (t, t), 1)
    tri = (cols < rows).astype(BF16)
    row_blk = lambda w: pl.BlockSpec((t, w), lambda i: (i, 0))
    const = lambda arr: pl.BlockSpec(arr.shape, lambda i: (0,) * arr.ndim)
    return pl.pallas_call(
        functools.partial(_mix_kernel, pos0=pos0, pos_stride=pos_stride, zero_first=zero_first),
        grid=(n // t,),
        in_specs=[row_blk(ATTN_WIDTH), row_blk(POOL_WIDTH), pl.BlockSpec((HIST_ROWS, POOL_WIDTH), hist_map),
                  row_blk(D_MODEL), const(pw), const(ps), const(wo), const(g2), const(wrh), const(wrl), const(br),
                  const(tri)],
        out_specs=[row_blk(ROW_WIDTH), row_blk(LANES), pl.BlockSpec((8, LANES), lambda i: (0, 0))],
        out_shape=(jax.ShapeDtypeStruct((n, ROW_WIDTH), F32), jax.ShapeDtypeStruct((n, LANES), F32),
                   jax.ShapeDtypeStruct((8, LANES), F32)),
        compiler_params=_cparams(("arbitrary",)),
        name="pool_mix_route",
    )(a, p, hist_src, x2d, pw, ps, wo, g2, wrh, wrl, br, tri)


def _issue_row_scatter(dest_ref, x_ref, xs_ref, sem):
    t = x_ref.shape[0]

    def issue(g, carry):
        base = pl.multiple_of(g * SUBLANES, SUBLANES)
        tile_rows = x_ref.at[pl.ds(base, SUBLANES)]
        for u in range(SUBLANES):
            pltpu.make_async_copy(tile_rows.at[pl.ds(u, 1)], xs_ref.at[pl.ds(dest_ref[base + u], 1)],
                                  sem).start(priority=u % 2)
        return carry

    lax.fori_loop(0, t // SUBLANES, issue, 0)
    pltpu.make_async_copy(x_ref, xs_ref.at[pl.ds(0, t)], sem).wait()


def _scatter_rows_kernel(fill_ref, dest_a_ref, dest_b_ref, xa_ref, xb_ref, xs_ref, zero_ref, sem, *, steps_a):
    step = pl.program_id(0)

    @pl.when(step == 0)
    def _():
        zero_ref[...] = jnp.zeros_like(zero_ref)
        for g in range(fill_ref.shape[0]):
            fill = pltpu.make_async_copy(
                zero_ref, xs_ref.at[pl.ds(pl.multiple_of(fill_ref[g], SUBLANES), zero_ref.shape[0])], sem)
            fill.start()
            fill.wait()

    @pl.when(step < steps_a)
    def _():
        _issue_row_scatter(dest_a_ref, xa_ref, xs_ref, sem)

    @pl.when(step >= steps_a)
    def _():
        _issue_row_scatter(dest_b_ref, xb_ref, xs_ref, sem)


def _scatter_rows(fill_start, dest_a, xa, dest_b, xb, m_rows, t):
    w = xa.shape[1]
    steps_a, steps_b = xa.shape[0] // t, xb.shape[0] // t
    in_a = lambda i: jnp.minimum(i, steps_a - 1)
    in_b = lambda i: jnp.maximum(i - steps_a, 0)
    smem = pltpu.SMEM
    return pl.pallas_call(
        functools.partial(_scatter_rows_kernel, steps_a=steps_a),
        grid=(steps_a + steps_b,),
        in_specs=[pl.BlockSpec(fill_start.shape, lambda i: (0,), memory_space=smem),
                  pl.BlockSpec((t,), lambda i: (in_a(i),), memory_space=smem),
                  pl.BlockSpec((t,), lambda i: (in_b(i),), memory_space=smem),
                  pl.BlockSpec((t, w), lambda i: (in_a(i), 0)),
                  pl.BlockSpec((t, w), lambda i: (in_b(i), 0))],
        out_specs=pl.BlockSpec(memory_space=pl.ANY),
        out_shape=jax.ShapeDtypeStruct((m_rows, w), xa.dtype),
        scratch_shapes=[pltpu.VMEM((t, w), xa.dtype), pltpu.SemaphoreType.DMA],
        compiler_params=_cparams(("arbitrary",)),
        name="moe_scatter_rows",
    )(fill_start, dest_a, dest_b, xa, xb)


def _gather_rows_kernel(dest_ref, ys_ref, y_ref, sem):
    t = y_ref.shape[0]

    def issue(g, carry):
        base = pl.multiple_of(g * SUBLANES, SUBLANES)
        tile_rows = y_ref.at[pl.ds(base, SUBLANES)]
        for u in range(SUBLANES):
            pltpu.make_async_copy(ys_ref.at[pl.ds(dest_ref[base + u], 1)], tile_rows.at[pl.ds(u, 1)],
                                  sem).start(priority=u % 2)
        return carry

    lax.fori_loop(0, t // SUBLANES, issue, 0)
    pltpu.make_async_copy(ys_ref.at[pl.ds(0, t)], y_ref, sem).wait()


def _gather_rows(dest, ys, t):
    n = dest.shape[0]
    w = ys.shape[1]
    return pl.pallas_call(
        _gather_rows_kernel,
        grid=(n // t,),
        in_specs=[pl.BlockSpec((t,), lambda i: (i,), memory_space=pltpu.SMEM),
                  pl.BlockSpec(memory_space=pl.ANY)],
        out_specs=pl.BlockSpec((t, w), lambda i: (i, 0)),
        out_shape=jax.ShapeDtypeStruct((n, w), ys.dtype),
        scratch_shapes=[pltpu.SemaphoreType.DMA],
        compiler_params=_cparams(("arbitrary",)),
        name="moe_gather_rows",
    )(dest, ys)


def _moe_kernel(tg_ref, tv_ref, xs_ref, g2_ref, wg_ref, wu_ref, wd_ref, y_ref):
    m = pl.program_id(0)

    @pl.when(tv_ref[m] > 0)
    def _():
        x1 = xs_ref[:, 0:D_MODEL]
        route = xs_ref[:, D_MODEL:]
        lane = lax.broadcasted_iota(jnp.int32, route.shape, 1)
        r = lax.rsqrt(jnp.mean(x1 * x1, axis=-1, keepdims=True) + EPS)
        n2 = ((x1 * r) * g2_ref[...]).astype(BF16)
        first_expert = tg_ref[m] * EXPERTS_PER_GROUP
        acc = x1
        for e in range(EXPERTS_PER_GROUP):
            ge = jnp.sum(jnp.where(lane == first_expert + e, route, 0.0), axis=1, keepdims=True)
            hg = jnp.dot(n2, wg_ref[e], preferred_element_type=F32)
            hu = jnp.dot(n2, wu_ref[e], preferred_element_type=F32)
            h = (hg * jax.nn.sigmoid(hg)) * hu * ge
            acc = acc + jnp.dot(h.astype(BF16), wd_ref[e], preferred_element_type=F32)
        y_ref[...] = acc

    @pl.when(tv_ref[m] == 0)
    def _():
        y_ref[...] = jnp.zeros_like(y_ref)


def _moe_sorted(tile_group, tile_valid, xs, g2, wg4, wu4, wd4, tm):
    m_pad = tile_group.shape[0] * tm
    grid_spec = pltpu.PrefetchScalarGridSpec(
        num_scalar_prefetch=2,
        grid=(m_pad // tm,),
        in_specs=[
            pl.BlockSpec((tm, ROW_WIDTH), lambda m, tg, tv: (m, 0)),
            pl.BlockSpec(g2.shape, lambda m, tg, tv: (0, 0)),
            pl.BlockSpec((None, EXPERTS_PER_GROUP, D_MODEL, D_EXPERT), lambda m, tg, tv: (tg[m], 0, 0, 0)),
            pl.BlockSpec((None, EXPERTS_PER_GROUP, D_MODEL, D_EXPERT), lambda m, tg, tv: (tg[m], 0, 0, 0)),
            pl.BlockSpec((None, EXPERTS_PER_GROUP, D_EXPERT, D_MODEL), lambda m, tg, tv: (tg[m], 0, 0, 0)),
        ],
        out_specs=pl.BlockSpec((tm, D_MODEL), lambda m, tg, tv: (m, 0)),
    )
    return pl.pallas_call(
        _moe_kernel,
        grid_spec=grid_spec,
        out_shape=jax.ShapeDtypeStruct((m_pad, D_MODEL), F32),
        compiler_params=_cparams(("arbitrary",)),
        name="moe_experts",
    )(tile_group, tile_valid, xs, g2, wg4, wu4, wd4)


def _moe_plan(route_p, cnt_p, route_s, cnt_s, tm):
    n_total = route_p.shape[0] + route_s.shape[0]
    n_tiles = n_total // tm + N_GROUPS
    cp = cnt_p[0, :N_GROUPS].astype(jnp.int32)
    cs = cnt_s[0, :N_GROUPS].astype(jnp.int32)
    total = cp + cs
    tiles = (total + tm - 1) // tm
    tile_end = jnp.cumsum(tiles)
    offset = (tile_end - tiles) * tm
    gp = route_p[:, ROUTE_GROUP_LANE].astype(jnp.int32)
    gs = route_s[:, ROUTE_GROUP_LANE].astype(jnp.int32)
    dest_p = offset[gp] + route_p[:, ROUTE_RANK_LANE].astype(jnp.int32)
    dest_s = offset[gs] + cp[gs] + route_s[:, ROUTE_RANK_LANE].astype(jnp.int32)
    tile_id = jnp.arange(n_tiles)
    tile_group = jnp.minimum(jnp.sum(tile_id[:, None] >= tile_end[None, :], axis=1), N_GROUPS - 1).astype(jnp.int32)
    used = jnp.clip(offset[tile_group] + total[tile_group] - tile_id * tm, 0, tm)
    tile_valid = jnp.where(tile_id < tile_end[N_GROUPS - 1], used, 0).astype(jnp.int32)
    m_rows = (n_tiles + 1) * tm
    ends = ((offset + total) // SUBLANES) * SUBLANES
    tail = ends[N_GROUPS - 1] + tm * jnp.arange(1, m_rows // tm - n_total // tm + 1)
    fill_start = jnp.minimum(jnp.concatenate([ends, tail]), m_rows - tm).astype(jnp.int32)
    return dest_p, dest_s, fill_start, tile_group, tile_valid, m_rows


def kernel(x_prompt, x_sample, cache_k, cache_v, cache_logf, state_pool, norm1_g, w_in, b_f, q_norm_g, k_norm_g,
           pool_w, pool_scale, w_out, norm2_g, w_router_group, b_router_group, w_router_expert, b_router_expert,
           w_gate, w_up, w_down):
    depth = norm1_g.shape[0]
    assert depth == 1, "single-layer step"
    b_p, s_p, _ = x_prompt.shape
    assert b_p == 1, "prompt kernels assume one stream"
    b_s, t_s, _ = x_sample.shape
    p_len = cache_k.shape[2]
    l = 0

    w = w_in[l]
    a3 = 3 * ATTN_WIDTH
    w_all = jnp.concatenate(
        [w[:, :a3], w[:, a3 + N_HEADS:], jnp.pad(w[:, a3:a3 + N_HEADS], ((0, 0), (0, LANES - N_HEADS)))],
        axis=1).astype(BF16)
    bf_pad = jnp.pad(b_f[l], (0, LANES - N_HEADS)).reshape(1, LANES)
    g1 = norm1_g[l].reshape(1, D_MODEL)
    qg = jnp.tile(q_norm_g[l], N_HEADS).reshape(1, ATTN_WIDTH)
    kg = jnp.tile(k_norm_g[l], N_HEADS).reshape(1, ATTN_WIDTH)
    hr = lax.broadcasted_iota(jnp.int32, (ATTN_WIDTH, ATTN_WIDTH), 0) // HEAD_DIM
    hc = lax.broadcasted_iota(jnp.int32, (ATTN_WIDTH, ATTN_WIDTH), 1) // HEAD_DIM
    hm = jnp.where(hr == hc, 1.0 / HEAD_DIM, 0.0).astype(BF16)
    pw = pool_w[l].astype(BF16)
    ps = pool_scale[l].reshape(1, POOL_WIDTH)
    wo = w_out[l].astype(BF16)
    g2 = norm2_g[l].reshape(1, D_MODEL)
    w_r = jnp.pad(jnp.concatenate([w_router_expert[l], w_router_group[l]], axis=1),
                  ((0, 0), (0, LANES - N_EXPERTS - N_GROUPS)))
    wrh = w_r.astype(BF16)
    wrl = (w_r - wrh.astype(F32)).astype(BF16)
    br = jnp.pad(jnp.concatenate([b_router_expert[l], b_router_group[l]]),
                 (0, LANES - N_EXPERTS - N_GROUPS)).reshape(1, LANES)
    wg4 = w_gate[l].astype(BF16).reshape(N_GROUPS, EXPERTS_PER_GROUP, D_MODEL, D_EXPERT)
    wu4 = w_up[l].astype(BF16).reshape(N_GROUPS, EXPERTS_PER_GROUP, D_MODEL, D_EXPERT)
    wd4 = w_down[l].astype(BF16).reshape(N_GROUPS, EXPERTS_PER_GROUP, D_EXPERT, D_MODEL)

    tile = 512
    key_tile = 256

    xp = x_prompt.reshape(s_p, D_MODEL)
    q, kf, _, vf, _, ka, vt, pin, logf, c = _in_project(xp, tile, tile, True, key_tile, g1, w_all, bf_pad, qg, kg, hm)
    c_blocks = c[0::key_tile, :N_HEADS]
    j0 = _first_key_blocks(c, q_norm_g[l], k_norm_g[l], tile, key_tile)
    a = _attention_prompt(j0, c_blocks, q, ka, vt, tile, key_tile)
    hist_map = lambda i: (jnp.maximum(i * (tile // HIST_ROWS) - 1, 0), 0)
    xg_p, route_p, cnt_p = _mix(a, pin, pin, hist_map, xp, tile, 0, tile, True, pw, ps, wo, g2, wrh, wrl, br)
    k_prompt = kf.reshape(depth, b_p, s_p, N_HEADS, HEAD_DIM)
    v_prompt = vf.reshape(depth, b_p, s_p, N_HEADS, HEAD_DIM)
    logf_prompt = logf[:, :N_HEADS].reshape(depth, b_p, s_p, N_HEADS)
    pool_prompt = pin[s_p - POOL_PAD:].reshape(depth, b_p, POOL_PAD, POOL_WIDTH)

    n_s = b_s * t_s
    xs = x_sample.reshape(n_s, D_MODEL)
    q, kf, kb, vf, vb, _, _, pin, logf, d = _in_project(xs, tile, t_s, False, key_tile, g1, w_all, bf_pad, qg, kg, hm)
    dt4 = d[:, :N_HEADS].reshape(b_s, t_s, N_PAIRS, 2).transpose(0, 2, 3, 1)
    clf = cache_logf[l].astype(F32).transpose(0, 2, 1).reshape(b_s * N_HEADS, p_len)
    r4 = _suffix_sums(clf).reshape(b_s, N_PAIRS, 2, p_len)
    ck = cache_k[l].astype(BF16).reshape(b_s, p_len, ATTN_WIDTH)
    cv = cache_v[l].astype(BF16).reshape(b_s, p_len, ATTN_WIDTH)
    a = _attention_sample(q, kb, vb, ck, cv, d, dt4, r4, t_s)
    hist_s = jnp.pad(state_pool[l], ((0, 0), (HIST_ROWS - POOL_PAD, 0), (0, 0))).reshape(b_s * HIST_ROWS, POOL_WIDTH)
    xg_s, route_s, cnt_s = _mix(a, pin, hist_s, lambda i: (i, 0), xs, t_s, p_len, 0, False, pw, ps, wo, g2, wrh, wrl,
                                br)
    k_sample = kf.reshape(depth, b_s, t_s, N_HEADS, HEAD_DIM)
    v_sample = vf.reshape(depth, b_s, t_s, N_HEADS, HEAD_DIM)
    logf_sample = logf[:, :N_HEADS].reshape(depth, b_s, t_s, N_HEADS)
    pool_sample = pin.reshape(b_s, t_s, POOL_WIDTH)[:, t_s - POOL_PAD:].reshape(depth, b_s, POOL_PAD, POOL_WIDTH)

    dest_p, dest_s, fill_start, tile_group, tile_valid, m_rows = _moe_plan(route_p, cnt_p, route_s, cnt_s, tile)
    rows = _scatter_rows(fill_start, dest_p, xg_p, dest_s, xg_s, m_rows, tile)
    ys = _moe_sorted(tile_group, tile_valid, rows, g2, wg4, wu4, wd4, tile)
    y_prompt = _gather_rows(dest_p, ys, tile).reshape(b_p, s_p, D_MODEL)
    y_sample = _gather_rows(dest_s, ys, tile).reshape(b_s, t_s, D_MODEL)

    return (y_prompt, y_sample, k_prompt, v_prompt, logf_prompt, pool_prompt,
            k_sample, v_sample, logf_sample, pool_sample)
```

```python
import functools

import jax
import jax.numpy as jnp
from jax import lax
from jax.experimental import pallas as pl
from jax.experimental.pallas import tpu as pltpu

F32 = jnp.float32
BF16 = jnp.bfloat16

D_MODEL = 1024
ATTN_WIDTH = 512
N_HEADS = 8
HEAD_DIM = 64
POOL_WIDTH = 512
POOL_WINDOWS = (2, 4, 8, 16)
POOL_GROUP_DIM = 128
POOL_PAD = 15
HIST_ROWS = 16
N_GROUPS = 4
EXPERTS_PER_GROUP = 4
N_EXPERTS = 16
D_EXPERT = 512
EPS = 1e-6
LANES = 128
SUBLANES = 8
PAIR = 2 * HEAD_DIM
N_PAIRS = N_HEADS // 2
ROWSUM_ROW = (HEAD_DIM, 0)
VMEM_LIMIT = 56 * 1024 * 1024
ROW_WIDTH = D_MODEL + LANES
PAIRS_PER_GROUP = 6
N_CATEGORIES = N_GROUPS * PAIRS_PER_GROUP
CATEGORY_EXPERTS = tuple((4 * g + a, 4 * g + b) for g in range(N_GROUPS) for a in range(4) for b in range(a + 1, 4))
ROUTE_CATEGORY_LANE = N_EXPERTS
ROUTE_RANK_LANE = N_EXPERTS + 1


def _cparams(sem):
    return pltpu.CompilerParams(dimension_semantics=sem, vmem_limit_bytes=VMEM_LIMIT)


def _split3(x):
    hi = x.astype(BF16)
    r1 = x - hi.astype(F32)
    mid = r1.astype(BF16)
    lo = (r1 - mid.astype(F32)).astype(BF16)
    return hi, mid, lo


def _split2(x):
    hi = x.astype(BF16)
    lo = (x - hi.astype(F32)).astype(BF16)
    return hi, lo


def _inproj_kernel(x_ref, g1_ref, w_ref, bf_ref, qg_ref, kg_ref, hm_ref, tri_ref, place_ref,
                   q_ref, kf_ref, kb_ref, vf_ref, vb_ref, ka_ref, vt_ref, p_ref, logf_ref, c_ref, carry_ref, *,
                   carry_rows, key_blk):
    x = x_ref[...]
    r = lax.rsqrt(jnp.mean(x * x, axis=-1, keepdims=True) + EPS)
    n = ((x * r) * g1_ref[...]).astype(BF16)
    z = jnp.dot(n, w_ref[...], preferred_element_type=F32)

    hm = hm_ref[...]

    def head_norm(zz, g):
        hi, lo = _split2(zz * zz)
        ms = jnp.dot(hi, hm, preferred_element_type=F32) + jnp.dot(lo, hm, preferred_element_type=F32)
        return (zz * lax.rsqrt(ms + EPS)) * g

    q = head_norm(z[:, 0:ATTN_WIDTH], qg_ref[...])
    q_ref[...] = (q * (HEAD_DIM ** -0.5)).astype(BF16)
    k = head_norm(z[:, ATTN_WIDTH:2 * ATTN_WIDTH], kg_ref[...])
    kf_ref[...] = k
    kb_ref[...] = k.astype(BF16)
    v = z[:, 2 * ATTN_WIDTH:3 * ATTN_WIDTH]
    vf_ref[...] = v
    vb_ref[...] = v.astype(BF16)
    p_ref[...] = z[:, 3 * ATTN_WIDTH:3 * ATTN_WIDTH + POOL_WIDTH]
    prow = lax.broadcasted_iota(jnp.int32, (PAIR, v.shape[0]), 0)
    for hp in range(N_PAIRS):
        vpair_t = v[:, hp * PAIR:(hp + 1) * PAIR].T
        for h in range(2):
            own = (prow < HEAD_DIM) if h == 0 else (prow >= HEAD_DIM)
            marker = jnp.where(prow == ROWSUM_ROW[h], 1.0, 0.0)
            vt_ref[(2 * hp + h) * LANES:(2 * hp + h + 1) * LANES, :] = jnp.where(own, vpair_t, marker).astype(BF16)

    f = z[:, 3 * ATTN_WIDTH + POOL_WIDTH:] + bf_ref[...]
    lane = lax.broadcasted_iota(jnp.int32, f.shape, 1)
    logf = -(jnp.maximum(-f, 0.0) + jnp.log1p(jnp.exp(-jnp.abs(f))))
    logf = jnp.where(lane < N_HEADS, logf, 0.0)
    logf_ref[...] = logf

    tri = tri_ref[...]
    hi, mid, lo = _split3(logf)
    c = (jnp.dot(tri, hi, preferred_element_type=F32) + jnp.dot(tri, mid, preferred_element_type=F32)
         + jnp.dot(tri, lo, preferred_element_type=F32))
    if carry_rows:
        @pl.when(pl.program_id(0) == 0)
        def _():
            carry_ref[...] = jnp.zeros_like(carry_ref)
        c = c + carry_ref[0:1, :]
        carry_ref[...] = jnp.broadcast_to(c[c.shape[0] - 1:, :], carry_ref.shape)
    c_ref[...] = c

    tm = c.shape[0]
    rel = jnp.concatenate([c[r:r + key_blk, :] - c[r:r + 1, :] for r in range(0, tm, key_blk)], axis=0)
    parts = jnp.concatenate(_split3(rel), axis=1)
    decay = jnp.dot(parts, place_ref[...], preferred_element_type=F32).astype(BF16)
    kb = k.astype(BF16)
    for hp in range(N_PAIRS):
        ka_ref[:, 2 * hp * LANES:(2 * hp + 1) * LANES] = kb[:, hp * PAIR:(hp + 1) * PAIR]
        ka_ref[:, (2 * hp + 1) * LANES:(2 * hp + 2) * LANES] = decay[:, hp * LANES:(hp + 1) * LANES]


def _decay_lane(h, part):
    return 3 * h + part


def _in_project(x2d, tm, seg, carry_rows, key_blk, g1, w_all, bf_pad, qg, kg, hm):
    n = x2d.shape[0]
    rows = lax.broadcasted_iota(jnp.int32, (tm, tm), 0)
    cols = lax.broadcasted_iota(jnp.int32, (tm, tm), 1)
    tri = ((cols <= rows) & (rows // seg == cols // seg)).astype(BF16)
    src = jnp.arange(3 * LANES)
    part, head = src // LANES, src % LANES
    dst = (head // 2) * LANES + _decay_lane(head % 2, part)
    place = ((jnp.arange(N_PAIRS * LANES)[None, :] == dst[:, None]) & (head < N_HEADS)[:, None]).astype(BF16)
    row_blk = lambda w: pl.BlockSpec((tm, w), lambda i: (i, 0))
    const = lambda a: pl.BlockSpec(a.shape, lambda i: (0,) * a.ndim)
    out_shape = (
        jax.ShapeDtypeStruct((n, ATTN_WIDTH), BF16),
        jax.ShapeDtypeStruct((n, ATTN_WIDTH), F32),
        jax.ShapeDtypeStruct((n, ATTN_WIDTH), BF16),
        jax.ShapeDtypeStruct((n, ATTN_WIDTH), F32),
        jax.ShapeDtypeStruct((n, ATTN_WIDTH), BF16),
        jax.ShapeDtypeStruct((n, N_PAIRS * 2 * LANES), BF16),
        jax.ShapeDtypeStruct((N_HEADS * LANES, n), BF16),
        jax.ShapeDtypeStruct((n, POOL_WIDTH), F32),
        jax.ShapeDtypeStruct((n, LANES), F32),
        jax.ShapeDtypeStruct((n, LANES), F32),
    )
    return pl.pallas_call(
        functools.partial(_inproj_kernel, carry_rows=carry_rows, key_blk=key_blk),
        grid=(n // tm,),
        in_specs=[row_blk(D_MODEL), const(g1), const(w_all), const(bf_pad), const(qg), const(kg), const(hm),
                  const(tri), const(place)],
        out_specs=[row_blk(ATTN_WIDTH)] * 5 + [row_blk(N_PAIRS * 2 * LANES),
                                                pl.BlockSpec((N_HEADS * LANES, tm), lambda i: (0, i)),
                                                row_blk(POOL_WIDTH), row_blk(LANES), row_blk(LANES)],
        out_shape=out_shape,
        scratch_shapes=[pltpu.VMEM((8, LANES), F32)],
        compiler_params=_cparams(("arbitrary",)),
        name="in_project",
    )(x2d, g1, w_all, bf_pad, qg, kg, hm, tri, place)


def _attn_prompt_kernel(j0_ref, cb_ref, q_ref, k_ref, vt_ref, o_ref, acc_ref, m_ref, *, tq, tk):
    hp = pl.program_id(0)
    i = pl.program_id(1)
    nk = tq // tk
    lane = lax.broadcasted_iota(jnp.int32, (1, PAIR), 1)
    first = lane < HEAD_DIM
    q = q_ref[...]
    zero = jnp.zeros_like(q)
    qw = []
    for h in range(2):
        minus = jnp.zeros((1, LANES), F32)
        for part in range(3):
            minus = jnp.where(lane == _decay_lane(h, part), -1.0, minus)
        qw.append(jnp.concatenate([jnp.where(first == (h == 0), q, zero),
                                   jnp.broadcast_to(minus, (tq, LANES)).astype(BF16)], axis=1))
    q_start = i * tq

    acc_ref[...] = jnp.zeros_like(acc_ref)
    m_ref[...] = jnp.full_like(m_ref, -jnp.inf)

    last = (i + 1) * nk - 1

    def scores(h, j, masked=True):
        start = pl.multiple_of(jnp.minimum(j, last) * tk, tk)
        k = k_ref[pl.ds(start, tk), :]
        s = lax.dot_general(k, qw[h], (((1,), (1,)), ((), ())), preferred_element_type=F32)
        if masked:
            key = lax.broadcasted_iota(jnp.int32, s.shape, 0)
            qry = lax.broadcasted_iota(jnp.int32, s.shape, 1)
            s = jnp.where(key - qry <= q_start - j * tk, s, -jnp.inf)
        return s, jnp.max(s, axis=0, keepdims=True)

    def update(h, j, s, colmax):
        jc = jnp.minimum(j, last)
        shift = cb_ref[jc, 2 * hp + h] - cb_ref[i * nk, 2 * hp + h]
        start = pl.multiple_of(jc * tk, tk)
        vt = vt_ref[h * LANES:(h + 1) * LANES, pl.ds(start, tk)]
        m_prev = m_ref[h]
        m_new = jnp.maximum(m_prev, colmax - shift)
        alpha = jnp.exp(m_prev - m_new)
        p = jnp.exp(s - (m_new + shift))
        m_ref[h] = m_new
        acc_ref[h] = acc_ref[h] * alpha + jnp.dot(vt, p.astype(BF16), preferred_element_type=F32)

    j_first = [j0_ref[i, 2 * hp + h] for h in range(2)]
    j_both = jnp.maximum(j_first[0], j_first[1])
    start = [scores(h, j_first[h]) for h in range(2)]
    ahead = []
    for h in range(2):
        def body_one(j, carry, h=h):
            nxt = scores(h, j + 1)
            update(h, j, *carry)
            return nxt

        ahead.append(lax.fori_loop(j_first[h], j_both, body_one, start[h]))

    def body_both(j, carry, masked):
        nxt = (scores(0, j + 1, masked), scores(1, j + 1, masked))
        update(0, j, *carry[0])
        update(1, j, *carry[1])
        return nxt

    j_mask = jnp.maximum(j_both, i * nk - 1)
    pairs = (j_mask - j_both) // 2

    def body_two(t, carry):
        j = j_both + 2 * t
        return body_both(j + 1, body_both(j, carry, False), False)

    carry = lax.fori_loop(0, pairs, body_two, (ahead[0], ahead[1]))
    lax.fori_loop(j_both + 2 * pairs, last + 1, functools.partial(body_both, masked=True), carry)

    out = [acc_ref[h] / acc_ref[h][ROWSUM_ROW[h]:ROWSUM_ROW[h] + 1, :] for h in range(2)]
    row = lax.broadcasted_iota(jnp.int32, (PAIR, 1), 0)
    o_ref[...] = jnp.where(row < HEAD_DIM, out[0], out[1]).T.astype(o_ref.dtype)


def _attention_prompt(j0, c_blocks, q, ka, vt, tq, tk):
    n = q.shape[0]
    grid_spec = pltpu.PrefetchScalarGridSpec(
        num_scalar_prefetch=2,
        grid=(N_PAIRS, n // tq),
        in_specs=[
            pl.BlockSpec((tq, PAIR), lambda hp, i, j0, cb: (i, hp)),
            pl.BlockSpec((n, 2 * LANES), lambda hp, i, j0, cb: (0, hp)),
            pl.BlockSpec((2 * LANES, n), lambda hp, i, j0, cb: (hp, 0)),
        ],
        out_specs=pl.BlockSpec((tq, PAIR), lambda hp, i, j0, cb: (i, hp)),
        scratch_shapes=[pltpu.VMEM((2, LANES, tq), F32), pltpu.VMEM((2, 1, tq), F32)],
    )
    return pl.pallas_call(
        functools.partial(_attn_prompt_kernel, tq=tq, tk=tk),
        grid_spec=grid_spec,
        out_shape=jax.ShapeDtypeStruct((n, ATTN_WIDTH), BF16),
        compiler_params=_cparams(("arbitrary", "arbitrary")),
        name="attention_prompt",
    )(j0, c_blocks, q, ka, vt)


def _first_key_blocks(c, q_gain, k_gain, tq, tk):
    n = c.shape[0]
    qk_bound = 1.02 * HEAD_DIM ** 0.5 * jnp.max(jnp.abs(q_gain)) * jnp.max(jnp.abs(k_gain))
    threshold = 104.0 + 2.0 * qk_bound + 1.0
    c_start = c[0::tq, :N_HEADS]
    c_end = c[tk - 1::tk, :N_HEADS]
    gap = c_start[:, None, :] - c_end[None, :, :]
    earlier = (jnp.arange(n // tk)[None, :, None] + 1) * tk <= jnp.arange(n // tq)[:, None, None] * tq
    return jnp.sum((gap < -threshold) & earlier, axis=1).astype(jnp.int32)


def _suffix_kernel(x_ref, o_ref):
    x = x_ref[...]
    p_len = x.shape[1]
    rows = lax.broadcasted_iota(jnp.int32, (p_len, p_len), 0)
    cols = lax.broadcasted_iota(jnp.int32, (p_len, p_len), 1)
    u = (rows > cols).astype(BF16)
    hi, mid, lo = _split3(x)
    o_ref[...] = (jnp.dot(hi, u, preferred_element_type=F32) + jnp.dot(mid, u, preferred_element_type=F32)
                  + jnp.dot(lo, u, preferred_element_type=F32))


def _suffix_sums(x):
    return pl.pallas_call(
        _suffix_kernel,
        out_shape=jax.ShapeDtypeStruct(x.shape, F32),
        compiler_params=pltpu.CompilerParams(vmem_limit_bytes=VMEM_LIMIT),
        name="cache_suffix_sums",
    )(x)


def _attn_sample_kernel(q_ref, kn_ref, vn_ref, ck_ref, cv_ref, d_ref, dt_ref, r_ref, o_ref):
    hp = pl.program_id(1)
    lane = lax.broadcasted_iota(jnp.int32, (1, PAIR), 1)
    first = lane < HEAD_DIM
    q = q_ref[...]
    zero = jnp.zeros_like(q)
    qh = (jnp.where(first, q, zero), jnp.where(first, zero, q))
    dblk = d_ref[...]
    dlane = lax.broadcasted_iota(jnp.int32, dblk.shape, 1)
    kn = kn_ref[...]
    vn = vn_ref[...]
    ck = ck_ref[...]
    cv = cv_ref[...]
    nt = (((1,), (1,)), ((), ()))
    outs, ls = [], []
    for h in range(2):
        dq = jnp.sum(jnp.where(dlane == 2 * hp + h, dblk, 0.0), axis=1, keepdims=True)
        s1 = lax.dot_general(qh[h], ck, nt, preferred_element_type=F32) + dq + r_ref[h:h + 1, :]
        s2 = lax.dot_general(qh[h], kn, nt, preferred_element_type=F32) + dq - dt_ref[h:h + 1, :]
        row = lax.broadcasted_iota(jnp.int32, s2.shape, 0)
        col = lax.broadcasted_iota(jnp.int32, s2.shape, 1)
        s2 = jnp.where(col <= row, s2, -jnp.inf)
        m = jnp.maximum(jnp.max(s1, axis=1, keepdims=True), jnp.max(s2, axis=1, keepdims=True))
        p1 = jnp.exp(s1 - m)
        p2 = jnp.exp(s2 - m)
        ls.append(jnp.sum(p1, axis=1, keepdims=True) + jnp.sum(p2, axis=1, keepdims=True))
        outs.append(jnp.dot(p1.astype(BF16), cv, preferred_element_type=F32)
                    + jnp.dot(p2.astype(BF16), vn, preferred_element_type=F32))
    o_ref[...] = (jnp.where(first, outs[0], outs[1]) / jnp.where(first, ls[0], ls[1])).astype(o_ref.dtype)


def _attention_sample(q, kb, vb, cache_k, cache_v, d, dt4, r4, t):
    nb, p_len = cache_k.shape[0], cache_k.shape[1]
    return pl.pallas_call(
        _attn_sample_kernel,
        grid=(nb, N_PAIRS),
        in_specs=[
            pl.BlockSpec((t, PAIR), lambda b, hp: (b, hp)),
            pl.BlockSpec((t, PAIR), lambda b, hp: (b, hp)),
            pl.BlockSpec((t, PAIR), lambda b, hp: (b, hp)),
            pl.BlockSpec((None, p_len, PAIR), lambda b, hp: (b, 0, hp)),
            pl.BlockSpec((None, p_len, PAIR), lambda b, hp: (b, 0, hp)),
            pl.BlockSpec((t, LANES), lambda b, hp: (b, 0)),
            pl.BlockSpec((None, None, 2, t), lambda b, hp: (b, hp, 0, 0)),
            pl.BlockSpec((None, None, 2, p_len), lambda b, hp: (b, hp, 0, 0)),
        ],
        out_specs=pl.BlockSpec((t, PAIR), lambda b, hp: (b, hp)),
        out_shape=jax.ShapeDtypeStruct((nb * t, ATTN_WIDTH), BF16),
        compiler_params=_cparams(("arbitrary", "arbitrary")),
        name="attention_sample",
    )(q, kb, vb, cache_k, cache_v, d, dt4, r4)


def _mix_kernel(a_ref, p_ref, hist_ref, x_ref, pw_ref, ps_ref, wo_ref, g2_ref, wrh_ref, wrl_ref, br_ref, tri_ref,
                xg_ref, route_ref, cnt_ref, *, pos0, pos_stride, zero_first):
    i = pl.program_id(0)
    t = p_ref.shape[0]
    p = p_ref[...]
    hist = hist_ref[...]
    if zero_first:
        hist = jnp.where(i == 0, 0.0, hist)
    xh = jnp.concatenate([hist, p], axis=0)
    pos = pos0 + i * pos_stride + lax.broadcasted_iota(jnp.int32, (t, 1), 0)
    ys = []
    for g, w in enumerate(POOL_WINDOWS):
        col = xh[:, g * POOL_GROUP_DIM:(g + 1) * POOL_GROUP_DIM]
        acc = col
        span = 1
        while span < w:
            acc = acc + pltpu.roll(acc, span, axis=0)
            span *= 2
        cnt = jnp.minimum(pos + 1, w).astype(F32)
        dgrp = acc[HIST_ROWS:, :] / cnt - col[HIST_ROWS:, :]
        ys.append(jnp.dot(dgrp.astype(BF16), pw_ref[g], preferred_element_type=F32))
    pm = (jnp.concatenate(ys, axis=1) * ps_ref[...]).astype(BF16)
    mix = (jnp.dot(a_ref[...], wo_ref[0:ATTN_WIDTH, :], preferred_element_type=F32)
           + jnp.dot(pm, wo_ref[ATTN_WIDTH:, :], preferred_element_type=F32))
    x1 = x_ref[...] + mix
    xg_ref[:, 0:D_MODEL] = x1

    r = lax.rsqrt(jnp.mean(x1 * x1, axis=-1, keepdims=True) + EPS)
    n2 = (x1 * r) * g2_ref[...]

    nh, nl = _split2(n2)
    wrh = wrh_ref[...]
    logits = (jnp.dot(nh, wrh, preferred_element_type=F32) + jnp.dot(nl, wrh, preferred_element_type=F32)
              + jnp.dot(nh, wrl_ref[...], preferred_element_type=F32)) + br_ref[...]
    lane = lax.broadcasted_iota(jnp.int32, logits.shape, 1).astype(F32)
    neg = -jnp.inf
    big = jnp.float32(1 << 20)
    is_g = (lane >= N_EXPERTS) & (lane < N_EXPERTS + N_GROUPS)
    lg = jnp.where(is_g, logits, neg)
    gmax = jnp.max(lg, axis=1, keepdims=True)
    g_lane = jnp.min(jnp.where(lg == gmax, lane, big), axis=1, keepdims=True)
    top_pg = 1.0 / jnp.sum(jnp.exp(lg - gmax), axis=1, keepdims=True)
    g_idx = g_lane - N_EXPERTS
    sel = (lane >= g_idx * EXPERTS_PER_GROUP) & (lane < (g_idx + 1) * EXPERTS_PER_GROUP)
    le = jnp.where(sel, logits, neg)
    m1 = jnp.max(le, axis=1, keepdims=True)
    i1 = jnp.min(jnp.where(le == m1, lane, big), axis=1, keepdims=True)
    le2 = jnp.where(lane == i1, neg, le)
    m2 = jnp.max(le2, axis=1, keepdims=True)
    i2 = jnp.min(jnp.where(le2 == m2, lane, big), axis=1, keepdims=True)
    e2 = jnp.exp(m2 - m1)
    w1 = top_pg / (1.0 + e2)
    w2 = top_pg * e2 / (1.0 + e2)
    gate = jnp.where(lane == i1, w1, 0.0) + jnp.where(lane == i2, w2, 0.0)

    @pl.when(i == 0)
    def _():
        cnt_ref[...] = jnp.zeros_like(cnt_ref)
    e_lo = jnp.minimum(i1, i2) - g_idx * EXPERTS_PER_GROUP
    e_hi = jnp.maximum(i1, i2) - g_idx * EXPERTS_PER_GROUP
    category = g_idx * PAIRS_PER_GROUP + (e_lo * (7.0 - e_lo) * 0.5 + (e_hi - e_lo - 1.0))
    member = lane == category
    before = jnp.dot(tri_ref[...], member.astype(BF16), preferred_element_type=F32) + cnt_ref[0:1, :]
    rank = jnp.sum(jnp.where(member, before, 0.0), axis=1, keepdims=True)
    cnt_ref[...] = cnt_ref[...] + jnp.sum(member.astype(F32), axis=0, keepdims=True)
    route = jnp.where(lane == ROUTE_CATEGORY_LANE, category, jnp.where(lane == ROUTE_RANK_LANE, rank, gate))
    route_ref[...] = route
    xg_ref[:, D_MODEL:] = route


def _mix(a, p, hist_src, hist_map, x2d, t, pos0, pos_stride, zero_first, pw, ps, wo, g2, wrh, wrl, br):
    n = x2d.shape[0]
    rows = lax.broadcasted_iota(jnp.int32, (t, t), 0)
    cols = lax.broadcasted_iota(jnp.int32, (t, t), 1)
    tri = (cols < rows).astype(BF16)
    row_blk = lambda w: pl.BlockSpec((t, w), lambda i: (i, 0))
    const = lambda arr: pl.BlockSpec(arr.shape, lambda i: (0,) * arr.ndim)
    return pl.pallas_call(
        functools.partial(_mix_kernel, pos0=pos0, pos_stride=pos_stride, zero_first=zero_first),
        grid=(n // t,),
        in_specs=[row_blk(ATTN_WIDTH), row_blk(POOL_WIDTH), pl.BlockSpec((HIST_ROWS, POOL_WIDTH), hist_map),
                  row_blk(D_MODEL), const(pw), const(ps), const(wo), const(g2), const(wrh), const(wrl), const(br),
                  const(tri)],
        out_specs=[row_blk(ROW_WIDTH), row_blk(LANES), pl.BlockSpec((8, LANES), lambda i: (0, 0))],
        out_shape=(jax.ShapeDtypeStruct((n, ROW_WIDTH), F32), jax.ShapeDtypeStruct((n, LANES), F32),
                   jax.ShapeDtypeStruct((8, LANES), F32)),
        compiler_params=_cparams(("arbitrary",)),
        name="pool_mix_route",
    )(a, p, hist_src, x2d, pw, ps, wo, g2, wrh, wrl, br, tri)


def _issue_row_scatter(dest_ref, x_ref, xs_ref, sem):
    t = x_ref.shape[0]

    def issue(g, carry):
        base = pl.multiple_of(g * SUBLANES, SUBLANES)
        tile_rows = x_ref.at[pl.ds(base, SUBLANES)]
        for u in range(SUBLANES):
            pltpu.make_async_copy(tile_rows.at[pl.ds(u, 1)], xs_ref.at[pl.ds(dest_ref[base + u], 1)],
                                  sem).start(priority=u % 2)
        return carry

    lax.fori_loop(0, t // SUBLANES, issue, 0)
    pltpu.make_async_copy(x_ref, xs_ref.at[pl.ds(0, t)], sem).wait()


def _scatter_rows_kernel(fill_ref, dest_a_ref, dest_b_ref, xa_ref, xb_ref, xs_ref, zero_ref, sem, *, steps_a):
    step = pl.program_id(0)

    @pl.when(step == 0)
    def _():
        zero_ref[...] = jnp.zeros_like(zero_ref)
        fill_rows = zero_ref.shape[0]
        last_start = xs_ref.shape[0] - fill_rows
        for g in range(fill_ref.shape[0]):
            @pl.when(fill_ref[g] < xs_ref.shape[0])
            def _(g=g):
                start = pl.multiple_of(jnp.minimum(fill_ref[g], last_start), SUBLANES)
                fill = pltpu.make_async_copy(zero_ref, xs_ref.at[pl.ds(start, fill_rows)], sem)
                fill.start()
                fill.wait()

    @pl.when(step < steps_a)
    def _():
        _issue_row_scatter(dest_a_ref, xa_ref, xs_ref, sem)

    @pl.when(step >= steps_a)
    def _():
        _issue_row_scatter(dest_b_ref, xb_ref, xs_ref, sem)


def _scatter_rows(fill_start, dest_a, xa, dest_b, xb, m_rows, t, fill_rows):
    w = xa.shape[1]
    steps_a, steps_b = xa.shape[0] // t, xb.shape[0] // t
    in_a = lambda i: jnp.minimum(i, steps_a - 1)
    in_b = lambda i: jnp.maximum(i - steps_a, 0)
    smem = pltpu.SMEM
    return pl.pallas_call(
        functools.partial(_scatter_rows_kernel, steps_a=steps_a),
        grid=(steps_a + steps_b,),
        in_specs=[pl.BlockSpec(fill_start.shape, lambda i: (0,), memory_space=smem),
                  pl.BlockSpec((t,), lambda i: (in_a(i),), memory_space=smem),
                  pl.BlockSpec((t,), lambda i: (in_b(i),), memory_space=smem),
                  pl.BlockSpec((t, w), lambda i: (in_a(i), 0)),
                  pl.BlockSpec((t, w), lambda i: (in_b(i), 0))],
        out_specs=pl.BlockSpec(memory_space=pl.ANY),
        out_shape=jax.ShapeDtypeStruct((m_rows, w), xa.dtype),
        scratch_shapes=[pltpu.VMEM((fill_rows, w), xa.dtype), pltpu.SemaphoreType.DMA],
        compiler_params=_cparams(("arbitrary",)),
        name="moe_scatter_rows",
    )(fill_start, dest_a, dest_b, xa, xb)


def _gather_rows_kernel(dest_ref, ys_ref, y_ref, sem):
    t = y_ref.shape[0]

    def issue(g, carry):
        base = pl.multiple_of(g * SUBLANES, SUBLANES)
        tile_rows = y_ref.at[pl.ds(base, SUBLANES)]
        for u in range(SUBLANES):
            pltpu.make_async_copy(ys_ref.at[pl.ds(dest_ref[base + u], 1)], tile_rows.at[pl.ds(u, 1)],
                                  sem).start(priority=u % 2)
        return carry

    lax.fori_loop(0, t // SUBLANES, issue, 0)
    pltpu.make_async_copy(ys_ref.at[pl.ds(0, t)], y_ref, sem).wait()


def _gather_rows(dest, ys, t):
    n = dest.shape[0]
    w = ys.shape[1]
    return pl.pallas_call(
        _gather_rows_kernel,
        grid=(n // t,),
        in_specs=[pl.BlockSpec((t,), lambda i: (i,), memory_space=pltpu.SMEM),
                  pl.BlockSpec(memory_space=pl.ANY)],
        out_specs=pl.BlockSpec((t, w), lambda i: (i, 0)),
        out_shape=jax.ShapeDtypeStruct((n, w), ys.dtype),
        scratch_shapes=[pltpu.SemaphoreType.DMA],
        compiler_params=_cparams(("arbitrary",)),
        name="moe_gather_rows",
    )(dest, ys)


def _moe_kernel(e1_ref, e2_ref, tv_ref, xs_ref, g2_ref, wg1_ref, wu1_ref, wd1_ref, wg2_ref, wu2_ref, wd2_ref, y_ref):
    m = pl.program_id(0)

    @pl.when(tv_ref[m] > 0)
    def _():
        x1 = xs_ref[:, 0:D_MODEL]
        route = xs_ref[:, D_MODEL:]
        lane = lax.broadcasted_iota(jnp.int32, route.shape, 1)
        r = lax.rsqrt(jnp.mean(x1 * x1, axis=-1, keepdims=True) + EPS)
        n2 = ((x1 * r) * g2_ref[...]).astype(BF16)
        acc = x1
        for e_ref, wg_ref, wu_ref, wd_ref in ((e1_ref, wg1_ref, wu1_ref, wd1_ref), (e2_ref, wg2_ref, wu2_ref, wd2_ref)):
            ge = jnp.sum(jnp.where(lane == e_ref[m], route, 0.0), axis=1, keepdims=True)
            hg = jnp.dot(n2, wg_ref[...], preferred_element_type=F32)
            hu = jnp.dot(n2, wu_ref[...], preferred_element_type=F32)
            h = (hg * jax.nn.sigmoid(hg)) * hu * ge
            acc = acc + jnp.dot(h.astype(BF16), wd_ref[...], preferred_element_type=F32)
        y_ref[...] = acc

    @pl.when(tv_ref[m] == 0)
    def _():
        y_ref[...] = jnp.zeros_like(y_ref)


def _moe_sorted(tile_e1, tile_e2, tile_valid, xs, g2, wg, wu, wd, tm):
    m_pad = tile_valid.shape[0] * tm
    up = lambda e: pl.BlockSpec((None, D_MODEL, D_EXPERT), lambda m, e1, e2, tv: ((e1, e2)[e][m], 0, 0))
    down = lambda e: pl.BlockSpec((None, D_EXPERT, D_MODEL), lambda m, e1, e2, tv: ((e1, e2)[e][m], 0, 0))
    grid_spec = pltpu.PrefetchScalarGridSpec(
        num_scalar_prefetch=3,
        grid=(m_pad // tm,),
        in_specs=[
            pl.BlockSpec((tm, ROW_WIDTH), lambda m, e1, e2, tv: (m, 0)),
            pl.BlockSpec(g2.shape, lambda m, e1, e2, tv: (0, 0)),
            up(0), up(0), down(0), up(1), up(1), down(1),
        ],
        out_specs=pl.BlockSpec((tm, D_MODEL), lambda m, e1, e2, tv: (m, 0)),
    )
    return pl.pallas_call(
        _moe_kernel,
        grid_spec=grid_spec,
        out_shape=jax.ShapeDtypeStruct((m_pad, D_MODEL), F32),
        compiler_params=_cparams(("arbitrary",)),
        name="moe_experts",
    )(tile_e1, tile_e2, tile_valid, xs, g2, wg, wu, wd, wg, wu, wd)


def _moe_plan(route_p, cnt_p, route_s, cnt_s, tm):
    n_total = route_p.shape[0] + route_s.shape[0]
    n_tiles = n_total // tm + N_CATEGORIES
    cp = cnt_p[0, :N_CATEGORIES].astype(jnp.int32)
    cs = cnt_s[0, :N_CATEGORIES].astype(jnp.int32)
    total = cp + cs
    tiles = (total + tm - 1) // tm
    tile_end = jnp.cumsum(tiles)
    offset = (tile_end - tiles) * tm
    kp = route_p[:, ROUTE_CATEGORY_LANE].astype(jnp.int32)
    ks = route_s[:, ROUTE_CATEGORY_LANE].astype(jnp.int32)
    dest_p = offset[kp] + route_p[:, ROUTE_RANK_LANE].astype(jnp.int32)
    dest_s = offset[ks] + cp[ks] + route_s[:, ROUTE_RANK_LANE].astype(jnp.int32)
    tile_id = jnp.arange(n_tiles)
    tile_cat = jnp.minimum(jnp.sum(tile_id[:, None] >= tile_end[None, :], axis=1), N_CATEGORIES - 1)
    used = jnp.clip(offset[tile_cat] + total[tile_cat] - tile_id * tm, 0, tm)
    tile_valid = jnp.where(tile_id < tile_end[N_CATEGORIES - 1], used, 0).astype(jnp.int32)
    experts = jnp.asarray(CATEGORY_EXPERTS, jnp.int32)
    tile_e1, tile_e2 = experts[tile_cat, 0], experts[tile_cat, 1]
    m_rows = (n_tiles + 1) * tm
    ends = ((offset + total) // SUBLANES) * SUBLANES
    tail = ends[N_CATEGORIES - 1] + tm * jnp.arange(1, m_rows // tm - n_total // tm + 1)
    fill_start = jnp.concatenate([ends, tail]).astype(jnp.int32)
    return dest_p, dest_s, fill_start, tile_e1, tile_e2, tile_valid, m_rows


def kernel(x_prompt, x_sample, cache_k, cache_v, cache_logf, state_pool, norm1_g, w_in, b_f, q_norm_g, k_norm_g,
           pool_w, pool_scale, w_out, norm2_g, w_router_group, b_router_group, w_router_expert, b_router_expert,
           w_gate, w_up, w_down):
    depth = norm1_g.shape[0]
    assert depth == 1, "single-layer step"
    b_p, s_p, _ = x_prompt.shape
    assert b_p == 1, "prompt kernels assume one stream"
    b_s, t_s, _ = x_sample.shape
    p_len = cache_k.shape[2]
    l = 0

    w = w_in[l]
    a3 = 3 * ATTN_WIDTH
    w_all = jnp.concatenate(
        [w[:, :a3], w[:, a3 + N_HEADS:], jnp.pad(w[:, a3:a3 + N_HEADS], ((0, 0), (0, LANES - N_HEADS)))],
        axis=1).astype(BF16)
    bf_pad = jnp.pad(b_f[l], (0, LANES - N_HEADS)).reshape(1, LANES)
    g1 = norm1_g[l].reshape(1, D_MODEL)
    qg = jnp.tile(q_norm_g[l], N_HEADS).reshape(1, ATTN_WIDTH)
    kg = jnp.tile(k_norm_g[l], N_HEADS).reshape(1, ATTN_WIDTH)
    hr = lax.broadcasted_iota(jnp.int32, (ATTN_WIDTH, ATTN_WIDTH), 0) // HEAD_DIM
    hc = lax.broadcasted_iota(jnp.int32, (ATTN_WIDTH, ATTN_WIDTH), 1) // HEAD_DIM
    hm = jnp.where(hr == hc, 1.0 / HEAD_DIM, 0.0).astype(BF16)
    pw = pool_w[l].astype(BF16)
    ps = pool_scale[l].reshape(1, POOL_WIDTH)
    wo = w_out[l].astype(BF16)
    g2 = norm2_g[l].reshape(1, D_MODEL)
    w_r = jnp.pad(jnp.concatenate([w_router_expert[l], w_router_group[l]], axis=1),
                  ((0, 0), (0, LANES - N_EXPERTS - N_GROUPS)))
    wrh = w_r.astype(BF16)
    wrl = (w_r - wrh.astype(F32)).astype(BF16)
    br = jnp.pad(jnp.concatenate([b_router_expert[l], b_router_group[l]]),
                 (0, LANES - N_EXPERTS - N_GROUPS)).reshape(1, LANES)
    wg = w_gate[l].astype(BF16)
    wu = w_up[l].astype(BF16)
    wd = w_down[l].astype(BF16)

    tile = 512
    query_tile = 512
    key_tile = 256
    expert_tile = 256

    xp = x_prompt.reshape(s_p, D_MODEL)
    q, kf, _, vf, _, ka, vt, pin, logf, c = _in_project(xp, tile, tile, True, key_tile, g1, w_all, bf_pad, qg, kg, hm)
    c_blocks = c[0::key_tile, :N_HEADS]
    j0 = _first_key_blocks(c, q_norm_g[l], k_norm_g[l], query_tile, key_tile)
    a = _attention_prompt(j0, c_blocks, q, ka, vt, query_tile, key_tile)
    hist_map = lambda i: (jnp.maximum(i * (tile // HIST_ROWS) - 1, 0), 0)
    xg_p, route_p, cnt_p = _mix(a, pin, pin, hist_map, xp, tile, 0, tile, True, pw, ps, wo, g2, wrh, wrl, br)
    k_prompt = kf.reshape(depth, b_p, s_p, N_HEADS, HEAD_DIM)
    v_prompt = vf.reshape(depth, b_p, s_p, N_HEADS, HEAD_DIM)
    logf_prompt = logf[:, :N_HEADS].reshape(depth, b_p, s_p, N_HEADS)
    pool_prompt = pin[s_p - POOL_PAD:].reshape(depth, b_p, POOL_PAD, POOL_WIDTH)

    n_s = b_s * t_s
    xs = x_sample.reshape(n_s, D_MODEL)
    q, kf, kb, vf, vb, _, _, pin, logf, d = _in_project(xs, tile, t_s, False, key_tile, g1, w_all, bf_pad, qg, kg, hm)
    dt4 = d[:, :N_HEADS].reshape(b_s, t_s, N_PAIRS, 2).transpose(0, 2, 3, 1)
    clf = cache_logf[l].astype(F32).transpose(0, 2, 1).reshape(b_s * N_HEADS, p_len)
    r4 = _suffix_sums(clf).reshape(b_s, N_PAIRS, 2, p_len)
    ck = cache_k[l].astype(BF16).reshape(b_s, p_len, ATTN_WIDTH)
    cv = cache_v[l].astype(BF16).reshape(b_s, p_len, ATTN_WIDTH)
    a = _attention_sample(q, kb, vb, ck, cv, d, dt4, r4, t_s)
    hist_s = jnp.pad(state_pool[l], ((0, 0), (HIST_ROWS - POOL_PAD, 0), (0, 0))).reshape(b_s * HIST_ROWS, POOL_WIDTH)
    xg_s, route_s, cnt_s = _mix(a, pin, hist_s, lambda i: (i, 0), xs, t_s, p_len, 0, False, pw, ps, wo, g2, wrh, wrl,
                                br)
    k_sample = kf.reshape(depth, b_s, t_s, N_HEADS, HEAD_DIM)
    v_sample = vf.reshape(depth, b_s, t_s, N_HEADS, HEAD_DIM)
    logf_sample = logf[:, :N_HEADS].reshape(depth, b_s, t_s, N_HEADS)
    pool_sample = pin.reshape(b_s, t_s, POOL_WIDTH)[:, t_s - POOL_PAD:].reshape(depth, b_s, POOL_PAD, POOL_WIDTH)

    dest_p, dest_s, fill_start, tile_e1, tile_e2, tile_valid, m_rows = _moe_plan(route_p, cnt_p, route_s, cnt_s,
                                                                              expert_tile)
    rows = _scatter_rows(fill_start, dest_p, xg_p, dest_s, xg_s, m_rows, tile, expert_tile)
    ys = _moe_sorted(tile_e1, tile_e2, tile_valid, rows, g2, wg, wu, wd, expert_tile)
    y_prompt = _gather_rows(dest_p, ys, tile).reshape(b_p, s_p, D_MODEL)
    y_sample = _gather_rows(dest_s, ys, tile).reshape(b_s, t_s, D_MODEL)

    return (y_prompt, y_sample, k_prompt, v_prompt, logf_prompt, pool_prompt,
            k_sample, v_sample, logf_sample, pool_sample)
```

```python
import functools

import jax
import jax.numpy as jnp
from jax import lax
from jax.experimental import pallas as pl
from jax.experimental.pallas import tpu as pltpu

F32 = jnp.float32
BF16 = jnp.bfloat16

D_MODEL = 1024
ATTN_WIDTH = 512
N_HEADS = 8
HEAD_DIM = 64
POOL_WIDTH = 512
POOL_WINDOWS = (2, 4, 8, 16)
POOL_GROUP_DIM = 128
POOL_PAD = 15
HIST_ROWS = 16
N_GROUPS = 4
EXPERTS_PER_GROUP = 4
N_EXPERTS = 16
D_EXPERT = 512
EPS = 1e-6
LANES = 128
SUBLANES = 8
PAIR = 2 * HEAD_DIM
N_PAIRS = N_HEADS // 2
ROWSUM_ROW = (HEAD_DIM, 0)
VMEM_LIMIT = 56 * 1024 * 1024
ROW_WIDTH = D_MODEL + LANES
PAIRS_PER_GROUP = 6
N_CATEGORIES = N_GROUPS * PAIRS_PER_GROUP
CATEGORY_EXPERTS = tuple((4 * g + a, 4 * g + b) for g in range(N_GROUPS) for a in range(4) for b in range(a + 1, 4))
ROUTE_CATEGORY_LANE = N_EXPERTS
ROUTE_RANK_LANE = N_EXPERTS + 1


def _cparams(sem):
    return pltpu.CompilerParams(dimension_semantics=sem, vmem_limit_bytes=VMEM_LIMIT)


def _split3(x):
    hi = x.astype(BF16)
    r1 = x - hi.astype(F32)
    mid = r1.astype(BF16)
    lo = (r1 - mid.astype(F32)).astype(BF16)
    return hi, mid, lo


def _split2(x):
    hi = x.astype(BF16)
    lo = (x - hi.astype(F32)).astype(BF16)
    return hi, lo


def _inproj_kernel(x_ref, g1_ref, w_ref, bf_ref, qg_ref, kg_ref, hm_ref, tri_ref, place_ref,
                   q_ref, kf_ref, kb_ref, vf_ref, vb_ref, ka_ref, vt_ref, p_ref, logf_ref, c_ref, edge_ref, carry_ref, *,
                   carry_rows, key_blk):
    x = x_ref[...]
    r = lax.rsqrt(jnp.mean(x * x, axis=-1, keepdims=True) + EPS)
    n = ((x * r) * g1_ref[...]).astype(BF16)
    z = jnp.dot(n, w_ref[...], preferred_element_type=F32)

    hm = hm_ref[...]

    def head_norm(zz, g):
        hi, lo = _split2(zz * zz)
        ms = jnp.dot(hi, hm, preferred_element_type=F32) + jnp.dot(lo, hm, preferred_element_type=F32)
        return (zz * lax.rsqrt(ms + EPS)) * g

    q = head_norm(z[:, 0:ATTN_WIDTH], qg_ref[...])
    q_ref[...] = (q * (HEAD_DIM ** -0.5)).astype(BF16)
    k = head_norm(z[:, ATTN_WIDTH:2 * ATTN_WIDTH], kg_ref[...])
    kf_ref[...] = k
    kb_ref[...] = k.astype(BF16)
    v = z[:, 2 * ATTN_WIDTH:3 * ATTN_WIDTH]
    vf_ref[...] = v
    vb_ref[...] = v.astype(BF16)
    p_ref[...] = z[:, 3 * ATTN_WIDTH:3 * ATTN_WIDTH + POOL_WIDTH]
    prow = lax.broadcasted_iota(jnp.int32, (PAIR, v.shape[0]), 0)
    for hp in range(N_PAIRS):
        vpair_t = v[:, hp * PAIR:(hp + 1) * PAIR].T
        for h in range(2):
            own = (prow < HEAD_DIM) if h == 0 else (prow >= HEAD_DIM)
            marker = jnp.where(prow == ROWSUM_ROW[h], 1.0, 0.0)
            vt_ref[(2 * hp + h) * LANES:(2 * hp + h + 1) * LANES, :] = jnp.where(own, vpair_t, marker).astype(BF16)

    f = z[:, 3 * ATTN_WIDTH + POOL_WIDTH:] + bf_ref[...]
    lane = lax.broadcasted_iota(jnp.int32, f.shape, 1)
    logf = -(jnp.maximum(-f, 0.0) + jnp.log1p(jnp.exp(-jnp.abs(f))))
    logf = jnp.where(lane < N_HEADS, logf, 0.0)
    logf_ref[...] = logf

    tri = tri_ref[...]
    hi, mid, lo = _split3(logf)
    c = (jnp.dot(tri, hi, preferred_element_type=F32) + jnp.dot(tri, mid, preferred_element_type=F32)
         + jnp.dot(tri, lo, preferred_element_type=F32))
    if carry_rows:
        @pl.when(pl.program_id(0) == 0)
        def _():
            carry_ref[...] = jnp.zeros_like(carry_ref)
        c = c + carry_ref[0:1, :]
        carry_ref[...] = jnp.broadcast_to(c[c.shape[0] - 1:, :], carry_ref.shape)
    c_ref[...] = c
    edges = [c[r:r + 1, :] for b in range(c.shape[0] // key_blk) for r in (b * key_blk, (b + 1) * key_blk - 1)]
    edge_ref[...] = jnp.concatenate(edges + [jnp.zeros((SUBLANES - len(edges), LANES), F32)], axis=0)

    tm = c.shape[0]
    rel = jnp.concatenate([c[r:r + key_blk, :] - c[r:r + 1, :] for r in range(0, tm, key_blk)], axis=0)
    parts = jnp.concatenate(_split3(rel), axis=1)
    decay = jnp.dot(parts, place_ref[...], preferred_element_type=F32).astype(BF16)
    kb = k.astype(BF16)
    for hp in range(N_PAIRS):
        ka_ref[:, 2 * hp * LANES:(2 * hp + 1) * LANES] = kb[:, hp * PAIR:(hp + 1) * PAIR]
        ka_ref[:, (2 * hp + 1) * LANES:(2 * hp + 2) * LANES] = decay[:, hp * LANES:(hp + 1) * LANES]


def _decay_lane(h, part):
    return 3 * h + part


def _in_project(x2d, tm, seg, carry_rows, key_blk, g1, w_all, bf_pad, qg, kg, hm):
    n = x2d.shape[0]
    rows = lax.broadcasted_iota(jnp.int32, (tm, tm), 0)
    cols = lax.broadcasted_iota(jnp.int32, (tm, tm), 1)
    tri = ((cols <= rows) & (rows // seg == cols // seg)).astype(BF16)
    src = jnp.arange(3 * LANES)
    part, head = src // LANES, src % LANES
    dst = (head // 2) * LANES + _decay_lane(head % 2, part)
    place = ((jnp.arange(N_PAIRS * LANES)[None, :] == dst[:, None]) & (head < N_HEADS)[:, None]).astype(BF16)
    row_blk = lambda w: pl.BlockSpec((tm, w), lambda i: (i, 0))
    const = lambda a: pl.BlockSpec(a.shape, lambda i: (0,) * a.ndim)
    out_shape = (
        jax.ShapeDtypeStruct((n, ATTN_WIDTH), BF16),
        jax.ShapeDtypeStruct((n, ATTN_WIDTH), F32),
        jax.ShapeDtypeStruct((n, ATTN_WIDTH), BF16),
        jax.ShapeDtypeStruct((n, ATTN_WIDTH), F32),
        jax.ShapeDtypeStruct((n, ATTN_WIDTH), BF16),
        jax.ShapeDtypeStruct((n, N_PAIRS * 2 * LANES), BF16),
        jax.ShapeDtypeStruct((N_HEADS * LANES, n), BF16),
        jax.ShapeDtypeStruct((n, POOL_WIDTH), F32),
        jax.ShapeDtypeStruct((n, LANES), F32),
        jax.ShapeDtypeStruct((n, LANES), F32),
        jax.ShapeDtypeStruct((n // tm * SUBLANES, LANES), F32),
    )
    return pl.pallas_call(
        functools.partial(_inproj_kernel, carry_rows=carry_rows, key_blk=key_blk),
        grid=(n // tm,),
        in_specs=[row_blk(D_MODEL), const(g1), const(w_all), const(bf_pad), const(qg), const(kg), const(hm),
                  const(tri), const(place)],
        out_specs=[row_blk(ATTN_WIDTH)] * 5 + [row_blk(N_PAIRS * 2 * LANES),
                                                pl.BlockSpec((N_HEADS * LANES, tm), lambda i: (0, i)),
                                                row_blk(POOL_WIDTH), row_blk(LANES), row_blk(LANES),
                                                pl.BlockSpec((SUBLANES, LANES), lambda i: (i, 0))],
        out_shape=out_shape,
        scratch_shapes=[pltpu.VMEM((8, LANES), F32)],
        compiler_params=_cparams(("arbitrary",)),
        name="in_project",
    )(x2d, g1, w_all, bf_pad, qg, kg, hm, tri, place)


def _attn_prompt_kernel(j0_ref, cb_ref, q_ref, k_ref, vt_ref, o_ref, acc_ref, m_ref, *, tq, tk):
    hp = pl.program_id(0)
    i = pl.program_id(1)
    nk = tq // tk
    lane = lax.broadcasted_iota(jnp.int32, (1, PAIR), 1)
    first = lane < HEAD_DIM
    q = q_ref[...]
    zero = jnp.zeros_like(q)
    qw = []
    for h in range(2):
        minus = jnp.zeros((1, LANES), F32)
        for part in range(3):
            minus = jnp.where(lane == _decay_lane(h, part), -1.0, minus)
        qw.append(jnp.concatenate([jnp.where(first == (h == 0), q, zero),
                                   jnp.broadcast_to(minus, (tq, LANES)).astype(BF16)], axis=1))
    q_start = i * tq

    acc_ref[...] = jnp.zeros_like(acc_ref)
    m_ref[...] = jnp.full_like(m_ref, -jnp.inf)

    last = (i + 1) * nk - 1

    def scores(h, j, masked=True):
        start = pl.multiple_of(jnp.minimum(j, last) * tk, tk)
        k = k_ref[pl.ds(start, tk), :]
        s = lax.dot_general(k, qw[h], (((1,), (1,)), ((), ())), preferred_element_type=F32)
        if masked:
            key = lax.broadcasted_iota(jnp.int32, s.shape, 0)
            qry = lax.broadcasted_iota(jnp.int32, s.shape, 1)
            s = jnp.where(key - qry <= q_start - j * tk, s, -jnp.inf)
        return s, jnp.max(s, axis=0, keepdims=True)

    def update(h, j, s, colmax):
        jc = jnp.minimum(j, last)
        shift = cb_ref[jc, 2 * hp + h] - cb_ref[i * nk, 2 * hp + h]
        start = pl.multiple_of(jc * tk, tk)
        vt = vt_ref[h * LANES:(h + 1) * LANES, pl.ds(start, tk)]
        m_prev = m_ref[h]
        m_new = jnp.maximum(m_prev, colmax - shift)
        alpha = jnp.exp(m_prev - m_new)
        p = jnp.exp(s - (m_new + shift))
        m_ref[h] = m_new
        acc_ref[h] = acc_ref[h] * alpha + jnp.dot(vt, p.astype(BF16), preferred_element_type=F32)

    j_first = [j0_ref[i, 2 * hp + h] for h in range(2)]
    j_both = jnp.maximum(j_first[0], j_first[1])
    start = [scores(h, j_first[h]) for h in range(2)]
    ahead = []
    for h in range(2):
        def body_one(j, carry, h=h):
            nxt = scores(h, j + 1)
            update(h, j, *carry)
            return nxt

        ahead.append(lax.fori_loop(j_first[h], j_both, body_one, start[h]))

    def body_both(j, carry, masked):
        nxt = (scores(0, j + 1, masked), scores(1, j + 1, masked))
        update(0, j, *carry[0])
        update(1, j, *carry[1])
        return nxt

    j_mask = jnp.maximum(j_both, i * nk - 1)
    pairs = (j_mask - j_both) // 2

    def body_two(t, carry):
        j = j_both + 2 * t
        return body_both(j + 1, body_both(j, carry, False), False)

    carry = lax.fori_loop(0, pairs, body_two, (ahead[0], ahead[1]))
    lax.fori_loop(j_both + 2 * pairs, last + 1, functools.partial(body_both, masked=True), carry)

    out = [acc_ref[h] / acc_ref[h][ROWSUM_ROW[h]:ROWSUM_ROW[h] + 1, :] for h in range(2)]
    row = lax.broadcasted_iota(jnp.int32, (PAIR, 1), 0)
    o_ref[...] = jnp.where(row < HEAD_DIM, out[0], out[1]).T.astype(o_ref.dtype)


def _attention_prompt(j0, c_blocks, q, ka, vt, tq, tk):
    n = q.shape[0]
    grid_spec = pltpu.PrefetchScalarGridSpec(
        num_scalar_prefetch=2,
        grid=(N_PAIRS, n // tq),
        in_specs=[
            pl.BlockSpec((tq, PAIR), lambda hp, i, j0, cb: (i, hp)),
            pl.BlockSpec((n, 2 * LANES), lambda hp, i, j0, cb: (0, hp)),
            pl.BlockSpec((2 * LANES, n), lambda hp, i, j0, cb: (hp, 0)),
        ],
        out_specs=pl.BlockSpec((tq, PAIR), lambda hp, i, j0, cb: (i, hp)),
        scratch_shapes=[pltpu.VMEM((2, LANES, tq), F32), pltpu.VMEM((2, 1, tq), F32)],
    )
    return pl.pallas_call(
        functools.partial(_attn_prompt_kernel, tq=tq, tk=tk),
        grid_spec=grid_spec,
        out_shape=jax.ShapeDtypeStruct((n, ATTN_WIDTH), BF16),
        compiler_params=_cparams(("arbitrary", "arbitrary")),
        name="attention_prompt",
    )(j0, c_blocks, q, ka, vt)


def _first_key_blocks(c_first, c_last, q_gain, k_gain, tq, tk):
    n_k = c_first.shape[0]
    n_q = n_k * tk // tq
    qk_bound = 1.02 * HEAD_DIM ** 0.5 * jnp.max(jnp.abs(q_gain)) * jnp.max(jnp.abs(k_gain))
    threshold = 104.0 + 2.0 * qk_bound + 1.0
    c_start = c_first[0::tq // tk]
    gap = c_start[:, None, :] - c_last[None, :, :]
    earlier = (jnp.arange(n_k)[None, :, None] + 1) * tk <= jnp.arange(n_q)[:, None, None] * tq
    return jnp.sum((gap < -threshold) & earlier, axis=1).astype(jnp.int32)


def _suffix_kernel(x_ref, o_ref):
    x = x_ref[...]
    p_len = x.shape[1]
    rows = lax.broadcasted_iota(jnp.int32, (p_len, p_len), 0)
    cols = lax.broadcasted_iota(jnp.int32, (p_len, p_len), 1)
    u = (rows > cols).astype(BF16)
    hi, mid, lo = _split3(x)
    o_ref[...] = (jnp.dot(hi, u, preferred_element_type=F32) + jnp.dot(mid, u, preferred_element_type=F32)
                  + jnp.dot(lo, u, preferred_element_type=F32))


def _suffix_sums(x):
    return pl.pallas_call(
        _suffix_kernel,
        out_shape=jax.ShapeDtypeStruct(x.shape, F32),
        compiler_params=pltpu.CompilerParams(vmem_limit_bytes=VMEM_LIMIT),
        name="cache_suffix_sums",
    )(x)


def _attn_sample_kernel(q_ref, kn_ref, vn_ref, ck_ref, cv_ref, d_ref, dt_ref, r_ref, o_ref):
    hp = pl.program_id(1)
    lane = lax.broadcasted_iota(jnp.int32, (1, PAIR), 1)
    first = lane < HEAD_DIM
    q = q_ref[...]
    zero = jnp.zeros_like(q)
    qh = (jnp.where(first, q, zero), jnp.where(first, zero, q))
    dblk = d_ref[...]
    dlane = lax.broadcasted_iota(jnp.int32, dblk.shape, 1)
    kn = kn_ref[...]
    vn = vn_ref[...]
    ck = ck_ref[...]
    cv = cv_ref[...]
    nt = (((1,), (1,)), ((), ()))
    outs, ls = [], []
    for h in range(2):
        dq = jnp.sum(jnp.where(dlane == 2 * hp + h, dblk, 0.0), axis=1, keepdims=True)
        s1 = lax.dot_general(qh[h], ck, nt, preferred_element_type=F32) + dq + r_ref[h:h + 1, :]
        s2 = lax.dot_general(qh[h], kn, nt, preferred_element_type=F32) + dq - dt_ref[h:h + 1, :]
        row = lax.broadcasted_iota(jnp.int32, s2.shape, 0)
        col = lax.broadcasted_iota(jnp.int32, s2.shape, 1)
        s2 = jnp.where(col <= row, s2, -jnp.inf)
        m = jnp.maximum(jnp.max(s1, axis=1, keepdims=True), jnp.max(s2, axis=1, keepdims=True))
        p1 = jnp.exp(s1 - m)
        p2 = jnp.exp(s2 - m)
        ls.append(jnp.sum(p1, axis=1, keepdims=True) + jnp.sum(p2, axis=1, keepdims=True))
        outs.append(jnp.dot(p1.astype(BF16), cv, preferred_element_type=F32)
                    + jnp.dot(p2.astype(BF16), vn, preferred_element_type=F32))
    o_ref[...] = (jnp.where(first, outs[0], outs[1]) / jnp.where(first, ls[0], ls[1])).astype(o_ref.dtype)


def _attention_sample(q, kb, vb, cache_k, cache_v, d, dt4, r4, t):
    nb, p_len = cache_k.shape[0], cache_k.shape[1]
    return pl.pallas_call(
        _attn_sample_kernel,
        grid=(nb, N_PAIRS),
        in_specs=[
            pl.BlockSpec((t, PAIR), lambda b, hp: (b, hp)),
            pl.BlockSpec((t, PAIR), lambda b, hp: (b, hp)),
            pl.BlockSpec((t, PAIR), lambda b, hp: (b, hp)),
            pl.BlockSpec((None, p_len, PAIR), lambda b, hp: (b, 0, hp)),
            pl.BlockSpec((None, p_len, PAIR), lambda b, hp: (b, 0, hp)),
            pl.BlockSpec((t, LANES), lambda b, hp: (b, 0)),
            pl.BlockSpec((None, None, 2, t), lambda b, hp: (b, hp, 0, 0)),
            pl.BlockSpec((None, None, 2, p_len), lambda b, hp: (b, hp, 0, 0)),
        ],
        out_specs=pl.BlockSpec((t, PAIR), lambda b, hp: (b, hp)),
        out_shape=jax.ShapeDtypeStruct((nb * t, ATTN_WIDTH), BF16),
        compiler_params=_cparams(("arbitrary", "arbitrary")),
        name="attention_sample",
    )(q, kb, vb, cache_k, cache_v, d, dt4, r4)


def _mix_kernel(a_ref, p_ref, hist_ref, x_ref, pw_ref, ps_ref, wo_ref, g2_ref, wrh_ref, wrl_ref, br_ref, tri_ref,
                xg_ref, plan_ref, cnt_ref, *, pos0, pos_stride, zero_first):
    i = pl.program_id(0)
    t = p_ref.shape[0]
    p = p_ref[...]
    hist = hist_ref[...]
    if zero_first:
        hist = jnp.where(i == 0, 0.0, hist)
    xh = jnp.concatenate([hist, p], axis=0)
    pos = pos0 + i * pos_stride + lax.broadcasted_iota(jnp.int32, (t, 1), 0)
    ys = []
    for g, w in enumerate(POOL_WINDOWS):
        col = xh[:, g * POOL_GROUP_DIM:(g + 1) * POOL_GROUP_DIM]
        acc = col
        span = 1
        while span < w:
            acc = acc + pltpu.roll(acc, span, axis=0)
            span *= 2
        cnt = jnp.minimum(pos + 1, w).astype(F32)
        dgrp = acc[HIST_ROWS:, :] / cnt - col[HIST_ROWS:, :]
        ys.append(jnp.dot(dgrp.astype(BF16), pw_ref[g], preferred_element_type=F32))
    pm = (jnp.concatenate(ys, axis=1) * ps_ref[...]).astype(BF16)
    mix = (jnp.dot(a_ref[...], wo_ref[0:ATTN_WIDTH, :], preferred_element_type=F32)
           + jnp.dot(pm, wo_ref[ATTN_WIDTH:, :], preferred_element_type=F32))
    x1 = x_ref[...] + mix
    xg_ref[:, 0:D_MODEL] = x1

    r = lax.rsqrt(jnp.mean(x1 * x1, axis=-1, keepdims=True) + EPS)
    n2 = (x1 * r) * g2_ref[...]

    nh, nl = _split2(n2)
    wrh = wrh_ref[...]
    logits = (jnp.dot(nh, wrh, preferred_element_type=F32) + jnp.dot(nl, wrh, preferred_element_type=F32)
              + jnp.dot(nh, wrl_ref[...], preferred_element_type=F32)) + br_ref[...]
    lane = lax.broadcasted_iota(jnp.int32, logits.shape, 1).astype(F32)
    neg = -jnp.inf
    big = jnp.float32(1 << 20)
    is_g = (lane >= N_EXPERTS) & (lane < N_EXPERTS + N_GROUPS)
    lg = jnp.where(is_g, logits, neg)
    gmax = jnp.max(lg, axis=1, keepdims=True)
    g_lane = jnp.min(jnp.where(lg == gmax, lane, big), axis=1, keepdims=True)
    top_pg = 1.0 / jnp.sum(jnp.exp(lg - gmax), axis=1, keepdims=True)
    g_idx = g_lane - N_EXPERTS
    sel = (lane >= g_idx * EXPERTS_PER_GROUP) & (lane < (g_idx + 1) * EXPERTS_PER_GROUP)
    le = jnp.where(sel, logits, neg)
    m1 = jnp.max(le, axis=1, keepdims=True)
    i1 = jnp.min(jnp.where(le == m1, lane, big), axis=1, keepdims=True)
    le2 = jnp.where(lane == i1, neg, le)
    m2 = jnp.max(le2, axis=1, keepdims=True)
    i2 = jnp.min(jnp.where(le2 == m2, lane, big), axis=1, keepdims=True)
    e2 = jnp.exp(m2 - m1)
    w1 = top_pg / (1.0 + e2)
    w2 = top_pg * e2 / (1.0 + e2)
    gate = jnp.where(lane == i1, w1, 0.0) + jnp.where(lane == i2, w2, 0.0)

    @pl.when(i == 0)
    def _():
        cnt_ref[...] = jnp.zeros_like(cnt_ref)
    e_lo = jnp.minimum(i1, i2) - g_idx * EXPERTS_PER_GROUP
    e_hi = jnp.maximum(i1, i2) - g_idx * EXPERTS_PER_GROUP
    category = g_idx * PAIRS_PER_GROUP + (e_lo * (7.0 - e_lo) * 0.5 + (e_hi - e_lo - 1.0))
    member = lane == category
    before = jnp.dot(tri_ref[...], member.astype(BF16), preferred_element_type=F32) + cnt_ref[0:1, :]
    rank = jnp.sum(jnp.where(member, before, 0.0), axis=1, keepdims=True)
    cnt_ref[...] = cnt_ref[...] + jnp.sum(member.astype(F32), axis=0, keepdims=True)
    route = jnp.where(lane == ROUTE_CATEGORY_LANE, category, jnp.where(lane == ROUTE_RANK_LANE, rank, gate))
    xg_ref[:, D_MODEL:] = route
    srow = lax.broadcasted_iota(jnp.int32, (SUBLANES, LANES), 0)
    slane = lax.broadcasted_iota(jnp.int32, (SUBLANES, LANES), 1)
    pick = (((srow == 0) & (slane == ROUTE_CATEGORY_LANE)) | ((srow == 1) & (slane == ROUTE_RANK_LANE))).astype(BF16)
    nt = (((1,), (1,)), ((), ()))
    ids = jnp.where(lane >= N_EXPERTS, route, 0.0)
    hi, lo = _split2(ids)
    plan_ref[...] = (lax.dot_general(pick, hi, nt, preferred_element_type=F32)
                     + lax.dot_general(pick, lo, nt, preferred_element_type=F32))


def _mix(a, p, hist_src, hist_map, x2d, t, pos0, pos_stride, zero_first, pw, ps, wo, g2, wrh, wrl, br):
    n = x2d.shape[0]
    rows = lax.broadcasted_iota(jnp.int32, (t, t), 0)
    cols = lax.broadcasted_iota(jnp.int32, (t, t), 1)
    tri = (cols < rows).astype(BF16)
    row_blk = lambda w: pl.BlockSpec((t, w), lambda i: (i, 0))
    const = lambda arr: pl.BlockSpec(arr.shape, lambda i: (0,) * arr.ndim)
    return pl.pallas_call(
        functools.partial(_mix_kernel, pos0=pos0, pos_stride=pos_stride, zero_first=zero_first),
        grid=(n // t,),
        in_specs=[row_blk(ATTN_WIDTH), row_blk(POOL_WIDTH), pl.BlockSpec((HIST_ROWS, POOL_WIDTH), hist_map),
                  row_blk(D_MODEL), const(pw), const(ps), const(wo), const(g2), const(wrh), const(wrl), const(br),
                  const(tri)],
        out_specs=[row_blk(ROW_WIDTH), pl.BlockSpec((SUBLANES, t), lambda i: (i, 0)),
                   pl.BlockSpec((SUBLANES, LANES), lambda i: (0, 0))],
        out_shape=(jax.ShapeDtypeStruct((n, ROW_WIDTH), F32), jax.ShapeDtypeStruct((n // t * SUBLANES, t), F32),
                   jax.ShapeDtypeStruct((SUBLANES, LANES), F32)),
        compiler_params=_cparams(("arbitrary",)),
        name="pool_mix_route",
    )(a, p, hist_src, x2d, pw, ps, wo, g2, wrh, wrl, br, tri)


def _issue_row_scatter(dest_ref, x_ref, xs_ref, sem):
    t = x_ref.shape[0]

    def issue(g, carry):
        base = pl.multiple_of(g * SUBLANES, SUBLANES)
        tile_rows = x_ref.at[pl.ds(base, SUBLANES)]
        for u in range(SUBLANES):
            pltpu.make_async_copy(tile_rows.at[pl.ds(u, 1)], xs_ref.at[pl.ds(dest_ref[base + u], 1)],
                                  sem).start(priority=u % 2)
        return carry

    lax.fori_loop(0, t // SUBLANES, issue, 0)
    pltpu.make_async_copy(x_ref, xs_ref.at[pl.ds(0, t)], sem).wait()


def _scatter_rows_kernel(fill_ref, dest_a_ref, dest_b_ref, xa_ref, xb_ref, xs_ref, zero_ref, sem, *, steps_a):
    step = pl.program_id(0)

    @pl.when(step == 0)
    def _():
        zero_ref[...] = jnp.zeros_like(zero_ref)
        fill_rows = zero_ref.shape[0]
        last_start = xs_ref.shape[0] - fill_rows
        for g in range(fill_ref.shape[0]):
            @pl.when(fill_ref[g] < xs_ref.shape[0])
            def _(g=g):
                start = pl.multiple_of(jnp.minimum(fill_ref[g], last_start), SUBLANES)
                fill = pltpu.make_async_copy(zero_ref, xs_ref.at[pl.ds(start, fill_rows)], sem)
                fill.start()
                fill.wait()

    @pl.when(step < steps_a)
    def _():
        _issue_row_scatter(dest_a_ref, xa_ref, xs_ref, sem)

    @pl.when(step >= steps_a)
    def _():
        _issue_row_scatter(dest_b_ref, xb_ref, xs_ref, sem)


def _scatter_rows(fill_start, dest_a, xa, dest_b, xb, m_rows, t, fill_rows):
    w = xa.shape[1]
    steps_a, steps_b = xa.shape[0] // t, xb.shape[0] // t
    in_a = lambda i: jnp.minimum(i, steps_a - 1)
    in_b = lambda i: jnp.maximum(i - steps_a, 0)
    smem = pltpu.SMEM
    return pl.pallas_call(
        functools.partial(_scatter_rows_kernel, steps_a=steps_a),
        grid=(steps_a + steps_b,),
        in_specs=[pl.BlockSpec(fill_start.shape, lambda i: (0,), memory_space=smem),
                  pl.BlockSpec((t,), lambda i: (in_a(i),), memory_space=smem),
                  pl.BlockSpec((t,), lambda i: (in_b(i),), memory_space=smem),
                  pl.BlockSpec((t, w), lambda i: (in_a(i), 0)),
                  pl.BlockSpec((t, w), lambda i: (in_b(i), 0))],
        out_specs=pl.BlockSpec(memory_space=pl.ANY),
        out_shape=jax.ShapeDtypeStruct((m_rows, w), xa.dtype),
        scratch_shapes=[pltpu.VMEM((fill_rows, w), xa.dtype), pltpu.SemaphoreType.DMA],
        compiler_params=_cparams(("arbitrary",)),
        name="moe_scatter_rows",
    )(fill_start, dest_a, dest_b, xa, xb)


def _gather_rows_kernel(dest_ref, ys_ref, y_ref, sem):
    t = y_ref.shape[0]

    def issue(g, carry):
        base = pl.multiple_of(g * SUBLANES, SUBLANES)
        tile_rows = y_ref.at[pl.ds(base, SUBLANES)]
        for u in range(SUBLANES):
            pltpu.make_async_copy(ys_ref.at[pl.ds(dest_ref[base + u], 1)], tile_rows.at[pl.ds(u, 1)],
                                  sem).start(priority=u % 2)
        return carry

    lax.fori_loop(0, t // SUBLANES, issue, 0)
    pltpu.make_async_copy(ys_ref.at[pl.ds(0, t)], y_ref, sem).wait()


def _gather_rows(dest, ys, t):
    n = dest.shape[0]
    w = ys.shape[1]
    return pl.pallas_call(
        _gather_rows_kernel,
        grid=(n // t,),
        in_specs=[pl.BlockSpec((t,), lambda i: (i,), memory_space=pltpu.SMEM),
                  pl.BlockSpec(memory_space=pl.ANY)],
        out_specs=pl.BlockSpec((t, w), lambda i: (i, 0)),
        out_shape=jax.ShapeDtypeStruct((n, w), ys.dtype),
        scratch_shapes=[pltpu.SemaphoreType.DMA],
        compiler_params=_cparams(("arbitrary",)),
        name="moe_gather_rows",
    )(dest, ys)


def _moe_kernel(e1_ref, e2_ref, tv_ref, xs_ref, g2_ref, wg1_ref, wu1_ref, wd1_ref, wg2_ref, wu2_ref, wd2_ref, y_ref):
    m = pl.program_id(0)

    @pl.when(tv_ref[m] > 0)
    def _():
        x1 = xs_ref[:, 0:D_MODEL]
        route = xs_ref[:, D_MODEL:]
        lane = lax.broadcasted_iota(jnp.int32, route.shape, 1)
        r = lax.rsqrt(jnp.mean(x1 * x1, axis=-1, keepdims=True) + EPS)
        n2 = ((x1 * r) * g2_ref[...]).astype(BF16)
        acc = x1
        for e_ref, wg_ref, wu_ref, wd_ref in ((e1_ref, wg1_ref, wu1_ref, wd1_ref), (e2_ref, wg2_ref, wu2_ref, wd2_ref)):
            ge = jnp.sum(jnp.where(lane == e_ref[m], route, 0.0), axis=1, keepdims=True)
            hg = jnp.dot(n2, wg_ref[...], preferred_element_type=F32)
            hu = jnp.dot(n2, wu_ref[...], preferred_element_type=F32)
            h = (hg * jax.nn.sigmoid(hg)) * hu * ge
            acc = acc + jnp.dot(h.astype(BF16), wd_ref[...], preferred_element_type=F32)
        y_ref[...] = acc

    @pl.when(tv_ref[m] == 0)
    def _():
        y_ref[...] = jnp.zeros_like(y_ref)


def _moe_sorted(tile_e1, tile_e2, tile_valid, xs, g2, wg, wu, wd, tm):
    m_pad = tile_valid.shape[0] * tm
    up = lambda e: pl.BlockSpec((None, D_MODEL, D_EXPERT), lambda m, e1, e2, tv: ((e1, e2)[e][m], 0, 0))
    down = lambda e: pl.BlockSpec((None, D_EXPERT, D_MODEL), lambda m, e1, e2, tv: ((e1, e2)[e][m], 0, 0))
    grid_spec = pltpu.PrefetchScalarGridSpec(
        num_scalar_prefetch=3,
        grid=(m_pad // tm,),
        in_specs=[
            pl.BlockSpec((tm, ROW_WIDTH), lambda m, e1, e2, tv: (m, 0)),
            pl.BlockSpec(g2.shape, lambda m, e1, e2, tv: (0, 0)),
            up(0), up(0), down(0), up(1), up(1), down(1),
        ],
        out_specs=pl.BlockSpec((tm, D_MODEL), lambda m, e1, e2, tv: (m, 0)),
    )
    return pl.pallas_call(
        _moe_kernel,
        grid_spec=grid_spec,
        out_shape=jax.ShapeDtypeStruct((m_pad, D_MODEL), F32),
        compiler_params=_cparams(("arbitrary",)),
        name="moe_experts",
    )(tile_e1, tile_e2, tile_valid, xs, g2, wg, wu, wd, wg, wu, wd)


def _moe_plan(plan_p, cnt_p, plan_s, cnt_s, tm):
    def ids(plan):
        blocks = plan.reshape(-1, SUBLANES, plan.shape[1]).astype(jnp.int32)
        return blocks[:, 0, :], blocks[:, 1, :]

    def lookup(table, cat):
        out = jnp.zeros_like(cat)
        for c in range(N_CATEGORIES):
            out = jnp.where(cat == c, table[c], out)
        return out

    (kp, rank_p), (ks, rank_s) = ids(plan_p), ids(plan_s)
    n_total = kp.size + ks.size
    n_tiles = n_total // tm + N_CATEGORIES
    cp = cnt_p[0, :N_CATEGORIES].astype(jnp.int32)
    cs = cnt_s[0, :N_CATEGORIES].astype(jnp.int32)
    total = cp + cs
    tiles = (total + tm - 1) // tm
    tile_end = jnp.cumsum(tiles)
    offset = (tile_end - tiles) * tm
    dest_p = (lookup(offset, kp) + rank_p).reshape(-1)
    dest_s = (lookup(offset + cp, ks) + rank_s).reshape(-1)
    tile_id = jnp.arange(n_tiles)
    tile_cat = jnp.minimum(jnp.sum(tile_id[:, None] >= tile_end[None, :], axis=1), N_CATEGORIES - 1)
    used = jnp.clip(offset[tile_cat] + total[tile_cat] - tile_id * tm, 0, tm)
    tile_valid = jnp.where(tile_id < tile_end[N_CATEGORIES - 1], used, 0).astype(jnp.int32)
    experts = jnp.asarray(CATEGORY_EXPERTS, jnp.int32)
    tile_e1, tile_e2 = experts[tile_cat, 0], experts[tile_cat, 1]
    m_rows = (n_tiles + 1) * tm
    ends = ((offset + total) // SUBLANES) * SUBLANES
    tail = ends[N_CATEGORIES - 1] + tm * jnp.arange(1, m_rows // tm - n_total // tm + 1)
    fill_start = jnp.concatenate([ends, tail]).astype(jnp.int32)
    return dest_p, dest_s, fill_start, tile_e1, tile_e2, tile_valid, m_rows


def kernel(x_prompt, x_sample, cache_k, cache_v, cache_logf, state_pool, norm1_g, w_in, b_f, q_norm_g, k_norm_g,
           pool_w, pool_scale, w_out, norm2_g, w_router_group, b_router_group, w_router_expert, b_router_expert,
           w_gate, w_up, w_down):
    depth = norm1_g.shape[0]
    assert depth == 1, "single-layer step"
    b_p, s_p, _ = x_prompt.shape
    assert b_p == 1, "prompt kernels assume one stream"
    b_s, t_s, _ = x_sample.shape
    p_len = cache_k.shape[2]
    l = 0

    w = w_in[l]
    a3 = 3 * ATTN_WIDTH
    w_all = jnp.concatenate(
        [w[:, :a3], w[:, a3 + N_HEADS:], jnp.pad(w[:, a3:a3 + N_HEADS], ((0, 0), (0, LANES - N_HEADS)))],
        axis=1).astype(BF16)
    bf_pad = jnp.pad(b_f[l], (0, LANES - N_HEADS)).reshape(1, LANES)
    g1 = norm1_g[l].reshape(1, D_MODEL)
    qg = jnp.tile(q_norm_g[l], N_HEADS).reshape(1, ATTN_WIDTH)
    kg = jnp.tile(k_norm_g[l], N_HEADS).reshape(1, ATTN_WIDTH)
    hr = lax.broadcasted_iota(jnp.int32, (ATTN_WIDTH, ATTN_WIDTH), 0) // HEAD_DIM
    hc = lax.broadcasted_iota(jnp.int32, (ATTN_WIDTH, ATTN_WIDTH), 1) // HEAD_DIM
    hm = jnp.where(hr == hc, 1.0 / HEAD_DIM, 0.0).astype(BF16)
    pw = pool_w[l].astype(BF16)
    ps = pool_scale[l].reshape(1, POOL_WIDTH)
    wo = w_out[l].astype(BF16)
    g2 = norm2_g[l].reshape(1, D_MODEL)
    w_r = jnp.pad(jnp.concatenate([w_router_expert[l], w_router_group[l]], axis=1),
                  ((0, 0), (0, LANES - N_EXPERTS - N_GROUPS)))
    wrh = w_r.astype(BF16)
    wrl = (w_r - wrh.astype(F32)).astype(BF16)
    br = jnp.pad(jnp.concatenate([b_router_expert[l], b_router_group[l]]),
                 (0, LANES - N_EXPERTS - N_GROUPS)).reshape(1, LANES)
    wg = w_gate[l].astype(BF16)
    wu = w_up[l].astype(BF16)
    wd = w_down[l].astype(BF16)

    tile = 512
    query_tile = 512
    key_tile = 256
    expert_tile = 256

    xp = x_prompt.reshape(s_p, D_MODEL)
    q, kf, _, vf, _, ka, vt, pin, logf, _, edge = _in_project(xp, tile, tile, True, key_tile, g1, w_all, bf_pad, qg, kg,
                                                              hm)
    blocks_per_tile = tile // key_tile
    edge = edge.reshape(s_p // tile, SUBLANES, LANES)[:, :2 * blocks_per_tile, :N_HEADS]
    edge = edge.reshape(s_p // key_tile, 2, N_HEADS)
    c_blocks = edge[:, 0, :]
    j0 = _first_key_blocks(c_blocks, edge[:, 1, :], q_norm_g[l], k_norm_g[l], query_tile, key_tile)
    a = _attention_prompt(j0, c_blocks, q, ka, vt, query_tile, key_tile)
    hist_map = lambda i: (jnp.maximum(i * (tile // HIST_ROWS) - 1, 0), 0)
    xg_p, plan_p, cnt_p = _mix(a, pin, pin, hist_map, xp, tile, 0, tile, True, pw, ps, wo, g2, wrh, wrl, br)
    k_prompt = kf.reshape(depth, b_p, s_p, N_HEADS, HEAD_DIM)
    v_prompt = vf.reshape(depth, b_p, s_p, N_HEADS, HEAD_DIM)
    logf_prompt = logf[:, :N_HEADS].reshape(depth, b_p, s_p, N_HEADS)
    pool_prompt = pin[s_p - POOL_PAD:].reshape(depth, b_p, POOL_PAD, POOL_WIDTH)

    n_s = b_s * t_s
    xs = x_sample.reshape(n_s, D_MODEL)
    q, kf, kb, vf, vb, _, _, pin, logf, d, _ = _in_project(xs, tile, t_s, False, key_tile, g1, w_all, bf_pad, qg, kg, hm)
    dt4 = d[:, :N_HEADS].reshape(b_s, t_s, N_PAIRS, 2).transpose(0, 2, 3, 1)
    clf = cache_logf[l].astype(F32).transpose(0, 2, 1).reshape(b_s * N_HEADS, p_len)
    r4 = _suffix_sums(clf).reshape(b_s, N_PAIRS, 2, p_len)
    ck = cache_k[l].astype(BF16).reshape(b_s, p_len, ATTN_WIDTH)
    cv = cache_v[l].astype(BF16).reshape(b_s, p_len, ATTN_WIDTH)
    a = _attention_sample(q, kb, vb, ck, cv, d, dt4, r4, t_s)
    hist_s = jnp.pad(state_pool[l], ((0, 0), (HIST_ROWS - POOL_PAD, 0), (0, 0))).reshape(b_s * HIST_ROWS, POOL_WIDTH)
    xg_s, plan_s, cnt_s = _mix(a, pin, hist_s, lambda i: (i, 0), xs, t_s, p_len, 0, False, pw, ps, wo, g2, wrh, wrl,
                               br)
    k_sample = kf.reshape(depth, b_s, t_s, N_HEADS, HEAD_DIM)
    v_sample = vf.reshape(depth, b_s, t_s, N_HEADS, HEAD_DIM)
    logf_sample = logf[:, :N_HEADS].reshape(depth, b_s, t_s, N_HEADS)
    pool_sample = pin.reshape(b_s, t_s, POOL_WIDTH)[:, t_s - POOL_PAD:].reshape(depth, b_s, POOL_PAD, POOL_WIDTH)

    dest_p, dest_s, fill_start, tile_e1, tile_e2, tile_valid, m_rows = _moe_plan(plan_p, cnt_p, plan_s, cnt_s,
                                                                              expert_tile)
    rows = _scatter_rows(fill_start, dest_p, xg_p, dest_s, xg_s, m_rows, tile, expert_tile)
    ys = _moe_sorted(tile_e1, tile_e2, tile_valid, rows, g2, wg, wu, wd, expert_tile)
    y_prompt = _gather_rows(dest_p, ys, tile).reshape(b_p, s_p, D_MODEL)
    y_sample = _gather_rows(dest_s, ys, tile).reshape(b_s, t_s, D_MODEL)

    return (y_prompt, y_sample, k_prompt, v_prompt, logf_prompt, pool_prompt,
            k_sample, v_sample, logf_sample, pool_sample)
```

```python
import functools

import jax
import jax.numpy as jnp
from jax import lax
from jax.experimental import pallas as pl
from jax.experimental.pallas import tpu as pltpu

F32 = jnp.float32
BF16 = jnp.bfloat16

D_MODEL = 1024
ATTN_WIDTH = 512
N_HEADS = 8
HEAD_DIM = 64
POOL_WIDTH = 512
POOL_WINDOWS = (2, 4, 8, 16)
POOL_GROUP_DIM = 128
POOL_PAD = 15
HIST_ROWS = 16
N_GROUPS = 4
EXPERTS_PER_GROUP = 4
N_EXPERTS = 16
D_EXPERT = 512
EPS = 1e-6
LANES = 128
SUBLANES = 8
MXU_DIM = 256
PAIR = 2 * HEAD_DIM
N_PAIRS = N_HEADS // 2
ROWSUM_ROW = (HEAD_DIM, 0)
VMEM_LIMIT = 56 * 1024 * 1024
ROW_WIDTH = D_MODEL + LANES
PAIRS_PER_GROUP = 6
N_CATEGORIES = N_GROUPS * PAIRS_PER_GROUP
CATEGORY_EXPERTS = tuple((4 * g + a, 4 * g + b) for g in range(N_GROUPS) for a in range(4) for b in range(a + 1, 4))
ROUTE_CATEGORY_LANE = N_EXPERTS
ROUTE_RANK_LANE = N_EXPERTS + 1


def _cparams(sem):
    return pltpu.CompilerParams(dimension_semantics=sem, vmem_limit_bytes=VMEM_LIMIT)


def _split3(x):
    hi = x.astype(BF16)
    r1 = x - hi.astype(F32)
    mid = r1.astype(BF16)
    lo = (r1 - mid.astype(F32)).astype(BF16)
    return hi, mid, lo


def _split2(x):
    hi = x.astype(BF16)
    lo = (x - hi.astype(F32)).astype(BF16)
    return hi, lo


def _inproj_kernel(x_ref, g1_ref, w_ref, bf_ref, qg_ref, kg_ref, hm_ref, tri_ref, place_ref,
                   q_ref, kf_ref, kb_ref, vf_ref, vb_ref, ka_ref, vt_ref, p_ref, logf_ref, c_ref, edge_ref, carry_ref, *,
                   carry_rows, key_blk):
    x = x_ref[...]
    r = lax.rsqrt(jnp.mean(x * x, axis=-1, keepdims=True) + EPS)
    n = ((x * r) * g1_ref[...]).astype(BF16)
    z = jnp.dot(n, w_ref[...], preferred_element_type=F32)

    hm = hm_ref[...]

    def head_norm(zz, g):
        hi, lo = _split2(zz * zz)
        half = hm.shape[0]
        ms = jnp.concatenate(
            [jnp.dot(hi[:, r:r + half], hm, preferred_element_type=F32)
             + jnp.dot(lo[:, r:r + half], hm, preferred_element_type=F32) for r in range(0, zz.shape[1], half)], axis=1)
        return (zz * lax.rsqrt(ms + EPS)) * g

    q = head_norm(z[:, 0:ATTN_WIDTH], qg_ref[...])
    q_ref[...] = (q * (HEAD_DIM ** -0.5)).astype(BF16)
    k = head_norm(z[:, ATTN_WIDTH:2 * ATTN_WIDTH], kg_ref[...])
    kf_ref[...] = k
    kb_ref[...] = k.astype(BF16)
    v = z[:, 2 * ATTN_WIDTH:3 * ATTN_WIDTH]
    vf_ref[...] = v
    vb_ref[...] = v.astype(BF16)
    p_ref[...] = z[:, 3 * ATTN_WIDTH:3 * ATTN_WIDTH + POOL_WIDTH]
    prow = lax.broadcasted_iota(jnp.int32, (PAIR, v.shape[0]), 0)
    for hp in range(N_PAIRS):
        vpair_t = v[:, hp * PAIR:(hp + 1) * PAIR].T
        for h in range(2):
            own = (prow < HEAD_DIM) if h == 0 else (prow >= HEAD_DIM)
            marker = jnp.where(prow == ROWSUM_ROW[h], 1.0, 0.0)
            vt_ref[(2 * hp + h) * LANES:(2 * hp + h + 1) * LANES, :] = jnp.where(own, vpair_t, marker).astype(BF16)

    f = z[:, 3 * ATTN_WIDTH + POOL_WIDTH:] + bf_ref[...]
    lane = lax.broadcasted_iota(jnp.int32, f.shape, 1)
    logf = -(jnp.maximum(-f, 0.0) + jnp.log1p(jnp.exp(-jnp.abs(f))))
    logf = jnp.where(lane < N_HEADS, logf, 0.0)
    logf_ref[...] = logf

    tri = tri_ref[...]
    blk = tri.shape[0]
    parts = _split3(logf)
    pieces = []
    for r in range(0, logf.shape[0], blk):
        piece = sum(jnp.dot(tri, part[r:r + blk, :], preferred_element_type=F32) for part in parts)
        if carry_rows and pieces:
            piece = piece + pieces[-1][blk - 1:blk, :]
        pieces.append(piece)
    c = jnp.concatenate(pieces, axis=0)
    if carry_rows:
        @pl.when(pl.program_id(0) == 0)
        def _():
            carry_ref[...] = jnp.zeros_like(carry_ref)
        c = c + carry_ref[0:1, :]
        carry_ref[...] = jnp.broadcast_to(c[c.shape[0] - 1:, :], carry_ref.shape)
    c_ref[...] = c
    edges = [c[r:r + 1, :] for b in range(c.shape[0] // key_blk) for r in (b * key_blk, (b + 1) * key_blk - 1)]
    edge_ref[...] = jnp.concatenate(edges + [jnp.zeros((SUBLANES - len(edges), LANES), F32)], axis=0)

    tm = c.shape[0]
    rel = jnp.concatenate([c[r:r + key_blk, :] - c[r:r + 1, :] for r in range(0, tm, key_blk)], axis=0)
    parts = jnp.concatenate(_split3(rel), axis=1)
    decay = jnp.dot(parts, place_ref[...], preferred_element_type=F32).astype(BF16)
    kb = k.astype(BF16)
    for hp in range(N_PAIRS):
        ka_ref[:, 2 * hp * LANES:(2 * hp + 1) * LANES] = kb[:, hp * PAIR:(hp + 1) * PAIR]
        ka_ref[:, (2 * hp + 1) * LANES:(2 * hp + 2) * LANES] = decay[:, hp * LANES:(hp + 1) * LANES]


def _decay_lane(h, part):
    return 3 * h + part


def _in_project(x2d, tm, seg, carry_rows, key_blk, g1, w_all, bf_pad, qg, kg, hm):
    n = x2d.shape[0]
    blk = min(tm, MXU_DIM)
    assert seg % blk == 0 or blk % seg == 0, "running-sum segments must nest with the triangular block"
    rows = lax.broadcasted_iota(jnp.int32, (blk, blk), 0)
    cols = lax.broadcasted_iota(jnp.int32, (blk, blk), 1)
    tri = ((cols <= rows) & (rows // seg == cols // seg)).astype(BF16)
    src = jnp.arange(3 * LANES)
    part, head = src // LANES, src % LANES
    dst = (head // 2) * LANES + _decay_lane(head % 2, part)
    place = ((jnp.arange(N_PAIRS * LANES)[None, :] == dst[:, None]) & (head < N_HEADS)[:, None]).astype(BF16)
    row_blk = lambda w: pl.BlockSpec((tm, w), lambda i: (i, 0))
    const = lambda a: pl.BlockSpec(a.shape, lambda i: (0,) * a.ndim)
    out_shape = (
        jax.ShapeDtypeStruct((n, ATTN_WIDTH), BF16),
        jax.ShapeDtypeStruct((n, ATTN_WIDTH), F32),
        jax.ShapeDtypeStruct((n, ATTN_WIDTH), BF16),
        jax.ShapeDtypeStruct((n, ATTN_WIDTH), F32),
        jax.ShapeDtypeStruct((n, ATTN_WIDTH), BF16),
        jax.ShapeDtypeStruct((n, N_PAIRS * 2 * LANES), BF16),
        jax.ShapeDtypeStruct((N_HEADS * LANES, n), BF16),
        jax.ShapeDtypeStruct((n, POOL_WIDTH), F32),
        jax.ShapeDtypeStruct((n, LANES), F32),
        jax.ShapeDtypeStruct((n, LANES), F32),
        jax.ShapeDtypeStruct((n // tm * SUBLANES, LANES), F32),
    )
    return pl.pallas_call(
        functools.partial(_inproj_kernel, carry_rows=carry_rows, key_blk=key_blk),
        grid=(n // tm,),
        in_specs=[row_blk(D_MODEL), const(g1), const(w_all), const(bf_pad), const(qg), const(kg), const(hm),
                  const(tri), const(place)],
        out_specs=[row_blk(ATTN_WIDTH)] * 5 + [row_blk(N_PAIRS * 2 * LANES),
                                                pl.BlockSpec((N_HEADS * LANES, tm), lambda i: (0, i)),
                                                row_blk(POOL_WIDTH), row_blk(LANES), row_blk(LANES),
                                                pl.BlockSpec((SUBLANES, LANES), lambda i: (i, 0))],
        out_shape=out_shape,
        scratch_shapes=[pltpu.VMEM((8, LANES), F32)],
        compiler_params=_cparams(("arbitrary",)),
        name="in_project",
    )(x2d, g1, w_all, bf_pad, qg, kg, hm, tri, place)


def _attn_prompt_kernel(j0_ref, cb_ref, q_ref, k_ref, vt_ref, o_ref, acc_ref, m_ref, *, tq, tk):
    hp = pl.program_id(0)
    i = pl.program_id(1)
    nk = tq // tk
    lane = lax.broadcasted_iota(jnp.int32, (1, PAIR), 1)
    first = lane < HEAD_DIM
    q = q_ref[...]
    zero = jnp.zeros_like(q)
    qw = []
    for h in range(2):
        minus = jnp.zeros((1, LANES), F32)
        for part in range(3):
            minus = jnp.where(lane == _decay_lane(h, part), -1.0, minus)
        qw.append(jnp.concatenate([jnp.where(first == (h == 0), q, zero),
                                   jnp.broadcast_to(minus, (tq, LANES)).astype(BF16)], axis=1))
    q_start = i * tq

    acc_ref[...] = jnp.zeros_like(acc_ref)
    m_ref[...] = jnp.full_like(m_ref, -jnp.inf)

    last = (i + 1) * nk - 1

    def scores(h, j, masked=True):
        start = pl.multiple_of(jnp.minimum(j, last) * tk, tk)
        k = k_ref[pl.ds(start, tk), :]
        s = lax.dot_general(k, qw[h], (((1,), (1,)), ((), ())), preferred_element_type=F32)
        if masked:
            key = lax.broadcasted_iota(jnp.int32, s.shape, 0)
            qry = lax.broadcasted_iota(jnp.int32, s.shape, 1)
            s = jnp.where(key - qry <= q_start - j * tk, s, -jnp.inf)
        return s, jnp.max(s, axis=0, keepdims=True)

    def update(h, j, s, colmax):
        jc = jnp.minimum(j, last)
        shift = cb_ref[jc, 2 * hp + h] - cb_ref[i * nk, 2 * hp + h]
        start = pl.multiple_of(jc * tk, tk)
        vt = vt_ref[h * LANES:(h + 1) * LANES, pl.ds(start, tk)]
        m_prev = m_ref[h]
        m_new = jnp.maximum(m_prev, colmax - shift)
        alpha = jnp.exp(m_prev - m_new)
        p = jnp.exp(s - (m_new + shift))
        m_ref[h] = m_new
        acc_ref[h] = acc_ref[h] * alpha + jnp.dot(vt, p.astype(BF16), preferred_element_type=F32)

    j_first = [j0_ref[i, 2 * hp + h] for h in range(2)]
    j_both = jnp.maximum(j_first[0], j_first[1])
    start = [scores(h, j_first[h]) for h in range(2)]
    ahead = []
    for h in range(2):
        def body_one(j, carry, h=h):
            nxt = scores(h, j + 1)
            update(h, j, *carry)
            return nxt

        ahead.append(lax.fori_loop(j_first[h], j_both, body_one, start[h]))

    def body_both(j, carry, masked):
        nxt = (scores(0, j + 1, masked), scores(1, j + 1, masked))
        update(0, j, *carry[0])
        update(1, j, *carry[1])
        return nxt

    j_mask = jnp.maximum(j_both, i * nk - 1)
    pairs = (j_mask - j_both) // 2

    def body_two(t, carry):
        j = j_both + 2 * t
        return body_both(j + 1, body_both(j, carry, False), False)

    carry = lax.fori_loop(0, pairs, body_two, (ahead[0], ahead[1]))
    lax.fori_loop(j_both + 2 * pairs, last + 1, functools.partial(body_both, masked=True), carry)

    out = [acc_ref[h] / acc_ref[h][ROWSUM_ROW[h]:ROWSUM_ROW[h] + 1, :] for h in range(2)]
    row = lax.broadcasted_iota(jnp.int32, (PAIR, 1), 0)
    o_ref[...] = jnp.where(row < HEAD_DIM, out[0], out[1]).T.astype(o_ref.dtype)


def _attention_prompt(j0, c_blocks, q, ka, vt, tq, tk):
    n = q.shape[0]
    grid_spec = pltpu.PrefetchScalarGridSpec(
        num_scalar_prefetch=2,
        grid=(N_PAIRS, n // tq),
        in_specs=[
            pl.BlockSpec((tq, PAIR), lambda hp, i, j0, cb: (i, hp)),
            pl.BlockSpec((n, 2 * LANES), lambda hp, i, j0, cb: (0, hp)),
            pl.BlockSpec((2 * LANES, n), lambda hp, i, j0, cb: (hp, 0)),
        ],
        out_specs=pl.BlockSpec((tq, PAIR), lambda hp, i, j0, cb: (i, hp)),
        scratch_shapes=[pltpu.VMEM((2, LANES, tq), F32), pltpu.VMEM((2, 1, tq), F32)],
    )
    return pl.pallas_call(
        functools.partial(_attn_prompt_kernel, tq=tq, tk=tk),
        grid_spec=grid_spec,
        out_shape=jax.ShapeDtypeStruct((n, ATTN_WIDTH), BF16),
        compiler_params=_cparams(("arbitrary", "arbitrary")),
        name="attention_prompt",
    )(j0, c_blocks, q, ka, vt)


def _first_key_blocks(c_first, c_last, q_gain, k_gain, tq, tk):
    n_k = c_first.shape[0]
    n_q = n_k * tk // tq
    qk_bound = 1.02 * HEAD_DIM ** 0.5 * jnp.max(jnp.abs(q_gain)) * jnp.max(jnp.abs(k_gain))
    threshold = 104.0 + 2.0 * qk_bound + 1.0
    c_start = c_first[0::tq // tk]
    gap = c_start[:, None, :] - c_last[None, :, :]
    earlier = (jnp.arange(n_k)[None, :, None] + 1) * tk <= jnp.arange(n_q)[:, None, None] * tq
    return jnp.sum((gap < -threshold) & earlier, axis=1).astype(jnp.int32)


def _suffix_kernel(x_ref, o_ref):
    x = x_ref[...]
    p_len = x.shape[1]
    rows = lax.broadcasted_iota(jnp.int32, (p_len, p_len), 0)
    cols = lax.broadcasted_iota(jnp.int32, (p_len, p_len), 1)
    u = (rows > cols).astype(BF16)
    hi, mid, lo = _split3(x)
    o_ref[...] = (jnp.dot(hi, u, preferred_element_type=F32) + jnp.dot(mid, u, preferred_element_type=F32)
                  + jnp.dot(lo, u, preferred_element_type=F32))


def _suffix_sums(x):
    return pl.pallas_call(
        _suffix_kernel,
        out_shape=jax.ShapeDtypeStruct(x.shape, F32),
        compiler_params=pltpu.CompilerParams(vmem_limit_bytes=VMEM_LIMIT),
        name="cache_suffix_sums",
    )(x)


def _attn_sample_kernel(q_ref, kn_ref, vn_ref, ck_ref, cv_ref, d_ref, dt_ref, r_ref, o_ref):
    hp = pl.program_id(1)
    lane = lax.broadcasted_iota(jnp.int32, (1, PAIR), 1)
    first = lane < HEAD_DIM
    q = q_ref[...]
    zero = jnp.zeros_like(q)
    qh = (jnp.where(first, q, zero), jnp.where(first, zero, q))
    dblk = d_ref[...]
    dlane = lax.broadcasted_iota(jnp.int32, dblk.shape, 1)
    kn = kn_ref[...]
    vn = vn_ref[...]
    ck = ck_ref[...]
    cv = cv_ref[...]
    nt = (((1,), (1,)), ((), ()))
    outs, ls = [], []
    for h in range(2):
        dq = jnp.sum(jnp.where(dlane == 2 * hp + h, dblk, 0.0), axis=1, keepdims=True)
        s1 = lax.dot_general(qh[h], ck, nt, preferred_element_type=F32) + dq + r_ref[h:h + 1, :]
        s2 = lax.dot_general(qh[h], kn, nt, preferred_element_type=F32) + dq - dt_ref[h:h + 1, :]
        row = lax.broadcasted_iota(jnp.int32, s2.shape, 0)
        col = lax.broadcasted_iota(jnp.int32, s2.shape, 1)
        s2 = jnp.where(col <= row, s2, -jnp.inf)
        m = jnp.maximum(jnp.max(s1, axis=1, keepdims=True), jnp.max(s2, axis=1, keepdims=True))
        p1 = jnp.exp(s1 - m)
        p2 = jnp.exp(s2 - m)
        ls.append(jnp.sum(p1, axis=1, keepdims=True) + jnp.sum(p2, axis=1, keepdims=True))
        outs.append(jnp.dot(p1.astype(BF16), cv, preferred_element_type=F32)
                    + jnp.dot(p2.astype(BF16), vn, preferred_element_type=F32))
    o_ref[...] = (jnp.where(first, outs[0], outs[1]) / jnp.where(first, ls[0], ls[1])).astype(o_ref.dtype)


def _attention_sample(q, kb, vb, cache_k, cache_v, d, dt4, r4, t):
    nb, p_len = cache_k.shape[0], cache_k.shape[1]
    return pl.pallas_call(
        _attn_sample_kernel,
        grid=(nb, N_PAIRS),
        in_specs=[
            pl.BlockSpec((t, PAIR), lambda b, hp: (b, hp)),
            pl.BlockSpec((t, PAIR), lambda b, hp: (b, hp)),
            pl.BlockSpec((t, PAIR), lambda b, hp: (b, hp)),
            pl.BlockSpec((None, p_len, PAIR), lambda b, hp: (b, 0, hp)),
            pl.BlockSpec((None, p_len, PAIR), lambda b, hp: (b, 0, hp)),
            pl.BlockSpec((t, LANES), lambda b, hp: (b, 0)),
            pl.BlockSpec((None, None, 2, t), lambda b, hp: (b, hp, 0, 0)),
            pl.BlockSpec((None, None, 2, p_len), lambda b, hp: (b, hp, 0, 0)),
        ],
        out_specs=pl.BlockSpec((t, PAIR), lambda b, hp: (b, hp)),
        out_shape=jax.ShapeDtypeStruct((nb * t, ATTN_WIDTH), BF16),
        compiler_params=_cparams(("arbitrary", "arbitrary")),
        name="attention_sample",
    )(q, kb, vb, cache_k, cache_v, d, dt4, r4)


def _mix_kernel(a_ref, p_ref, hist_ref, x_ref, pw_ref, ps_ref, wo_ref, g2_ref, wrh_ref, wrl_ref, br_ref, tri_ref,
                xg_ref, plan_ref, cnt_ref, *, pos0, pos_stride, zero_first):
    i = pl.program_id(0)
    t = p_ref.shape[0]
    p = p_ref[...]
    hist = hist_ref[...]
    if zero_first:
        hist = jnp.where(i == 0, 0.0, hist)
    xh = jnp.concatenate([hist, p], axis=0)
    pos = pos0 + i * pos_stride + lax.broadcasted_iota(jnp.int32, (t, 1), 0)
    ys = []
    for g, w in enumerate(POOL_WINDOWS):
        col = xh[:, g * POOL_GROUP_DIM:(g + 1) * POOL_GROUP_DIM]
        acc = col
        span = 1
        while span < w:
            acc = acc + pltpu.roll(acc, span, axis=0)
            span *= 2
        cnt = jnp.minimum(pos + 1, w).astype(F32)
        dgrp = acc[HIST_ROWS:, :] / cnt - col[HIST_ROWS:, :]
        ys.append(jnp.dot(dgrp.astype(BF16), pw_ref[g], preferred_element_type=F32))
    pm = (jnp.concatenate(ys, axis=1) * ps_ref[...]).astype(BF16)
    mix = (jnp.dot(a_ref[...], wo_ref[0:ATTN_WIDTH, :], preferred_element_type=F32)
           + jnp.dot(pm, wo_ref[ATTN_WIDTH:, :], preferred_element_type=F32))
    x1 = x_ref[...] + mix
    xg_ref[:, 0:D_MODEL] = x1

    r = lax.rsqrt(jnp.mean(x1 * x1, axis=-1, keepdims=True) + EPS)
    n2 = (x1 * r) * g2_ref[...]

    nh, nl = _split2(n2)
    wrh = wrh_ref[...]
    logits = (jnp.dot(nh, wrh, preferred_element_type=F32) + jnp.dot(nl, wrh, preferred_element_type=F32)
              + jnp.dot(nh, wrl_ref[...], preferred_element_type=F32)) + br_ref[...]
    lane = lax.broadcasted_iota(jnp.int32, logits.shape, 1).astype(F32)
    neg = -jnp.inf
    big = jnp.float32(1 << 20)
    is_g = (lane >= N_EXPERTS) & (lane < N_EXPERTS + N_GROUPS)
    lg = jnp.where(is_g, logits, neg)
    gmax = jnp.max(lg, axis=1, keepdims=True)
    g_lane = jnp.min(jnp.where(lg == gmax, lane, big), axis=1, keepdims=True)
    top_pg = 1.0 / jnp.sum(jnp.exp(lg - gmax), axis=1, keepdims=True)
    g_idx = g_lane - N_EXPERTS
    sel = (lane >= g_idx * EXPERTS_PER_GROUP) & (lane < (g_idx + 1) * EXPERTS_PER_GROUP)
    le = jnp.where(sel, logits, neg)
    m1 = jnp.max(le, axis=1, keepdims=True)
    i1 = jnp.min(jnp.where(le == m1, lane, big), axis=1, keepdims=True)
    le2 = jnp.where(lane == i1, neg, le)
    m2 = jnp.max(le2, axis=1, keepdims=True)
    i2 = jnp.min(jnp.where(le2 == m2, lane, big), axis=1, keepdims=True)
    e2 = jnp.exp(m2 - m1)
    w1 = top_pg / (1.0 + e2)
    w2 = top_pg * e2 / (1.0 + e2)
    gate = jnp.where(lane == i1, w1, 0.0) + jnp.where(lane == i2, w2, 0.0)

    @pl.when(i == 0)
    def _():
        cnt_ref[...] = jnp.zeros_like(cnt_ref)
    e_lo = jnp.minimum(i1, i2) - g_idx * EXPERTS_PER_GROUP
    e_hi = jnp.maximum(i1, i2) - g_idx * EXPERTS_PER_GROUP
    category = g_idx * PAIRS_PER_GROUP + (e_lo * (7.0 - e_lo) * 0.5 + (e_hi - e_lo - 1.0))
    member = lane == category
    tri = tri_ref[...]
    blk = tri.shape[0]
    seen = cnt_ref[0:1, :]
    pieces = []
    for r in range(0, t, blk):
        mem = member[r:r + blk, :].astype(BF16)
        pieces.append(jnp.dot(tri, mem, preferred_element_type=F32) + seen)
        seen = seen + jnp.sum(mem.astype(F32), axis=0, keepdims=True)
    before = jnp.concatenate(pieces, axis=0)
    rank = jnp.sum(jnp.where(member, before, 0.0), axis=1, keepdims=True)
    cnt_ref[...] = jnp.broadcast_to(seen, cnt_ref.shape)
    route = jnp.where(lane == ROUTE_CATEGORY_LANE, category, jnp.where(lane == ROUTE_RANK_LANE, rank, gate))
    xg_ref[:, D_MODEL:] = route
    srow = lax.broadcasted_iota(jnp.int32, (SUBLANES, LANES), 0)
    slane = lax.broadcasted_iota(jnp.int32, (SUBLANES, LANES), 1)
    pick = (((srow == 0) & (slane == ROUTE_CATEGORY_LANE)) | ((srow == 1) & (slane == ROUTE_RANK_LANE))).astype(BF16)
    nt = (((1,), (1,)), ((), ()))
    ids = jnp.where(lane >= N_EXPERTS, route, 0.0)
    hi, lo = _split2(ids)
    plan_ref[...] = (lax.dot_general(pick, hi, nt, preferred_element_type=F32)
                     + lax.dot_general(pick, lo, nt, preferred_element_type=F32))


def _mix(a, p, hist_src, hist_map, x2d, t, pos0, pos_stride, zero_first, pw, ps, wo, g2, wrh, wrl, br):
    n = x2d.shape[0]
    blk = min(t, MXU_DIM)
    rows = lax.broadcasted_iota(jnp.int32, (blk, blk), 0)
    cols = lax.broadcasted_iota(jnp.int32, (blk, blk), 1)
    tri = (cols < rows).astype(BF16)
    row_blk = lambda w: pl.BlockSpec((t, w), lambda i: (i, 0))
    const = lambda arr: pl.BlockSpec(arr.shape, lambda i: (0,) * arr.ndim)
    return pl.pallas_call(
        functools.partial(_mix_kernel, pos0=pos0, pos_stride=pos_stride, zero_first=zero_first),
        grid=(n // t,),
        in_specs=[row_blk(ATTN_WIDTH), row_blk(POOL_WIDTH), pl.BlockSpec((HIST_ROWS, POOL_WIDTH), hist_map),
                  row_blk(D_MODEL), const(pw), const(ps), const(wo), const(g2), const(wrh), const(wrl), const(br),
                  const(tri)],
        out_specs=[row_blk(ROW_WIDTH), pl.BlockSpec((SUBLANES, t), lambda i: (i, 0)),
                   pl.BlockSpec((SUBLANES, LANES), lambda i: (0, 0))],
        out_shape=(jax.ShapeDtypeStruct((n, ROW_WIDTH), F32), jax.ShapeDtypeStruct((n // t * SUBLANES, t), F32),
                   jax.ShapeDtypeStruct((SUBLANES, LANES), F32)),
        compiler_params=_cparams(("arbitrary",)),
        name="pool_mix_route",
    )(a, p, hist_src, x2d, pw, ps, wo, g2, wrh, wrl, br, tri)


def _issue_row_scatter(dest_ref, x_ref, xs_ref, sem):
    t = x_ref.shape[0]

    def issue(g, carry):
        base = pl.multiple_of(g * SUBLANES, SUBLANES)
        tile_rows = x_ref.at[pl.ds(base, SUBLANES)]
        for u in range(SUBLANES):
            pltpu.make_async_copy(tile_rows.at[pl.ds(u, 1)], xs_ref.at[pl.ds(dest_ref[base + u], 1)],
                                  sem).start(priority=u % 2)
        return carry

    lax.fori_loop(0, t // SUBLANES, issue, 0)
    pltpu.make_async_copy(x_ref, xs_ref.at[pl.ds(0, t)], sem).wait()


def _scatter_rows_kernel(fill_ref, dest_a_ref, dest_b_ref, xa_ref, xb_ref, xs_ref, zero_ref, sem, *, steps_a):
    step = pl.program_id(0)

    @pl.when(step == 0)
    def _():
        zero_ref[...] = jnp.zeros_like(zero_ref)
        fill_rows = zero_ref.shape[0]
        last_start = xs_ref.shape[0] - fill_rows
        for g in range(fill_ref.shape[0]):
            @pl.when(fill_ref[g] < xs_ref.shape[0])
            def _(g=g):
                start = pl.multiple_of(jnp.minimum(fill_ref[g], last_start), SUBLANES)
                fill = pltpu.make_async_copy(zero_ref, xs_ref.at[pl.ds(start, fill_rows)], sem)
                fill.start()
                fill.wait()

    @pl.when(step < steps_a)
    def _():
        _issue_row_scatter(dest_a_ref, xa_ref, xs_ref, sem)

    @pl.when(step >= steps_a)
    def _():
        _issue_row_scatter(dest_b_ref, xb_ref, xs_ref, sem)


def _scatter_rows(fill_start, dest_a, xa, dest_b, xb, m_rows, t, fill_rows):
    w = xa.shape[1]
    steps_a, steps_b = xa.shape[0] // t, xb.shape[0] // t
    in_a = lambda i: jnp.minimum(i, steps_a - 1)
    in_b = lambda i: jnp.maximum(i - steps_a, 0)
    smem = pltpu.SMEM
    return pl.pallas_call(
        functools.partial(_scatter_rows_kernel, steps_a=steps_a),
        grid=(steps_a + steps_b,),
        in_specs=[pl.BlockSpec(fill_start.shape, lambda i: (0,), memory_space=smem),
                  pl.BlockSpec((t,), lambda i: (in_a(i),), memory_space=smem),
                  pl.BlockSpec((t,), lambda i: (in_b(i),), memory_space=smem),
                  pl.BlockSpec((t, w), lambda i: (in_a(i), 0)),
                  pl.BlockSpec((t, w), lambda i: (in_b(i), 0))],
        out_specs=pl.BlockSpec(memory_space=pl.ANY),
        out_shape=jax.ShapeDtypeStruct((m_rows, w), xa.dtype),
        scratch_shapes=[pltpu.VMEM((fill_rows, w), xa.dtype), pltpu.SemaphoreType.DMA],
        compiler_params=_cparams(("arbitrary",)),
        name="moe_scatter_rows",
    )(fill_start, dest_a, dest_b, xa, xb)


def _gather_rows_kernel(dest_ref, ys_ref, y_ref, sem):
    t = y_ref.shape[0]

    def issue(g, carry):
        base = pl.multiple_of(g * SUBLANES, SUBLANES)
        tile_rows = y_ref.at[pl.ds(base, SUBLANES)]
        for u in range(SUBLANES):
            pltpu.make_async_copy(ys_ref.at[pl.ds(dest_ref[base + u], 1)], tile_rows.at[pl.ds(u, 1)],
                                  sem).start(priority=u % 2)
        return carry

    lax.fori_loop(0, t // SUBLANES, issue, 0)
    pltpu.make_async_copy(ys_ref.at[pl.ds(0, t)], y_ref, sem).wait()


def _gather_rows(dest, ys, t):
    n = dest.shape[0]
    w = ys.shape[1]
    return pl.pallas_call(
        _gather_rows_kernel,
        grid=(n // t,),
        in_specs=[pl.BlockSpec((t,), lambda i: (i,), memory_space=pltpu.SMEM),
                  pl.BlockSpec(memory_space=pl.ANY)],
        out_specs=pl.BlockSpec((t, w), lambda i: (i, 0)),
        out_shape=jax.ShapeDtypeStruct((n, w), ys.dtype),
        scratch_shapes=[pltpu.SemaphoreType.DMA],
        compiler_params=_cparams(("arbitrary",)),
        name="moe_gather_rows",
    )(dest, ys)


def _moe_kernel(e1_ref, e2_ref, tv_ref, xs_ref, g2_ref, wg1_ref, wu1_ref, wd1_ref, wg2_ref, wu2_ref, wd2_ref, y_ref):
    m = pl.program_id(0)

    @pl.when(tv_ref[m] > 0)
    def _():
        x1 = xs_ref[:, 0:D_MODEL]
        route = xs_ref[:, D_MODEL:]
        lane = lax.broadcasted_iota(jnp.int32, route.shape, 1)
        r = lax.rsqrt(jnp.mean(x1 * x1, axis=-1, keepdims=True) + EPS)
        n2 = ((x1 * r) * g2_ref[...]).astype(BF16)
        acc = x1
        for e_ref, wg_ref, wu_ref, wd_ref in ((e1_ref, wg1_ref, wu1_ref, wd1_ref), (e2_ref, wg2_ref, wu2_ref, wd2_ref)):
            ge = jnp.sum(jnp.where(lane == e_ref[m], route, 0.0), axis=1, keepdims=True)
            hg = jnp.dot(n2, wg_ref[...], preferred_element_type=F32)
            hu = jnp.dot(n2, wu_ref[...], preferred_element_type=F32)
            h = (hg * jax.nn.sigmoid(hg)) * hu * ge
            acc = acc + jnp.dot(h.astype(BF16), wd_ref[...], preferred_element_type=F32)
        y_ref[...] = acc

    @pl.when(tv_ref[m] == 0)
    def _():
        y_ref[...] = jnp.zeros_like(y_ref)


def _moe_sorted(tile_e1, tile_e2, tile_valid, xs, g2, wg, wu, wd, tm):
    m_pad = tile_valid.shape[0] * tm
    up = lambda e: pl.BlockSpec((None, D_MODEL, D_EXPERT), lambda m, e1, e2, tv: ((e1, e2)[e][m], 0, 0))
    down = lambda e: pl.BlockSpec((None, D_EXPERT, D_MODEL), lambda m, e1, e2, tv: ((e1, e2)[e][m], 0, 0))
    grid_spec = pltpu.PrefetchScalarGridSpec(
        num_scalar_prefetch=3,
        grid=(m_pad // tm,),
        in_specs=[
            pl.BlockSpec((tm, ROW_WIDTH), lambda m, e1, e2, tv: (m, 0)),
            pl.BlockSpec(g2.shape, lambda m, e1, e2, tv: (0, 0)),
            up(0), up(0), down(0), up(1), up(1), down(1),
        ],
        out_specs=pl.BlockSpec((tm, D_MODEL), lambda m, e1, e2, tv: (m, 0)),
    )
    return pl.pallas_call(
        _moe_kernel,
        grid_spec=grid_spec,
        out_shape=jax.ShapeDtypeStruct((m_pad, D_MODEL), F32),
        compiler_params=_cparams(("arbitrary",)),
        name="moe_experts",
    )(tile_e1, tile_e2, tile_valid, xs, g2, wg, wu, wd, wg, wu, wd)


def _moe_plan(plan_p, cnt_p, plan_s, cnt_s, tm):
    def ids(plan):
        blocks = plan.reshape(-1, SUBLANES, plan.shape[1]).astype(jnp.int32)
        return blocks[:, 0, :], blocks[:, 1, :]

    def lookup(table, cat):
        out = jnp.zeros_like(cat)
        for c in range(N_CATEGORIES):
            out = jnp.where(cat == c, table[c], out)
        return out

    (kp, rank_p), (ks, rank_s) = ids(plan_p), ids(plan_s)
    n_total = kp.size + ks.size
    n_tiles = n_total // tm + N_CATEGORIES
    cp = cnt_p[0, :N_CATEGORIES].astype(jnp.int32)
    cs = cnt_s[0, :N_CATEGORIES].astype(jnp.int32)
    total = cp + cs
    tiles = (total + tm - 1) // tm
    tile_end = jnp.cumsum(tiles)
    offset = (tile_end - tiles) * tm
    dest_p = (lookup(offset, kp) + rank_p).reshape(-1)
    dest_s = (lookup(offset + cp, ks) + rank_s).reshape(-1)
    tile_id = jnp.arange(n_tiles)
    tile_cat = jnp.minimum(jnp.sum(tile_id[:, None] >= tile_end[None, :], axis=1), N_CATEGORIES - 1)
    used = jnp.clip(offset[tile_cat] + total[tile_cat] - tile_id * tm, 0, tm)
    tile_valid = jnp.where(tile_id < tile_end[N_CATEGORIES - 1], used, 0).astype(jnp.int32)
    experts = jnp.asarray(CATEGORY_EXPERTS, jnp.int32)
    tile_e1, tile_e2 = experts[tile_cat, 0], experts[tile_cat, 1]
    m_rows = (n_tiles + 1) * tm
    ends = ((offset + total) // SUBLANES) * SUBLANES
    tail = ends[N_CATEGORIES - 1] + tm * jnp.arange(1, m_rows // tm - n_total // tm + 1)
    fill_start = jnp.concatenate([ends, tail]).astype(jnp.int32)
    return dest_p, dest_s, fill_start, tile_e1, tile_e2, tile_valid, m_rows


def kernel(x_prompt, x_sample, cache_k, cache_v, cache_logf, state_pool, norm1_g, w_in, b_f, q_norm_g, k_norm_g,
           pool_w, pool_scale, w_out, norm2_g, w_router_group, b_router_group, w_router_expert, b_router_expert,
           w_gate, w_up, w_down):
    depth = norm1_g.shape[0]
    assert depth == 1, "single-layer step"
    b_p, s_p, _ = x_prompt.shape
    assert b_p == 1, "prompt kernels assume one stream"
    b_s, t_s, _ = x_sample.shape
    p_len = cache_k.shape[2]
    l = 0

    w = w_in[l]
    a3 = 3 * ATTN_WIDTH
    w_all = jnp.concatenate(
        [w[:, :a3], w[:, a3 + N_HEADS:], jnp.pad(w[:, a3:a3 + N_HEADS], ((0, 0), (0, LANES - N_HEADS)))],
        axis=1).astype(BF16)
    bf_pad = jnp.pad(b_f[l], (0, LANES - N_HEADS)).reshape(1, LANES)
    g1 = norm1_g[l].reshape(1, D_MODEL)
    qg = jnp.tile(q_norm_g[l], N_HEADS).reshape(1, ATTN_WIDTH)
    kg = jnp.tile(k_norm_g[l], N_HEADS).reshape(1, ATTN_WIDTH)
    hr = lax.broadcasted_iota(jnp.int32, (MXU_DIM, MXU_DIM), 0) // HEAD_DIM
    hc = lax.broadcasted_iota(jnp.int32, (MXU_DIM, MXU_DIM), 1) // HEAD_DIM
    hm = jnp.where(hr == hc, 1.0 / HEAD_DIM, 0.0).astype(BF16)
    pw = pool_w[l].astype(BF16)
    ps = pool_scale[l].reshape(1, POOL_WIDTH)
    wo = w_out[l].astype(BF16)
    g2 = norm2_g[l].reshape(1, D_MODEL)
    w_r = jnp.pad(jnp.concatenate([w_router_expert[l], w_router_group[l]], axis=1),
                  ((0, 0), (0, LANES - N_EXPERTS - N_GROUPS)))
    wrh = w_r.astype(BF16)
    wrl = (w_r - wrh.astype(F32)).astype(BF16)
    br = jnp.pad(jnp.concatenate([b_router_expert[l], b_router_group[l]]),
                 (0, LANES - N_EXPERTS - N_GROUPS)).reshape(1, LANES)
    wg = w_gate[l].astype(BF16)
    wu = w_up[l].astype(BF16)
    wd = w_down[l].astype(BF16)

    tile = 512
    query_tile = 512
    key_tile = 256
    expert_tile = 256

    xp = x_prompt.reshape(s_p, D_MODEL)
    q, kf, _, vf, _, ka, vt, pin, logf, _, edge = _in_project(xp, tile, tile, True, key_tile, g1, w_all, bf_pad, qg, kg,
                                                              hm)
    blocks_per_tile = tile // key_tile
    edge = edge.reshape(s_p // tile, SUBLANES, LANES)[:, :2 * blocks_per_tile, :N_HEADS]
    edge = edge.reshape(s_p // key_tile, 2, N_HEADS)
    c_blocks = edge[:, 0, :]
    j0 = _first_key_blocks(c_blocks, edge[:, 1, :], q_norm_g[l], k_norm_g[l], query_tile, key_tile)
    a = _attention_prompt(j0, c_blocks, q, ka, vt, query_tile, key_tile)
    hist_map = lambda i: (jnp.maximum(i * (tile // HIST_ROWS) - 1, 0), 0)
    xg_p, plan_p, cnt_p = _mix(a, pin, pin, hist_map, xp, tile, 0, tile, True, pw, ps, wo, g2, wrh, wrl, br)
    k_prompt = kf.reshape(depth, b_p, s_p, N_HEADS, HEAD_DIM)
    v_prompt = vf.reshape(depth, b_p, s_p, N_HEADS, HEAD_DIM)
    logf_prompt = logf[:, :N_HEADS].reshape(depth, b_p, s_p, N_HEADS)
    pool_prompt = pin[s_p - POOL_PAD:].reshape(depth, b_p, POOL_PAD, POOL_WIDTH)

    n_s = b_s * t_s
    xs = x_sample.reshape(n_s, D_MODEL)
    q, kf, kb, vf, vb, _, _, pin, logf, d, _ = _in_project(xs, tile, t_s, False, key_tile, g1, w_all, bf_pad, qg, kg, hm)
    dt4 = d[:, :N_HEADS].reshape(b_s, t_s, N_PAIRS, 2).transpose(0, 2, 3, 1)
    clf = cache_logf[l].astype(F32).transpose(0, 2, 1).reshape(b_s * N_HEADS, p_len)
    r4 = _suffix_sums(clf).reshape(b_s, N_PAIRS, 2, p_len)
    ck = cache_k[l].reshape(b_s, p_len, ATTN_WIDTH).astype(BF16)
    cv = cache_v[l].reshape(b_s, p_len, ATTN_WIDTH).astype(BF16)
    a = _attention_sample(q, kb, vb, ck, cv, d, dt4, r4, t_s)
    hist_s = jnp.pad(state_pool[l], ((0, 0), (HIST_ROWS - POOL_PAD, 0), (0, 0))).reshape(b_s * HIST_ROWS, POOL_WIDTH)
    xg_s, plan_s, cnt_s = _mix(a, pin, hist_s, lambda i: (i, 0), xs, t_s, p_len, 0, False, pw, ps, wo, g2, wrh, wrl,
                               br)
    k_sample = kf.reshape(depth, b_s, t_s, N_HEADS, HEAD_DIM)
    v_sample = vf.reshape(depth, b_s, t_s, N_HEADS, HEAD_DIM)
    logf_sample = logf[:, :N_HEADS].reshape(depth, b_s, t_s, N_HEADS)
    pool_sample = pin.reshape(b_s, t_s, POOL_WIDTH)[:, t_s - POOL_PAD:].reshape(depth, b_s, POOL_PAD, POOL_WIDTH)

    dest_p, dest_s, fill_start, tile_e1, tile_e2, tile_valid, m_rows = _moe_plan(plan_p, cnt_p, plan_s, cnt_s,
                                                                              expert_tile)
    rows = _scatter_rows(fill_start, dest_p, xg_p, dest_s, xg_s, m_rows, tile, expert_tile)
    ys = _moe_sorted(tile_e1, tile_e2, tile_valid, rows, g2, wg, wu, wd, expert_tile)
    y_prompt = _gather_rows(dest_p, ys, tile).reshape(b_p, s_p, D_MODEL)
    y_sample = _gather_rows(dest_s, ys, tile).reshape(b_s, t_s, D_MODEL)

    return (y_prompt, y_sample, k_prompt, v_prompt, logf_prompt, pool_prompt,
            k_sample, v_sample, logf_sample, pool_sample)
```

```python
import functools

import jax
import jax.numpy as jnp
from jax import lax
from jax.experimental import pallas as pl
from jax.experimental.pallas import tpu as pltpu

F32 = jnp.float32
BF16 = jnp.bfloat16

D_MODEL = 1024
ATTN_WIDTH = 512
N_HEADS = 8
HEAD_DIM = 64
POOL_WIDTH = 512
POOL_WINDOWS = (2, 4, 8, 16)
POOL_GROUP_DIM = 128
POOL_PAD = 15
HIST_ROWS = 16
N_GROUPS = 4
EXPERTS_PER_GROUP = 4
N_EXPERTS = 16
D_EXPERT = 512
EPS = 1e-6
LANES = 128
SUBLANES = 8
MXU_DIM = 256
PAIR = 2 * HEAD_DIM
N_PAIRS = N_HEADS // 2
ROWSUM_ROW = (HEAD_DIM, 0)
VMEM_LIMIT = 56 * 1024 * 1024
ROW_WIDTH = D_MODEL + LANES
PAIRS_PER_GROUP = 6
N_CATEGORIES = N_GROUPS * PAIRS_PER_GROUP
CATEGORY_EXPERTS = tuple((4 * g + a, 4 * g + b) for g in range(N_GROUPS) for a in range(4) for b in range(a + 1, 4))
ROUTE_CATEGORY_LANE = N_EXPERTS
ROUTE_RANK_LANE = N_EXPERTS + 1


def _cparams(sem):
    return pltpu.CompilerParams(dimension_semantics=sem, vmem_limit_bytes=VMEM_LIMIT)


def _split3(x):
    hi = x.astype(BF16)
    r1 = x - hi.astype(F32)
    mid = r1.astype(BF16)
    lo = (r1 - mid.astype(F32)).astype(BF16)
    return hi, mid, lo


def _split2(x):
    hi = x.astype(BF16)
    lo = (x - hi.astype(F32)).astype(BF16)
    return hi, lo


def _inproj_kernel(x_ref, g1_ref, w_ref, bf_ref, qg_ref, kg_ref, hm_ref, tri_ref, place_ref,
                   q_ref, kf_ref, kb_ref, vf_ref, vb_ref, ka_ref, vt_ref, p_ref, logf_ref, c_ref, edge_ref, carry_ref, *,
                   carry_rows, key_blk):
    x = x_ref[...]
    r = lax.rsqrt(jnp.mean(x * x, axis=-1, keepdims=True) + EPS)
    n = ((x * r) * g1_ref[...]).astype(BF16)
    z = jnp.dot(n, w_ref[...], preferred_element_type=F32)

    hm = hm_ref[...]

    def head_norm(zz, g):
        hi, lo = _split2(zz * zz)
        half = hm.shape[0]
        ms = jnp.concatenate(
            [jnp.dot(hi[:, r:r + half], hm, preferred_element_type=F32)
             + jnp.dot(lo[:, r:r + half], hm, preferred_element_type=F32) for r in range(0, zz.shape[1], half)], axis=1)
        return (zz * lax.rsqrt(ms + EPS)) * g

    q = head_norm(z[:, 0:ATTN_WIDTH], qg_ref[...])
    q_ref[...] = (q * (HEAD_DIM ** -0.5)).astype(BF16)
    k = head_norm(z[:, ATTN_WIDTH:2 * ATTN_WIDTH], kg_ref[...])
    kf_ref[...] = k
    kb_ref[...] = k.astype(BF16)
    v = z[:, 2 * ATTN_WIDTH:3 * ATTN_WIDTH]
    vf_ref[...] = v
    vb_ref[...] = v.astype(BF16)
    p_ref[...] = z[:, 3 * ATTN_WIDTH:3 * ATTN_WIDTH + POOL_WIDTH]
    prow = lax.broadcasted_iota(jnp.int32, (PAIR, v.shape[0]), 0)
    for hp in range(N_PAIRS):
        vpair_t = v[:, hp * PAIR:(hp + 1) * PAIR].T
        for h in range(2):
            own = (prow < HEAD_DIM) if h == 0 else (prow >= HEAD_DIM)
            marker = jnp.where(prow == ROWSUM_ROW[h], 1.0, 0.0)
            vt_ref[(2 * hp + h) * LANES:(2 * hp + h + 1) * LANES, :] = jnp.where(own, vpair_t, marker).astype(BF16)

    f = z[:, 3 * ATTN_WIDTH + POOL_WIDTH:] + bf_ref[...]
    lane = lax.broadcasted_iota(jnp.int32, f.shape, 1)
    logf = -(jnp.maximum(-f, 0.0) + jnp.log1p(jnp.exp(-jnp.abs(f))))
    logf = jnp.where(lane < N_HEADS, logf, 0.0)
    logf_ref[...] = logf

    tri = tri_ref[...]
    blk = tri.shape[0]
    parts = _split3(logf)
    pieces = []
    for r in range(0, logf.shape[0], blk):
        piece = sum(jnp.dot(tri, part[r:r + blk, :], preferred_element_type=F32) for part in parts)
        if carry_rows and pieces:
            piece = piece + pieces[-1][blk - 1:blk, :]
        pieces.append(piece)
    c = jnp.concatenate(pieces, axis=0)
    if carry_rows:
        @pl.when(pl.program_id(0) == 0)
        def _():
            carry_ref[...] = jnp.zeros_like(carry_ref)
        c = c + carry_ref[0:1, :]
        carry_ref[...] = jnp.broadcast_to(c[c.shape[0] - 1:, :], carry_ref.shape)
    c_ref[...] = c
    edges = [c[r:r + 1, :] for b in range(c.shape[0] // key_blk) for r in (b * key_blk, (b + 1) * key_blk - 1)]
    edge_ref[...] = jnp.concatenate(edges + [jnp.zeros((SUBLANES - len(edges), LANES), F32)], axis=0)

    tm = c.shape[0]
    rel = jnp.concatenate([c[r:r + key_blk, :] - c[r:r + 1, :] for r in range(0, tm, key_blk)], axis=0)
    parts = jnp.concatenate(_split3(rel), axis=1)
    decay = jnp.dot(parts, place_ref[...], preferred_element_type=F32).astype(BF16)
    kb = k.astype(BF16)
    for hp in range(N_PAIRS):
        ka_ref[:, 2 * hp * LANES:(2 * hp + 1) * LANES] = kb[:, hp * PAIR:(hp + 1) * PAIR]
        ka_ref[:, (2 * hp + 1) * LANES:(2 * hp + 2) * LANES] = decay[:, hp * LANES:(hp + 1) * LANES]


def _decay_lane(h, part):
    return 3 * h + part


def _in_project(x2d, tm, seg, carry_rows, key_blk, g1, w_all, bf_pad, qg, kg, hm):
    n = x2d.shape[0]
    blk = min(tm, MXU_DIM)
    assert seg % blk == 0 or blk % seg == 0, "running-sum segments must nest with the triangular block"
    rows = lax.broadcasted_iota(jnp.int32, (blk, blk), 0)
    cols = lax.broadcasted_iota(jnp.int32, (blk, blk), 1)
    tri = ((cols <= rows) & (rows // seg == cols // seg)).astype(BF16)
    src = jnp.arange(3 * LANES)
    part, head = src // LANES, src % LANES
    dst = (head // 2) * LANES + _decay_lane(head % 2, part)
    place = ((jnp.arange(N_PAIRS * LANES)[None, :] == dst[:, None]) & (head < N_HEADS)[:, None]).astype(BF16)
    row_blk = lambda w: pl.BlockSpec((tm, w), lambda i: (i, 0))
    const = lambda a: pl.BlockSpec(a.shape, lambda i: (0,) * a.ndim)
    out_shape = (
        jax.ShapeDtypeStruct((n, ATTN_WIDTH), BF16),
        jax.ShapeDtypeStruct((n, ATTN_WIDTH), F32),
        jax.ShapeDtypeStruct((n, ATTN_WIDTH), BF16),
        jax.ShapeDtypeStruct((n, ATTN_WIDTH), F32),
        jax.ShapeDtypeStruct((n, ATTN_WIDTH), BF16),
        jax.ShapeDtypeStruct((n, N_PAIRS * 2 * LANES), BF16),
        jax.ShapeDtypeStruct((N_HEADS * LANES, n), BF16),
        jax.ShapeDtypeStruct((n, POOL_WIDTH), F32),
        jax.ShapeDtypeStruct((n, LANES), F32),
        jax.ShapeDtypeStruct((n, LANES), F32),
        jax.ShapeDtypeStruct((n // tm * SUBLANES, LANES), F32),
    )
    return pl.pallas_call(
        functools.partial(_inproj_kernel, carry_rows=carry_rows, key_blk=key_blk),
        grid=(n // tm,),
        in_specs=[row_blk(D_MODEL), const(g1), const(w_all), const(bf_pad), const(qg), const(kg), const(hm),
                  const(tri), const(place)],
        out_specs=[row_blk(ATTN_WIDTH)] * 5 + [row_blk(N_PAIRS * 2 * LANES),
                                                pl.BlockSpec((N_HEADS * LANES, tm), lambda i: (0, i)),
                                                row_blk(POOL_WIDTH), row_blk(LANES), row_blk(LANES),
                                                pl.BlockSpec((SUBLANES, LANES), lambda i: (i, 0))],
        out_shape=out_shape,
        scratch_shapes=[pltpu.VMEM((8, LANES), F32)],
        compiler_params=_cparams(("arbitrary",)),
        name="in_project",
    )(x2d, g1, w_all, bf_pad, qg, kg, hm, tri, place)


def _attn_prompt_kernel(j0_ref, cb_ref, q_ref, k_ref, vt_ref, o_ref, acc_ref, m_ref, *, tq, tk):
    hp = pl.program_id(0)
    i = pl.program_id(1)
    nk = tq // tk
    lane = lax.broadcasted_iota(jnp.int32, (1, PAIR), 1)
    first = lane < HEAD_DIM
    q = q_ref[...]
    zero = jnp.zeros_like(q)
    qw = []
    for h in range(2):
        minus = jnp.zeros((1, LANES), F32)
        for part in range(3):
            minus = jnp.where(lane == _decay_lane(h, part), -1.0, minus)
        qw.append(jnp.concatenate([jnp.where(first == (h == 0), q, zero),
                                   jnp.broadcast_to(minus, (tq, LANES)).astype(BF16)], axis=1))
    q_start = i * tq

    acc_ref[...] = jnp.zeros_like(acc_ref)
    m_ref[...] = jnp.full_like(m_ref, -jnp.inf)

    last = (i + 1) * nk - 1

    def scores(h, j, masked=True):
        start = pl.multiple_of(jnp.minimum(j, last) * tk, tk)
        k = k_ref[pl.ds(start, tk), :]
        s = lax.dot_general(k, qw[h], (((1,), (1,)), ((), ())), preferred_element_type=F32)
        if masked:
            key = lax.broadcasted_iota(jnp.int32, s.shape, 0)
            qry = lax.broadcasted_iota(jnp.int32, s.shape, 1)
            s = jnp.where(key - qry <= q_start - j * tk, s, -jnp.inf)
        return s, jnp.max(s, axis=0, keepdims=True)

    def update(h, j, s, colmax):
        jc = jnp.minimum(j, last)
        shift = cb_ref[jc, 2 * hp + h] - cb_ref[i * nk, 2 * hp + h]
        start = pl.multiple_of(jc * tk, tk)
        vt = vt_ref[h * LANES:(h + 1) * LANES, pl.ds(start, tk)]
        m_prev = m_ref[h]
        m_new = jnp.maximum(m_prev, colmax - shift)
        alpha = jnp.exp(m_prev - m_new)
        p = jnp.exp(s - (m_new + shift))
        m_ref[h] = m_new
        acc_ref[h] = acc_ref[h] * alpha + jnp.dot(vt, p.astype(BF16), preferred_element_type=F32)

    j_first = [j0_ref[i, 2 * hp + h] for h in range(2)]
    j_both = jnp.maximum(j_first[0], j_first[1])
    start = [scores(h, j_first[h]) for h in range(2)]
    ahead = []
    for h in range(2):
        def body_one(j, carry, h=h):
            nxt = scores(h, j + 1)
            update(h, j, *carry)
            return nxt

        ahead.append(lax.fori_loop(j_first[h], j_both, body_one, start[h]))

    def body_both(j, carry, masked):
        nxt = (scores(0, j + 1, masked), scores(1, j + 1, masked))
        update(0, j, *carry[0])
        update(1, j, *carry[1])
        return nxt

    j_mask = jnp.maximum(j_both, i * nk - 1)
    pairs = (j_mask - j_both) // 2

    def body_two(t, carry):
        j = j_both + 2 * t
        return body_both(j + 1, body_both(j, carry, False), False)

    carry = lax.fori_loop(0, pairs, body_two, (ahead[0], ahead[1]))
    lax.fori_loop(j_both + 2 * pairs, last + 1, functools.partial(body_both, masked=True), carry)

    out = [acc_ref[h] / acc_ref[h][ROWSUM_ROW[h]:ROWSUM_ROW[h] + 1, :] for h in range(2)]
    row = lax.broadcasted_iota(jnp.int32, (PAIR, 1), 0)
    o_ref[...] = jnp.where(row < HEAD_DIM, out[0], out[1]).T.astype(o_ref.dtype)


def _attention_prompt(j0, c_blocks, q, ka, vt, tq, tk):
    n = q.shape[0]
    grid_spec = pltpu.PrefetchScalarGridSpec(
        num_scalar_prefetch=2,
        grid=(N_PAIRS, n // tq),
        in_specs=[
            pl.BlockSpec((tq, PAIR), lambda hp, i, j0, cb: (i, hp)),
            pl.BlockSpec((n, 2 * LANES), lambda hp, i, j0, cb: (0, hp)),
            pl.BlockSpec((2 * LANES, n), lambda hp, i, j0, cb: (hp, 0)),
        ],
        out_specs=pl.BlockSpec((tq, PAIR), lambda hp, i, j0, cb: (i, hp)),
        scratch_shapes=[pltpu.VMEM((2, LANES, tq), F32), pltpu.VMEM((2, 1, tq), F32)],
    )
    return pl.pallas_call(
        functools.partial(_attn_prompt_kernel, tq=tq, tk=tk),
        grid_spec=grid_spec,
        out_shape=jax.ShapeDtypeStruct((n, ATTN_WIDTH), BF16),
        compiler_params=_cparams(("arbitrary", "arbitrary")),
        name="attention_prompt",
    )(j0, c_blocks, q, ka, vt)


def _first_key_blocks(c_first, c_last, q_gain, k_gain, tq, tk):
    n_k = c_first.shape[0]
    n_q = n_k * tk // tq
    qk_bound = 1.02 * HEAD_DIM ** 0.5 * jnp.max(jnp.abs(q_gain)) * jnp.max(jnp.abs(k_gain))
    threshold = 104.0 + 2.0 * qk_bound + 1.0
    c_start = c_first[0::tq // tk]
    gap = c_start[:, None, :] - c_last[None, :, :]
    earlier = (jnp.arange(n_k)[None, :, None] + 1) * tk <= jnp.arange(n_q)[:, None, None] * tq
    return jnp.sum((gap < -threshold) & earlier, axis=1).astype(jnp.int32)


def _suffix_kernel(x_ref, o_ref):
    x = x_ref[...]
    p_len = x.shape[1]
    rows = lax.broadcasted_iota(jnp.int32, (p_len, p_len), 0)
    cols = lax.broadcasted_iota(jnp.int32, (p_len, p_len), 1)
    u = (rows > cols).astype(BF16)
    hi, mid, lo = _split3(x)
    o_ref[...] = (jnp.dot(hi, u, preferred_element_type=F32) + jnp.dot(mid, u, preferred_element_type=F32)
                  + jnp.dot(lo, u, preferred_element_type=F32))


def _suffix_sums(x):
    return pl.pallas_call(
        _suffix_kernel,
        out_shape=jax.ShapeDtypeStruct(x.shape, F32),
        compiler_params=pltpu.CompilerParams(vmem_limit_bytes=VMEM_LIMIT),
        name="cache_suffix_sums",
    )(x)


def _attn_sample_kernel(q_ref, kn_ref, vn_ref, ck_ref, cv_ref, d_ref, dt_ref, r_ref, o_ref):
    hp = pl.program_id(1)
    lane = lax.broadcasted_iota(jnp.int32, (1, PAIR), 1)
    first = lane < HEAD_DIM
    q = q_ref[...]
    zero = jnp.zeros_like(q)
    qh = (jnp.where(first, q, zero), jnp.where(first, zero, q))
    dblk = d_ref[...]
    dlane = lax.broadcasted_iota(jnp.int32, dblk.shape, 1)
    kn = kn_ref[...]
    vn = vn_ref[...]
    ck = ck_ref[...]
    cv = cv_ref[...]
    nt = (((1,), (1,)), ((), ()))
    outs, ls = [], []
    for h in range(2):
        dq = jnp.sum(jnp.where(dlane == 2 * hp + h, dblk, 0.0), axis=1, keepdims=True)
        s1 = lax.dot_general(qh[h], ck, nt, preferred_element_type=F32) + dq + r_ref[h:h + 1, :]
        s2 = lax.dot_general(qh[h], kn, nt, preferred_element_type=F32) + dq - dt_ref[h:h + 1, :]
        row = lax.broadcasted_iota(jnp.int32, s2.shape, 0)
        col = lax.broadcasted_iota(jnp.int32, s2.shape, 1)
        s2 = jnp.where(col <= row, s2, -jnp.inf)
        m = jnp.maximum(jnp.max(s1, axis=1, keepdims=True), jnp.max(s2, axis=1, keepdims=True))
        p1 = jnp.exp(s1 - m)
        p2 = jnp.exp(s2 - m)
        ls.append(jnp.sum(p1, axis=1, keepdims=True) + jnp.sum(p2, axis=1, keepdims=True))
        outs.append(jnp.dot(p1.astype(BF16), cv, preferred_element_type=F32)
                    + jnp.dot(p2.astype(BF16), vn, preferred_element_type=F32))
    o_ref[...] = (jnp.where(first, outs[0], outs[1]) / jnp.where(first, ls[0], ls[1])).astype(o_ref.dtype)


def _attention_sample(q, kb, vb, cache_k, cache_v, d, dt4, r4, t):
    nb, p_len = cache_k.shape[0], cache_k.shape[1]
    return pl.pallas_call(
        _attn_sample_kernel,
        grid=(nb, N_PAIRS),
        in_specs=[
            pl.BlockSpec((t, PAIR), lambda b, hp: (b, hp)),
            pl.BlockSpec((t, PAIR), lambda b, hp: (b, hp)),
            pl.BlockSpec((t, PAIR), lambda b, hp: (b, hp)),
            pl.BlockSpec((None, p_len, PAIR), lambda b, hp: (b, 0, hp)),
            pl.BlockSpec((None, p_len, PAIR), lambda b, hp: (b, 0, hp)),
            pl.BlockSpec((t, LANES), lambda b, hp: (b, 0)),
            pl.BlockSpec((None, None, 2, t), lambda b, hp: (b, hp, 0, 0)),
            pl.BlockSpec((None, None, 2, p_len), lambda b, hp: (b, hp, 0, 0)),
        ],
        out_specs=pl.BlockSpec((t, PAIR), lambda b, hp: (b, hp)),
        out_shape=jax.ShapeDtypeStruct((nb * t, ATTN_WIDTH), BF16),
        compiler_params=_cparams(("arbitrary", "arbitrary")),
        name="attention_sample",
    )(q, kb, vb, cache_k, cache_v, d, dt4, r4)


def _mix_kernel(a_ref, p_ref, hist_ref, x_ref, pw_ref, ps_ref, wo_ref, g2_ref, wrh_ref, wrl_ref, br_ref, tri_ref,
                xg_ref, plan_ref, cnt_ref, *, pos0, pos_stride, zero_first):
    i = pl.program_id(0)
    t = p_ref.shape[0]
    p = p_ref[...]
    hist = hist_ref[...]
    if zero_first:
        hist = jnp.where(i == 0, 0.0, hist)
    xh = jnp.concatenate([hist, p], axis=0)
    pos = pos0 + i * pos_stride + lax.broadcasted_iota(jnp.int32, (t, 1), 0)
    ys = []
    for g, w in enumerate(POOL_WINDOWS):
        col = xh[:, g * POOL_GROUP_DIM:(g + 1) * POOL_GROUP_DIM]
        acc = col
        span = 1
        while span < w:
            acc = acc + pltpu.roll(acc, span, axis=0)
            span *= 2
        cnt = jnp.minimum(pos + 1, w).astype(F32)
        dgrp = acc[HIST_ROWS:, :] / cnt - col[HIST_ROWS:, :]
        ys.append(jnp.dot(dgrp.astype(BF16), pw_ref[g], preferred_element_type=F32))
    pm = (jnp.concatenate(ys, axis=1) * ps_ref[...]).astype(BF16)
    mix = (jnp.dot(a_ref[...], wo_ref[0:ATTN_WIDTH, :], preferred_element_type=F32)
           + jnp.dot(pm, wo_ref[ATTN_WIDTH:, :], preferred_element_type=F32))
    x1 = x_ref[...] + mix
    xg_ref[:, 0:D_MODEL] = x1

    r = lax.rsqrt(jnp.mean(x1 * x1, axis=-1, keepdims=True) + EPS)
    n2 = (x1 * r) * g2_ref[...]

    nh, nl = _split2(n2)
    wrh = wrh_ref[...]
    logits = (jnp.dot(nh, wrh, preferred_element_type=F32) + jnp.dot(nl, wrh, preferred_element_type=F32)
              + jnp.dot(nh, wrl_ref[...], preferred_element_type=F32)) + br_ref[...]
    lane = lax.broadcasted_iota(jnp.int32, logits.shape, 1).astype(F32)
    neg = -jnp.inf
    big = jnp.float32(1 << 20)
    is_g = (lane >= N_EXPERTS) & (lane < N_EXPERTS + N_GROUPS)
    lg = jnp.where(is_g, logits, neg)
    gmax = jnp.max(lg, axis=1, keepdims=True)
    g_lane = jnp.min(jnp.where(lg == gmax, lane, big), axis=1, keepdims=True)
    top_pg = 1.0 / jnp.sum(jnp.exp(lg - gmax), axis=1, keepdims=True)
    g_idx = g_lane - N_EXPERTS
    sel = (lane >= g_idx * EXPERTS_PER_GROUP) & (lane < (g_idx + 1) * EXPERTS_PER_GROUP)
    le = jnp.where(sel, logits, neg)
    m1 = jnp.max(le, axis=1, keepdims=True)
    i1 = jnp.min(jnp.where(le == m1, lane, big), axis=1, keepdims=True)
    le2 = jnp.where(lane == i1, neg, le)
    m2 = jnp.max(le2, axis=1, keepdims=True)
    i2 = jnp.min(jnp.where(le2 == m2, lane, big), axis=1, keepdims=True)
    e2 = jnp.exp(m2 - m1)
    w1 = top_pg / (1.0 + e2)
    w2 = top_pg * e2 / (1.0 + e2)
    gate = jnp.where(lane == i1, w1, 0.0) + jnp.where(lane == i2, w2, 0.0)

    @pl.when(i == 0)
    def _():
        cnt_ref[...] = jnp.zeros_like(cnt_ref)
    e_lo = jnp.minimum(i1, i2) - g_idx * EXPERTS_PER_GROUP
    e_hi = jnp.maximum(i1, i2) - g_idx * EXPERTS_PER_GROUP
    category = g_idx * PAIRS_PER_GROUP + (e_lo * (7.0 - e_lo) * 0.5 + (e_hi - e_lo - 1.0))
    member = lane == category
    tri = tri_ref[...]
    blk = tri.shape[0]
    seen = cnt_ref[0:1, :]
    pieces = []
    for r in range(0, t, blk):
        mem = member[r:r + blk, :].astype(BF16)
        pieces.append(jnp.dot(tri, mem, preferred_element_type=F32) + seen)
        seen = seen + jnp.sum(mem.astype(F32), axis=0, keepdims=True)
    before = jnp.concatenate(pieces, axis=0)
    rank = jnp.sum(jnp.where(member, before, 0.0), axis=1, keepdims=True)
    cnt_ref[...] = jnp.broadcast_to(seen, cnt_ref.shape)
    route = jnp.where(lane == ROUTE_CATEGORY_LANE, category, jnp.where(lane == ROUTE_RANK_LANE, rank, gate))
    xg_ref[:, D_MODEL:] = route
    srow = lax.broadcasted_iota(jnp.int32, (SUBLANES, LANES), 0)
    slane = lax.broadcasted_iota(jnp.int32, (SUBLANES, LANES), 1)
    pick = (((srow == 0) & (slane == ROUTE_CATEGORY_LANE)) | ((srow == 1) & (slane == ROUTE_RANK_LANE))).astype(BF16)
    nt = (((1,), (1,)), ((), ()))
    ids = jnp.where(lane >= N_EXPERTS, route, 0.0)
    hi, lo = _split2(ids)
    plan_ref[...] = (lax.dot_general(pick, hi, nt, preferred_element_type=F32)
                     + lax.dot_general(pick, lo, nt, preferred_element_type=F32))


def _mix(a, p, hist_src, hist_map, x2d, t, pos0, pos_stride, zero_first, pw, ps, wo, g2, wrh, wrl, br):
    n = x2d.shape[0]
    blk = min(t, MXU_DIM)
    rows = lax.broadcasted_iota(jnp.int32, (blk, blk), 0)
    cols = lax.broadcasted_iota(jnp.int32, (blk, blk), 1)
    tri = (cols < rows).astype(BF16)
    row_blk = lambda w: pl.BlockSpec((t, w), lambda i: (i, 0))
    const = lambda arr: pl.BlockSpec(arr.shape, lambda i: (0,) * arr.ndim)
    return pl.pallas_call(
        functools.partial(_mix_kernel, pos0=pos0, pos_stride=pos_stride, zero_first=zero_first),
        grid=(n // t,),
        in_specs=[row_blk(ATTN_WIDTH), row_blk(POOL_WIDTH), pl.BlockSpec((HIST_ROWS, POOL_WIDTH), hist_map),
                  row_blk(D_MODEL), const(pw), const(ps), const(wo), const(g2), const(wrh), const(wrl), const(br),
                  const(tri)],
        out_specs=[row_blk(ROW_WIDTH), pl.BlockSpec((SUBLANES, t), lambda i: (i, 0)),
                   pl.BlockSpec((SUBLANES, LANES), lambda i: (0, 0))],
        out_shape=(jax.ShapeDtypeStruct((n, ROW_WIDTH), F32), jax.ShapeDtypeStruct((n // t * SUBLANES, t), F32),
                   jax.ShapeDtypeStruct((SUBLANES, LANES), F32)),
        compiler_params=_cparams(("arbitrary",)),
        name="pool_mix_route",
    )(a, p, hist_src, x2d, pw, ps, wo, g2, wrh, wrl, br, tri)


def _issue_row_scatter(dest_ref, x_ref, xs_ref, sem):
    t = x_ref.shape[0]

    def issue(g, carry):
        base = pl.multiple_of(g * SUBLANES, SUBLANES)
        tile_rows = x_ref.at[pl.ds(base, SUBLANES)]
        for u in range(SUBLANES):
            pltpu.make_async_copy(tile_rows.at[pl.ds(u, 1)], xs_ref.at[pl.ds(dest_ref[base + u], 1)],
                                  sem).start(priority=u % 2)
        return carry

    lax.fori_loop(0, t // SUBLANES, issue, 0)
    pltpu.make_async_copy(x_ref, xs_ref.at[pl.ds(0, t)], sem).wait()


def _scatter_rows_kernel(fill_ref, dest_a_ref, dest_b_ref, xa_ref, xb_ref, xs_ref, zero_ref, sem, *, steps_a):
    step = pl.program_id(0)

    @pl.when(step == 0)
    def _():
        zero_ref[...] = jnp.zeros_like(zero_ref)
        fill_rows = zero_ref.shape[0]
        last_start = xs_ref.shape[0] - fill_rows
        for g in range(fill_ref.shape[0]):
            @pl.when(fill_ref[g] < xs_ref.shape[0])
            def _(g=g):
                start = pl.multiple_of(jnp.minimum(fill_ref[g], last_start), SUBLANES)
                fill = pltpu.make_async_copy(zero_ref, xs_ref.at[pl.ds(start, fill_rows)], sem)
                fill.start()
                fill.wait()

    @pl.when(step < steps_a)
    def _():
        _issue_row_scatter(dest_a_ref, xa_ref, xs_ref, sem)

    @pl.when(step >= steps_a)
    def _():
        _issue_row_scatter(dest_b_ref, xb_ref, xs_ref, sem)


def _scatter_rows(fill_start, dest_a, xa, dest_b, xb, m_rows, t, fill_rows):
    w = xa.shape[1]
    steps_a, steps_b = xa.shape[0] // t, xb.shape[0] // t
    in_a = lambda i: jnp.minimum(i, steps_a - 1)
    in_b = lambda i: jnp.maximum(i - steps_a, 0)
    smem = pltpu.SMEM
    return pl.pallas_call(
        functools.partial(_scatter_rows_kernel, steps_a=steps_a),
        grid=(steps_a + steps_b,),
        in_specs=[pl.BlockSpec(fill_start.shape, lambda i: (0,), memory_space=smem),
                  pl.BlockSpec((t,), lambda i: (in_a(i),), memory_space=smem),
                  pl.BlockSpec((t,), lambda i: (in_b(i),), memory_space=smem),
                  pl.BlockSpec((t, w), lambda i: (in_a(i), 0)),
                  pl.BlockSpec((t, w), lambda i: (in_b(i), 0))],
        out_specs=pl.BlockSpec(memory_space=pl.ANY),
        out_shape=jax.ShapeDtypeStruct((m_rows, w), xa.dtype),
        scratch_shapes=[pltpu.VMEM((fill_rows, w), xa.dtype), pltpu.SemaphoreType.DMA],
        compiler_params=_cparams(("arbitrary",)),
        name="moe_scatter_rows",
    )(fill_start, dest_a, dest_b, xa, xb)


def _gather_rows_kernel(dest_ref, ys_ref, y_ref, sem):
    t = y_ref.shape[0]

    def issue(g, carry):
        base = pl.multiple_of(g * SUBLANES, SUBLANES)
        tile_rows = y_ref.at[pl.ds(base, SUBLANES)]
        for u in range(SUBLANES):
            pltpu.make_async_copy(ys_ref.at[pl.ds(dest_ref[base + u], 1)], tile_rows.at[pl.ds(u, 1)],
                                  sem).start(priority=u % 2)
        return carry

    lax.fori_loop(0, t // SUBLANES, issue, 0)
    pltpu.make_async_copy(ys_ref.at[pl.ds(0, t)], y_ref, sem).wait()


def _gather_rows(dest, ys, t):
    n = dest.shape[0]
    w = ys.shape[1]
    return pl.pallas_call(
        _gather_rows_kernel,
        grid=(n // t,),
        in_specs=[pl.BlockSpec((t,), lambda i: (i,), memory_space=pltpu.SMEM),
                  pl.BlockSpec(memory_space=pl.ANY)],
        out_specs=pl.BlockSpec((t, w), lambda i: (i, 0)),
        out_shape=jax.ShapeDtypeStruct((n, w), ys.dtype),
        scratch_shapes=[pltpu.SemaphoreType.DMA],
        compiler_params=_cparams(("arbitrary",)),
        name="moe_gather_rows",
    )(dest, ys)


def _moe_kernel(e1_ref, e2_ref, tv_ref, xs_ref, g2_ref, wg1_ref, wu1_ref, wd1_ref, wg2_ref, wu2_ref, wd2_ref, y_ref):
    m = pl.program_id(0)

    @pl.when(tv_ref[m] > 0)
    def _():
        x1 = xs_ref[:, 0:D_MODEL]
        route = xs_ref[:, D_MODEL:]
        lane = lax.broadcasted_iota(jnp.int32, route.shape, 1)
        r = lax.rsqrt(jnp.mean(x1 * x1, axis=-1, keepdims=True) + EPS)
        n2 = ((x1 * r) * g2_ref[...]).astype(BF16)
        acc = x1
        for e_ref, wg_ref, wu_ref, wd_ref in ((e1_ref, wg1_ref, wu1_ref, wd1_ref), (e2_ref, wg2_ref, wu2_ref, wd2_ref)):
            ge = jnp.sum(jnp.where(lane == e_ref[m], route, 0.0), axis=1, keepdims=True)
            hg = jnp.dot(n2, wg_ref[...], preferred_element_type=F32)
            hu = jnp.dot(n2, wu_ref[...], preferred_element_type=F32)
            h = (hg * jax.nn.sigmoid(hg)) * hu * ge
            acc = acc + jnp.dot(h.astype(BF16), wd_ref[...], preferred_element_type=F32)
        y_ref[...] = acc

    @pl.when(tv_ref[m] == 0)
    def _():
        y_ref[...] = jnp.zeros_like(y_ref)


def _moe_sorted(tile_e1, tile_e2, tile_valid, xs, g2, wg, wu, wd, tm):
    m_pad = tile_valid.shape[0] * tm
    up = lambda e: pl.BlockSpec((None, D_MODEL, D_EXPERT), lambda m, e1, e2, tv: ((e1, e2)[e][m], 0, 0))
    down = lambda e: pl.BlockSpec((None, D_EXPERT, D_MODEL), lambda m, e1, e2, tv: ((e1, e2)[e][m], 0, 0))
    grid_spec = pltpu.PrefetchScalarGridSpec(
        num_scalar_prefetch=3,
        grid=(m_pad // tm,),
        in_specs=[
            pl.BlockSpec((tm, ROW_WIDTH), lambda m, e1, e2, tv: (m, 0)),
            pl.BlockSpec(g2.shape, lambda m, e1, e2, tv: (0, 0)),
            up(0), up(0), down(0), up(1), up(1), down(1),
        ],
        out_specs=pl.BlockSpec((tm, D_MODEL), lambda m, e1, e2, tv: (m, 0)),
    )
    return pl.pallas_call(
        _moe_kernel,
        grid_spec=grid_spec,
        out_shape=jax.ShapeDtypeStruct((m_pad, D_MODEL), F32),
        compiler_params=_cparams(("arbitrary",)),
        name="moe_experts",
    )(tile_e1, tile_e2, tile_valid, xs, g2, wg, wu, wd, wg, wu, wd)


def _moe_plan(plan_p, cnt_p, plan_s, cnt_s, tm):
    def ids(plan):
        blocks = plan.reshape(-1, SUBLANES, plan.shape[1]).astype(jnp.int32)
        return blocks[:, 0, :], blocks[:, 1, :]

    def lookup(table, cat):
        out = jnp.zeros_like(cat)
        for c in range(N_CATEGORIES):
            out = jnp.where(cat == c, table[c], out)
        return out

    (kp, rank_p), (ks, rank_s) = ids(plan_p), ids(plan_s)
    n_total = kp.size + ks.size
    n_tiles = n_total // tm + N_CATEGORIES
    cp = cnt_p[0, :N_CATEGORIES].astype(jnp.int32)
    cs = cnt_s[0, :N_CATEGORIES].astype(jnp.int32)
    total = cp + cs
    tiles = (total + tm - 1) // tm
    tile_end = jnp.cumsum(tiles)
    offset = (tile_end - tiles) * tm
    dest_p = (lookup(offset, kp) + rank_p).reshape(-1)
    dest_s = (lookup(offset + cp, ks) + rank_s).reshape(-1)
    tile_id = jnp.arange(n_tiles)
    tile_cat = jnp.minimum(jnp.sum(tile_id[:, None] >= tile_end[None, :], axis=1), N_CATEGORIES - 1)
    used = jnp.clip(offset[tile_cat] + total[tile_cat] - tile_id * tm, 0, tm)
    tile_valid = jnp.where(tile_id < tile_end[N_CATEGORIES - 1], used, 0).astype(jnp.int32)
    experts = jnp.asarray(CATEGORY_EXPERTS, jnp.int32)
    tile_e1, tile_e2 = experts[tile_cat, 0], experts[tile_cat, 1]
    m_rows = (n_tiles + 1) * tm
    ends = ((offset + total) // SUBLANES) * SUBLANES
    tail = ends[N_CATEGORIES - 1] + tm * jnp.arange(1, m_rows // tm - n_total // tm + 1)
    fill_start = jnp.concatenate([ends, tail]).astype(jnp.int32)
    return dest_p, dest_s, fill_start, tile_e1, tile_e2, tile_valid, m_rows


def kernel(x_prompt, x_sample, cache_k, cache_v, cache_logf, state_pool, norm1_g, w_in, b_f, q_norm_g, k_norm_g,
           pool_w, pool_scale, w_out, norm2_g, w_router_group, b_router_group, w_router_expert, b_router_expert,
           w_gate, w_up, w_down):
    depth = norm1_g.shape[0]
    assert depth == 1, "single-layer step"
    b_p, s_p, _ = x_prompt.shape
    assert b_p == 1, "prompt kernels assume one stream"
    b_s, t_s, _ = x_sample.shape
    p_len = cache_k.shape[2]
    l = 0

    w = w_in[l]
    a3 = 3 * ATTN_WIDTH
    w_all = jnp.concatenate(
        [w[:, :a3], w[:, a3 + N_HEADS:], jnp.pad(w[:, a3:a3 + N_HEADS], ((0, 0), (0, LANES - N_HEADS)))],
        axis=1).astype(BF16)
    bf_pad = jnp.pad(b_f[l], (0, LANES - N_HEADS)).reshape(1, LANES)
    g1 = norm1_g[l].reshape(1, D_MODEL)
    qg = jnp.tile(q_norm_g[l], N_HEADS).reshape(1, ATTN_WIDTH)
    kg = jnp.tile(k_norm_g[l], N_HEADS).reshape(1, ATTN_WIDTH)
    hr = lax.broadcasted_iota(jnp.int32, (MXU_DIM, MXU_DIM), 0) // HEAD_DIM
    hc = lax.broadcasted_iota(jnp.int32, (MXU_DIM, MXU_DIM), 1) // HEAD_DIM
    hm = jnp.where(hr == hc, 1.0 / HEAD_DIM, 0.0).astype(BF16)
    pw = pool_w[l].astype(BF16)
    ps = pool_scale[l].reshape(1, POOL_WIDTH)
    wo = w_out[l].astype(BF16)
    g2 = norm2_g[l].reshape(1, D_MODEL)
    w_r = jnp.pad(jnp.concatenate([w_router_expert[l], w_router_group[l]], axis=1),
                  ((0, 0), (0, LANES - N_EXPERTS - N_GROUPS)))
    wrh = w_r.astype(BF16)
    wrl = (w_r - wrh.astype(F32)).astype(BF16)
    br = jnp.pad(jnp.concatenate([b_router_expert[l], b_router_group[l]]),
                 (0, LANES - N_EXPERTS - N_GROUPS)).reshape(1, LANES)
    wg = w_gate[l].astype(BF16)
    wu = w_up[l].astype(BF16)
    wd = w_down[l].astype(BF16)

    tile = 512
    query_tile = 512
    key_tile = 256
    expert_tile = 512

    xp = x_prompt.reshape(s_p, D_MODEL)
    q, kf, _, vf, _, ka, vt, pin, logf, _, edge = _in_project(xp, tile, tile, True, key_tile, g1, w_all, bf_pad, qg, kg,
                                                              hm)
    blocks_per_tile = tile // key_tile
    edge = edge.reshape(s_p // tile, SUBLANES, LANES)[:, :2 * blocks_per_tile, :N_HEADS]
    edge = edge.reshape(s_p // key_tile, 2, N_HEADS)
    c_blocks = edge[:, 0, :]
    j0 = _first_key_blocks(c_blocks, edge[:, 1, :], q_norm_g[l], k_norm_g[l], query_tile, key_tile)
    a = _attention_prompt(j0, c_blocks, q, ka, vt, query_tile, key_tile)
    hist_map = lambda i: (jnp.maximum(i * (tile // HIST_ROWS) - 1, 0), 0)
    xg_p, plan_p, cnt_p = _mix(a, pin, pin, hist_map, xp, tile, 0, tile, True, pw, ps, wo, g2, wrh, wrl, br)
    k_prompt = kf.reshape(depth, b_p, s_p, N_HEADS, HEAD_DIM)
    v_prompt = vf.reshape(depth, b_p, s_p, N_HEADS, HEAD_DIM)
    logf_prompt = logf[:, :N_HEADS].reshape(depth, b_p, s_p, N_HEADS)
    pool_prompt = pin[s_p - POOL_PAD:].reshape(depth, b_p, POOL_PAD, POOL_WIDTH)

    n_s = b_s * t_s
    xs = x_sample.reshape(n_s, D_MODEL)
    q, kf, kb, vf, vb, _, _, pin, logf, d, _ = _in_project(xs, tile, t_s, False, key_tile, g1, w_all, bf_pad, qg, kg, hm)
    dt4 = d[:, :N_HEADS].reshape(b_s, t_s, N_PAIRS, 2).transpose(0, 2, 3, 1)
    clf = cache_logf[l].astype(F32).transpose(0, 2, 1).reshape(b_s * N_HEADS, p_len)
    r4 = _suffix_sums(clf).reshape(b_s, N_PAIRS, 2, p_len)
    ck = cache_k[l].reshape(b_s, p_len, ATTN_WIDTH).astype(BF16)
    cv = cache_v[l].reshape(b_s, p_len, ATTN_WIDTH).astype(BF16)
    a = _attention_sample(q, kb, vb, ck, cv, d, dt4, r4, t_s)
    hist_s = jnp.pad(state_pool[l], ((0, 0), (HIST_ROWS - POOL_PAD, 0), (0, 0))).reshape(b_s * HIST_ROWS, POOL_WIDTH)
    xg_s, plan_s, cnt_s = _mix(a, pin, hist_s, lambda i: (i, 0), xs, t_s, p_len, 0, False, pw, ps, wo, g2, wrh, wrl,
                               br)
    k_sample = kf.reshape(depth, b_s, t_s, N_HEADS, HEAD_DIM)
    v_sample = vf.reshape(depth, b_s, t_s, N_HEADS, HEAD_DIM)
    logf_sample = logf[:, :N_HEADS].reshape(depth, b_s, t_s, N_HEADS)
    pool_sample = pin.reshape(b_s, t_s, POOL_WIDTH)[:, t_s - POOL_PAD:].reshape(depth, b_s, POOL_PAD, POOL_WIDTH)

    dest_p, dest_s, fill_start, tile_e1, tile_e2, tile_valid, m_rows = _moe_plan(plan_p, cnt_p, plan_s, cnt_s,
                                                                              expert_tile)
    rows = _scatter_rows(fill_start, dest_p, xg_p, dest_s, xg_s, m_rows, tile, expert_tile)
    ys = _moe_sorted(tile_e1, tile_e2, tile_valid, rows, g2, wg, wu, wd, expert_tile)
    y_prompt = _gather_rows(dest_p, ys, tile).reshape(b_p, s_p, D_MODEL)
    y_sample = _gather_rows(dest_s, ys, tile).reshape(b_s, t_s, D_MODEL)

    return (y_prompt, y_sample, k_prompt, v_prompt, logf_prompt, pool_prompt,
            k_sample, v_sample, logf_sample, pool_sample)
```

```python
import functools

import jax
import jax.numpy as jnp
from jax import lax
from jax.experimental import pallas as pl
from jax.experimental.pallas import tpu as pltpu

F32 = jnp.float32
BF16 = jnp.bfloat16

D_MODEL = 1024
ATTN_WIDTH = 512
N_HEADS = 8
HEAD_DIM = 64
POOL_WIDTH = 512
POOL_WINDOWS = (2, 4, 8, 16)
POOL_GROUP_DIM = 128
POOL_PAD = 15
HIST_ROWS = 16
N_GROUPS = 4
EXPERTS_PER_GROUP = 4
N_EXPERTS = 16
D_EXPERT = 512
EPS = 1e-6
LANES = 128
SUBLANES = 8
MXU_DIM = 256
PAIR = 2 * HEAD_DIM
N_PAIRS = N_HEADS // 2
ROWSUM_ROW = (HEAD_DIM, 0)
VMEM_LIMIT = 56 * 1024 * 1024
ROW_WIDTH = D_MODEL + LANES
PAIRS_PER_GROUP = 6
N_CATEGORIES = N_GROUPS * PAIRS_PER_GROUP
PAIR_ORDER = ((0, 1), (0, 2), (1, 2), (1, 3), (2, 3), (0, 3))
CATEGORY_EXPERTS = tuple((4 * g + a, 4 * g + b) for g in range(N_GROUPS) for a, b in PAIR_ORDER)
ROUTE_CATEGORY_LANE = N_EXPERTS
ROUTE_RANK_LANE = N_EXPERTS + 1


def _cparams(sem):
    return pltpu.CompilerParams(dimension_semantics=sem, vmem_limit_bytes=VMEM_LIMIT)


def _split3(x):
    hi = x.astype(BF16)
    r1 = x - hi.astype(F32)
    mid = r1.astype(BF16)
    lo = (r1 - mid.astype(F32)).astype(BF16)
    return hi, mid, lo


def _split2(x):
    hi = x.astype(BF16)
    lo = (x - hi.astype(F32)).astype(BF16)
    return hi, lo


def _inproj_kernel(x_ref, g1_ref, w_ref, bf_ref, qg_ref, kg_ref, hm_ref, tri_ref, place_ref,
                   q_ref, kf_ref, kb_ref, vf_ref, vb_ref, ka_ref, vt_ref, p_ref, logf_ref, c_ref, edge_ref, carry_ref, *,
                   carry_rows, key_blk):
    x = x_ref[...]
    r = lax.rsqrt(jnp.mean(x * x, axis=-1, keepdims=True) + EPS)
    n = ((x * r) * g1_ref[...]).astype(BF16)
    z = jnp.dot(n, w_ref[...], preferred_element_type=F32)

    hm = hm_ref[...]

    def head_norm(zz, g):
        hi, lo = _split2(zz * zz)
        half = hm.shape[0]
        ms = jnp.concatenate(
            [jnp.dot(hi[:, r:r + half], hm, preferred_element_type=F32)
             + jnp.dot(lo[:, r:r + half], hm, preferred_element_type=F32) for r in range(0, zz.shape[1], half)], axis=1)
        return (zz * lax.rsqrt(ms + EPS)) * g

    q = head_norm(z[:, 0:ATTN_WIDTH], qg_ref[...])
    q_ref[...] = (q * (HEAD_DIM ** -0.5)).astype(BF16)
    k = head_norm(z[:, ATTN_WIDTH:2 * ATTN_WIDTH], kg_ref[...])
    kf_ref[...] = k
    kb_ref[...] = k.astype(BF16)
    v = z[:, 2 * ATTN_WIDTH:3 * ATTN_WIDTH]
    vf_ref[...] = v
    vb_ref[...] = v.astype(BF16)
    p_ref[...] = z[:, 3 * ATTN_WIDTH:3 * ATTN_WIDTH + POOL_WIDTH]
    prow = lax.broadcasted_iota(jnp.int32, (PAIR, v.shape[0]), 0)
    for hp in range(N_PAIRS):
        vpair_t = v[:, hp * PAIR:(hp + 1) * PAIR].T
        for h in range(2):
            own = (prow < HEAD_DIM) if h == 0 else (prow >= HEAD_DIM)
            marker = jnp.where(prow == ROWSUM_ROW[h], 1.0, 0.0)
            vt_ref[(2 * hp + h) * LANES:(2 * hp + h + 1) * LANES, :] = jnp.where(own, vpair_t, marker).astype(BF16)

    f = z[:, 3 * ATTN_WIDTH + POOL_WIDTH:] + bf_ref[...]
    lane = lax.broadcasted_iota(jnp.int32, f.shape, 1)
    logf = -(jnp.maximum(-f, 0.0) + jnp.log1p(jnp.exp(-jnp.abs(f))))
    logf = jnp.where(lane < N_HEADS, logf, 0.0)
    logf_ref[...] = logf

    tri = tri_ref[...]
    blk = tri.shape[0]
    parts = _split3(logf)
    pieces = []
    for r in range(0, logf.shape[0], blk):
        piece = sum(jnp.dot(tri, part[r:r + blk, :], preferred_element_type=F32) for part in parts)
        if carry_rows and pieces:
            piece = piece + pieces[-1][blk - 1:blk, :]
        pieces.append(piece)
    c = jnp.concatenate(pieces, axis=0)
    if carry_rows:
        @pl.when(pl.program_id(0) == 0)
        def _():
            carry_ref[...] = jnp.zeros_like(carry_ref)
        c = c + carry_ref[0:1, :]
        carry_ref[...] = jnp.broadcast_to(c[c.shape[0] - 1:, :], carry_ref.shape)
    c_ref[...] = c
    edges = [c[r:r + 1, :] for b in range(c.shape[0] // key_blk) for r in (b * key_blk, (b + 1) * key_blk - 1)]
    edge_ref[...] = jnp.concatenate(edges + [jnp.zeros((SUBLANES - len(edges), LANES), F32)], axis=0)

    tm = c.shape[0]
    rel = jnp.concatenate([c[r:r + key_blk, :] - c[r:r + 1, :] for r in range(0, tm, key_blk)], axis=0)
    parts = jnp.concatenate(_split3(rel), axis=1)
    decay = jnp.dot(parts, place_ref[...], preferred_element_type=F32).astype(BF16)
    kb = k.astype(BF16)
    for hp in range(N_PAIRS):
        ka_ref[:, 2 * hp * LANES:(2 * hp + 1) * LANES] = kb[:, hp * PAIR:(hp + 1) * PAIR]
        ka_ref[:, (2 * hp + 1) * LANES:(2 * hp + 2) * LANES] = decay[:, hp * LANES:(hp + 1) * LANES]


def _decay_lane(h, part):
    return 3 * h + part


def _in_project(x2d, tm, seg, carry_rows, key_blk, g1, w_all, bf_pad, qg, kg, hm):
    n = x2d.shape[0]
    blk = min(tm, MXU_DIM)
    assert seg % blk == 0 or blk % seg == 0, "running-sum segments must nest with the triangular block"
    rows = lax.broadcasted_iota(jnp.int32, (blk, blk), 0)
    cols = lax.broadcasted_iota(jnp.int32, (blk, blk), 1)
    tri = ((cols <= rows) & (rows // seg == cols // seg)).astype(BF16)
    src = jnp.arange(3 * LANES)
    part, head = src // LANES, src % LANES
    dst = (head // 2) * LANES + _decay_lane(head % 2, part)
    place = ((jnp.arange(N_PAIRS * LANES)[None, :] == dst[:, None]) & (head < N_HEADS)[:, None]).astype(BF16)
    row_blk = lambda w: pl.BlockSpec((tm, w), lambda i: (i, 0))
    const = lambda a: pl.BlockSpec(a.shape, lambda i: (0,) * a.ndim)
    out_shape = (
        jax.ShapeDtypeStruct((n, ATTN_WIDTH), BF16),
        jax.ShapeDtypeStruct((n, ATTN_WIDTH), F32),
        jax.ShapeDtypeStruct((n, ATTN_WIDTH), BF16),
        jax.ShapeDtypeStruct((n, ATTN_WIDTH), F32),
        jax.ShapeDtypeStruct((n, ATTN_WIDTH), BF16),
        jax.ShapeDtypeStruct((n, N_PAIRS * 2 * LANES), BF16),
        jax.ShapeDtypeStruct((N_HEADS * LANES, n), BF16),
        jax.ShapeDtypeStruct((n, POOL_WIDTH), F32),
        jax.ShapeDtypeStruct((n, LANES), F32),
        jax.ShapeDtypeStruct((n, LANES), F32),
        jax.ShapeDtypeStruct((n // tm * SUBLANES, LANES), F32),
    )
    return pl.pallas_call(
        functools.partial(_inproj_kernel, carry_rows=carry_rows, key_blk=key_blk),
        grid=(n // tm,),
        in_specs=[row_blk(D_MODEL), const(g1), const(w_all), const(bf_pad), const(qg), const(kg), const(hm),
                  const(tri), const(place)],
        out_specs=[row_blk(ATTN_WIDTH)] * 5 + [row_blk(N_PAIRS * 2 * LANES),
                                                pl.BlockSpec((N_HEADS * LANES, tm), lambda i: (0, i)),
                                                row_blk(POOL_WIDTH), row_blk(LANES), row_blk(LANES),
                                                pl.BlockSpec((SUBLANES, LANES), lambda i: (i, 0))],
        out_shape=out_shape,
        scratch_shapes=[pltpu.VMEM((8, LANES), F32)],
        compiler_params=_cparams(("arbitrary",)),
        name="in_project",
    )(x2d, g1, w_all, bf_pad, qg, kg, hm, tri, place)


def _attn_prompt_kernel(j0_ref, cb_ref, q_ref, k_ref, vt_ref, o_ref, acc_ref, m_ref, *, tq, tk):
    hp = pl.program_id(0)
    i = pl.program_id(1)
    nk = tq // tk
    lane = lax.broadcasted_iota(jnp.int32, (1, PAIR), 1)
    first = lane < HEAD_DIM
    q = q_ref[...]
    zero = jnp.zeros_like(q)
    qw = []
    for h in range(2):
        minus = jnp.zeros((1, LANES), F32)
        for part in range(3):
            minus = jnp.where(lane == _decay_lane(h, part), -1.0, minus)
        qw.append(jnp.concatenate([jnp.where(first == (h == 0), q, zero),
                                   jnp.broadcast_to(minus, (tq, LANES)).astype(BF16)], axis=1))
    q_start = i * tq

    acc_ref[...] = jnp.zeros_like(acc_ref)
    m_ref[...] = jnp.full_like(m_ref, -jnp.inf)

    last = (i + 1) * nk - 1

    def scores(h, j, masked=True):
        start = pl.multiple_of(j * tk, tk)
        k = k_ref[pl.ds(start, tk), :]
        s = lax.dot_general(k, qw[h], (((1,), (1,)), ((), ())), preferred_element_type=F32)
        if masked:
            key = lax.broadcasted_iota(jnp.int32, s.shape, 0)
            qry = lax.broadcasted_iota(jnp.int32, s.shape, 1)
            s = jnp.where(key - qry <= q_start - j * tk, s, -jnp.inf)
        return s, jnp.max(s, axis=0, keepdims=True)

    def update(h, j, s, colmax):
        shift = cb_ref[j, 2 * hp + h] - cb_ref[i * nk, 2 * hp + h]
        start = pl.multiple_of(j * tk, tk)
        vt = vt_ref[h * LANES:(h + 1) * LANES, pl.ds(start, tk)]
        m_prev = m_ref[h]
        m_new = jnp.maximum(m_prev, colmax - shift)
        alpha = jnp.exp(m_prev - m_new)
        p = jnp.exp(s - (m_new + shift))
        m_ref[h] = m_new
        acc_ref[h] = acc_ref[h] * alpha + jnp.dot(vt, p.astype(BF16), preferred_element_type=F32)

    j_first = [j0_ref[i, 2 * hp + h] for h in range(2)]
    j_both = jnp.maximum(j_first[0], j_first[1])
    start = [scores(h, j_first[h]) for h in range(2)]
    ahead = []
    for h in range(2):
        def body_one(j, carry, h=h):
            nxt = scores(h, j + 1)
            update(h, j, *carry)
            return nxt

        ahead.append(lax.fori_loop(j_first[h], j_both, body_one, start[h]))

    def body_both(j, carry, masked):
        nxt = (scores(0, j + 1, masked), scores(1, j + 1, masked))
        update(0, j, *carry[0])
        update(1, j, *carry[1])
        return nxt

    j_mask = jnp.maximum(j_both, i * nk - 1)
    pairs = (j_mask - j_both) // 2

    def body_two(t, carry):
        j = j_both + 2 * t
        return body_both(j + 1, body_both(j, carry, False), False)

    carry = lax.fori_loop(0, pairs, body_two, (ahead[0], ahead[1]))
    carry = lax.fori_loop(j_both + 2 * pairs, last, functools.partial(body_both, masked=True), carry)
    update(0, last, *carry[0])
    update(1, last, *carry[1])

    out = [acc_ref[h] / acc_ref[h][ROWSUM_ROW[h]:ROWSUM_ROW[h] + 1, :] for h in range(2)]
    row = lax.broadcasted_iota(jnp.int32, (PAIR, 1), 0)
    o_ref[...] = jnp.where(row < HEAD_DIM, out[0], out[1]).T.astype(o_ref.dtype)


def _attention_prompt(j0, c_blocks, q, ka, vt, tq, tk):
    n = q.shape[0]
    grid_spec = pltpu.PrefetchScalarGridSpec(
        num_scalar_prefetch=2,
        grid=(N_PAIRS, n // tq),
        in_specs=[
            pl.BlockSpec((tq, PAIR), lambda hp, i, j0, cb: (i, hp)),
            pl.BlockSpec((n, 2 * LANES), lambda hp, i, j0, cb: (0, hp)),
            pl.BlockSpec((2 * LANES, n), lambda hp, i, j0, cb: (hp, 0)),
        ],
        out_specs=pl.BlockSpec((tq, PAIR), lambda hp, i, j0, cb: (i, hp)),
        scratch_shapes=[pltpu.VMEM((2, LANES, tq), F32), pltpu.VMEM((2, 1, tq), F32)],
    )
    return pl.pallas_call(
        functools.partial(_attn_prompt_kernel, tq=tq, tk=tk),
        grid_spec=grid_spec,
        out_shape=jax.ShapeDtypeStruct((n, ATTN_WIDTH), BF16),
        compiler_params=_cparams(("arbitrary", "arbitrary")),
        name="attention_prompt",
    )(j0, c_blocks, q, ka, vt)


def _first_key_blocks(c_first, c_last, q_gain, k_gain, tq, tk):
    n_k = c_first.shape[0]
    n_q = n_k * tk // tq
    qk_bound = 1.02 * HEAD_DIM ** 0.5 * jnp.max(jnp.abs(q_gain)) * jnp.max(jnp.abs(k_gain))
    threshold = 104.0 + 2.0 * qk_bound + 1.0
    c_start = c_first[0::tq // tk]
    gap = c_start[:, None, :] - c_last[None, :, :]
    earlier = (jnp.arange(n_k)[None, :, None] + 1) * tk <= jnp.arange(n_q)[:, None, None] * tq
    return jnp.sum((gap < -threshold) & earlier, axis=1).astype(jnp.int32)


def _suffix_kernel(x_ref, o_ref):
    x = x_ref[...]
    p_len = x.shape[1]
    rows = lax.broadcasted_iota(jnp.int32, (p_len, p_len), 0)
    cols = lax.broadcasted_iota(jnp.int32, (p_len, p_len), 1)
    u = (rows > cols).astype(BF16)
    hi, mid, lo = _split3(x)
    o_ref[...] = (jnp.dot(hi, u, preferred_element_type=F32) + jnp.dot(mid, u, preferred_element_type=F32)
                  + jnp.dot(lo, u, preferred_element_type=F32))


def _suffix_sums(x):
    return pl.pallas_call(
        _suffix_kernel,
        out_shape=jax.ShapeDtypeStruct(x.shape, F32),
        compiler_params=pltpu.CompilerParams(vmem_limit_bytes=VMEM_LIMIT),
        name="cache_suffix_sums",
    )(x)


def _attn_sample_kernel(q_ref, kn_ref, vn_ref, ck_ref, cv_ref, d_ref, dt_ref, r_ref, o_ref):
    hp = pl.program_id(1)
    lane = lax.broadcasted_iota(jnp.int32, (1, PAIR), 1)
    first = lane < HEAD_DIM
    q = q_ref[...]
    zero = jnp.zeros_like(q)
    qh = (jnp.where(first, q, zero), jnp.where(first, zero, q))
    dblk = d_ref[...]
    dlane = lax.broadcasted_iota(jnp.int32, dblk.shape, 1)
    kn = kn_ref[...]
    vn = vn_ref[...]
    ck = ck_ref[...]
    cv = cv_ref[...]
    nt = (((1,), (1,)), ((), ()))
    outs, ls = [], []
    for h in range(2):
        dq = jnp.sum(jnp.where(dlane == 2 * hp + h, dblk, 0.0), axis=1, keepdims=True)
        s1 = lax.dot_general(qh[h], ck, nt, preferred_element_type=F32) + dq + r_ref[h:h + 1, :]
        s2 = lax.dot_general(qh[h], kn, nt, preferred_element_type=F32) + dq - dt_ref[h:h + 1, :]
        row = lax.broadcasted_iota(jnp.int32, s2.shape, 0)
        col = lax.broadcasted_iota(jnp.int32, s2.shape, 1)
        s2 = jnp.where(col <= row, s2, -jnp.inf)
        m = jnp.maximum(jnp.max(s1, axis=1, keepdims=True), jnp.max(s2, axis=1, keepdims=True))
        p1 = jnp.exp(s1 - m)
        p2 = jnp.exp(s2 - m)
        ls.append(jnp.sum(p1, axis=1, keepdims=True) + jnp.sum(p2, axis=1, keepdims=True))
        outs.append(jnp.dot(p1.astype(BF16), cv, preferred_element_type=F32)
                    + jnp.dot(p2.astype(BF16), vn, preferred_element_type=F32))
    o_ref[...] = (jnp.where(first, outs[0], outs[1]) / jnp.where(first, ls[0], ls[1])).astype(o_ref.dtype)


def _attention_sample(q, kb, vb, cache_k, cache_v, d, dt4, r4, t):
    nb, p_len = cache_k.shape[0], cache_k.shape[1]
    return pl.pallas_call(
        _attn_sample_kernel,
        grid=(nb, N_PAIRS),
        in_specs=[
            pl.BlockSpec((t, PAIR), lambda b, hp: (b, hp)),
            pl.BlockSpec((t, PAIR), lambda b, hp: (b, hp)),
            pl.BlockSpec((t, PAIR), lambda b, hp: (b, hp)),
            pl.BlockSpec((None, p_len, PAIR), lambda b, hp: (b, 0, hp)),
            pl.BlockSpec((None, p_len, PAIR), lambda b, hp: (b, 0, hp)),
            pl.BlockSpec((t, LANES), lambda b, hp: (b, 0)),
            pl.BlockSpec((None, None, 2, t), lambda b, hp: (b, hp, 0, 0)),
            pl.BlockSpec((None, None, 2, p_len), lambda b, hp: (b, hp, 0, 0)),
        ],
        out_specs=pl.BlockSpec((t, PAIR), lambda b, hp: (b, hp)),
        out_shape=jax.ShapeDtypeStruct((nb * t, ATTN_WIDTH), BF16),
        compiler_params=_cparams(("arbitrary", "arbitrary")),
        name="attention_sample",
    )(q, kb, vb, cache_k, cache_v, d, dt4, r4)


def _mix_kernel(a_ref, p_ref, hist_ref, x_ref, pw_ref, ps_ref, wo_ref, g2_ref, wrh_ref, wrl_ref, br_ref, tri_ref,
                xg_ref, plan_ref, cnt_ref, *, pos0, pos_stride, zero_first):
    i = pl.program_id(0)
    t = p_ref.shape[0]
    p = p_ref[...]
    hist = hist_ref[...]
    if zero_first:
        hist = jnp.where(i == 0, 0.0, hist)
    xh = jnp.concatenate([hist, p], axis=0)
    pos = pos0 + i * pos_stride + lax.broadcasted_iota(jnp.int32, (t, 1), 0)
    ys = []
    for g, w in enumerate(POOL_WINDOWS):
        col = xh[:, g * POOL_GROUP_DIM:(g + 1) * POOL_GROUP_DIM]
        acc = col
        span = 1
        while span < w:
            acc = acc + pltpu.roll(acc, span, axis=0)
            span *= 2
        cnt = jnp.minimum(pos + 1, w).astype(F32)
        dgrp = acc[HIST_ROWS:, :] / cnt - col[HIST_ROWS:, :]
        ys.append(jnp.dot(dgrp.astype(BF16), pw_ref[g], preferred_element_type=F32))
    pm = (jnp.concatenate(ys, axis=1) * ps_ref[...]).astype(BF16)
    mix = (jnp.dot(a_ref[...], wo_ref[0:ATTN_WIDTH, :], preferred_element_type=F32)
           + jnp.dot(pm, wo_ref[ATTN_WIDTH:, :], preferred_element_type=F32))
    x1 = x_ref[...] + mix
    xg_ref[:, 0:D_MODEL] = x1

    r = lax.rsqrt(jnp.mean(x1 * x1, axis=-1, keepdims=True) + EPS)
    n2 = (x1 * r) * g2_ref[...]

    nh, nl = _split2(n2)
    wrh = wrh_ref[...]
    logits = (jnp.dot(nh, wrh, preferred_element_type=F32) + jnp.dot(nl, wrh, preferred_element_type=F32)
              + jnp.dot(nh, wrl_ref[...], preferred_element_type=F32)) + br_ref[...]
    lane = lax.broadcasted_iota(jnp.int32, logits.shape, 1).astype(F32)
    neg = -jnp.inf
    big = jnp.float32(1 << 20)
    is_g = (lane >= N_EXPERTS) & (lane < N_EXPERTS + N_GROUPS)
    lg = jnp.where(is_g, logits, neg)
    gmax = jnp.max(lg, axis=1, keepdims=True)
    g_lane = jnp.min(jnp.where(lg == gmax, lane, big), axis=1, keepdims=True)
    top_pg = 1.0 / jnp.sum(jnp.exp(lg - gmax), axis=1, keepdims=True)
    g_idx = g_lane - N_EXPERTS
    sel = (lane >= g_idx * EXPERTS_PER_GROUP) & (lane < (g_idx + 1) * EXPERTS_PER_GROUP)
    le = jnp.where(sel, logits, neg)
    m1 = jnp.max(le, axis=1, keepdims=True)
    i1 = jnp.min(jnp.where(le == m1, lane, big), axis=1, keepdims=True)
    le2 = jnp.where(lane == i1, neg, le)
    m2 = jnp.max(le2, axis=1, keepdims=True)
    i2 = jnp.min(jnp.where(le2 == m2, lane, big), axis=1, keepdims=True)
    e2 = jnp.exp(m2 - m1)
    w1 = top_pg / (1.0 + e2)
    w2 = top_pg * e2 / (1.0 + e2)
    gate = jnp.where(lane == i1, w1, 0.0) + jnp.where(lane == i2, w2, 0.0)

    @pl.when(i == 0)
    def _():
        cnt_ref[...] = jnp.zeros_like(cnt_ref)
    e_lo = jnp.minimum(i1, i2) - g_idx * EXPERTS_PER_GROUP
    e_hi = jnp.maximum(i1, i2) - g_idx * EXPERTS_PER_GROUP
    pair = jnp.zeros_like(e_lo)
    for idx, (a, b) in enumerate(PAIR_ORDER):
        pair = jnp.where((e_lo == a) & (e_hi == b), float(idx), pair)
    category = g_idx * PAIRS_PER_GROUP + pair
    member = lane == category
    tri = tri_ref[...]
    blk = tri.shape[0]
    seen = cnt_ref[0:1, :]
    pieces = []
    for r in range(0, t, blk):
        mem = member[r:r + blk, :].astype(BF16)
        pieces.append(jnp.dot(tri, mem, preferred_element_type=F32) + seen)
        seen = seen + jnp.sum(mem.astype(F32), axis=0, keepdims=True)
    before = jnp.concatenate(pieces, axis=0)
    rank = jnp.sum(jnp.where(member, before, 0.0), axis=1, keepdims=True)
    cnt_ref[...] = jnp.broadcast_to(seen, cnt_ref.shape)
    route = jnp.where(lane == ROUTE_CATEGORY_LANE, category, jnp.where(lane == ROUTE_RANK_LANE, rank, gate))
    xg_ref[:, D_MODEL:] = route
    srow = lax.broadcasted_iota(jnp.int32, (SUBLANES, LANES), 0)
    slane = lax.broadcasted_iota(jnp.int32, (SUBLANES, LANES), 1)
    pick = (((srow == 0) & (slane == ROUTE_CATEGORY_LANE)) | ((srow == 1) & (slane == ROUTE_RANK_LANE))).astype(BF16)
    nt = (((1,), (1,)), ((), ()))
    ids = jnp.where(lane >= N_EXPERTS, route, 0.0)
    hi, lo = _split2(ids)
    plan_ref[...] = (lax.dot_general(pick, hi, nt, preferred_element_type=F32)
                     + lax.dot_general(pick, lo, nt, preferred_element_type=F32))


def _mix(a, p, hist_src, hist_map, x2d, t, pos0, pos_stride, zero_first, pw, ps, wo, g2, wrh, wrl, br):
    n = x2d.shape[0]
    blk = min(t, MXU_DIM)
    rows = lax.broadcasted_iota(jnp.int32, (blk, blk), 0)
    cols = lax.broadcasted_iota(jnp.int32, (blk, blk), 1)
    tri = (cols < rows).astype(BF16)
    row_blk = lambda w: pl.BlockSpec((t, w), lambda i: (i, 0))
    const = lambda arr: pl.BlockSpec(arr.shape, lambda i: (0,) * arr.ndim)
    return pl.pallas_call(
        functools.partial(_mix_kernel, pos0=pos0, pos_stride=pos_stride, zero_first=zero_first),
        grid=(n // t,),
        in_specs=[row_blk(ATTN_WIDTH), row_blk(POOL_WIDTH), pl.BlockSpec((HIST_ROWS, POOL_WIDTH), hist_map),
                  row_blk(D_MODEL), const(pw), const(ps), const(wo), const(g2), const(wrh), const(wrl), const(br),
                  const(tri)],
        out_specs=[row_blk(ROW_WIDTH), pl.BlockSpec((SUBLANES, t), lambda i: (i, 0)),
                   pl.BlockSpec((SUBLANES, LANES), lambda i: (0, 0))],
        out_shape=(jax.ShapeDtypeStruct((n, ROW_WIDTH), F32), jax.ShapeDtypeStruct((n // t * SUBLANES, t), F32),
                   jax.ShapeDtypeStruct((SUBLANES, LANES), F32)),
        compiler_params=_cparams(("arbitrary",)),
        name="pool_mix_route",
    )(a, p, hist_src, x2d, pw, ps, wo, g2, wrh, wrl, br, tri)


def _issue_row_scatter(dest_ref, x_ref, xs_ref, sem):
    t = x_ref.shape[0]

    def issue(g, carry):
        base = pl.multiple_of(g * SUBLANES, SUBLANES)
        tile_rows = x_ref.at[pl.ds(base, SUBLANES)]
        for u in range(SUBLANES):
            pltpu.make_async_copy(tile_rows.at[pl.ds(u, 1)], xs_ref.at[pl.ds(dest_ref[base + u], 1)],
                                  sem).start(priority=u % 2)
        return carry

    lax.fori_loop(0, t // SUBLANES, issue, 0)
    pltpu.make_async_copy(x_ref, xs_ref.at[pl.ds(0, t)], sem).wait()


def _scatter_rows_kernel(fill_ref, dest_a_ref, dest_b_ref, xa_ref, xb_ref, xs_ref, zero_ref, sem, *, steps_a):
    step = pl.program_id(0)

    @pl.when(step == 0)
    def _():
        zero_ref[...] = jnp.zeros_like(zero_ref)
        fill_rows = zero_ref.shape[0]
        last_start = xs_ref.shape[0] - fill_rows
        for g in range(fill_ref.shape[0]):
            @pl.when(fill_ref[g] < xs_ref.shape[0])
            def _(g=g):
                start = pl.multiple_of(jnp.minimum(fill_ref[g], last_start), SUBLANES)
                fill = pltpu.make_async_copy(zero_ref, xs_ref.at[pl.ds(start, fill_rows)], sem)
                fill.start()
                fill.wait()

    @pl.when(step < steps_a)
    def _():
        _issue_row_scatter(dest_a_ref, xa_ref, xs_ref, sem)

    @pl.when(step >= steps_a)
    def _():
        _issue_row_scatter(dest_b_ref, xb_ref, xs_ref, sem)


def _scatter_rows(fill_start, dest_a, xa, dest_b, xb, m_rows, t, fill_rows):
    w = xa.shape[1]
    steps_a, steps_b = xa.shape[0] // t, xb.shape[0] // t
    in_a = lambda i: jnp.minimum(i, steps_a - 1)
    in_b = lambda i: jnp.maximum(i - steps_a, 0)
    smem = pltpu.SMEM
    return pl.pallas_call(
        functools.partial(_scatter_rows_kernel, steps_a=steps_a),
        grid=(steps_a + steps_b,),
        in_specs=[pl.BlockSpec(fill_start.shape, lambda i: (0,), memory_space=smem),
                  pl.BlockSpec((t,), lambda i: (in_a(i),), memory_space=smem),
                  pl.BlockSpec((t,), lambda i: (in_b(i),), memory_space=smem),
                  pl.BlockSpec((t, w), lambda i: (in_a(i), 0)),
                  pl.BlockSpec((t, w), lambda i: (in_b(i), 0))],
        out_specs=pl.BlockSpec(memory_space=pl.ANY),
        out_shape=jax.ShapeDtypeStruct((m_rows, w), xa.dtype),
        scratch_shapes=[pltpu.VMEM((fill_rows, w), xa.dtype), pltpu.SemaphoreType.DMA],
        compiler_params=_cparams(("arbitrary",)),
        name="moe_scatter_rows",
    )(fill_start, dest_a, dest_b, xa, xb)


def _gather_rows_kernel(dest_ref, ys_ref, y_ref, sem):
    t = y_ref.shape[0]

    def issue(g, carry):
        base = pl.multiple_of(g * SUBLANES, SUBLANES)
        tile_rows = y_ref.at[pl.ds(base, SUBLANES)]
        for u in range(SUBLANES):
            pltpu.make_async_copy(ys_ref.at[pl.ds(dest_ref[base + u], 1)], tile_rows.at[pl.ds(u, 1)],
                                  sem).start(priority=u % 2)
        return carry

    lax.fori_loop(0, t // SUBLANES, issue, 0)
    pltpu.make_async_copy(ys_ref.at[pl.ds(0, t)], y_ref, sem).wait()


def _gather_rows(dest, ys, t):
    n = dest.shape[0]
    w = ys.shape[1]
    return pl.pallas_call(
        _gather_rows_kernel,
        grid=(n // t,),
        in_specs=[pl.BlockSpec((t,), lambda i: (i,), memory_space=pltpu.SMEM),
                  pl.BlockSpec(memory_space=pl.ANY)],
        out_specs=pl.BlockSpec((t, w), lambda i: (i, 0)),
        out_shape=jax.ShapeDtypeStruct((n, w), ys.dtype),
        scratch_shapes=[pltpu.SemaphoreType.DMA],
        compiler_params=_cparams(("arbitrary",)),
        name="moe_gather_rows",
    )(dest, ys)


def _moe_kernel(e1_ref, e2_ref, tv_ref, xs_ref, g2_ref, wg1_ref, wu1_ref, wd1_ref, wg2_ref, wu2_ref, wd2_ref, y_ref):
    m = pl.program_id(0)

    @pl.when(tv_ref[m] > 0)
    def _():
        x1 = xs_ref[:, 0:D_MODEL]
        route = xs_ref[:, D_MODEL:]
        lane = lax.broadcasted_iota(jnp.int32, route.shape, 1)
        r = lax.rsqrt(jnp.mean(x1 * x1, axis=-1, keepdims=True) + EPS)
        n2 = ((x1 * r) * g2_ref[...]).astype(BF16)
        acc = x1
        for e_ref, wg_ref, wu_ref, wd_ref in ((e1_ref, wg1_ref, wu1_ref, wd1_ref), (e2_ref, wg2_ref, wu2_ref, wd2_ref)):
            ge = jnp.sum(jnp.where(lane == e_ref[m], route, 0.0), axis=1, keepdims=True)
            hg = jnp.dot(n2, wg_ref[...], preferred_element_type=F32)
            hu = jnp.dot(n2, wu_ref[...], preferred_element_type=F32)
            h = (hg * jax.nn.sigmoid(hg)) * hu * ge
            acc = acc + jnp.dot(h.astype(BF16), wd_ref[...], preferred_element_type=F32)
        y_ref[...] = acc

    @pl.when(tv_ref[m] == 0)
    def _():
        y_ref[...] = jnp.zeros_like(y_ref)


def _moe_sorted(tile_e1, tile_e2, tile_valid, xs, g2, wg, wu, wd, tm):
    m_pad = tile_valid.shape[0] * tm
    up = lambda e: pl.BlockSpec((None, D_MODEL, D_EXPERT), lambda m, e1, e2, tv: ((e1, e2)[e][m], 0, 0))
    down = lambda e: pl.BlockSpec((None, D_EXPERT, D_MODEL), lambda m, e1, e2, tv: ((e1, e2)[e][m], 0, 0))
    grid_spec = pltpu.PrefetchScalarGridSpec(
        num_scalar_prefetch=3,
        grid=(m_pad // tm,),
        in_specs=[
            pl.BlockSpec((tm, ROW_WIDTH), lambda m, e1, e2, tv: (m, 0)),
            pl.BlockSpec(g2.shape, lambda m, e1, e2, tv: (0, 0)),
            up(0), up(0), down(0), up(1), up(1), down(1),
        ],
        out_specs=pl.BlockSpec((tm, D_MODEL), lambda m, e1, e2, tv: (m, 0)),
    )
    return pl.pallas_call(
        _moe_kernel,
        grid_spec=grid_spec,
        out_shape=jax.ShapeDtypeStruct((m_pad, D_MODEL), F32),
        compiler_params=_cparams(("arbitrary",)),
        name="moe_experts",
    )(tile_e1, tile_e2, tile_valid, xs, g2, wg, wu, wd, wg, wu, wd)


def _moe_plan(plan_p, cnt_p, plan_s, cnt_s, tm):
    def ids(plan):
        blocks = plan.reshape(-1, SUBLANES, plan.shape[1]).astype(jnp.int32)
        return blocks[:, 0, :], blocks[:, 1, :]

    def lookup(table, cat):
        out = jnp.zeros_like(cat)
        for c in range(N_CATEGORIES):
            out = jnp.where(cat == c, table[c], out)
        return out

    (kp, rank_p), (ks, rank_s) = ids(plan_p), ids(plan_s)
    n_total = kp.size + ks.size
    n_tiles = n_total // tm + N_CATEGORIES
    cp = cnt_p[0, :N_CATEGORIES].astype(jnp.int32)
    cs = cnt_s[0, :N_CATEGORIES].astype(jnp.int32)
    total = cp + cs
    tiles = (total + tm - 1) // tm
    tile_end = jnp.cumsum(tiles)
    offset = (tile_end - tiles) * tm
    dest_p = (lookup(offset, kp) + rank_p).reshape(-1)
    dest_s = (lookup(offset + cp, ks) + rank_s).reshape(-1)
    tile_id = jnp.arange(n_tiles)
    tile_cat = jnp.minimum(jnp.sum(tile_id[:, None] >= tile_end[None, :], axis=1), N_CATEGORIES - 1)
    used = jnp.clip(offset[tile_cat] + total[tile_cat] - tile_id * tm, 0, tm)
    tile_valid = jnp.where(tile_id < tile_end[N_CATEGORIES - 1], used, 0).astype(jnp.int32)
    experts = jnp.asarray(CATEGORY_EXPERTS, jnp.int32)
    tile_e1, tile_e2 = experts[tile_cat, 0], experts[tile_cat, 1]
    m_rows = (n_tiles + 1) * tm
    ends = ((offset + total) // SUBLANES) * SUBLANES
    tail = ends[N_CATEGORIES - 1] + tm * jnp.arange(1, m_rows // tm - n_total // tm + 1)
    fill_start = jnp.concatenate([ends, tail]).astype(jnp.int32)
    return dest_p, dest_s, fill_start, tile_e1, tile_e2, tile_valid, m_rows


def kernel(x_prompt, x_sample, cache_k, cache_v, cache_logf, state_pool, norm1_g, w_in, b_f, q_norm_g, k_norm_g,
           pool_w, pool_scale, w_out, norm2_g, w_router_group, b_router_group, w_router_expert, b_router_expert,
           w_gate, w_up, w_down):
    depth = norm1_g.shape[0]
    assert depth == 1, "single-layer step"
    b_p, s_p, _ = x_prompt.shape
    assert b_p == 1, "prompt kernels assume one stream"
    b_s, t_s, _ = x_sample.shape
    p_len = cache_k.shape[2]
    l = 0

    w = w_in[l]
    a3 = 3 * ATTN_WIDTH
    w_all = jnp.concatenate(
        [w[:, :a3], w[:, a3 + N_HEADS:], jnp.pad(w[:, a3:a3 + N_HEADS], ((0, 0), (0, LANES - N_HEADS)))],
        axis=1).astype(BF16)
    bf_pad = jnp.pad(b_f[l], (0, LANES - N_HEADS)).reshape(1, LANES)
    g1 = norm1_g[l].reshape(1, D_MODEL)
    qg = jnp.tile(q_norm_g[l], N_HEADS).reshape(1, ATTN_WIDTH)
    kg = jnp.tile(k_norm_g[l], N_HEADS).reshape(1, ATTN_WIDTH)
    hr = lax.broadcasted_iota(jnp.int32, (MXU_DIM, MXU_DIM), 0) // HEAD_DIM
    hc = lax.broadcasted_iota(jnp.int32, (MXU_DIM, MXU_DIM), 1) // HEAD_DIM
    hm = jnp.where(hr == hc, 1.0 / HEAD_DIM, 0.0).astype(BF16)
    pw = pool_w[l].astype(BF16)
    ps = pool_scale[l].reshape(1, POOL_WIDTH)
    wo = w_out[l].astype(BF16)
    g2 = norm2_g[l].reshape(1, D_MODEL)
    w_r = jnp.pad(jnp.concatenate([w_router_expert[l], w_router_group[l]], axis=1),
                  ((0, 0), (0, LANES - N_EXPERTS - N_GROUPS)))
    wrh = w_r.astype(BF16)
    wrl = (w_r - wrh.astype(F32)).astype(BF16)
    br = jnp.pad(jnp.concatenate([b_router_expert[l], b_router_group[l]]),
                 (0, LANES - N_EXPERTS - N_GROUPS)).reshape(1, LANES)
    wg = w_gate[l].astype(BF16)
    wu = w_up[l].astype(BF16)
    wd = w_down[l].astype(BF16)

    tile = 512
    query_tile = 512
    key_tile = 256
    expert_tile = 256

    xp = x_prompt.reshape(s_p, D_MODEL)
    q, kf, _, vf, _, ka, vt, pin, logf, _, edge = _in_project(xp, tile, tile, True, key_tile, g1, w_all, bf_pad, qg, kg,
                                                              hm)
    blocks_per_tile = tile // key_tile
    edge = edge.reshape(s_p // tile, SUBLANES, LANES)[:, :2 * blocks_per_tile, :N_HEADS]
    edge = edge.reshape(s_p // key_tile, 2, N_HEADS)
    c_blocks = edge[:, 0, :]
    j0 = _first_key_blocks(c_blocks, edge[:, 1, :], q_norm_g[l], k_norm_g[l], query_tile, key_tile)
    a = _attention_prompt(j0, c_blocks, q, ka, vt, query_tile, key_tile)
    hist_map = lambda i: (jnp.maximum(i * (tile // HIST_ROWS) - 1, 0), 0)
    xg_p, plan_p, cnt_p = _mix(a, pin, pin, hist_map, xp, tile, 0, tile, True, pw, ps, wo, g2, wrh, wrl, br)
    k_prompt = kf.reshape(depth, b_p, s_p, N_HEADS, HEAD_DIM)
    v_prompt = vf.reshape(depth, b_p, s_p, N_HEADS, HEAD_DIM)
    logf_prompt = logf[:, :N_HEADS].reshape(depth, b_p, s_p, N_HEADS)
    pool_prompt = pin[s_p - POOL_PAD:].reshape(depth, b_p, POOL_PAD, POOL_WIDTH)

    n_s = b_s * t_s
    xs = x_sample.reshape(n_s, D_MODEL)
    q, kf, kb, vf, vb, _, _, pin, logf, d, _ = _in_project(xs, tile, t_s, False, key_tile, g1, w_all, bf_pad, qg, kg, hm)
    dt4 = d[:, :N_HEADS].reshape(b_s, t_s, N_PAIRS, 2).transpose(0, 2, 3, 1)
    clf = cache_logf[l].astype(F32).transpose(0, 2, 1).reshape(b_s * N_HEADS, p_len)
    r4 = _suffix_sums(clf).reshape(b_s, N_PAIRS, 2, p_len)
    ck = cache_k[l].reshape(b_s, p_len, ATTN_WIDTH).astype(BF16)
    cv = cache_v[l].reshape(b_s, p_len, ATTN_WIDTH).astype(BF16)
    a = _attention_sample(q, kb, vb, ck, cv, d, dt4, r4, t_s)
    hist_s = jnp.pad(state_pool[l], ((0, 0), (HIST_ROWS - POOL_PAD, 0), (0, 0))).reshape(b_s * HIST_ROWS, POOL_WIDTH)
    xg_s, plan_s, cnt_s = _mix(a, pin, hist_s, lambda i: (i, 0), xs, t_s, p_len, 0, False, pw, ps, wo, g2, wrh, wrl,
                               br)
    k_sample = kf.reshape(depth, b_s, t_s, N_HEADS, HEAD_DIM)
    v_sample = vf.reshape(depth, b_s, t_s, N_HEADS, HEAD_DIM)
    logf_sample = logf[:, :N_HEADS].reshape(depth, b_s, t_s, N_HEADS)
    pool_sample = pin.reshape(b_s, t_s, POOL_WIDTH)[:, t_s - POOL_PAD:].reshape(depth, b_s, POOL_PAD, POOL_WIDTH)

    dest_p, dest_s, fill_start, tile_e1, tile_e2, tile_valid, m_rows = _moe_plan(plan_p, cnt_p, plan_s, cnt_s,
                                                                              expert_tile)
    rows = _scatter_rows(fill_start, dest_p, xg_p, dest_s, xg_s, m_rows, tile, expert_tile)
    ys = _moe_sorted(tile_e1, tile_e2, tile_valid, rows, g2, wg, wu, wd, expert_tile)
    y_prompt = _gather_rows(dest_p, ys, tile).reshape(b_p, s_p, D_MODEL)
    y_sample = _gather_rows(dest_s, ys, tile).reshape(b_s, t_s, D_MODEL)

    return (y_prompt, y_sample, k_prompt, v_prompt, logf_prompt, pool_prompt,
            k_sample, v_sample, logf_sample, pool_sample)
```

```python
import functools

import jax
import jax.numpy as jnp
from jax import lax
from jax.experimental import pallas as pl
from jax.experimental.pallas import tpu as pltpu

F32 = jnp.float32
BF16 = jnp.bfloat16

D_MODEL = 1024
ATTN_WIDTH = 512
N_HEADS = 8
HEAD_DIM = 64
POOL_WIDTH = 512
POOL_WINDOWS = (2, 4, 8, 16)
POOL_GROUP_DIM = 128
POOL_PAD = 15
HIST_ROWS = 16
N_GROUPS = 4
EXPERTS_PER_GROUP = 4
N_EXPERTS = 16
D_EXPERT = 512
EPS = 1e-6
F32_EXP_UNDERFLOW = 104.0
BF16_SLACK = 1.02
SKIP_MARGIN = 1.0
LANES = 128
SUBLANES = 8
MXU_DIM = 256
PAIR = 2 * HEAD_DIM
N_PAIRS = N_HEADS // 2
ROWSUM_ROW = (HEAD_DIM, 0)
VMEM_LIMIT = 56 * 1024 * 1024
ROW_WIDTH = D_MODEL + LANES
PAIRS_PER_GROUP = 6
N_CATEGORIES = N_GROUPS * PAIRS_PER_GROUP
PAIR_ORDER = ((0, 1), (0, 2), (1, 2), (1, 3), (2, 3), (0, 3))
CATEGORY_EXPERTS = tuple((4 * g + a, 4 * g + b) for g in range(N_GROUPS) for a, b in PAIR_ORDER)
ROUTE_CATEGORY_LANE = N_EXPERTS
ROUTE_RANK_LANE = N_EXPERTS + 1


def _cparams(sem):
    return pltpu.CompilerParams(dimension_semantics=sem, vmem_limit_bytes=VMEM_LIMIT)


def _split3(x):
    hi = x.astype(BF16)
    r1 = x - hi.astype(F32)
    mid = r1.astype(BF16)
    lo = (r1 - mid.astype(F32)).astype(BF16)
    return hi, mid, lo


def _split2(x):
    hi = x.astype(BF16)
    lo = (x - hi.astype(F32)).astype(BF16)
    return hi, lo


def _inproj_kernel(x_ref, g1_ref, w_ref, bf_ref, qg_ref, kg_ref, hm_ref, tri_ref, place_ref,
                   q_ref, kf_ref, kb_ref, vf_ref, vb_ref, ka_ref, vt_ref, p_ref, logf_ref, c_ref, edge_ref, carry_ref, *,
                   carry_rows, key_blk):
    x = x_ref[...]
    r = lax.rsqrt(jnp.mean(x * x, axis=-1, keepdims=True) + EPS)
    n = ((x * r) * g1_ref[...]).astype(BF16)
    z = jnp.dot(n, w_ref[...], preferred_element_type=F32)

    hm = hm_ref[...]

    def head_norm(zz, g):
        hi, lo = _split2(zz * zz)
        half = hm.shape[0]
        ms = jnp.concatenate(
            [jnp.dot(hi[:, r:r + half], hm, preferred_element_type=F32)
             + jnp.dot(lo[:, r:r + half], hm, preferred_element_type=F32) for r in range(0, zz.shape[1], half)], axis=1)
        return (zz * lax.rsqrt(ms + EPS)) * g

    q = head_norm(z[:, 0:ATTN_WIDTH], qg_ref[...])
    q_ref[...] = (q * (HEAD_DIM ** -0.5)).astype(BF16)
    k = head_norm(z[:, ATTN_WIDTH:2 * ATTN_WIDTH], kg_ref[...])
    kf_ref[...] = k
    kb_ref[...] = k.astype(BF16)
    v = z[:, 2 * ATTN_WIDTH:3 * ATTN_WIDTH]
    vf_ref[...] = v
    vb_ref[...] = v.astype(BF16)
    p_ref[...] = z[:, 3 * ATTN_WIDTH:3 * ATTN_WIDTH + POOL_WIDTH]
    prow = lax.broadcasted_iota(jnp.int32, (PAIR, v.shape[0]), 0)
    for hp in range(N_PAIRS):
        vpair_t = v[:, hp * PAIR:(hp + 1) * PAIR].T
        for h in range(2):
            own = (prow < HEAD_DIM) if h == 0 else (prow >= HEAD_DIM)
            marker = jnp.where(prow == ROWSUM_ROW[h], 1.0, 0.0)
            vt_ref[(2 * hp + h) * LANES:(2 * hp + h + 1) * LANES, :] = jnp.where(own, vpair_t, marker).astype(BF16)

    f = z[:, 3 * ATTN_WIDTH + POOL_WIDTH:] + bf_ref[...]
    lane = lax.broadcasted_iota(jnp.int32, f.shape, 1)
    logf = -(jnp.maximum(-f, 0.0) + jnp.log1p(jnp.exp(-jnp.abs(f))))
    logf = jnp.where(lane < N_HEADS, logf, 0.0)
    logf_ref[...] = logf

    tri = tri_ref[...]
    blk = tri.shape[0]
    parts = _split3(logf)
    pieces = []
    for r in range(0, logf.shape[0], blk):
        piece = sum(jnp.dot(tri, part[r:r + blk, :], preferred_element_type=F32) for part in parts)
        if carry_rows and pieces:
            piece = piece + pieces[-1][blk - 1:blk, :]
        pieces.append(piece)
    c = jnp.concatenate(pieces, axis=0)
    if carry_rows:
        @pl.when(pl.program_id(0) == 0)
        def _():
            carry_ref[...] = jnp.zeros_like(carry_ref)
        c = c + carry_ref[0:1, :]
        carry_ref[...] = jnp.broadcast_to(c[c.shape[0] - 1:, :], carry_ref.shape)
    c_ref[...] = c
    edges = [c[r:r + 1, :] for b in range(c.shape[0] // key_blk) for r in (b * key_blk, (b + 1) * key_blk - 1)]
    edge_ref[...] = jnp.concatenate(edges + [jnp.zeros((SUBLANES - len(edges), LANES), F32)], axis=0)

    tm = c.shape[0]
    rel = jnp.concatenate([c[r:r + key_blk, :] - c[r:r + 1, :] for r in range(0, tm, key_blk)], axis=0)
    parts = jnp.concatenate(_split3(rel), axis=1)
    decay = jnp.dot(parts, place_ref[...], preferred_element_type=F32).astype(BF16)
    kb = k.astype(BF16)
    for hp in range(N_PAIRS):
        ka_ref[:, 2 * hp * LANES:(2 * hp + 1) * LANES] = kb[:, hp * PAIR:(hp + 1) * PAIR]
        ka_ref[:, (2 * hp + 1) * LANES:(2 * hp + 2) * LANES] = decay[:, hp * LANES:(hp + 1) * LANES]


def _decay_lane(h, part):
    return 3 * h + part


def _in_project(x2d, tm, seg, carry_rows, key_blk, g1, w_all, bf_pad, qg, kg, hm):
    n = x2d.shape[0]
    blk = min(tm, MXU_DIM)
    assert seg % blk == 0 or blk % seg == 0, "running-sum segments must nest with the triangular block"
    rows = lax.broadcasted_iota(jnp.int32, (blk, blk), 0)
    cols = lax.broadcasted_iota(jnp.int32, (blk, blk), 1)
    tri = ((cols <= rows) & (rows // seg == cols // seg)).astype(BF16)
    src = jnp.arange(3 * LANES)
    part, head = src // LANES, src % LANES
    dst = (head // 2) * LANES + _decay_lane(head % 2, part)
    place = ((jnp.arange(N_PAIRS * LANES)[None, :] == dst[:, None]) & (head < N_HEADS)[:, None]).astype(BF16)
    row_blk = lambda w: pl.BlockSpec((tm, w), lambda i: (i, 0))
    const = lambda a: pl.BlockSpec(a.shape, lambda i: (0,) * a.ndim)
    out_shape = (
        jax.ShapeDtypeStruct((n, ATTN_WIDTH), BF16),
        jax.ShapeDtypeStruct((n, ATTN_WIDTH), F32),
        jax.ShapeDtypeStruct((n, ATTN_WIDTH), BF16),
        jax.ShapeDtypeStruct((n, ATTN_WIDTH), F32),
        jax.ShapeDtypeStruct((n, ATTN_WIDTH), BF16),
        jax.ShapeDtypeStruct((n, N_PAIRS * 2 * LANES), BF16),
        jax.ShapeDtypeStruct((N_HEADS * LANES, n), BF16),
        jax.ShapeDtypeStruct((n, POOL_WIDTH), F32),
        jax.ShapeDtypeStruct((n, LANES), F32),
        jax.ShapeDtypeStruct((n, LANES), F32),
        jax.ShapeDtypeStruct((n // tm * SUBLANES, LANES), F32),
    )
    return pl.pallas_call(
        functools.partial(_inproj_kernel, carry_rows=carry_rows, key_blk=key_blk),
        grid=(n // tm,),
        in_specs=[row_blk(D_MODEL), const(g1), const(w_all), const(bf_pad), const(qg), const(kg), const(hm),
                  const(tri), const(place)],
        out_specs=[row_blk(ATTN_WIDTH)] * 5 + [row_blk(N_PAIRS * 2 * LANES),
                                                pl.BlockSpec((N_HEADS * LANES, tm), lambda i: (0, i)),
                                                row_blk(POOL_WIDTH), row_blk(LANES), row_blk(LANES),
                                                pl.BlockSpec((SUBLANES, LANES), lambda i: (i, 0))],
        out_shape=out_shape,
        scratch_shapes=[pltpu.VMEM((8, LANES), F32)],
        compiler_params=_cparams(("arbitrary",)),
        name="in_project",
    )(x2d, g1, w_all, bf_pad, qg, kg, hm, tri, place)


def _attn_prompt_kernel(j0_ref, cb_ref, q_ref, k_ref, vt_ref, o_ref, acc_ref, m_ref, *, tq, tk):
    hp = pl.program_id(0)
    i = pl.program_id(1)
    nk = tq // tk
    lane = lax.broadcasted_iota(jnp.int32, (1, PAIR), 1)
    first = lane < HEAD_DIM
    q = q_ref[...]
    zero = jnp.zeros_like(q)
    qw = []
    for h in range(2):
        minus = jnp.zeros((1, LANES), F32)
        for part in range(3):
            minus = jnp.where(lane == _decay_lane(h, part), -1.0, minus)
        qw.append(jnp.concatenate([jnp.where(first == (h == 0), q, zero),
                                   jnp.broadcast_to(minus, (tq, LANES)).astype(BF16)], axis=1))
    q_start = i * tq

    acc_ref[...] = jnp.zeros_like(acc_ref)
    m_ref[...] = jnp.full_like(m_ref, -jnp.inf)

    last = (i + 1) * nk - 1

    def scores(h, j, masked=True):
        start = pl.multiple_of(j * tk, tk)
        k = k_ref[pl.ds(start, tk), :]
        s = lax.dot_general(k, qw[h], (((1,), (1,)), ((), ())), preferred_element_type=F32)
        if masked:
            key = lax.broadcasted_iota(jnp.int32, s.shape, 0)
            qry = lax.broadcasted_iota(jnp.int32, s.shape, 1)
            s = jnp.where(key - qry <= q_start - j * tk, s, -jnp.inf)
        return s, jnp.max(s, axis=0, keepdims=True)

    def update(h, j, s, colmax):
        shift = cb_ref[j, 2 * hp + h] - cb_ref[i * nk, 2 * hp + h]
        start = pl.multiple_of(j * tk, tk)
        vt = vt_ref[h * LANES:(h + 1) * LANES, pl.ds(start, tk)]
        m_prev = m_ref[h]
        m_new = jnp.maximum(m_prev, colmax - shift)
        alpha = jnp.exp(m_prev - m_new)
        p = jnp.exp(s - (m_new + shift))
        m_ref[h] = m_new
        acc_ref[h] = acc_ref[h] * alpha + jnp.dot(vt, p.astype(BF16), preferred_element_type=F32)

    j_first = [j0_ref[i, 2 * hp + h] for h in range(2)]
    j_both = jnp.maximum(j_first[0], j_first[1])
    start = [scores(h, j_first[h]) for h in range(2)]
    ahead = []
    for h in range(2):
        def body_one(j, carry, h=h):
            nxt = scores(h, j + 1)
            update(h, j, *carry)
            return nxt

        ahead.append(lax.fori_loop(j_first[h], j_both, body_one, start[h]))

    def body_both(j, carry, masked):
        nxt = (scores(0, j + 1, masked), scores(1, j + 1, masked))
        update(0, j, *carry[0])
        update(1, j, *carry[1])
        return nxt

    j_mask = jnp.maximum(j_both, i * nk - 1)
    pairs = (j_mask - j_both) // 2

    def body_two(t, carry):
        j = j_both + 2 * t
        return body_both(j + 1, body_both(j, carry, False), False)

    carry = lax.fori_loop(0, pairs, body_two, (ahead[0], ahead[1]))
    carry = lax.fori_loop(j_both + 2 * pairs, last, functools.partial(body_both, masked=True), carry)
    update(0, last, *carry[0])
    update(1, last, *carry[1])

    out = [acc_ref[h] / acc_ref[h][ROWSUM_ROW[h]:ROWSUM_ROW[h] + 1, :] for h in range(2)]
    row = lax.broadcasted_iota(jnp.int32, (PAIR, 1), 0)
    o_ref[...] = jnp.where(row < HEAD_DIM, out[0], out[1]).T.astype(o_ref.dtype)


def _attention_prompt(j0, c_blocks, q, ka, vt, tq, tk):
    n = q.shape[0]
    grid_spec = pltpu.PrefetchScalarGridSpec(
        num_scalar_prefetch=2,
        grid=(N_PAIRS, n // tq),
        in_specs=[
            pl.BlockSpec((tq, PAIR), lambda hp, i, j0, cb: (i, hp)),
            pl.BlockSpec((n, 2 * LANES), lambda hp, i, j0, cb: (0, hp)),
            pl.BlockSpec((2 * LANES, n), lambda hp, i, j0, cb: (hp, 0)),
        ],
        out_specs=pl.BlockSpec((tq, PAIR), lambda hp, i, j0, cb: (i, hp)),
        scratch_shapes=[pltpu.VMEM((2, LANES, tq), F32), pltpu.VMEM((2, 1, tq), F32)],
    )
    return pl.pallas_call(
        functools.partial(_attn_prompt_kernel, tq=tq, tk=tk),
        grid_spec=grid_spec,
        out_shape=jax.ShapeDtypeStruct((n, ATTN_WIDTH), BF16),
        compiler_params=_cparams(("arbitrary", "arbitrary")),
        name="attention_prompt",
    )(j0, c_blocks, q, ka, vt)


def _first_key_blocks(c_first, c_last, q_gain, k_gain, tq, tk):
    n_k = c_first.shape[0]
    n_q = n_k * tk // tq
    qk_bound = BF16_SLACK * HEAD_DIM ** 0.5 * jnp.max(jnp.abs(q_gain)) * jnp.max(jnp.abs(k_gain))
    threshold = F32_EXP_UNDERFLOW + 2.0 * qk_bound + SKIP_MARGIN
    c_start = c_first[0::tq // tk]
    gap = c_start[:, None, :] - c_last[None, :, :]
    earlier = (jnp.arange(n_k)[None, :, None] + 1) * tk <= jnp.arange(n_q)[:, None, None] * tq
    return jnp.sum((gap < -threshold) & earlier, axis=1).astype(jnp.int32)


def _suffix_kernel(x_ref, o_ref):
    x = x_ref[...]
    p_len = x.shape[1]
    rows = lax.broadcasted_iota(jnp.int32, (p_len, p_len), 0)
    cols = lax.broadcasted_iota(jnp.int32, (p_len, p_len), 1)
    u = (rows > cols).astype(BF16)
    hi, mid, lo = _split3(x)
    o_ref[...] = (jnp.dot(hi, u, preferred_element_type=F32) + jnp.dot(mid, u, preferred_element_type=F32)
                  + jnp.dot(lo, u, preferred_element_type=F32))


def _suffix_sums(x):
    return pl.pallas_call(
        _suffix_kernel,
        out_shape=jax.ShapeDtypeStruct(x.shape, F32),
        compiler_params=pltpu.CompilerParams(vmem_limit_bytes=VMEM_LIMIT),
        name="cache_suffix_sums",
    )(x)


def _attn_sample_kernel(q_ref, kn_ref, vn_ref, ck_ref, cv_ref, d_ref, dt_ref, r_ref, o_ref):
    lane = lax.broadcasted_iota(jnp.int32, (1, PAIR), 1)
    first = lane < HEAD_DIM
    dblk = d_ref[...]
    dlane = lax.broadcasted_iota(jnp.int32, dblk.shape, 1)
    nt = (((1,), (1,)), ((), ()))
    for hp in range(N_PAIRS):
        cols = slice(hp * PAIR, (hp + 1) * PAIR)
        q = q_ref[:, cols]
        zero = jnp.zeros_like(q)
        qh = (jnp.where(first, q, zero), jnp.where(first, zero, q))
        kn = kn_ref[:, cols]
        vn = vn_ref[:, cols]
        ck = ck_ref[:, cols]
        cv = cv_ref[:, cols]
        outs, ls = [], []
        for h in range(2):
            dq = jnp.sum(jnp.where(dlane == 2 * hp + h, dblk, 0.0), axis=1, keepdims=True)
            s1 = lax.dot_general(qh[h], ck, nt, preferred_element_type=F32) + dq + r_ref[hp, h:h + 1, :]
            s2 = lax.dot_general(qh[h], kn, nt, preferred_element_type=F32) + dq - dt_ref[hp, h:h + 1, :]
            row = lax.broadcasted_iota(jnp.int32, s2.shape, 0)
            col = lax.broadcasted_iota(jnp.int32, s2.shape, 1)
            s2 = jnp.where(col <= row, s2, -jnp.inf)
            m = jnp.maximum(jnp.max(s1, axis=1, keepdims=True), jnp.max(s2, axis=1, keepdims=True))
            p1 = jnp.exp(s1 - m)
            p2 = jnp.exp(s2 - m)
            ls.append(jnp.sum(p1, axis=1, keepdims=True) + jnp.sum(p2, axis=1, keepdims=True))
            outs.append(jnp.dot(p1.astype(BF16), cv, preferred_element_type=F32)
                        + jnp.dot(p2.astype(BF16), vn, preferred_element_type=F32))
        o_ref[:, cols] = (jnp.where(first, outs[0], outs[1]) / jnp.where(first, ls[0], ls[1])).astype(o_ref.dtype)


def _attention_sample(q, kb, vb, cache_k, cache_v, d, dt4, r4, t):
    nb, p_len = cache_k.shape[0], cache_k.shape[1]
    row_blk = pl.BlockSpec((t, ATTN_WIDTH), lambda b: (b, 0))
    cache_blk = pl.BlockSpec((None, p_len, ATTN_WIDTH), lambda b: (b, 0, 0))
    return pl.pallas_call(
        _attn_sample_kernel,
        grid=(nb,),
        in_specs=[row_blk, row_blk, row_blk, cache_blk, cache_blk,
                  pl.BlockSpec((t, LANES), lambda b: (b, 0)),
                  pl.BlockSpec((None, N_PAIRS, 2, t), lambda b: (b, 0, 0, 0)),
                  pl.BlockSpec((None, N_PAIRS, 2, p_len), lambda b: (b, 0, 0, 0))],
        out_specs=row_blk,
        out_shape=jax.ShapeDtypeStruct((nb * t, ATTN_WIDTH), BF16),
        compiler_params=_cparams(("arbitrary",)),
        name="attention_sample",
    )(q, kb, vb, cache_k, cache_v, d, dt4, r4)


def _mix_kernel(a_ref, p_ref, hist_ref, x_ref, pw_ref, ps_ref, wo_ref, g2_ref, wrh_ref, wrl_ref, br_ref, tri_ref,
                xg_ref, plan_ref, cnt_ref, *, pos0, pos_stride, zero_first):
    i = pl.program_id(0)
    t = p_ref.shape[0]
    p = p_ref[...]
    hist = hist_ref[...]
    if zero_first:
        hist = jnp.where(i == 0, 0.0, hist)
    xh = jnp.concatenate([hist, p], axis=0)
    pos = pos0 + i * pos_stride + lax.broadcasted_iota(jnp.int32, (t, 1), 0)
    ys = []
    for g, w in enumerate(POOL_WINDOWS):
        col = xh[:, g * POOL_GROUP_DIM:(g + 1) * POOL_GROUP_DIM]
        acc = col
        span = 1
        while span < w:
            acc = acc + pltpu.roll(acc, span, axis=0)
            span *= 2
        cnt = jnp.minimum(pos + 1, w).astype(F32)
        dgrp = acc[HIST_ROWS:, :] / cnt - col[HIST_ROWS:, :]
        ys.append(jnp.dot(dgrp.astype(BF16), pw_ref[g], preferred_element_type=F32))
    pm = (jnp.concatenate(ys, axis=1) * ps_ref[...]).astype(BF16)
    mix = (jnp.dot(a_ref[...], wo_ref[0:ATTN_WIDTH, :], preferred_element_type=F32)
           + jnp.dot(pm, wo_ref[ATTN_WIDTH:, :], preferred_element_type=F32))
    x1 = x_ref[...] + mix
    xg_ref[:, 0:D_MODEL] = x1

    r = lax.rsqrt(jnp.mean(x1 * x1, axis=-1, keepdims=True) + EPS)
    n2 = (x1 * r) * g2_ref[...]

    nh, nl = _split2(n2)
    wrh = wrh_ref[...]
    logits = (jnp.dot(nh, wrh, preferred_element_type=F32) + jnp.dot(nl, wrh, preferred_element_type=F32)
              + jnp.dot(nh, wrl_ref[...], preferred_element_type=F32)) + br_ref[...]
    lane = lax.broadcasted_iota(jnp.int32, logits.shape, 1).astype(F32)
    neg = -jnp.inf
    big = jnp.float32(1 << 20)
    is_g = (lane >= N_EXPERTS) & (lane < N_EXPERTS + N_GROUPS)
    lg = jnp.where(is_g, logits, neg)
    gmax = jnp.max(lg, axis=1, keepdims=True)
    g_lane = jnp.min(jnp.where(lg == gmax, lane, big), axis=1, keepdims=True)
    top_pg = 1.0 / jnp.sum(jnp.exp(lg - gmax), axis=1, keepdims=True)
    g_idx = g_lane - N_EXPERTS
    sel = (lane >= g_idx * EXPERTS_PER_GROUP) & (lane < (g_idx + 1) * EXPERTS_PER_GROUP)
    le = jnp.where(sel, logits, neg)
    m1 = jnp.max(le, axis=1, keepdims=True)
    i1 = jnp.min(jnp.where(le == m1, lane, big), axis=1, keepdims=True)
    le2 = jnp.where(lane == i1, neg, le)
    m2 = jnp.max(le2, axis=1, keepdims=True)
    i2 = jnp.min(jnp.where(le2 == m2, lane, big), axis=1, keepdims=True)
    e2 = jnp.exp(m2 - m1)
    w1 = top_pg / (1.0 + e2)
    w2 = top_pg * e2 / (1.0 + e2)
    gate = jnp.where(lane == i1, w1, 0.0) + jnp.where(lane == i2, w2, 0.0)

    @pl.when(i == 0)
    def _():
        cnt_ref[...] = jnp.zeros_like(cnt_ref)
    e_lo = jnp.minimum(i1, i2) - g_idx * EXPERTS_PER_GROUP
    e_hi = jnp.maximum(i1, i2) - g_idx * EXPERTS_PER_GROUP
    pair = jnp.zeros_like(e_lo)
    for idx, (a, b) in enumerate(PAIR_ORDER):
        pair = jnp.where((e_lo == a) & (e_hi == b), float(idx), pair)
    category = g_idx * PAIRS_PER_GROUP + pair
    member = lane == category
    tri = tri_ref[...]
    blk = tri.shape[0]
    seen = cnt_ref[0:1, :]
    pieces = []
    for r in range(0, t, blk):
        mem = member[r:r + blk, :].astype(BF16)
        pieces.append(jnp.dot(tri, mem, preferred_element_type=F32) + seen)
        seen = seen + jnp.sum(mem.astype(F32), axis=0, keepdims=True)
    before = jnp.concatenate(pieces, axis=0)
    rank = jnp.sum(jnp.where(member, before, 0.0), axis=1, keepdims=True)
    cnt_ref[...] = jnp.broadcast_to(seen, cnt_ref.shape)
    route = jnp.where(lane == ROUTE_CATEGORY_LANE, category, jnp.where(lane == ROUTE_RANK_LANE, rank, gate))
    xg_ref[:, D_MODEL:] = route
    srow = lax.broadcasted_iota(jnp.int32, (SUBLANES, LANES), 0)
    slane = lax.broadcasted_iota(jnp.int32, (SUBLANES, LANES), 1)
    pick = (((srow == 0) & (slane == ROUTE_CATEGORY_LANE)) | ((srow == 1) & (slane == ROUTE_RANK_LANE))).astype(BF16)
    nt = (((1,), (1,)), ((), ()))
    ids = jnp.where(lane >= N_EXPERTS, route, 0.0)
    hi, lo = _split2(ids)
    plan_ref[...] = (lax.dot_general(pick, hi, nt, preferred_element_type=F32)
                     + lax.dot_general(pick, lo, nt, preferred_element_type=F32))


def _mix(a, p, hist_src, hist_map, x2d, t, pos0, pos_stride, zero_first, pw, ps, wo, g2, wrh, wrl, br):
    n = x2d.shape[0]
    blk = min(t, MXU_DIM)
    rows = lax.broadcasted_iota(jnp.int32, (blk, blk), 0)
    cols = lax.broadcasted_iota(jnp.int32, (blk, blk), 1)
    tri = (cols < rows).astype(BF16)
    row_blk = lambda w: pl.BlockSpec((t, w), lambda i: (i, 0))
    const = lambda arr: pl.BlockSpec(arr.shape, lambda i: (0,) * arr.ndim)
    return pl.pallas_call(
        functools.partial(_mix_kernel, pos0=pos0, pos_stride=pos_stride, zero_first=zero_first),
        grid=(n // t,),
        in_specs=[row_blk(ATTN_WIDTH), row_blk(POOL_WIDTH), pl.BlockSpec((HIST_ROWS, POOL_WIDTH), hist_map),
                  row_blk(D_MODEL), const(pw), const(ps), const(wo), const(g2), const(wrh), const(wrl), const(br),
                  const(tri)],
        out_specs=[row_blk(ROW_WIDTH), pl.BlockSpec((SUBLANES, t), lambda i: (i, 0)),
                   pl.BlockSpec((SUBLANES, LANES), lambda i: (0, 0))],
        out_shape=(jax.ShapeDtypeStruct((n, ROW_WIDTH), F32), jax.ShapeDtypeStruct((n // t * SUBLANES, t), F32),
                   jax.ShapeDtypeStruct((SUBLANES, LANES), F32)),
        compiler_params=_cparams(("arbitrary",)),
        name="pool_mix_route",
    )(a, p, hist_src, x2d, pw, ps, wo, g2, wrh, wrl, br, tri)


def _issue_row_scatter(dest_ref, x_ref, xs_ref, sem):
    t = x_ref.shape[0]

    def issue(g, carry):
        base = pl.multiple_of(g * SUBLANES, SUBLANES)
        tile_rows = x_ref.at[pl.ds(base, SUBLANES)]
        for u in range(SUBLANES):
            pltpu.make_async_copy(tile_rows.at[pl.ds(u, 1)], xs_ref.at[pl.ds(dest_ref[base + u], 1)],
                                  sem).start(priority=u % 2)
        return carry

    lax.fori_loop(0, t // SUBLANES, issue, 0)
    pltpu.make_async_copy(x_ref, xs_ref.at[pl.ds(0, t)], sem).wait()


def _scatter_rows_kernel(fill_ref, dest_a_ref, dest_b_ref, xa_ref, xb_ref, xs_ref, zero_ref, sem, *, steps_a):
    step = pl.program_id(0)

    @pl.when(step == 0)
    def _():
        zero_ref[...] = jnp.zeros_like(zero_ref)
        fill_rows = zero_ref.shape[0]
        last_start = xs_ref.shape[0] - fill_rows
        for g in range(fill_ref.shape[0]):
            @pl.when(fill_ref[g] < xs_ref.shape[0])
            def _(g=g):
                start = pl.multiple_of(jnp.minimum(fill_ref[g], last_start), SUBLANES)
                fill = pltpu.make_async_copy(zero_ref, xs_ref.at[pl.ds(start, fill_rows)], sem)
                fill.start()
                fill.wait()

    @pl.when(step < steps_a)
    def _():
        _issue_row_scatter(dest_a_ref, xa_ref, xs_ref, sem)

    @pl.when(step >= steps_a)
    def _():
        _issue_row_scatter(dest_b_ref, xb_ref, xs_ref, sem)


def _scatter_rows(fill_start, dest_a, xa, dest_b, xb, m_rows, t, fill_rows):
    w = xa.shape[1]
    steps_a, steps_b = xa.shape[0] // t, xb.shape[0] // t
    in_a = lambda i: jnp.minimum(i, steps_a - 1)
    in_b = lambda i: jnp.maximum(i - steps_a, 0)
    smem = pltpu.SMEM
    return pl.pallas_call(
        functools.partial(_scatter_rows_kernel, steps_a=steps_a),
        grid=(steps_a + steps_b,),
        in_specs=[pl.BlockSpec(fill_start.shape, lambda i: (0,), memory_space=smem),
                  pl.BlockSpec((t,), lambda i: (in_a(i),), memory_space=smem),
                  pl.BlockSpec((t,), lambda i: (in_b(i),), memory_space=smem),
                  pl.BlockSpec((t, w), lambda i: (in_a(i), 0)),
                  pl.BlockSpec((t, w), lambda i: (in_b(i), 0))],
        out_specs=pl.BlockSpec(memory_space=pl.ANY),
        out_shape=jax.ShapeDtypeStruct((m_rows, w), xa.dtype),
        scratch_shapes=[pltpu.VMEM((fill_rows, w), xa.dtype), pltpu.SemaphoreType.DMA],
        compiler_params=_cparams(("arbitrary",)),
        name="moe_scatter_rows",
    )(fill_start, dest_a, dest_b, xa, xb)


def _gather_rows_kernel(dest_ref, ys_ref, y_ref, sem):
    t = y_ref.shape[0]

    def issue(g, carry):
        base = pl.multiple_of(g * SUBLANES, SUBLANES)
        tile_rows = y_ref.at[pl.ds(base, SUBLANES)]
        for u in range(SUBLANES):
            pltpu.make_async_copy(ys_ref.at[pl.ds(dest_ref[base + u], 1)], tile_rows.at[pl.ds(u, 1)],
                                  sem).start(priority=u % 2)
        return carry

    lax.fori_loop(0, t // SUBLANES, issue, 0)
    pltpu.make_async_copy(ys_ref.at[pl.ds(0, t)], y_ref, sem).wait()


def _gather_rows(dest, ys, t):
    n = dest.shape[0]
    w = ys.shape[1]
    return pl.pallas_call(
        _gather_rows_kernel,
        grid=(n // t,),
        in_specs=[pl.BlockSpec((t,), lambda i: (i,), memory_space=pltpu.SMEM),
                  pl.BlockSpec(memory_space=pl.ANY)],
        out_specs=pl.BlockSpec((t, w), lambda i: (i, 0)),
        out_shape=jax.ShapeDtypeStruct((n, w), ys.dtype),
        scratch_shapes=[pltpu.SemaphoreType.DMA],
        compiler_params=_cparams(("arbitrary",)),
        name="moe_gather_rows",
    )(dest, ys)


def _moe_kernel(e1_ref, e2_ref, tv_ref, xs_ref, g2_ref, wg1_ref, wu1_ref, wd1_ref, wg2_ref, wu2_ref, wd2_ref, y_ref):
    m = pl.program_id(0)

    @pl.when(tv_ref[m] > 0)
    def _():
        x1 = xs_ref[:, 0:D_MODEL]
        route = xs_ref[:, D_MODEL:]
        lane = lax.broadcasted_iota(jnp.int32, route.shape, 1)
        r = lax.rsqrt(jnp.mean(x1 * x1, axis=-1, keepdims=True) + EPS)
        n2 = ((x1 * r) * g2_ref[...]).astype(BF16)
        acc = x1
        for e_ref, wg_ref, wu_ref, wd_ref in ((e1_ref, wg1_ref, wu1_ref, wd1_ref), (e2_ref, wg2_ref, wu2_ref, wd2_ref)):
            ge = jnp.sum(jnp.where(lane == e_ref[m], route, 0.0), axis=1, keepdims=True)
            hg = jnp.dot(n2, wg_ref[...], preferred_element_type=F32)
            hu = jnp.dot(n2, wu_ref[...], preferred_element_type=F32)
            h = (hg * jax.nn.sigmoid(hg)) * hu * ge
            acc = acc + jnp.dot(h.astype(BF16), wd_ref[...], preferred_element_type=F32)
        y_ref[...] = acc

    @pl.when(tv_ref[m] == 0)
    def _():
        y_ref[...] = jnp.zeros_like(y_ref)


def _moe_sorted(tile_e1, tile_e2, tile_valid, xs, g2, wg, wu, wd, tm):
    m_pad = tile_valid.shape[0] * tm
    up = lambda e: pl.BlockSpec((None, D_MODEL, D_EXPERT), lambda m, e1, e2, tv: ((e1, e2)[e][m], 0, 0))
    down = lambda e: pl.BlockSpec((None, D_EXPERT, D_MODEL), lambda m, e1, e2, tv: ((e1, e2)[e][m], 0, 0))
    grid_spec = pltpu.PrefetchScalarGridSpec(
        num_scalar_prefetch=3,
        grid=(m_pad // tm,),
        in_specs=[
            pl.BlockSpec((tm, ROW_WIDTH), lambda m, e1, e2, tv: (m, 0)),
            pl.BlockSpec(g2.shape, lambda m, e1, e2, tv: (0, 0)),
            up(0), up(0), down(0), up(1), up(1), down(1),
        ],
        out_specs=pl.BlockSpec((tm, D_MODEL), lambda m, e1, e2, tv: (m, 0)),
    )
    return pl.pallas_call(
        _moe_kernel,
        grid_spec=grid_spec,
        out_shape=jax.ShapeDtypeStruct((m_pad, D_MODEL), F32),
        compiler_params=_cparams(("arbitrary",)),
        name="moe_experts",
    )(tile_e1, tile_e2, tile_valid, xs, g2, wg, wu, wd, wg, wu, wd)


def _moe_plan(plan_p, cnt_p, plan_s, cnt_s, tm):
    def ids(plan):
        blocks = plan.reshape(-1, SUBLANES, plan.shape[1]).astype(jnp.int32)
        return blocks[:, 0, :], blocks[:, 1, :]

    def lookup(table, cat):
        out = jnp.zeros_like(cat)
        for c in range(N_CATEGORIES):
            out = jnp.where(cat == c, table[c], out)
        return out

    (kp, rank_p), (ks, rank_s) = ids(plan_p), ids(plan_s)
    n_total = kp.size + ks.size
    n_tiles = n_total // tm + N_CATEGORIES
    cp = cnt_p[0, :N_CATEGORIES].astype(jnp.int32)
    cs = cnt_s[0, :N_CATEGORIES].astype(jnp.int32)
    total = cp + cs
    tiles = (total + tm - 1) // tm
    tile_end = jnp.cumsum(tiles)
    offset = (tile_end - tiles) * tm
    dest_p = (lookup(offset, kp) + rank_p).reshape(-1)
    dest_s = (lookup(offset + cp, ks) + rank_s).reshape(-1)
    tile_id = jnp.arange(n_tiles)
    tile_cat = jnp.minimum(jnp.sum(tile_id[:, None] >= tile_end[None, :], axis=1), N_CATEGORIES - 1)
    used = jnp.clip(offset[tile_cat] + total[tile_cat] - tile_id * tm, 0, tm)
    tile_valid = jnp.where(tile_id < tile_end[N_CATEGORIES - 1], used, 0).astype(jnp.int32)
    experts = jnp.asarray(CATEGORY_EXPERTS, jnp.int32)
    tile_e1, tile_e2 = experts[tile_cat, 0], experts[tile_cat, 1]
    m_rows = (n_tiles + 1) * tm
    ends = ((offset + total) // SUBLANES) * SUBLANES
    tail = ends[N_CATEGORIES - 1] + tm * jnp.arange(1, m_rows // tm - n_total // tm + 1)
    fill_start = jnp.concatenate([ends, tail]).astype(jnp.int32)
    return dest_p, dest_s, fill_start, tile_e1, tile_e2, tile_valid, m_rows


def kernel(x_prompt, x_sample, cache_k, cache_v, cache_logf, state_pool, norm1_g, w_in, b_f, q_norm_g, k_norm_g,
           pool_w, pool_scale, w_out, norm2_g, w_router_group, b_router_group, w_router_expert, b_router_expert,
           w_gate, w_up, w_down):
    depth = norm1_g.shape[0]
    assert depth == 1, "single-layer step"
    b_p, s_p, _ = x_prompt.shape
    assert b_p == 1, "prompt kernels assume one stream"
    b_s, t_s, _ = x_sample.shape
    p_len = cache_k.shape[2]
    l = 0

    w = w_in[l]
    a3 = 3 * ATTN_WIDTH
    w_all = jnp.concatenate(
        [w[:, :a3], w[:, a3 + N_HEADS:], jnp.pad(w[:, a3:a3 + N_HEADS], ((0, 0), (0, LANES - N_HEADS)))],
        axis=1).astype(BF16)
    bf_pad = jnp.pad(b_f[l], (0, LANES - N_HEADS)).reshape(1, LANES)
    g1 = norm1_g[l].reshape(1, D_MODEL)
    qg = jnp.tile(q_norm_g[l], N_HEADS).reshape(1, ATTN_WIDTH)
    kg = jnp.tile(k_norm_g[l], N_HEADS).reshape(1, ATTN_WIDTH)
    hr = lax.broadcasted_iota(jnp.int32, (MXU_DIM, MXU_DIM), 0) // HEAD_DIM
    hc = lax.broadcasted_iota(jnp.int32, (MXU_DIM, MXU_DIM), 1) // HEAD_DIM
    hm = jnp.where(hr == hc, 1.0 / HEAD_DIM, 0.0).astype(BF16)
    pw = pool_w[l].astype(BF16)
    ps = pool_scale[l].reshape(1, POOL_WIDTH)
    wo = w_out[l].astype(BF16)
    g2 = norm2_g[l].reshape(1, D_MODEL)
    w_r = jnp.pad(jnp.concatenate([w_router_expert[l], w_router_group[l]], axis=1),
                  ((0, 0), (0, LANES - N_EXPERTS - N_GROUPS)))
    wrh = w_r.astype(BF16)
    wrl = (w_r - wrh.astype(F32)).astype(BF16)
    br = jnp.pad(jnp.concatenate([b_router_expert[l], b_router_group[l]]),
                 (0, LANES - N_EXPERTS - N_GROUPS)).reshape(1, LANES)
    wg = w_gate[l].astype(BF16)
    wu = w_up[l].astype(BF16)
    wd = w_down[l].astype(BF16)

    tile = 512
    query_tile = 512
    key_tile = 256
    expert_tile = 256

    xp = x_prompt.reshape(s_p, D_MODEL)
    q, kf, _, vf, _, ka, vt, pin, logf, _, edge = _in_project(xp, tile, tile, True, key_tile, g1, w_all, bf_pad, qg, kg,
                                                              hm)
    blocks_per_tile = tile // key_tile
    edge = edge.reshape(s_p // tile, SUBLANES, LANES)[:, :2 * blocks_per_tile, :N_HEADS]
    edge = edge.reshape(s_p // key_tile, 2, N_HEADS)
    c_blocks = edge[:, 0, :]
    j0 = _first_key_blocks(c_blocks, edge[:, 1, :], q_norm_g[l], k_norm_g[l], query_tile, key_tile)
    a = _attention_prompt(j0, c_blocks, q, ka, vt, query_tile, key_tile)
    hist_map = lambda i: (jnp.maximum(i * (tile // HIST_ROWS) - 1, 0), 0)
    xg_p, plan_p, cnt_p = _mix(a, pin, pin, hist_map, xp, tile, 0, tile, True, pw, ps, wo, g2, wrh, wrl, br)
    k_prompt = kf.reshape(depth, b_p, s_p, N_HEADS, HEAD_DIM)
    v_prompt = vf.reshape(depth, b_p, s_p, N_HEADS, HEAD_DIM)
    logf_prompt = logf[:, :N_HEADS].reshape(depth, b_p, s_p, N_HEADS)
    pool_prompt = pin[s_p - POOL_PAD:].reshape(depth, b_p, POOL_PAD, POOL_WIDTH)

    n_s = b_s * t_s
    xs = x_sample.reshape(n_s, D_MODEL)
    q, kf, kb, vf, vb, _, _, pin, logf, d, _ = _in_project(xs, tile, t_s, False, key_tile, g1, w_all, bf_pad, qg, kg, hm)
    dt4 = d[:, :N_HEADS].reshape(b_s, t_s, N_PAIRS, 2).transpose(0, 2, 3, 1)
    clf = cache_logf[l].astype(F32).transpose(0, 2, 1).reshape(b_s * N_HEADS, p_len)
    r4 = _suffix_sums(clf).reshape(b_s, N_PAIRS, 2, p_len)
    ck = cache_k[l].reshape(b_s, p_len, ATTN_WIDTH).astype(BF16)
    cv = cache_v[l].reshape(b_s, p_len, ATTN_WIDTH).astype(BF16)
    a = _attention_sample(q, kb, vb, ck, cv, d, dt4, r4, t_s)
    hist_s = jnp.pad(state_pool[l], ((0, 0), (HIST_ROWS - POOL_PAD, 0), (0, 0))).reshape(b_s * HIST_ROWS, POOL_WIDTH)
    xg_s, plan_s, cnt_s = _mix(a, pin, hist_s, lambda i: (i, 0), xs, t_s, p_len, 0, False, pw, ps, wo, g2, wrh, wrl,
                               br)
    k_sample = kf.reshape(depth, b_s, t_s, N_HEADS, HEAD_DIM)
    v_sample = vf.reshape(depth, b_s, t_s, N_HEADS, HEAD_DIM)
    logf_sample = logf[:, :N_HEADS].reshape(depth, b_s, t_s, N_HEADS)
    pool_sample = pin.reshape(b_s, t_s, POOL_WIDTH)[:, t_s - POOL_PAD:].reshape(depth, b_s, POOL_PAD, POOL_WIDTH)

    dest_p, dest_s, fill_start, tile_e1, tile_e2, tile_valid, m_rows = _moe_plan(plan_p, cnt_p, plan_s, cnt_s,
                                                                              expert_tile)
    rows = _scatter_rows(fill_start, dest_p, xg_p, dest_s, xg_s, m_rows, tile, expert_tile)
    ys = _moe_sorted(tile_e1, tile_e2, tile_valid, rows, g2, wg, wu, wd, expert_tile)
    y_prompt = _gather_rows(dest_p, ys, tile).reshape(b_p, s_p, D_MODEL)
    y_sample = _gather_rows(dest_s, ys, tile).reshape(b_s, t_s, D_MODEL)

    return (y_prompt, y_sample, k_prompt, v_prompt, logf_prompt, pool_prompt,
            k_sample, v_sample, logf_sample, pool_sample)
```

```python
import functools

import jax
import jax.numpy as jnp
from jax import lax
from jax.experimental import pallas as pl
from jax.experimental.pallas import tpu as pltpu

F32 = jnp.float32
BF16 = jnp.bfloat16

D_MODEL = 1024
ATTN_WIDTH = 512
N_HEADS = 8
HEAD_DIM = 64
POOL_WIDTH = 512
POOL_WINDOWS = (2, 4, 8, 16)
POOL_GROUP_DIM = 128
POOL_PAD = 15
HIST_ROWS = 16
N_GROUPS = 4
EXPERTS_PER_GROUP = 4
N_EXPERTS = 16
D_EXPERT = 512
EPS = 1e-6
F32_EXP_UNDERFLOW = 104.0
BF16_SLACK = 1.02
SKIP_MARGIN = 1.0
LANES = 128
SUBLANES = 8
MXU_DIM = 256
PAIR = 2 * HEAD_DIM
N_PAIRS = N_HEADS // 2
ROWSUM_ROW = (HEAD_DIM, 0)
VMEM_LIMIT = 56 * 1024 * 1024
ROW_WIDTH = D_MODEL + LANES
PAIRS_PER_GROUP = 6
N_CATEGORIES = N_GROUPS * PAIRS_PER_GROUP
PAIR_ORDER = ((0, 1), (0, 2), (1, 2), (1, 3), (2, 3), (0, 3))
CATEGORY_EXPERTS = tuple((4 * g + a, 4 * g + b) for g in range(N_GROUPS) for a, b in PAIR_ORDER)
ROUTE_CATEGORY_LANE = N_EXPERTS
ROUTE_RANK_LANE = N_EXPERTS + 1


def _cparams(sem):
    return pltpu.CompilerParams(dimension_semantics=sem, vmem_limit_bytes=VMEM_LIMIT)


def _split3(x):
    hi = x.astype(BF16)
    r1 = x - hi.astype(F32)
    mid = r1.astype(BF16)
    lo = (r1 - mid.astype(F32)).astype(BF16)
    return hi, mid, lo


def _split2(x):
    hi = x.astype(BF16)
    lo = (x - hi.astype(F32)).astype(BF16)
    return hi, lo


def _inproj_kernel(x_ref, g1_ref, w_ref, bf_ref, qg_ref, kg_ref, hm_ref, tri_ref, place_ref,
                   q_ref, kf_ref, kb_ref, vf_ref, vb_ref, ka_ref, vt_ref, p_ref, logf_ref, c_ref, edge_ref, carry_ref, *,
                   carry_rows, key_blk):
    x = x_ref[...]
    r = lax.rsqrt(jnp.mean(x * x, axis=-1, keepdims=True) + EPS)
    n = ((x * r) * g1_ref[...]).astype(BF16)
    z = jnp.dot(n, w_ref[...], preferred_element_type=F32)

    hm = hm_ref[...]

    def head_norm(zz, g):
        hi, lo = _split2(zz * zz)
        half = hm.shape[0]
        ms = jnp.concatenate(
            [jnp.dot(hi[:, r:r + half], hm, preferred_element_type=F32)
             + jnp.dot(lo[:, r:r + half], hm, preferred_element_type=F32) for r in range(0, zz.shape[1], half)], axis=1)
        return (zz * lax.rsqrt(ms + EPS)) * g

    q = head_norm(z[:, 0:ATTN_WIDTH], qg_ref[...])
    q_ref[...] = (q * (HEAD_DIM ** -0.5)).astype(BF16)
    k = head_norm(z[:, ATTN_WIDTH:2 * ATTN_WIDTH], kg_ref[...])
    kf_ref[...] = k
    kb_ref[...] = k.astype(BF16)
    v = z[:, 2 * ATTN_WIDTH:3 * ATTN_WIDTH]
    vf_ref[...] = v
    vb_ref[...] = v.astype(BF16)
    p_ref[...] = z[:, 3 * ATTN_WIDTH:3 * ATTN_WIDTH + POOL_WIDTH]
    prow = lax.broadcasted_iota(jnp.int32, (PAIR, v.shape[0]), 0)
    for hp in range(N_PAIRS):
        vpair_t = v[:, hp * PAIR:(hp + 1) * PAIR].T
        for h in range(2):
            own = (prow < HEAD_DIM) if h == 0 else (prow >= HEAD_DIM)
            marker = jnp.where(prow == ROWSUM_ROW[h], 1.0, 0.0)
            vt_ref[(2 * hp + h) * LANES:(2 * hp + h + 1) * LANES, :] = jnp.where(own, vpair_t, marker).astype(BF16)

    f = z[:, 3 * ATTN_WIDTH + POOL_WIDTH:] + bf_ref[...]
    lane = lax.broadcasted_iota(jnp.int32, f.shape, 1)
    logf = -(jnp.maximum(-f, 0.0) + jnp.log1p(jnp.exp(-jnp.abs(f))))
    logf = jnp.where(lane < N_HEADS, logf, 0.0)
    logf_ref[...] = logf

    tri = tri_ref[...]
    blk = tri.shape[0]
    parts = _split3(logf)
    pieces = []
    for r in range(0, logf.shape[0], blk):
        piece = sum(jnp.dot(tri, part[r:r + blk, :], preferred_element_type=F32) for part in parts)
        if carry_rows and pieces:
            piece = piece + pieces[-1][blk - 1:blk, :]
        pieces.append(piece)
    c = jnp.concatenate(pieces, axis=0)
    if carry_rows:
        @pl.when(pl.program_id(0) == 0)
        def _():
            carry_ref[...] = jnp.zeros_like(carry_ref)
        c = c + carry_ref[0:1, :]
        carry_ref[...] = jnp.broadcast_to(c[c.shape[0] - 1:, :], carry_ref.shape)
    c_ref[...] = c
    edges = [c[r:r + 1, :] for b in range(c.shape[0] // key_blk) for r in (b * key_blk, (b + 1) * key_blk - 1)]
    edge_ref[...] = jnp.concatenate(edges + [jnp.zeros((SUBLANES - len(edges), LANES), F32)], axis=0)

    tm = c.shape[0]
    rel = jnp.concatenate([c[r:r + key_blk, :] - c[r:r + 1, :] for r in range(0, tm, key_blk)], axis=0)
    parts = jnp.concatenate(_split3(rel), axis=1)
    decay = jnp.dot(parts, place_ref[...], preferred_element_type=F32).astype(BF16)
    kb = k.astype(BF16)
    for hp in range(N_PAIRS):
        ka_ref[:, 2 * hp * LANES:(2 * hp + 1) * LANES] = kb[:, hp * PAIR:(hp + 1) * PAIR]
        ka_ref[:, (2 * hp + 1) * LANES:(2 * hp + 2) * LANES] = decay[:, hp * LANES:(hp + 1) * LANES]


def _decay_lane(h, part):
    return 3 * h + part


def _in_project(x2d, tm, seg, carry_rows, key_blk, g1, w_all, bf_pad, qg, kg, hm):
    n = x2d.shape[0]
    blk = min(tm, MXU_DIM)
    assert seg % blk == 0 or blk % seg == 0, "running-sum segments must nest with the triangular block"
    rows = lax.broadcasted_iota(jnp.int32, (blk, blk), 0)
    cols = lax.broadcasted_iota(jnp.int32, (blk, blk), 1)
    tri = ((cols <= rows) & (rows // seg == cols // seg)).astype(BF16)
    src = jnp.arange(3 * LANES)
    part, head = src // LANES, src % LANES
    dst = (head // 2) * LANES + _decay_lane(head % 2, part)
    place = ((jnp.arange(N_PAIRS * LANES)[None, :] == dst[:, None]) & (head < N_HEADS)[:, None]).astype(BF16)
    row_blk = lambda w: pl.BlockSpec((tm, w), lambda i: (i, 0))
    const = lambda a: pl.BlockSpec(a.shape, lambda i: (0,) * a.ndim)
    out_shape = (
        jax.ShapeDtypeStruct((n, ATTN_WIDTH), BF16),
        jax.ShapeDtypeStruct((n, ATTN_WIDTH), F32),
        jax.ShapeDtypeStruct((n, ATTN_WIDTH), BF16),
        jax.ShapeDtypeStruct((n, ATTN_WIDTH), F32),
        jax.ShapeDtypeStruct((n, ATTN_WIDTH), BF16),
        jax.ShapeDtypeStruct((n, N_PAIRS * 2 * LANES), BF16),
        jax.ShapeDtypeStruct((N_HEADS * LANES, n), BF16),
        jax.ShapeDtypeStruct((n, POOL_WIDTH), F32),
        jax.ShapeDtypeStruct((n, LANES), F32),
        jax.ShapeDtypeStruct((n, LANES), F32),
        jax.ShapeDtypeStruct((n // tm * SUBLANES, LANES), F32),
    )
    return pl.pallas_call(
        functools.partial(_inproj_kernel, carry_rows=carry_rows, key_blk=key_blk),
        grid=(n // tm,),
        in_specs=[row_blk(D_MODEL), const(g1), const(w_all), const(bf_pad), const(qg), const(kg), const(hm),
                  const(tri), const(place)],
        out_specs=[row_blk(ATTN_WIDTH)] * 5 + [row_blk(N_PAIRS * 2 * LANES),
                                                pl.BlockSpec((N_HEADS * LANES, tm), lambda i: (0, i)),
                                                row_blk(POOL_WIDTH), row_blk(LANES), row_blk(LANES),
                                                pl.BlockSpec((SUBLANES, LANES), lambda i: (i, 0))],
        out_shape=out_shape,
        scratch_shapes=[pltpu.VMEM((8, LANES), F32)],
        compiler_params=_cparams(("arbitrary",)),
        name="in_project",
    )(x2d, g1, w_all, bf_pad, qg, kg, hm, tri, place)


def _attn_prompt_kernel(j0_ref, cb_ref, q_ref, k_ref, vt_ref, o_ref, acc_ref, m_ref, *, tq, tk):
    hp = pl.program_id(0)
    i = pl.program_id(1)
    nk = tq // tk
    lane = lax.broadcasted_iota(jnp.int32, (1, PAIR), 1)
    first = lane < HEAD_DIM
    q = q_ref[...]
    zero = jnp.zeros_like(q)
    qw = []
    for h in range(2):
        minus = jnp.zeros((1, LANES), F32)
        for part in range(3):
            minus = jnp.where(lane == _decay_lane(h, part), -1.0, minus)
        qw.append(jnp.concatenate([jnp.where(first == (h == 0), q, zero),
                                   jnp.broadcast_to(minus, (tq, LANES)).astype(BF16)], axis=1))
    q_start = i * tq

    acc_ref[...] = jnp.zeros_like(acc_ref)
    m_ref[...] = jnp.full_like(m_ref, -jnp.inf)

    last = (i + 1) * nk - 1

    def scores(h, j, masked=True):
        start = pl.multiple_of(j * tk, tk)
        k = k_ref[pl.ds(start, tk), :]
        s = lax.dot_general(k, qw[h], (((1,), (1,)), ((), ())), preferred_element_type=F32)
        if masked:
            key = lax.broadcasted_iota(jnp.int32, s.shape, 0)
            qry = lax.broadcasted_iota(jnp.int32, s.shape, 1)
            s = jnp.where(key - qry <= q_start - j * tk, s, -jnp.inf)
        return s, jnp.max(s, axis=0, keepdims=True)

    def update(h, j, s, colmax):
        shift = cb_ref[j, 2 * hp + h] - cb_ref[i * nk, 2 * hp + h]
        start = pl.multiple_of(j * tk, tk)
        vt = vt_ref[h * LANES:(h + 1) * LANES, pl.ds(start, tk)]
        m_prev = m_ref[h]
        m_new = jnp.maximum(m_prev, colmax - shift)
        alpha = jnp.exp(m_prev - m_new)
        p = jnp.exp(s - (m_new + shift))
        m_ref[h] = m_new
        acc_ref[h] = acc_ref[h] * alpha + jnp.dot(vt, p.astype(BF16), preferred_element_type=F32)

    j_first = [j0_ref[i, 2 * hp + h] for h in range(2)]
    j_both = jnp.maximum(j_first[0], j_first[1])
    start = [scores(h, j_first[h]) for h in range(2)]
    ahead = []
    for h in range(2):
        def body_one(j, carry, h=h):
            nxt = scores(h, j + 1)
            update(h, j, *carry)
            return nxt

        ahead.append(lax.fori_loop(j_first[h], j_both, body_one, start[h]))

    def body_both(j, carry, masked):
        nxt = (scores(0, j + 1, masked), scores(1, j + 1, masked))
        update(0, j, *carry[0])
        update(1, j, *carry[1])
        return nxt

    j_mask = jnp.maximum(j_both, i * nk - 1)
    pairs = (j_mask - j_both) // 2

    def body_two(t, carry):
        j = j_both + 2 * t
        return body_both(j + 1, body_both(j, carry, False), False)

    carry = lax.fori_loop(0, pairs, body_two, (ahead[0], ahead[1]))
    carry = lax.fori_loop(j_both + 2 * pairs, last, functools.partial(body_both, masked=True), carry)
    update(0, last, *carry[0])
    update(1, last, *carry[1])

    out = [acc_ref[h] / acc_ref[h][ROWSUM_ROW[h]:ROWSUM_ROW[h] + 1, :] for h in range(2)]
    row = lax.broadcasted_iota(jnp.int32, (PAIR, 1), 0)
    o_ref[...] = jnp.where(row < HEAD_DIM, out[0], out[1]).T.astype(o_ref.dtype)


def _attention_prompt(j0, c_blocks, q, ka, vt, tq, tk):
    n = q.shape[0]
    grid_spec = pltpu.PrefetchScalarGridSpec(
        num_scalar_prefetch=2,
        grid=(N_PAIRS, n // tq),
        in_specs=[
            pl.BlockSpec((tq, PAIR), lambda hp, i, j0, cb: (i, hp)),
            pl.BlockSpec((n, 2 * LANES), lambda hp, i, j0, cb: (0, hp)),
            pl.BlockSpec((2 * LANES, n), lambda hp, i, j0, cb: (hp, 0)),
        ],
        out_specs=pl.BlockSpec((tq, PAIR), lambda hp, i, j0, cb: (i, hp)),
        scratch_shapes=[pltpu.VMEM((2, LANES, tq), F32), pltpu.VMEM((2, 1, tq), F32)],
    )
    return pl.pallas_call(
        functools.partial(_attn_prompt_kernel, tq=tq, tk=tk),
        grid_spec=grid_spec,
        out_shape=jax.ShapeDtypeStruct((n, ATTN_WIDTH), BF16),
        compiler_params=_cparams(("arbitrary", "arbitrary")),
        name="attention_prompt",
    )(j0, c_blocks, q, ka, vt)


def _first_key_blocks(c_first, c_last, q_gain, k_gain, tq, tk):
    n_k = c_first.shape[0]
    n_q = n_k * tk // tq
    qk_bound = BF16_SLACK * HEAD_DIM ** 0.5 * jnp.max(jnp.abs(q_gain)) * jnp.max(jnp.abs(k_gain))
    threshold = F32_EXP_UNDERFLOW + 2.0 * qk_bound + SKIP_MARGIN
    c_start = c_first[0::tq // tk]
    gap = c_start[:, None, :] - c_last[None, :, :]
    earlier = (jnp.arange(n_k)[None, :, None] + 1) * tk <= jnp.arange(n_q)[:, None, None] * tq
    return jnp.sum((gap < -threshold) & earlier, axis=1).astype(jnp.int32)


def _suffix_kernel(x_ref, o_ref):
    x = x_ref[...]
    p_len = x.shape[1]
    rows = lax.broadcasted_iota(jnp.int32, (p_len, p_len), 0)
    cols = lax.broadcasted_iota(jnp.int32, (p_len, p_len), 1)
    u = (rows > cols).astype(BF16)
    hi, mid, lo = _split3(x)
    o_ref[...] = (jnp.dot(hi, u, preferred_element_type=F32) + jnp.dot(mid, u, preferred_element_type=F32)
                  + jnp.dot(lo, u, preferred_element_type=F32))


def _suffix_sums(x):
    return pl.pallas_call(
        _suffix_kernel,
        out_shape=jax.ShapeDtypeStruct(x.shape, F32),
        compiler_params=pltpu.CompilerParams(vmem_limit_bytes=VMEM_LIMIT),
        name="cache_suffix_sums",
    )(x)


def _attn_sample_kernel(q_ref, kn_ref, vn_ref, ck_ref, cv_ref, d_ref, dt_ref, r_ref, o_ref):
    lane = lax.broadcasted_iota(jnp.int32, (1, PAIR), 1)
    first = lane < HEAD_DIM
    dblk = d_ref[...]
    dlane = lax.broadcasted_iota(jnp.int32, dblk.shape, 1)
    nt = (((1,), (1,)), ((), ()))
    for hp in range(N_PAIRS):
        cols = slice(hp * PAIR, (hp + 1) * PAIR)
        q = q_ref[:, cols]
        zero = jnp.zeros_like(q)
        qh = (jnp.where(first, q, zero), jnp.where(first, zero, q))
        kn = kn_ref[:, cols]
        vn = vn_ref[:, cols]
        ck = ck_ref[:, cols]
        cv = cv_ref[:, cols]
        outs, ls = [], []
        for h in range(2):
            dq = jnp.sum(jnp.where(dlane == 2 * hp + h, dblk, 0.0), axis=1, keepdims=True)
            s1 = lax.dot_general(qh[h], ck, nt, preferred_element_type=F32) + dq + r_ref[hp, h:h + 1, :]
            s2 = lax.dot_general(qh[h], kn, nt, preferred_element_type=F32) + dq - dt_ref[hp, h:h + 1, :]
            row = lax.broadcasted_iota(jnp.int32, s2.shape, 0)
            col = lax.broadcasted_iota(jnp.int32, s2.shape, 1)
            s2 = jnp.where(col <= row, s2, -jnp.inf)
            m = jnp.maximum(jnp.max(s1, axis=1, keepdims=True), jnp.max(s2, axis=1, keepdims=True))
            p1 = jnp.exp(s1 - m)
            p2 = jnp.exp(s2 - m)
            ls.append(jnp.sum(p1, axis=1, keepdims=True) + jnp.sum(p2, axis=1, keepdims=True))
            outs.append(jnp.dot(p1.astype(BF16), cv, preferred_element_type=F32)
                        + jnp.dot(p2.astype(BF16), vn, preferred_element_type=F32))
        o_ref[:, cols] = (jnp.where(first, outs[0], outs[1]) / jnp.where(first, ls[0], ls[1])).astype(o_ref.dtype)


def _attention_sample(q, kb, vb, cache_k, cache_v, d, dt4, r4, t):
    nb, p_len = cache_k.shape[0], cache_k.shape[1]
    row_blk = pl.BlockSpec((t, ATTN_WIDTH), lambda b: (b, 0))
    cache_blk = pl.BlockSpec((None, p_len, ATTN_WIDTH), lambda b: (b, 0, 0))
    return pl.pallas_call(
        _attn_sample_kernel,
        grid=(nb,),
        in_specs=[row_blk, row_blk, row_blk, cache_blk, cache_blk,
                  pl.BlockSpec((t, LANES), lambda b: (b, 0)),
                  pl.BlockSpec((None, N_PAIRS, 2, t), lambda b: (b, 0, 0, 0)),
                  pl.BlockSpec((None, N_PAIRS, 2, p_len), lambda b: (b, 0, 0, 0))],
        out_specs=row_blk,
        out_shape=jax.ShapeDtypeStruct((nb * t, ATTN_WIDTH), BF16),
        compiler_params=_cparams(("arbitrary",)),
        name="attention_sample",
    )(q, kb, vb, cache_k, cache_v, d, dt4, r4)


def _mix_kernel(a_ref, p_ref, hist_ref, x_ref, pw_ref, ps_ref, wo_ref, g2_ref, wrh_ref, wrl_ref, br_ref, tri_ref,
                xg_ref, plan_ref, cnt_ref, *, seg, pos0, pos_stride, zero_first):
    i = pl.program_id(0)
    t = p_ref.shape[0]
    p = p_ref[...]
    hist = hist_ref[...]
    if zero_first:
        hist = jnp.where(i == 0, 0.0, hist)
    ext = seg + HIST_ROWS
    xh = jnp.concatenate([piece for b in range(t // seg)
                          for piece in (hist[b * HIST_ROWS:(b + 1) * HIST_ROWS, :], p[b * seg:(b + 1) * seg, :])], axis=0)

    def own_rows(v):
        return jnp.concatenate([v[b * ext + HIST_ROWS:(b + 1) * ext, :] for b in range(t // seg)], axis=0)

    pos = pos0 + i * pos_stride + lax.broadcasted_iota(jnp.int32, (t, 1), 0) % seg
    ys = []
    for g, w in enumerate(POOL_WINDOWS):
        col = xh[:, g * POOL_GROUP_DIM:(g + 1) * POOL_GROUP_DIM]
        acc = col
        span = 1
        while span < w:
            acc = acc + pltpu.roll(acc, span, axis=0)
            span *= 2
        cnt = jnp.minimum(pos + 1, w).astype(F32)
        dgrp = own_rows(acc) / cnt - own_rows(col)
        ys.append(jnp.dot(dgrp.astype(BF16), pw_ref[g], preferred_element_type=F32))
    pm = (jnp.concatenate(ys, axis=1) * ps_ref[...]).astype(BF16)
    mix = (jnp.dot(a_ref[...], wo_ref[0:ATTN_WIDTH, :], preferred_element_type=F32)
           + jnp.dot(pm, wo_ref[ATTN_WIDTH:, :], preferred_element_type=F32))
    x1 = x_ref[...] + mix
    xg_ref[:, 0:D_MODEL] = x1

    r = lax.rsqrt(jnp.mean(x1 * x1, axis=-1, keepdims=True) + EPS)
    n2 = (x1 * r) * g2_ref[...]

    nh, nl = _split2(n2)
    wrh = wrh_ref[...]
    logits = (jnp.dot(nh, wrh, preferred_element_type=F32) + jnp.dot(nl, wrh, preferred_element_type=F32)
              + jnp.dot(nh, wrl_ref[...], preferred_element_type=F32)) + br_ref[...]
    lane = lax.broadcasted_iota(jnp.int32, logits.shape, 1).astype(F32)
    neg = -jnp.inf
    big = jnp.float32(1 << 20)
    is_g = (lane >= N_EXPERTS) & (lane < N_EXPERTS + N_GROUPS)
    lg = jnp.where(is_g, logits, neg)
    gmax = jnp.max(lg, axis=1, keepdims=True)
    g_lane = jnp.min(jnp.where(lg == gmax, lane, big), axis=1, keepdims=True)
    top_pg = 1.0 / jnp.sum(jnp.exp(lg - gmax), axis=1, keepdims=True)
    g_idx = g_lane - N_EXPERTS
    sel = (lane >= g_idx * EXPERTS_PER_GROUP) & (lane < (g_idx + 1) * EXPERTS_PER_GROUP)
    le = jnp.where(sel, logits, neg)
    m1 = jnp.max(le, axis=1, keepdims=True)
    i1 = jnp.min(jnp.where(le == m1, lane, big), axis=1, keepdims=True)
    le2 = jnp.where(lane == i1, neg, le)
    m2 = jnp.max(le2, axis=1, keepdims=True)
    i2 = jnp.min(jnp.where(le2 == m2, lane, big), axis=1, keepdims=True)
    e2 = jnp.exp(m2 - m1)
    w1 = top_pg / (1.0 + e2)
    w2 = top_pg * e2 / (1.0 + e2)
    gate = jnp.where(lane == i1, w1, 0.0) + jnp.where(lane == i2, w2, 0.0)

    @pl.when(i == 0)
    def _():
        cnt_ref[...] = jnp.zeros_like(cnt_ref)
    e_lo = jnp.minimum(i1, i2) - g_idx * EXPERTS_PER_GROUP
    e_hi = jnp.maximum(i1, i2) - g_idx * EXPERTS_PER_GROUP
    pair = jnp.zeros_like(e_lo)
    for idx, (a, b) in enumerate(PAIR_ORDER):
        pair = jnp.where((e_lo == a) & (e_hi == b), float(idx), pair)
    category = g_idx * PAIRS_PER_GROUP + pair
    member = lane == category
    tri = tri_ref[...]
    blk = tri.shape[0]
    seen = cnt_ref[0:1, :]
    pieces = []
    for r in range(0, t, blk):
        mem = member[r:r + blk, :].astype(BF16)
        pieces.append(jnp.dot(tri, mem, preferred_element_type=F32) + seen)
        seen = seen + jnp.sum(mem.astype(F32), axis=0, keepdims=True)
    before = jnp.concatenate(pieces, axis=0)
    rank = jnp.sum(jnp.where(member, before, 0.0), axis=1, keepdims=True)
    cnt_ref[...] = jnp.broadcast_to(seen, cnt_ref.shape)
    route = jnp.where(lane == ROUTE_CATEGORY_LANE, category, jnp.where(lane == ROUTE_RANK_LANE, rank, gate))
    xg_ref[:, D_MODEL:] = route
    srow = lax.broadcasted_iota(jnp.int32, (SUBLANES, LANES), 0)
    slane = lax.broadcasted_iota(jnp.int32, (SUBLANES, LANES), 1)
    pick = (((srow == 0) & (slane == ROUTE_CATEGORY_LANE)) | ((srow == 1) & (slane == ROUTE_RANK_LANE))).astype(BF16)
    nt = (((1,), (1,)), ((), ()))
    ids = jnp.where(lane >= N_EXPERTS, route, 0.0)
    hi, lo = _split2(ids)
    plan_ref[...] = (lax.dot_general(pick, hi, nt, preferred_element_type=F32)
                     + lax.dot_general(pick, lo, nt, preferred_element_type=F32))


def _mix(a, p, hist_src, hist_map, x2d, t, seg, pos0, pos_stride, zero_first, pw, ps, wo, g2, wrh, wrl, br):
    n = x2d.shape[0]
    blk = min(t, MXU_DIM)
    rows = lax.broadcasted_iota(jnp.int32, (blk, blk), 0)
    cols = lax.broadcasted_iota(jnp.int32, (blk, blk), 1)
    tri = (cols < rows).astype(BF16)
    row_blk = lambda w: pl.BlockSpec((t, w), lambda i: (i, 0))
    const = lambda arr: pl.BlockSpec(arr.shape, lambda i: (0,) * arr.ndim)
    return pl.pallas_call(
        functools.partial(_mix_kernel, seg=seg, pos0=pos0, pos_stride=pos_stride, zero_first=zero_first),
        grid=(n // t,),
        in_specs=[row_blk(ATTN_WIDTH), row_blk(POOL_WIDTH),
                  pl.BlockSpec((t // seg * HIST_ROWS, POOL_WIDTH), hist_map),
                  row_blk(D_MODEL), const(pw), const(ps), const(wo), const(g2), const(wrh), const(wrl), const(br),
                  const(tri)],
        out_specs=[row_blk(ROW_WIDTH), pl.BlockSpec((SUBLANES, t), lambda i: (i, 0)),
                   pl.BlockSpec((SUBLANES, LANES), lambda i: (0, 0))],
        out_shape=(jax.ShapeDtypeStruct((n, ROW_WIDTH), F32), jax.ShapeDtypeStruct((n // t * SUBLANES, t), F32),
                   jax.ShapeDtypeStruct((SUBLANES, LANES), F32)),
        compiler_params=_cparams(("arbitrary",)),
        name="pool_mix_route",
    )(a, p, hist_src, x2d, pw, ps, wo, g2, wrh, wrl, br, tri)


def _issue_row_scatter(dest_ref, x_ref, xs_ref, sem):
    t = x_ref.shape[0]

    def issue(g, carry):
        base = pl.multiple_of(g * SUBLANES, SUBLANES)
        tile_rows = x_ref.at[pl.ds(base, SUBLANES)]
        for u in range(SUBLANES):
            pltpu.make_async_copy(tile_rows.at[pl.ds(u, 1)], xs_ref.at[pl.ds(dest_ref[base + u], 1)],
                                  sem).start(priority=u % 2)
        return carry

    lax.fori_loop(0, t // SUBLANES, issue, 0)
    pltpu.make_async_copy(x_ref, xs_ref.at[pl.ds(0, t)], sem).wait()


def _scatter_rows_kernel(fill_ref, dest_a_ref, dest_b_ref, xa_ref, xb_ref, xs_ref, zero_ref, sem, *, steps_a):
    step = pl.program_id(0)

    @pl.when(step == 0)
    def _():
        zero_ref[...] = jnp.zeros_like(zero_ref)
        fill_rows = zero_ref.shape[0]
        last_start = xs_ref.shape[0] - fill_rows
        for g in range(fill_ref.shape[0]):
            @pl.when(fill_ref[g] < xs_ref.shape[0])
            def _(g=g):
                start = pl.multiple_of(jnp.minimum(fill_ref[g], last_start), SUBLANES)
                fill = pltpu.make_async_copy(zero_ref, xs_ref.at[pl.ds(start, fill_rows)], sem)
                fill.start()
                fill.wait()

    @pl.when(step < steps_a)
    def _():
        _issue_row_scatter(dest_a_ref, xa_ref, xs_ref, sem)

    @pl.when(step >= steps_a)
    def _():
        _issue_row_scatter(dest_b_ref, xb_ref, xs_ref, sem)


def _scatter_rows(fill_start, dest_a, xa, dest_b, xb, m_rows, t, fill_rows):
    w = xa.shape[1]
    steps_a, steps_b = xa.shape[0] // t, xb.shape[0] // t
    in_a = lambda i: jnp.minimum(i, steps_a - 1)
    in_b = lambda i: jnp.maximum(i - steps_a, 0)
    smem = pltpu.SMEM
    return pl.pallas_call(
        functools.partial(_scatter_rows_kernel, steps_a=steps_a),
        grid=(steps_a + steps_b,),
        in_specs=[pl.BlockSpec(fill_start.shape, lambda i: (0,), memory_space=smem),
                  pl.BlockSpec((t,), lambda i: (in_a(i),), memory_space=smem),
                  pl.BlockSpec((t,), lambda i: (in_b(i),), memory_space=smem),
                  pl.BlockSpec((t, w), lambda i: (in_a(i), 0)),
                  pl.BlockSpec((t, w), lambda i: (in_b(i), 0))],
        out_specs=pl.BlockSpec(memory_space=pl.ANY),
        out_shape=jax.ShapeDtypeStruct((m_rows, w), xa.dtype),
        scratch_shapes=[pltpu.VMEM((fill_rows, w), xa.dtype), pltpu.SemaphoreType.DMA],
        compiler_params=_cparams(("arbitrary",)),
        name="moe_scatter_rows",
    )(fill_start, dest_a, dest_b, xa, xb)


def _gather_rows_kernel(dest_ref, ys_ref, y_ref, sem):
    t = y_ref.shape[0]

    def issue(g, carry):
        base = pl.multiple_of(g * SUBLANES, SUBLANES)
        tile_rows = y_ref.at[pl.ds(base, SUBLANES)]
        for u in range(SUBLANES):
            pltpu.make_async_copy(ys_ref.at[pl.ds(dest_ref[base + u], 1)], tile_rows.at[pl.ds(u, 1)],
                                  sem).start(priority=u % 2)
        return carry

    lax.fori_loop(0, t // SUBLANES, issue, 0)
    pltpu.make_async_copy(ys_ref.at[pl.ds(0, t)], y_ref, sem).wait()


def _gather_rows(dest, ys, t):
    n = dest.shape[0]
    w = ys.shape[1]
    return pl.pallas_call(
        _gather_rows_kernel,
        grid=(n // t,),
        in_specs=[pl.BlockSpec((t,), lambda i: (i,), memory_space=pltpu.SMEM),
                  pl.BlockSpec(memory_space=pl.ANY)],
        out_specs=pl.BlockSpec((t, w), lambda i: (i, 0)),
        out_shape=jax.ShapeDtypeStruct((n, w), ys.dtype),
        scratch_shapes=[pltpu.SemaphoreType.DMA],
        compiler_params=_cparams(("arbitrary",)),
        name="moe_gather_rows",
    )(dest, ys)


def _moe_kernel(e1_ref, e2_ref, tv_ref, xs_ref, g2_ref, wg1_ref, wu1_ref, wd1_ref, wg2_ref, wu2_ref, wd2_ref, y_ref):
    m = pl.program_id(0)

    @pl.when(tv_ref[m] > 0)
    def _():
        x1 = xs_ref[:, 0:D_MODEL]
        route = xs_ref[:, D_MODEL:]
        lane = lax.broadcasted_iota(jnp.int32, route.shape, 1)
        r = lax.rsqrt(jnp.mean(x1 * x1, axis=-1, keepdims=True) + EPS)
        n2 = ((x1 * r) * g2_ref[...]).astype(BF16)
        acc = x1
        for e_ref, wg_ref, wu_ref, wd_ref in ((e1_ref, wg1_ref, wu1_ref, wd1_ref), (e2_ref, wg2_ref, wu2_ref, wd2_ref)):
            ge = jnp.sum(jnp.where(lane == e_ref[m], route, 0.0), axis=1, keepdims=True)
            hg = jnp.dot(n2, wg_ref[...], preferred_element_type=F32)
            hu = jnp.dot(n2, wu_ref[...], preferred_element_type=F32)
            h = (hg * jax.nn.sigmoid(hg)) * hu * ge
            acc = acc + jnp.dot(h.astype(BF16), wd_ref[...], preferred_element_type=F32)
        y_ref[...] = acc

    @pl.when(tv_ref[m] == 0)
    def _():
        y_ref[...] = jnp.zeros_like(y_ref)


def _moe_sorted(tile_e1, tile_e2, tile_valid, xs, g2, wg, wu, wd, tm):
    m_pad = tile_valid.shape[0] * tm
    up = lambda e: pl.BlockSpec((None, D_MODEL, D_EXPERT), lambda m, e1, e2, tv: ((e1, e2)[e][m], 0, 0))
    down = lambda e: pl.BlockSpec((None, D_EXPERT, D_MODEL), lambda m, e1, e2, tv: ((e1, e2)[e][m], 0, 0))
    grid_spec = pltpu.PrefetchScalarGridSpec(
        num_scalar_prefetch=3,
        grid=(m_pad // tm,),
        in_specs=[
            pl.BlockSpec((tm, ROW_WIDTH), lambda m, e1, e2, tv: (m, 0)),
            pl.BlockSpec(g2.shape, lambda m, e1, e2, tv: (0, 0)),
            up(0), up(0), down(0), up(1), up(1), down(1),
        ],
        out_specs=pl.BlockSpec((tm, D_MODEL), lambda m, e1, e2, tv: (m, 0)),
    )
    return pl.pallas_call(
        _moe_kernel,
        grid_spec=grid_spec,
        out_shape=jax.ShapeDtypeStruct((m_pad, D_MODEL), F32),
        compiler_params=_cparams(("arbitrary",)),
        name="moe_experts",
    )(tile_e1, tile_e2, tile_valid, xs, g2, wg, wu, wd, wg, wu, wd)


def _moe_plan(plan_p, cnt_p, plan_s, cnt_s, tm):
    def ids(plan):
        blocks = plan.reshape(-1, SUBLANES, plan.shape[1]).astype(jnp.int32)
        return blocks[:, 0, :], blocks[:, 1, :]

    def lookup(table, cat):
        out = jnp.zeros_like(cat)
        for c in range(N_CATEGORIES):
            out = jnp.where(cat == c, table[c], out)
        return out

    (kp, rank_p), (ks, rank_s) = ids(plan_p), ids(plan_s)
    n_total = kp.size + ks.size
    n_tiles = n_total // tm + N_CATEGORIES
    cp = cnt_p[0, :N_CATEGORIES].astype(jnp.int32)
    cs = cnt_s[0, :N_CATEGORIES].astype(jnp.int32)
    total = cp + cs
    tiles = (total + tm - 1) // tm
    tile_end = jnp.cumsum(tiles)
    offset = (tile_end - tiles) * tm
    dest_p = (lookup(offset, kp) + rank_p).reshape(-1)
    dest_s = (lookup(offset + cp, ks) + rank_s).reshape(-1)
    tile_id = jnp.arange(n_tiles)
    tile_cat = jnp.minimum(jnp.sum(tile_id[:, None] >= tile_end[None, :], axis=1), N_CATEGORIES - 1)
    used = jnp.clip(offset[tile_cat] + total[tile_cat] - tile_id * tm, 0, tm)
    tile_valid = jnp.where(tile_id < tile_end[N_CATEGORIES - 1], used, 0).astype(jnp.int32)
    experts = jnp.asarray(CATEGORY_EXPERTS, jnp.int32)
    tile_e1, tile_e2 = experts[tile_cat, 0], experts[tile_cat, 1]
    m_rows = (n_tiles + 1) * tm
    ends = ((offset + total) // SUBLANES) * SUBLANES
    tail = ends[N_CATEGORIES - 1] + tm * jnp.arange(1, m_rows // tm - n_total // tm + 1)
    fill_start = jnp.concatenate([ends, tail]).astype(jnp.int32)
    return dest_p, dest_s, fill_start, tile_e1, tile_e2, tile_valid, m_rows


def kernel(x_prompt, x_sample, cache_k, cache_v, cache_logf, state_pool, norm1_g, w_in, b_f, q_norm_g, k_norm_g,
           pool_w, pool_scale, w_out, norm2_g, w_router_group, b_router_group, w_router_expert, b_router_expert,
           w_gate, w_up, w_down):
    depth = norm1_g.shape[0]
    assert depth == 1, "single-layer step"
    b_p, s_p, _ = x_prompt.shape
    assert b_p == 1, "prompt kernels assume one stream"
    b_s, t_s, _ = x_sample.shape
    p_len = cache_k.shape[2]
    l = 0

    w = w_in[l]
    a3 = 3 * ATTN_WIDTH
    w_all = jnp.concatenate(
        [w[:, :a3], w[:, a3 + N_HEADS:], jnp.pad(w[:, a3:a3 + N_HEADS], ((0, 0), (0, LANES - N_HEADS)))],
        axis=1).astype(BF16)
    bf_pad = jnp.pad(b_f[l], (0, LANES - N_HEADS)).reshape(1, LANES)
    g1 = norm1_g[l].reshape(1, D_MODEL)
    qg = jnp.tile(q_norm_g[l], N_HEADS).reshape(1, ATTN_WIDTH)
    kg = jnp.tile(k_norm_g[l], N_HEADS).reshape(1, ATTN_WIDTH)
    hr = lax.broadcasted_iota(jnp.int32, (MXU_DIM, MXU_DIM), 0) // HEAD_DIM
    hc = lax.broadcasted_iota(jnp.int32, (MXU_DIM, MXU_DIM), 1) // HEAD_DIM
    hm = jnp.where(hr == hc, 1.0 / HEAD_DIM, 0.0).astype(BF16)
    pw = pool_w[l].astype(BF16)
    ps = pool_scale[l].reshape(1, POOL_WIDTH)
    wo = w_out[l].astype(BF16)
    g2 = norm2_g[l].reshape(1, D_MODEL)
    w_r = jnp.pad(jnp.concatenate([w_router_expert[l], w_router_group[l]], axis=1),
                  ((0, 0), (0, LANES - N_EXPERTS - N_GROUPS)))
    wrh = w_r.astype(BF16)
    wrl = (w_r - wrh.astype(F32)).astype(BF16)
    br = jnp.pad(jnp.concatenate([b_router_expert[l], b_router_group[l]]),
                 (0, LANES - N_EXPERTS - N_GROUPS)).reshape(1, LANES)
    wg = w_gate[l].astype(BF16)
    wu = w_up[l].astype(BF16)
    wd = w_down[l].astype(BF16)

    tile = 512
    query_tile = 512
    key_tile = 256
    expert_tile = 256

    xp = x_prompt.reshape(s_p, D_MODEL)
    q, kf, _, vf, _, ka, vt, pin, logf, _, edge = _in_project(xp, tile, tile, True, key_tile, g1, w_all, bf_pad, qg, kg,
                                                              hm)
    blocks_per_tile = tile // key_tile
    edge = edge.reshape(s_p // tile, SUBLANES, LANES)[:, :2 * blocks_per_tile, :N_HEADS]
    edge = edge.reshape(s_p // key_tile, 2, N_HEADS)
    c_blocks = edge[:, 0, :]
    j0 = _first_key_blocks(c_blocks, edge[:, 1, :], q_norm_g[l], k_norm_g[l], query_tile, key_tile)
    a = _attention_prompt(j0, c_blocks, q, ka, vt, query_tile, key_tile)
    hist_map = lambda i: (jnp.maximum(i * (tile // HIST_ROWS) - 1, 0), 0)
    xg_p, plan_p, cnt_p = _mix(a, pin, pin, hist_map, xp, tile, tile, 0, tile, True, pw, ps, wo, g2, wrh, wrl, br)
    k_prompt = kf.reshape(depth, b_p, s_p, N_HEADS, HEAD_DIM)
    v_prompt = vf.reshape(depth, b_p, s_p, N_HEADS, HEAD_DIM)
    logf_prompt = logf[:, :N_HEADS].reshape(depth, b_p, s_p, N_HEADS)
    pool_prompt = pin[s_p - POOL_PAD:].reshape(depth, b_p, POOL_PAD, POOL_WIDTH)

    n_s = b_s * t_s
    xs = x_sample.reshape(n_s, D_MODEL)
    q, kf, kb, vf, vb, _, _, pin, logf, d, _ = _in_project(xs, tile, t_s, False, key_tile, g1, w_all, bf_pad, qg, kg, hm)
    dt4 = d[:, :N_HEADS].reshape(b_s, t_s, N_PAIRS, 2).transpose(0, 2, 3, 1)
    clf = cache_logf[l].astype(F32).transpose(0, 2, 1).reshape(b_s * N_HEADS, p_len)
    r4 = _suffix_sums(clf).reshape(b_s, N_PAIRS, 2, p_len)
    ck = cache_k[l].reshape(b_s, p_len, ATTN_WIDTH).astype(BF16)
    cv = cache_v[l].reshape(b_s, p_len, ATTN_WIDTH).astype(BF16)
    a = _attention_sample(q, kb, vb, ck, cv, d, dt4, r4, t_s)
    hist_s = jnp.pad(state_pool[l], ((0, 0), (HIST_ROWS - POOL_PAD, 0), (0, 0))).reshape(b_s * HIST_ROWS, POOL_WIDTH)
    xg_s, plan_s, cnt_s = _mix(a, pin, hist_s, lambda i: (i, 0), xs, tile, t_s, p_len, 0, False, pw, ps, wo, g2, wrh,
                               wrl, br)
    k_sample = kf.reshape(depth, b_s, t_s, N_HEADS, HEAD_DIM)
    v_sample = vf.reshape(depth, b_s, t_s, N_HEADS, HEAD_DIM)
    logf_sample = logf[:, :N_HEADS].reshape(depth, b_s, t_s, N_HEADS)
    pool_sample = pin.reshape(b_s, t_s, POOL_WIDTH)[:, t_s - POOL_PAD:].reshape(depth, b_s, POOL_PAD, POOL_WIDTH)

    dest_p, dest_s, fill_start, tile_e1, tile_e2, tile_valid, m_rows = _moe_plan(plan_p, cnt_p, plan_s, cnt_s,
                                                                              expert_tile)
    rows = _scatter_rows(fill_start, dest_p, xg_p, dest_s, xg_s, m_rows, tile, expert_tile)
    ys = _moe_sorted(tile_e1, tile_e2, tile_valid, rows, g2, wg, wu, wd, expert_tile)
    y_prompt = _gather_rows(dest_p, ys, tile).reshape(b_p, s_p, D_MODEL)
    y_sample = _gather_rows(dest_s, ys, tile).reshape(b_s, t_s, D_MODEL)

    return (y_prompt, y_sample, k_prompt, v_prompt, logf_prompt, pool_prompt,
            k_sample, v_sample, logf_sample, pool_sample)
```

```python
import functools

import jax
import jax.numpy as jnp
from jax import lax
from jax.experimental import pallas as pl
from jax.experimental.pallas import tpu as pltpu

F32 = jnp.float32
BF16 = jnp.bfloat16

D_MODEL = 1024
ATTN_WIDTH = 512
N_HEADS = 8
HEAD_DIM = 64
POOL_WIDTH = 512
POOL_WINDOWS = (2, 4, 8, 16)
POOL_GROUP_DIM = 128
POOL_PAD = 15
HIST_ROWS = 16
N_GROUPS = 4
EXPERTS_PER_GROUP = 4
N_EXPERTS = 16
D_EXPERT = 512
EPS = 1e-6
F32_EXP_UNDERFLOW = 104.0
BF16_SLACK = 1.02
SKIP_MARGIN = 1.0
LANES = 128
SUBLANES = 8
MXU_DIM = 256
PAIR = 2 * HEAD_DIM
N_PAIRS = N_HEADS // 2
ROWSUM_ROW = (HEAD_DIM, 0)
VMEM_LIMIT = 56 * 1024 * 1024
ROW_WIDTH = D_MODEL + LANES
PAIRS_PER_GROUP = 6
N_CATEGORIES = N_GROUPS * PAIRS_PER_GROUP
PAIR_ORDER = ((0, 1), (0, 2), (1, 2), (1, 3), (2, 3), (0, 3))
CATEGORY_EXPERTS = tuple((4 * g + a, 4 * g + b) for g in range(N_GROUPS) for a, b in PAIR_ORDER)
ROUTE_CATEGORY_LANE = N_EXPERTS
ROUTE_RANK_LANE = N_EXPERTS + 1


def _cparams(sem):
    return pltpu.CompilerParams(dimension_semantics=sem, vmem_limit_bytes=VMEM_LIMIT)


def _split3(x):
    hi = x.astype(BF16)
    r1 = x - hi.astype(F32)
    mid = r1.astype(BF16)
    lo = (r1 - mid.astype(F32)).astype(BF16)
    return hi, mid, lo


def _split2(x):
    hi = x.astype(BF16)
    lo = (x - hi.astype(F32)).astype(BF16)
    return hi, lo


def _inproj_kernel(x_ref, g1_ref, w_ref, bf_ref, qg_ref, kg_ref, hm_ref, tri_ref, place_ref,
                   q_ref, kf_ref, kb_ref, vf_ref, vb_ref, ka_ref, vt_ref, p_ref, logf_ref, c_ref, edge_ref, carry_ref, *,
                   carry_rows, key_blk):
    x = x_ref[...]
    r = lax.rsqrt(jnp.mean(x * x, axis=-1, keepdims=True) + EPS)
    n = ((x * r) * g1_ref[...]).astype(BF16)
    z = jnp.dot(n, w_ref[...], preferred_element_type=F32)

    hm = hm_ref[...]

    def head_norm(zz, g):
        hi, lo = _split2(zz * zz)
        half = hm.shape[0]
        ms = jnp.concatenate(
            [jnp.dot(hi[:, r:r + half], hm, preferred_element_type=F32)
             + jnp.dot(lo[:, r:r + half], hm, preferred_element_type=F32) for r in range(0, zz.shape[1], half)], axis=1)
        return (zz * lax.rsqrt(ms + EPS)) * g

    q = head_norm(z[:, 0:ATTN_WIDTH], qg_ref[...])
    q_ref[...] = (q * (HEAD_DIM ** -0.5)).astype(BF16)
    k = head_norm(z[:, ATTN_WIDTH:2 * ATTN_WIDTH], kg_ref[...])
    kf_ref[...] = k
    kb_ref[...] = k.astype(BF16)
    v = z[:, 2 * ATTN_WIDTH:3 * ATTN_WIDTH]
    vf_ref[...] = v
    vb_ref[...] = v.astype(BF16)
    p_ref[...] = z[:, 3 * ATTN_WIDTH:3 * ATTN_WIDTH + POOL_WIDTH]
    prow = lax.broadcasted_iota(jnp.int32, (PAIR, v.shape[0]), 0)
    for hp in range(N_PAIRS):
        vpair_t = v[:, hp * PAIR:(hp + 1) * PAIR].T
        for h in range(2):
            own = (prow < HEAD_DIM) if h == 0 else (prow >= HEAD_DIM)
            marker = jnp.where(prow == ROWSUM_ROW[h], 1.0, 0.0)
            vt_ref[(2 * hp + h) * LANES:(2 * hp + h + 1) * LANES, :] = jnp.where(own, vpair_t, marker).astype(BF16)

    f = z[:, 3 * ATTN_WIDTH + POOL_WIDTH:] + bf_ref[...]
    lane = lax.broadcasted_iota(jnp.int32, f.shape, 1)
    logf = -(jnp.maximum(-f, 0.0) + jnp.log1p(jnp.exp(-jnp.abs(f))))
    logf = jnp.where(lane < N_HEADS, logf, 0.0)
    logf_ref[...] = logf

    tri = tri_ref[...]
    blk = tri.shape[0]
    parts = _split3(logf)
    pieces = []
    for r in range(0, logf.shape[0], blk):
        piece = sum(jnp.dot(tri, part[r:r + blk, :], preferred_element_type=F32) for part in parts)
        if carry_rows and pieces:
            piece = piece + pieces[-1][blk - 1:blk, :]
        pieces.append(piece)
    c = jnp.concatenate(pieces, axis=0)
    if carry_rows:
        @pl.when(pl.program_id(0) == 0)
        def _():
            carry_ref[...] = jnp.zeros_like(carry_ref)
        c = c + carry_ref[0:1, :]
        carry_ref[...] = jnp.broadcast_to(c[c.shape[0] - 1:, :], carry_ref.shape)
    c_ref[...] = c
    edges = [c[r:r + 1, :] for b in range(c.shape[0] // key_blk) for r in (b * key_blk, (b + 1) * key_blk - 1)]
    edge_ref[...] = jnp.concatenate(edges + [jnp.zeros((SUBLANES - len(edges), LANES), F32)], axis=0)

    tm = c.shape[0]
    rel = jnp.concatenate([c[r:r + key_blk, :] - c[r:r + 1, :] for r in range(0, tm, key_blk)], axis=0)
    parts = jnp.concatenate(_split3(rel), axis=1)
    decay = jnp.dot(parts, place_ref[...], preferred_element_type=F32).astype(BF16)
    kb = k.astype(BF16)
    for hp in range(N_PAIRS):
        ka_ref[:, 2 * hp * LANES:(2 * hp + 1) * LANES] = kb[:, hp * PAIR:(hp + 1) * PAIR]
        ka_ref[:, (2 * hp + 1) * LANES:(2 * hp + 2) * LANES] = decay[:, hp * LANES:(hp + 1) * LANES]


def _decay_lane(h, part):
    return 3 * h + part


def _in_project(x2d, tm, seg, carry_rows, key_blk, g1, w_all, bf_pad, qg, kg, hm):
    n = x2d.shape[0]
    blk = min(tm, MXU_DIM)
    assert seg % blk == 0 or blk % seg == 0, "running-sum segments must nest with the triangular block"
    rows = lax.broadcasted_iota(jnp.int32, (blk, blk), 0)
    cols = lax.broadcasted_iota(jnp.int32, (blk, blk), 1)
    tri = ((cols <= rows) & (rows // seg == cols // seg)).astype(BF16)
    src = jnp.arange(3 * LANES)
    part, head = src // LANES, src % LANES
    dst = (head // 2) * LANES + _decay_lane(head % 2, part)
    place = ((jnp.arange(N_PAIRS * LANES)[None, :] == dst[:, None]) & (head < N_HEADS)[:, None]).astype(BF16)
    row_blk = lambda w: pl.BlockSpec((tm, w), lambda i: (i, 0))
    const = lambda a: pl.BlockSpec(a.shape, lambda i: (0,) * a.ndim)
    out_shape = (
        jax.ShapeDtypeStruct((n, ATTN_WIDTH), BF16),
        jax.ShapeDtypeStruct((n, ATTN_WIDTH), F32),
        jax.ShapeDtypeStruct((n, ATTN_WIDTH), BF16),
        jax.ShapeDtypeStruct((n, ATTN_WIDTH), F32),
        jax.ShapeDtypeStruct((n, ATTN_WIDTH), BF16),
        jax.ShapeDtypeStruct((n, N_PAIRS * 2 * LANES), BF16),
        jax.ShapeDtypeStruct((N_HEADS * LANES, n), BF16),
        jax.ShapeDtypeStruct((n, POOL_WIDTH), F32),
        jax.ShapeDtypeStruct((n, LANES), F32),
        jax.ShapeDtypeStruct((n, LANES), F32),
        jax.ShapeDtypeStruct((n // tm * SUBLANES, LANES), F32),
    )
    return pl.pallas_call(
        functools.partial(_inproj_kernel, carry_rows=carry_rows, key_blk=key_blk),
        grid=(n // tm,),
        in_specs=[row_blk(D_MODEL), const(g1), const(w_all), const(bf_pad), const(qg), const(kg), const(hm),
                  const(tri), const(place)],
        out_specs=[row_blk(ATTN_WIDTH)] * 5 + [row_blk(N_PAIRS * 2 * LANES),
                                                pl.BlockSpec((N_HEADS * LANES, tm), lambda i: (0, i)),
                                                row_blk(POOL_WIDTH), row_blk(LANES), row_blk(LANES),
                                                pl.BlockSpec((SUBLANES, LANES), lambda i: (i, 0))],
        out_shape=out_shape,
        scratch_shapes=[pltpu.VMEM((8, LANES), F32)],
        compiler_params=_cparams(("arbitrary",)),
        name="in_project",
    )(x2d, g1, w_all, bf_pad, qg, kg, hm, tri, place)


def _attn_prompt_kernel(j0_ref, cb_ref, q_ref, k_ref, vt_ref, o_ref, acc_ref, m_ref, *, tq, tk):
    hp = pl.program_id(0)
    i = pl.program_id(1)
    nk = tq // tk
    lane = lax.broadcasted_iota(jnp.int32, (1, PAIR), 1)
    first = lane < HEAD_DIM
    q = q_ref[...]
    zero = jnp.zeros_like(q)
    qw = []
    for h in range(2):
        minus = jnp.zeros((1, LANES), F32)
        for part in range(3):
            minus = jnp.where(lane == _decay_lane(h, part), -1.0, minus)
        qw.append(jnp.concatenate([jnp.where(first == (h == 0), q, zero),
                                   jnp.broadcast_to(minus, (tq, LANES)).astype(BF16)], axis=1))
    q_start = i * tq

    acc_ref[...] = jnp.zeros_like(acc_ref)
    m_ref[...] = jnp.full_like(m_ref, -jnp.inf)

    last = (i + 1) * nk - 1

    def scores(h, j, masked=True):
        start = pl.multiple_of(j * tk, tk)
        k = k_ref[pl.ds(start, tk), :]
        s = lax.dot_general(k, qw[h], (((1,), (1,)), ((), ())), preferred_element_type=F32)
        if masked:
            key = lax.broadcasted_iota(jnp.int32, s.shape, 0)
            qry = lax.broadcasted_iota(jnp.int32, s.shape, 1)
            s = jnp.where(key - qry <= q_start - j * tk, s, -jnp.inf)
        return s, jnp.max(s, axis=0, keepdims=True)

    def update(h, j, s, colmax):
        shift = cb_ref[j, 2 * hp + h] - cb_ref[i * nk, 2 * hp + h]
        start = pl.multiple_of(j * tk, tk)
        vt = vt_ref[h * LANES:(h + 1) * LANES, pl.ds(start, tk)]
        m_prev = m_ref[h]
        m_new = jnp.maximum(m_prev, colmax - shift)
        alpha = jnp.exp(m_prev - m_new)
        p = jnp.exp(s - (m_new + shift))
        m_ref[h] = m_new
        acc_ref[h] = acc_ref[h] * alpha + jnp.dot(vt, p.astype(BF16), preferred_element_type=F32)

    j_first = [j0_ref[i, 2 * hp + h] for h in range(2)]
    j_both = jnp.maximum(j_first[0], j_first[1])
    start = [scores(h, j_first[h]) for h in range(2)]
    ahead = []
    for h in range(2):
        def body_one(j, carry, h=h):
            nxt = scores(h, j + 1)
            update(h, j, *carry)
            return nxt

        ahead.append(lax.fori_loop(j_first[h], j_both, body_one, start[h]))

    def body_both(j, carry, masked):
        nxt = (scores(0, j + 1, masked), scores(1, j + 1, masked))
        update(0, j, *carry[0])
        update(1, j, *carry[1])
        return nxt

    j_mask = jnp.maximum(j_both, i * nk - 1)
    pairs = (j_mask - j_both) // 2

    def body_two(t, carry):
        j = j_both + 2 * t
        return body_both(j + 1, body_both(j, carry, False), False)

    carry = lax.fori_loop(0, pairs, body_two, (ahead[0], ahead[1]))
    carry = lax.fori_loop(j_both + 2 * pairs, last, functools.partial(body_both, masked=True), carry)
    update(0, last, *carry[0])
    update(1, last, *carry[1])

    out = [acc_ref[h] / acc_ref[h][ROWSUM_ROW[h]:ROWSUM_ROW[h] + 1, :] for h in range(2)]
    row = lax.broadcasted_iota(jnp.int32, (PAIR, 1), 0)
    o_ref[...] = jnp.where(row < HEAD_DIM, out[0], out[1]).T.astype(o_ref.dtype)


def _attention_prompt(j0, c_blocks, q, ka, vt, tq, tk):
    n = q.shape[0]
    grid_spec = pltpu.PrefetchScalarGridSpec(
        num_scalar_prefetch=2,
        grid=(N_PAIRS, n // tq),
        in_specs=[
            pl.BlockSpec((tq, PAIR), lambda hp, i, j0, cb: (i, hp)),
            pl.BlockSpec((n, 2 * LANES), lambda hp, i, j0, cb: (0, hp)),
            pl.BlockSpec((2 * LANES, n), lambda hp, i, j0, cb: (hp, 0)),
        ],
        out_specs=pl.BlockSpec((tq, PAIR), lambda hp, i, j0, cb: (i, hp)),
        scratch_shapes=[pltpu.VMEM((2, LANES, tq), F32), pltpu.VMEM((2, 1, tq), F32)],
    )
    return pl.pallas_call(
        functools.partial(_attn_prompt_kernel, tq=tq, tk=tk),
        grid_spec=grid_spec,
        out_shape=jax.ShapeDtypeStruct((n, ATTN_WIDTH), BF16),
        compiler_params=_cparams(("arbitrary", "arbitrary")),
        name="attention_prompt",
    )(j0, c_blocks, q, ka, vt)


def _first_key_blocks(c_first, c_last, q_gain, k_gain, tq, tk):
    n_k = c_first.shape[0]
    n_q = n_k * tk // tq
    qk_bound = BF16_SLACK * HEAD_DIM ** 0.5 * jnp.max(jnp.abs(q_gain)) * jnp.max(jnp.abs(k_gain))
    threshold = F32_EXP_UNDERFLOW + 2.0 * qk_bound + SKIP_MARGIN
    c_start = c_first[0::tq // tk]
    gap = c_start[:, None, :] - c_last[None, :, :]
    earlier = (jnp.arange(n_k)[None, :, None] + 1) * tk <= jnp.arange(n_q)[:, None, None] * tq
    return jnp.sum((gap < -threshold) & earlier, axis=1).astype(jnp.int32)


def _suffix_kernel(x_ref, o_ref):
    x = x_ref[...]
    p_len = x.shape[1]
    rows = lax.broadcasted_iota(jnp.int32, (p_len, p_len), 0)
    cols = lax.broadcasted_iota(jnp.int32, (p_len, p_len), 1)
    u = (rows > cols).astype(BF16)
    hi, mid, lo = _split3(x)
    o_ref[...] = (jnp.dot(hi, u, preferred_element_type=F32) + jnp.dot(mid, u, preferred_element_type=F32)
                  + jnp.dot(lo, u, preferred_element_type=F32))


def _suffix_sums(x):
    return pl.pallas_call(
        _suffix_kernel,
        out_shape=jax.ShapeDtypeStruct(x.shape, F32),
        compiler_params=pltpu.CompilerParams(vmem_limit_bytes=VMEM_LIMIT),
        name="cache_suffix_sums",
    )(x)


def _attn_sample_kernel(q_ref, kn_ref, vn_ref, ck_ref, cv_ref, d_ref, dt_ref, r_ref, o_ref):
    lane = lax.broadcasted_iota(jnp.int32, (1, PAIR), 1)
    first = lane < HEAD_DIM
    dblk = d_ref[...]
    dlane = lax.broadcasted_iota(jnp.int32, dblk.shape, 1)
    nt = (((1,), (1,)), ((), ()))
    for hp in range(N_PAIRS):
        cols = slice(hp * PAIR, (hp + 1) * PAIR)
        q = q_ref[:, cols]
        zero = jnp.zeros_like(q)
        qh = (jnp.where(first, q, zero), jnp.where(first, zero, q))
        kn = kn_ref[:, cols]
        vn = vn_ref[:, cols]
        ck = ck_ref[:, cols]
        cv = cv_ref[:, cols]
        outs, ls = [], []
        for h in range(2):
            dq = jnp.sum(jnp.where(dlane == 2 * hp + h, dblk, 0.0), axis=1, keepdims=True)
            s1 = lax.dot_general(qh[h], ck, nt, preferred_element_type=F32) + dq + r_ref[hp, h:h + 1, :]
            s2 = lax.dot_general(qh[h], kn, nt, preferred_element_type=F32) + dq - dt_ref[hp, h:h + 1, :]
            row = lax.broadcasted_iota(jnp.int32, s2.shape, 0)
            col = lax.broadcasted_iota(jnp.int32, s2.shape, 1)
            s2 = jnp.where(col <= row, s2, -jnp.inf)
            m = jnp.maximum(jnp.max(s1, axis=1, keepdims=True), jnp.max(s2, axis=1, keepdims=True))
            p1 = jnp.exp(s1 - m)
            p2 = jnp.exp(s2 - m)
            ls.append(jnp.sum(p1, axis=1, keepdims=True) + jnp.sum(p2, axis=1, keepdims=True))
            outs.append(jnp.dot(p1.astype(BF16), cv, preferred_element_type=F32)
                        + jnp.dot(p2.astype(BF16), vn, preferred_element_type=F32))
        o_ref[:, cols] = (jnp.where(first, outs[0], outs[1]) / jnp.where(first, ls[0], ls[1])).astype(o_ref.dtype)


def _attention_sample(q, kb, vb, cache_k, cache_v, d, dt4, r4, t):
    nb, p_len = cache_k.shape[0], cache_k.shape[1]
    row_blk = pl.BlockSpec((t, ATTN_WIDTH), lambda b: (b, 0))
    cache_blk = pl.BlockSpec((None, p_len, ATTN_WIDTH), lambda b: (b, 0, 0))
    return pl.pallas_call(
        _attn_sample_kernel,
        grid=(nb,),
        in_specs=[row_blk, row_blk, row_blk, cache_blk, cache_blk,
                  pl.BlockSpec((t, LANES), lambda b: (b, 0)),
                  pl.BlockSpec((None, N_PAIRS, 2, t), lambda b: (b, 0, 0, 0)),
                  pl.BlockSpec((None, N_PAIRS, 2, p_len), lambda b: (b, 0, 0, 0))],
        out_specs=row_blk,
        out_shape=jax.ShapeDtypeStruct((nb * t, ATTN_WIDTH), BF16),
        compiler_params=_cparams(("arbitrary",)),
        name="attention_sample",
    )(q, kb, vb, cache_k, cache_v, d, dt4, r4)


def _mix_kernel(a_ref, p_ref, hist_ref, x_ref, pw_ref, ps_ref, wo_ref, g2_ref, wrh_ref, wrl_ref, br_ref, tri_ref,
                xg_ref, plan_ref, cnt_ref, *, seg, pos0, pos_stride, zero_first):
    i = pl.program_id(0)
    t = p_ref.shape[0]
    p = p_ref[...]
    hist = hist_ref[...]
    if zero_first:
        hist = jnp.where(i == 0, 0.0, hist)
    ext = seg + HIST_ROWS
    xh = jnp.concatenate([piece for b in range(t // seg)
                          for piece in (hist[b * HIST_ROWS:(b + 1) * HIST_ROWS, :], p[b * seg:(b + 1) * seg, :])], axis=0)

    def own_rows(v):
        return jnp.concatenate([v[b * ext + HIST_ROWS:(b + 1) * ext, :] for b in range(t // seg)], axis=0)

    pos = pos0 + i * pos_stride + lax.broadcasted_iota(jnp.int32, (t, 1), 0) % seg
    ys = []
    for g, w in enumerate(POOL_WINDOWS):
        col = xh[:, g * POOL_GROUP_DIM:(g + 1) * POOL_GROUP_DIM]
        acc = col
        span = 1
        while span < w:
            acc = acc + pltpu.roll(acc, span, axis=0)
            span *= 2
        cnt = jnp.minimum(pos + 1, w).astype(F32)
        dgrp = own_rows(acc) / cnt - own_rows(col)
        ys.append(jnp.dot(dgrp.astype(BF16), pw_ref[g], preferred_element_type=F32))
    pm = (jnp.concatenate(ys, axis=1) * ps_ref[...]).astype(BF16)
    mix = (jnp.dot(a_ref[...], wo_ref[0:ATTN_WIDTH, :], preferred_element_type=F32)
           + jnp.dot(pm, wo_ref[ATTN_WIDTH:, :], preferred_element_type=F32))
    x1 = x_ref[...] + mix
    xg_ref[:, 0:D_MODEL] = x1

    r = lax.rsqrt(jnp.mean(x1 * x1, axis=-1, keepdims=True) + EPS)
    n2 = (x1 * r) * g2_ref[...]

    nh, nl = _split2(n2)
    wrh = wrh_ref[...]
    logits = (jnp.dot(nh, wrh, preferred_element_type=F32) + jnp.dot(nl, wrh, preferred_element_type=F32)
              + jnp.dot(nh, wrl_ref[...], preferred_element_type=F32)) + br_ref[...]
    lane = lax.broadcasted_iota(jnp.int32, logits.shape, 1).astype(F32)
    neg = -jnp.inf
    big = jnp.float32(1 << 20)
    is_g = (lane >= N_EXPERTS) & (lane < N_EXPERTS + N_GROUPS)
    lg = jnp.where(is_g, logits, neg)
    gmax = jnp.max(lg, axis=1, keepdims=True)
    g_lane = jnp.min(jnp.where(lg == gmax, lane, big), axis=1, keepdims=True)
    top_pg = 1.0 / jnp.sum(jnp.exp(lg - gmax), axis=1, keepdims=True)
    g_idx = g_lane - N_EXPERTS
    sel = (lane >= g_idx * EXPERTS_PER_GROUP) & (lane < (g_idx + 1) * EXPERTS_PER_GROUP)
    le = jnp.where(sel, logits, neg)
    m1 = jnp.max(le, axis=1, keepdims=True)
    i1 = jnp.min(jnp.where(le == m1, lane, big), axis=1, keepdims=True)
    le2 = jnp.where(lane == i1, neg, le)
    m2 = jnp.max(le2, axis=1, keepdims=True)
    i2 = jnp.min(jnp.where(le2 == m2, lane, big), axis=1, keepdims=True)
    e2 = jnp.exp(m2 - m1)
    w1 = top_pg / (1.0 + e2)
    w2 = top_pg * e2 / (1.0 + e2)
    gate = jnp.where(lane == i1, w1, 0.0) + jnp.where(lane == i2, w2, 0.0)

    @pl.when(i == 0)
    def _():
        cnt_ref[...] = jnp.zeros_like(cnt_ref)
    e_lo = jnp.minimum(i1, i2) - g_idx * EXPERTS_PER_GROUP
    e_hi = jnp.maximum(i1, i2) - g_idx * EXPERTS_PER_GROUP
    pair = jnp.zeros_like(e_lo)
    for idx, (a, b) in enumerate(PAIR_ORDER):
        pair = jnp.where((e_lo == a) & (e_hi == b), float(idx), pair)
    category = g_idx * PAIRS_PER_GROUP + pair
    member = lane == category
    tri = tri_ref[...]
    blk = tri.shape[0]
    seen = cnt_ref[0:1, :]
    pieces = []
    for r in range(0, t, blk):
        mem = member[r:r + blk, :].astype(BF16)
        pieces.append(jnp.dot(tri, mem, preferred_element_type=F32) + seen)
        seen = seen + jnp.sum(mem.astype(F32), axis=0, keepdims=True)
    before = jnp.concatenate(pieces, axis=0)
    rank = jnp.sum(jnp.where(member, before, 0.0), axis=1, keepdims=True)
    cnt_ref[...] = jnp.broadcast_to(seen, cnt_ref.shape)
    route = jnp.where(lane == ROUTE_CATEGORY_LANE, category, jnp.where(lane == ROUTE_RANK_LANE, rank, gate))
    xg_ref[:, D_MODEL:] = route
    srow = lax.broadcasted_iota(jnp.int32, (SUBLANES, LANES), 0)
    slane = lax.broadcasted_iota(jnp.int32, (SUBLANES, LANES), 1)
    pick = (((srow == 0) & (slane == ROUTE_CATEGORY_LANE)) | ((srow == 1) & (slane == ROUTE_RANK_LANE))).astype(BF16)
    nt = (((1,), (1,)), ((), ()))
    ids = jnp.where(lane >= N_EXPERTS, route, 0.0)
    hi, lo = _split2(ids)
    plan_ref[...] = (lax.dot_general(pick, hi, nt, preferred_element_type=F32)
                     + lax.dot_general(pick, lo, nt, preferred_element_type=F32))


def _mix(a, p, hist_src, hist_map, x2d, t, seg, pos0, pos_stride, zero_first, pw, ps, wo, g2, wrh, wrl, br):
    n = x2d.shape[0]
    blk = min(t, MXU_DIM)
    rows = lax.broadcasted_iota(jnp.int32, (blk, blk), 0)
    cols = lax.broadcasted_iota(jnp.int32, (blk, blk), 1)
    tri = (cols < rows).astype(BF16)
    row_blk = lambda w: pl.BlockSpec((t, w), lambda i: (i, 0))
    const = lambda arr: pl.BlockSpec(arr.shape, lambda i: (0,) * arr.ndim)
    return pl.pallas_call(
        functools.partial(_mix_kernel, seg=seg, pos0=pos0, pos_stride=pos_stride, zero_first=zero_first),
        grid=(n // t,),
        in_specs=[row_blk(ATTN_WIDTH), row_blk(POOL_WIDTH),
                  pl.BlockSpec((t // seg * HIST_ROWS, POOL_WIDTH), hist_map),
                  row_blk(D_MODEL), const(pw), const(ps), const(wo), const(g2), const(wrh), const(wrl), const(br),
                  const(tri)],
        out_specs=[row_blk(ROW_WIDTH), pl.BlockSpec((SUBLANES, t), lambda i: (i, 0)),
                   pl.BlockSpec((SUBLANES, LANES), lambda i: (0, 0))],
        out_shape=(jax.ShapeDtypeStruct((n, ROW_WIDTH), F32), jax.ShapeDtypeStruct((n // t * SUBLANES, t), F32),
                   jax.ShapeDtypeStruct((SUBLANES, LANES), F32)),
        compiler_params=_cparams(("arbitrary",)),
        name="pool_mix_route",
    )(a, p, hist_src, x2d, pw, ps, wo, g2, wrh, wrl, br, tri)


def _issue_row_scatter(dest_ref, x_ref, xs_ref, sem):
    t = x_ref.shape[0]

    def issue(g, carry):
        base = pl.multiple_of(g * SUBLANES, SUBLANES)
        tile_rows = x_ref.at[pl.ds(base, SUBLANES)]
        for u in range(SUBLANES):
            pltpu.make_async_copy(tile_rows.at[pl.ds(u, 1)], xs_ref.at[pl.ds(dest_ref[base + u], 1)],
                                  sem).start(priority=u % 2)
        return carry

    lax.fori_loop(0, t // SUBLANES, issue, 0, unroll=2)
    pltpu.make_async_copy(x_ref, xs_ref.at[pl.ds(0, t)], sem).wait()


def _scatter_rows_kernel(fill_ref, dest_a_ref, dest_b_ref, xa_ref, xb_ref, xs_ref, zero_ref, sem, *, steps_a):
    step = pl.program_id(0)

    @pl.when(step == 0)
    def _():
        zero_ref[...] = jnp.zeros_like(zero_ref)
        fill_rows = zero_ref.shape[0]
        last_start = xs_ref.shape[0] - fill_rows
        for g in range(fill_ref.shape[0]):
            @pl.when(fill_ref[g] < xs_ref.shape[0])
            def _(g=g):
                start = pl.multiple_of(jnp.minimum(fill_ref[g], last_start), SUBLANES)
                fill = pltpu.make_async_copy(zero_ref, xs_ref.at[pl.ds(start, fill_rows)], sem)
                fill.start()
                fill.wait()

    @pl.when(step < steps_a)
    def _():
        _issue_row_scatter(dest_a_ref, xa_ref, xs_ref, sem)

    @pl.when(step >= steps_a)
    def _():
        _issue_row_scatter(dest_b_ref, xb_ref, xs_ref, sem)


def _scatter_rows(fill_start, dest_a, xa, dest_b, xb, m_rows, t, fill_rows):
    w = xa.shape[1]
    steps_a, steps_b = xa.shape[0] // t, xb.shape[0] // t
    in_a = lambda i: jnp.minimum(i, steps_a - 1)
    in_b = lambda i: jnp.maximum(i - steps_a, 0)
    smem = pltpu.SMEM
    return pl.pallas_call(
        functools.partial(_scatter_rows_kernel, steps_a=steps_a),
        grid=(steps_a + steps_b,),
        in_specs=[pl.BlockSpec(fill_start.shape, lambda i: (0,), memory_space=smem),
                  pl.BlockSpec((t,), lambda i: (in_a(i),), memory_space=smem),
                  pl.BlockSpec((t,), lambda i: (in_b(i),), memory_space=smem),
                  pl.BlockSpec((t, w), lambda i: (in_a(i), 0)),
                  pl.BlockSpec((t, w), lambda i: (in_b(i), 0))],
        out_specs=pl.BlockSpec(memory_space=pl.ANY),
        out_shape=jax.ShapeDtypeStruct((m_rows, w), xa.dtype),
        scratch_shapes=[pltpu.VMEM((fill_rows, w), xa.dtype), pltpu.SemaphoreType.DMA],
        compiler_params=_cparams(("arbitrary",)),
        name="moe_scatter_rows",
    )(fill_start, dest_a, dest_b, xa, xb)


def _gather_rows_kernel(dest_ref, ys_ref, y_ref, sem):
    t = y_ref.shape[0]

    def issue(g, carry):
        base = pl.multiple_of(g * SUBLANES, SUBLANES)
        tile_rows = y_ref.at[pl.ds(base, SUBLANES)]
        for u in range(SUBLANES):
            pltpu.make_async_copy(ys_ref.at[pl.ds(dest_ref[base + u], 1)], tile_rows.at[pl.ds(u, 1)],
                                  sem).start(priority=u % 2)
        return carry

    lax.fori_loop(0, t // SUBLANES, issue, 0, unroll=2)
    pltpu.make_async_copy(ys_ref.at[pl.ds(0, t)], y_ref, sem).wait()


def _gather_rows(dest, ys, t):
    n = dest.shape[0]
    w = ys.shape[1]
    return pl.pallas_call(
        _gather_rows_kernel,
        grid=(n // t,),
        in_specs=[pl.BlockSpec((t,), lambda i: (i,), memory_space=pltpu.SMEM),
                  pl.BlockSpec(memory_space=pl.ANY)],
        out_specs=pl.BlockSpec((t, w), lambda i: (i, 0)),
        out_shape=jax.ShapeDtypeStruct((n, w), ys.dtype),
        scratch_shapes=[pltpu.SemaphoreType.DMA],
        compiler_params=_cparams(("arbitrary",)),
        name="moe_gather_rows",
    )(dest, ys)


def _moe_kernel(e1_ref, e2_ref, tv_ref, xs_ref, g2_ref, wg1_ref, wu1_ref, wd1_ref, wg2_ref, wu2_ref, wd2_ref, y_ref):
    m = pl.program_id(0)

    @pl.when(tv_ref[m] > 0)
    def _():
        x1 = xs_ref[:, 0:D_MODEL]
        route = xs_ref[:, D_MODEL:]
        lane = lax.broadcasted_iota(jnp.int32, route.shape, 1)
        r = lax.rsqrt(jnp.mean(x1 * x1, axis=-1, keepdims=True) + EPS)
        n2 = ((x1 * r) * g2_ref[...]).astype(BF16)
        acc = x1
        for e_ref, wg_ref, wu_ref, wd_ref in ((e1_ref, wg1_ref, wu1_ref, wd1_ref), (e2_ref, wg2_ref, wu2_ref, wd2_ref)):
            ge = jnp.sum(jnp.where(lane == e_ref[m], route, 0.0), axis=1, keepdims=True)
            hg = jnp.dot(n2, wg_ref[...], preferred_element_type=F32)
            hu = jnp.dot(n2, wu_ref[...], preferred_element_type=F32)
            h = (hg * jax.nn.sigmoid(hg)) * hu * ge
            acc = acc + jnp.dot(h.astype(BF16), wd_ref[...], preferred_element_type=F32)
        y_ref[...] = acc

    @pl.when(tv_ref[m] == 0)
    def _():
        y_ref[...] = jnp.zeros_like(y_ref)


def _moe_sorted(tile_e1, tile_e2, tile_valid, xs, g2, wg, wu, wd, tm):
    m_pad = tile_valid.shape[0] * tm
    up = lambda e: pl.BlockSpec((None, D_MODEL, D_EXPERT), lambda m, e1, e2, tv: ((e1, e2)[e][m], 0, 0))
    down = lambda e: pl.BlockSpec((None, D_EXPERT, D_MODEL), lambda m, e1, e2, tv: ((e1, e2)[e][m], 0, 0))
    grid_spec = pltpu.PrefetchScalarGridSpec(
        num_scalar_prefetch=3,
        grid=(m_pad // tm,),
        in_specs=[
            pl.BlockSpec((tm, ROW_WIDTH), lambda m, e1, e2, tv: (m, 0)),
            pl.BlockSpec(g2.shape, lambda m, e1, e2, tv: (0, 0)),
            up(0), up(0), down(0), up(1), up(1), down(1),
        ],
        out_specs=pl.BlockSpec((tm, D_MODEL), lambda m, e1, e2, tv: (m, 0)),
    )
    return pl.pallas_call(
        _moe_kernel,
        grid_spec=grid_spec,
        out_shape=jax.ShapeDtypeStruct((m_pad, D_MODEL), F32),
        compiler_params=_cparams(("arbitrary",)),
        name="moe_experts",
    )(tile_e1, tile_e2, tile_valid, xs, g2, wg, wu, wd, wg, wu, wd)


def _moe_plan(plan_p, cnt_p, plan_s, cnt_s, tm):
    def ids(plan):
        blocks = plan.reshape(-1, SUBLANES, plan.shape[1]).astype(jnp.int32)
        return blocks[:, 0, :], blocks[:, 1, :]

    def lookup(table, cat):
        out = jnp.zeros_like(cat)
        for c in range(N_CATEGORIES):
            out = jnp.where(cat == c, table[c], out)
        return out

    (kp, rank_p), (ks, rank_s) = ids(plan_p), ids(plan_s)
    n_total = kp.size + ks.size
    n_tiles = n_total // tm + N_CATEGORIES
    cp = cnt_p[0, :N_CATEGORIES].astype(jnp.int32)
    cs = cnt_s[0, :N_CATEGORIES].astype(jnp.int32)
    total = cp + cs
    tiles = (total + tm - 1) // tm
    tile_end = jnp.cumsum(tiles)
    offset = (tile_end - tiles) * tm
    dest_p = (lookup(offset, kp) + rank_p).reshape(-1)
    dest_s = (lookup(offset + cp, ks) + rank_s).reshape(-1)
    tile_id = jnp.arange(n_tiles)
    tile_cat = jnp.minimum(jnp.sum(tile_id[:, None] >= tile_end[None, :], axis=1), N_CATEGORIES - 1)
    used = jnp.clip(offset[tile_cat] + total[tile_cat] - tile_id * tm, 0, tm)
    tile_valid = jnp.where(tile_id < tile_end[N_CATEGORIES - 1], used, 0).astype(jnp.int32)
    experts = jnp.asarray(CATEGORY_EXPERTS, jnp.int32)
    tile_e1, tile_e2 = experts[tile_cat, 0], experts[tile_cat, 1]
    m_rows = (n_tiles + 1) * tm
    ends = ((offset + total) // SUBLANES) * SUBLANES
    tail = ends[N_CATEGORIES - 1] + tm * jnp.arange(1, m_rows // tm - n_total // tm + 1)
    fill_start = jnp.concatenate([ends, tail]).astype(jnp.int32)
    return dest_p, dest_s, fill_start, tile_e1, tile_e2, tile_valid, m_rows


def kernel(x_prompt, x_sample, cache_k, cache_v, cache_logf, state_pool, norm1_g, w_in, b_f, q_norm_g, k_norm_g,
           pool_w, pool_scale, w_out, norm2_g, w_router_group, b_router_group, w_router_expert, b_router_expert,
           w_gate, w_up, w_down):
    depth = norm1_g.shape[0]
    assert depth == 1, "single-layer step"
    b_p, s_p, _ = x_prompt.shape
    assert b_p == 1, "prompt kernels assume one stream"
    b_s, t_s, _ = x_sample.shape
    p_len = cache_k.shape[2]
    l = 0

    w = w_in[l]
    a3 = 3 * ATTN_WIDTH
    w_all = jnp.concatenate(
        [w[:, :a3], w[:, a3 + N_HEADS:], jnp.pad(w[:, a3:a3 + N_HEADS], ((0, 0), (0, LANES - N_HEADS)))],
        axis=1).astype(BF16)
    bf_pad = jnp.pad(b_f[l], (0, LANES - N_HEADS)).reshape(1, LANES)
    g1 = norm1_g[l].reshape(1, D_MODEL)
    qg = jnp.tile(q_norm_g[l], N_HEADS).reshape(1, ATTN_WIDTH)
    kg = jnp.tile(k_norm_g[l], N_HEADS).reshape(1, ATTN_WIDTH)
    hr = lax.broadcasted_iota(jnp.int32, (MXU_DIM, MXU_DIM), 0) // HEAD_DIM
    hc = lax.broadcasted_iota(jnp.int32, (MXU_DIM, MXU_DIM), 1) // HEAD_DIM
    hm = jnp.where(hr == hc, 1.0 / HEAD_DIM, 0.0).astype(BF16)
    pw = pool_w[l].astype(BF16)
    ps = pool_scale[l].reshape(1, POOL_WIDTH)
    wo = w_out[l].astype(BF16)
    g2 = norm2_g[l].reshape(1, D_MODEL)
    w_r = jnp.pad(jnp.concatenate([w_router_expert[l], w_router_group[l]], axis=1),
                  ((0, 0), (0, LANES - N_EXPERTS - N_GROUPS)))
    wrh = w_r.astype(BF16)
    wrl = (w_r - wrh.astype(F32)).astype(BF16)
    br = jnp.pad(jnp.concatenate([b_router_expert[l], b_router_group[l]]),
                 (0, LANES - N_EXPERTS - N_GROUPS)).reshape(1, LANES)
    wg = w_gate[l].astype(BF16)
    wu = w_up[l].astype(BF16)
    wd = w_down[l].astype(BF16)

    tile = 512
    query_tile = 512
    key_tile = 256
    expert_tile = 256
    move_tile = 1024

    xp = x_prompt.reshape(s_p, D_MODEL)
    q, kf, _, vf, _, ka, vt, pin, logf, _, edge = _in_project(xp, tile, tile, True, key_tile, g1, w_all, bf_pad, qg, kg,
                                                              hm)
    blocks_per_tile = tile // key_tile
    edge = edge.reshape(s_p // tile, SUBLANES, LANES)[:, :2 * blocks_per_tile, :N_HEADS]
    edge = edge.reshape(s_p // key_tile, 2, N_HEADS)
    c_blocks = edge[:, 0, :]
    j0 = _first_key_blocks(c_blocks, edge[:, 1, :], q_norm_g[l], k_norm_g[l], query_tile, key_tile)
    a = _attention_prompt(j0, c_blocks, q, ka, vt, query_tile, key_tile)
    hist_map = lambda i: (jnp.maximum(i * (tile // HIST_ROWS) - 1, 0), 0)
    xg_p, plan_p, cnt_p = _mix(a, pin, pin, hist_map, xp, tile, tile, 0, tile, True, pw, ps, wo, g2, wrh, wrl, br)
    k_prompt = kf.reshape(depth, b_p, s_p, N_HEADS, HEAD_DIM)
    v_prompt = vf.reshape(depth, b_p, s_p, N_HEADS, HEAD_DIM)
    logf_prompt = logf[:, :N_HEADS].reshape(depth, b_p, s_p, N_HEADS)
    pool_prompt = pin[s_p - POOL_PAD:].reshape(depth, b_p, POOL_PAD, POOL_WIDTH)

    n_s = b_s * t_s
    xs = x_sample.reshape(n_s, D_MODEL)
    q, kf, kb, vf, vb, _, _, pin, logf, d, _ = _in_project(xs, tile, t_s, False, key_tile, g1, w_all, bf_pad, qg, kg, hm)
    dt4 = d[:, :N_HEADS].reshape(b_s, t_s, N_PAIRS, 2).transpose(0, 2, 3, 1)
    clf = cache_logf[l].astype(F32).transpose(0, 2, 1).reshape(b_s * N_HEADS, p_len)
    r4 = _suffix_sums(clf).reshape(b_s, N_PAIRS, 2, p_len)
    ck = cache_k[l].reshape(b_s, p_len, ATTN_WIDTH).astype(BF16)
    cv = cache_v[l].reshape(b_s, p_len, ATTN_WIDTH).astype(BF16)
    a = _attention_sample(q, kb, vb, ck, cv, d, dt4, r4, t_s)
    hist_s = jnp.pad(state_pool[l], ((0, 0), (HIST_ROWS - POOL_PAD, 0), (0, 0))).reshape(b_s * HIST_ROWS, POOL_WIDTH)
    xg_s, plan_s, cnt_s = _mix(a, pin, hist_s, lambda i: (i, 0), xs, tile, t_s, p_len, 0, False, pw, ps, wo, g2, wrh,
                               wrl, br)
    k_sample = kf.reshape(depth, b_s, t_s, N_HEADS, HEAD_DIM)
    v_sample = vf.reshape(depth, b_s, t_s, N_HEADS, HEAD_DIM)
    logf_sample = logf[:, :N_HEADS].reshape(depth, b_s, t_s, N_HEADS)
    pool_sample = pin.reshape(b_s, t_s, POOL_WIDTH)[:, t_s - POOL_PAD:].reshape(depth, b_s, POOL_PAD, POOL_WIDTH)

    dest_p, dest_s, fill_start, tile_e1, tile_e2, tile_valid, m_rows = _moe_plan(plan_p, cnt_p, plan_s, cnt_s,
                                                                              expert_tile)
    rows = _scatter_rows(fill_start, dest_p, xg_p, dest_s, xg_s, m_rows, move_tile, expert_tile)
    ys = _moe_sorted(tile_e1, tile_e2, tile_valid, rows, g2, wg, wu, wd, expert_tile)
    y_prompt = _gather_rows(dest_p, ys, move_tile).reshape(b_p, s_p, D_MODEL)
    y_sample = _gather_rows(dest_s, ys, move_tile).reshape(b_s, t_s, D_MODEL)

    return (y_prompt, y_sample, k_prompt, v_prompt, logf_prompt, pool_prompt,
            k_sample, v_sample, logf_sample, pool_sample)
```

```python
import functools

import jax
import jax.numpy as jnp
from jax import lax
from jax.experimental import pallas as pl
from jax.experimental.pallas import tpu as pltpu

F32 = jnp.float32
BF16 = jnp.bfloat16

D_MODEL = 1024
ATTN_WIDTH = 512
N_HEADS = 8
HEAD_DIM = 64
POOL_WIDTH = 512
POOL_WINDOWS = (2, 4, 8, 16)
POOL_GROUP_DIM = 128
POOL_PAD = 15
HIST_ROWS = 16
N_GROUPS = 4
EXPERTS_PER_GROUP = 4
N_EXPERTS = 16
D_EXPERT = 512
EPS = 1e-6
F32_EXP_UNDERFLOW = 104.0
BF16_SLACK = 1.02
SKIP_MARGIN = 1.0
LANES = 128
SUBLANES = 8
MXU_DIM = 256
PAIR = 2 * HEAD_DIM
N_PAIRS = N_HEADS // 2
ROWSUM_ROW = (HEAD_DIM, 0)
VMEM_LIMIT = 56 * 1024 * 1024
ROW_WIDTH = D_MODEL + LANES
PAIRS_PER_GROUP = 6
N_CATEGORIES = N_GROUPS * PAIRS_PER_GROUP
PAIR_ORDER = ((0, 1), (0, 2), (1, 2), (1, 3), (2, 3), (0, 3))
CATEGORY_EXPERTS = tuple((4 * g + a, 4 * g + b) for g in range(N_GROUPS) for a, b in PAIR_ORDER)
ROUTE_CATEGORY_LANE = N_EXPERTS
ROUTE_RANK_LANE = N_EXPERTS + 1


def _cparams(sem):
    return pltpu.CompilerParams(dimension_semantics=sem, vmem_limit_bytes=VMEM_LIMIT)


def _split3(x):
    hi = x.astype(BF16)
    r1 = x - hi.astype(F32)
    mid = r1.astype(BF16)
    lo = (r1 - mid.astype(F32)).astype(BF16)
    return hi, mid, lo


def _split2(x):
    hi = x.astype(BF16)
    lo = (x - hi.astype(F32)).astype(BF16)
    return hi, lo


def _inproj_kernel(x_ref, g1_ref, w_ref, bf_ref, qg_ref, kg_ref, hm_ref, tri_ref, place_ref,
                   q_ref, kf_ref, kb_ref, vf_ref, vb_ref, ka_ref, vt_ref, p_ref, logf_ref, c_ref, edge_ref, carry_ref, *,
                   carry_rows, key_blk):
    x = x_ref[...]
    r = lax.rsqrt(jnp.mean(x * x, axis=-1, keepdims=True) + EPS)
    n = ((x * r) * g1_ref[...]).astype(BF16)
    z = jnp.dot(n, w_ref[...], preferred_element_type=F32)

    hm = hm_ref[...]

    def head_norm(zz, g):
        hi, lo = _split2(zz * zz)
        half = hm.shape[0]
        ms = jnp.concatenate(
            [jnp.dot(hi[:, r:r + half], hm, preferred_element_type=F32)
             + jnp.dot(lo[:, r:r + half], hm, preferred_element_type=F32) for r in range(0, zz.shape[1], half)], axis=1)
        return (zz * lax.rsqrt(ms + EPS)) * g

    q = head_norm(z[:, 0:ATTN_WIDTH], qg_ref[...])
    q_ref[...] = (q * (HEAD_DIM ** -0.5)).astype(BF16)
    k = head_norm(z[:, ATTN_WIDTH:2 * ATTN_WIDTH], kg_ref[...])
    kf_ref[...] = k
    kb_ref[...] = k.astype(BF16)
    v = z[:, 2 * ATTN_WIDTH:3 * ATTN_WIDTH]
    vf_ref[...] = v
    vb_ref[...] = v.astype(BF16)
    p_ref[...] = z[:, 3 * ATTN_WIDTH:3 * ATTN_WIDTH + POOL_WIDTH]
    prow = lax.broadcasted_iota(jnp.int32, (PAIR, v.shape[0]), 0)
    for hp in range(N_PAIRS):
        vpair_t = v[:, hp * PAIR:(hp + 1) * PAIR].T
        for h in range(2):
            own = (prow < HEAD_DIM) if h == 0 else (prow >= HEAD_DIM)
            marker = jnp.where(prow == ROWSUM_ROW[h], 1.0, 0.0)
            vt_ref[(2 * hp + h) * LANES:(2 * hp + h + 1) * LANES, :] = jnp.where(own, vpair_t, marker).astype(BF16)

    f = z[:, 3 * ATTN_WIDTH + POOL_WIDTH:] + bf_ref[...]
    lane = lax.broadcasted_iota(jnp.int32, f.shape, 1)
    logf = -(jnp.maximum(-f, 0.0) + jnp.log1p(jnp.exp(-jnp.abs(f))))
    logf = jnp.where(lane < N_HEADS, logf, 0.0)
    logf_ref[...] = logf

    tri = tri_ref[...]
    blk = tri.shape[0]
    parts = _split3(logf)
    pieces = []
    for r in range(0, logf.shape[0], blk):
        piece = sum(jnp.dot(tri, part[r:r + blk, :], preferred_element_type=F32) for part in parts)
        if carry_rows and pieces:
            piece = piece + pieces[-1][blk - 1:blk, :]
        pieces.append(piece)
    c = jnp.concatenate(pieces, axis=0)
    if carry_rows:
        @pl.when(pl.program_id(0) == 0)
        def _():
            carry_ref[...] = jnp.zeros_like(carry_ref)
        c = c + carry_ref[0:1, :]
        carry_ref[...] = jnp.broadcast_to(c[c.shape[0] - 1:, :], carry_ref.shape)
    c_ref[...] = c
    edges = [c[r:r + 1, :] for b in range(c.shape[0] // key_blk) for r in (b * key_blk, (b + 1) * key_blk - 1)]
    edge_ref[...] = jnp.concatenate(edges + [jnp.zeros((SUBLANES - len(edges), LANES), F32)], axis=0)

    tm = c.shape[0]
    rel = jnp.concatenate([c[r:r + key_blk, :] - c[r:r + 1, :] for r in range(0, tm, key_blk)], axis=0)
    parts = jnp.concatenate(_split3(rel), axis=1)
    decay = jnp.dot(parts, place_ref[...], preferred_element_type=F32).astype(BF16)
    kb = k.astype(BF16)
    for hp in range(N_PAIRS):
        ka_ref[:, 2 * hp * LANES:(2 * hp + 1) * LANES] = kb[:, hp * PAIR:(hp + 1) * PAIR]
        ka_ref[:, (2 * hp + 1) * LANES:(2 * hp + 2) * LANES] = decay[:, hp * LANES:(hp + 1) * LANES]


def _decay_lane(h, part):
    return 3 * h + part


def _in_project(x2d, tm, seg, carry_rows, key_blk, g1, w_all, bf_pad, qg, kg, hm):
    n = x2d.shape[0]
    blk = min(tm, MXU_DIM)
    assert seg % blk == 0 or blk % seg == 0, "running-sum segments must nest with the triangular block"
    rows = lax.broadcasted_iota(jnp.int32, (blk, blk), 0)
    cols = lax.broadcasted_iota(jnp.int32, (blk, blk), 1)
    tri = ((cols <= rows) & (rows // seg == cols // seg)).astype(BF16)
    src = jnp.arange(3 * LANES)
    part, head = src // LANES, src % LANES
    dst = (head // 2) * LANES + _decay_lane(head % 2, part)
    place = ((jnp.arange(N_PAIRS * LANES)[None, :] == dst[:, None]) & (head < N_HEADS)[:, None]).astype(BF16)
    row_blk = lambda w: pl.BlockSpec((tm, w), lambda i: (i, 0))
    const = lambda a: pl.BlockSpec(a.shape, lambda i: (0,) * a.ndim)
    out_shape = (
        jax.ShapeDtypeStruct((n, ATTN_WIDTH), BF16),
        jax.ShapeDtypeStruct((n, ATTN_WIDTH), F32),
        jax.ShapeDtypeStruct((n, ATTN_WIDTH), BF16),
        jax.ShapeDtypeStruct((n, ATTN_WIDTH), F32),
        jax.ShapeDtypeStruct((n, ATTN_WIDTH), BF16),
        jax.ShapeDtypeStruct((n, N_PAIRS * 2 * LANES), BF16),
        jax.ShapeDtypeStruct((N_HEADS * LANES, n), BF16),
        jax.ShapeDtypeStruct((n, POOL_WIDTH), F32),
        jax.ShapeDtypeStruct((n, LANES), F32),
        jax.ShapeDtypeStruct((n, LANES), F32),
        jax.ShapeDtypeStruct((n // tm * SUBLANES, LANES), F32),
    )
    return pl.pallas_call(
        functools.partial(_inproj_kernel, carry_rows=carry_rows, key_blk=key_blk),
        grid=(n // tm,),
        in_specs=[row_blk(D_MODEL), const(g1), const(w_all), const(bf_pad), const(qg), const(kg), const(hm),
                  const(tri), const(place)],
        out_specs=[row_blk(ATTN_WIDTH)] * 5 + [row_blk(N_PAIRS * 2 * LANES),
                                                pl.BlockSpec((N_HEADS * LANES, tm), lambda i: (0, i)),
                                                row_blk(POOL_WIDTH), row_blk(LANES), row_blk(LANES),
                                                pl.BlockSpec((SUBLANES, LANES), lambda i: (i, 0))],
        out_shape=out_shape,
        scratch_shapes=[pltpu.VMEM((8, LANES), F32)],
        compiler_params=_cparams(("arbitrary",)),
        name="in_project",
    )(x2d, g1, w_all, bf_pad, qg, kg, hm, tri, place)


def _attn_prompt_kernel(j0_ref, cb_ref, q_ref, k_ref, vt_ref, o_ref, acc_ref, m_ref, *, tq, tk):
    hp = pl.program_id(0)
    i = pl.program_id(1)
    nk = tq // tk
    lane = lax.broadcasted_iota(jnp.int32, (1, PAIR), 1)
    first = lane < HEAD_DIM
    q = q_ref[...]
    zero = jnp.zeros_like(q)
    qw = []
    for h in range(2):
        minus = jnp.zeros((1, LANES), F32)
        for part in range(3):
            minus = jnp.where(lane == _decay_lane(h, part), -1.0, minus)
        qw.append(jnp.concatenate([jnp.where(first == (h == 0), q, zero),
                                   jnp.broadcast_to(minus, (tq, LANES)).astype(BF16)], axis=1))
    q_start = i * tq

    acc_ref[...] = jnp.zeros_like(acc_ref)
    m_ref[...] = jnp.full_like(m_ref, -jnp.inf)

    last = (i + 1) * nk - 1

    def scores(h, j, masked=True):
        start = pl.multiple_of(j * tk, tk)
        k = k_ref[pl.ds(start, tk), :]
        s = lax.dot_general(k, qw[h], (((1,), (1,)), ((), ())), preferred_element_type=F32)
        if masked:
            key = lax.broadcasted_iota(jnp.int32, s.shape, 0)
            qry = lax.broadcasted_iota(jnp.int32, s.shape, 1)
            s = jnp.where(key - qry <= q_start - j * tk, s, -jnp.inf)
        return s, jnp.max(s, axis=0, keepdims=True)

    def update(h, j, s, colmax):
        shift = cb_ref[j, 2 * hp + h] - cb_ref[i * nk, 2 * hp + h]
        start = pl.multiple_of(j * tk, tk)
        vt = vt_ref[h * LANES:(h + 1) * LANES, pl.ds(start, tk)]
        m_prev = m_ref[h]
        m_new = jnp.maximum(m_prev, colmax - shift)
        alpha = jnp.exp(m_prev - m_new)
        p = jnp.exp(s - (m_new + shift))
        m_ref[h] = m_new
        acc_ref[h] = acc_ref[h] * alpha + jnp.dot(vt, p.astype(BF16), preferred_element_type=F32)

    j_first = [j0_ref[i, 2 * hp + h] for h in range(2)]
    j_both = jnp.maximum(j_first[0], j_first[1])
    start = [scores(h, j_first[h]) for h in range(2)]
    ahead = []
    for h in range(2):
        def body_one(j, carry, h=h):
            nxt = scores(h, j + 1)
            update(h, j, *carry)
            return nxt

        ahead.append(lax.fori_loop(j_first[h], j_both, body_one, start[h]))

    def body_both(j, carry, masked):
        nxt = (scores(0, j + 1, masked), scores(1, j + 1, masked))
        update(0, j, *carry[0])
        update(1, j, *carry[1])
        return nxt

    j_mask = jnp.maximum(j_both, i * nk - 1)
    pairs = (j_mask - j_both) // 2

    def body_two(t, carry):
        j = j_both + 2 * t
        return body_both(j + 1, body_both(j, carry, False), False)

    carry = lax.fori_loop(0, pairs, body_two, (ahead[0], ahead[1]))
    carry = lax.fori_loop(j_both + 2 * pairs, last, functools.partial(body_both, masked=True), carry)
    update(0, last, *carry[0])
    update(1, last, *carry[1])

    out = [acc_ref[h] / acc_ref[h][ROWSUM_ROW[h]:ROWSUM_ROW[h] + 1, :] for h in range(2)]
    row = lax.broadcasted_iota(jnp.int32, (PAIR, 1), 0)
    o_ref[...] = jnp.where(row < HEAD_DIM, out[0], out[1]).T.astype(o_ref.dtype)


def _attention_prompt(j0, c_blocks, q, ka, vt, tq, tk):
    n = q.shape[0]
    grid_spec = pltpu.PrefetchScalarGridSpec(
        num_scalar_prefetch=2,
        grid=(N_PAIRS, n // tq),
        in_specs=[
            pl.BlockSpec((tq, PAIR), lambda hp, i, j0, cb: (i, hp)),
            pl.BlockSpec((n, 2 * LANES), lambda hp, i, j0, cb: (0, hp)),
            pl.BlockSpec((2 * LANES, n), lambda hp, i, j0, cb: (hp, 0)),
        ],
        out_specs=pl.BlockSpec((tq, PAIR), lambda hp, i, j0, cb: (i, hp)),
        scratch_shapes=[pltpu.VMEM((2, LANES, tq), F32), pltpu.VMEM((2, 1, tq), F32)],
    )
    return pl.pallas_call(
        functools.partial(_attn_prompt_kernel, tq=tq, tk=tk),
        grid_spec=grid_spec,
        out_shape=jax.ShapeDtypeStruct((n, ATTN_WIDTH), BF16),
        compiler_params=_cparams(("arbitrary", "arbitrary")),
        name="attention_prompt",
    )(j0, c_blocks, q, ka, vt)


def _first_key_blocks(c_first, c_last, q_gain, k_gain, tq, tk):
    n_k = c_first.shape[0]
    n_q = n_k * tk // tq
    qk_bound = BF16_SLACK * HEAD_DIM ** 0.5 * jnp.max(jnp.abs(q_gain)) * jnp.max(jnp.abs(k_gain))
    threshold = F32_EXP_UNDERFLOW + 2.0 * qk_bound + SKIP_MARGIN
    c_start = c_first[0::tq // tk]
    gap = c_start[:, None, :] - c_last[None, :, :]
    earlier = (jnp.arange(n_k)[None, :, None] + 1) * tk <= jnp.arange(n_q)[:, None, None] * tq
    return jnp.sum((gap < -threshold) & earlier, axis=1).astype(jnp.int32)


def _suffix_kernel(x_ref, o_ref):
    x = x_ref[...]
    p_len = x.shape[1]
    rows = lax.broadcasted_iota(jnp.int32, (p_len, p_len), 0)
    cols = lax.broadcasted_iota(jnp.int32, (p_len, p_len), 1)
    u = (rows > cols).astype(BF16)
    hi, mid, lo = _split3(x)
    o_ref[...] = (jnp.dot(hi, u, preferred_element_type=F32) + jnp.dot(mid, u, preferred_element_type=F32)
                  + jnp.dot(lo, u, preferred_element_type=F32))


def _suffix_sums(x):
    return pl.pallas_call(
        _suffix_kernel,
        out_shape=jax.ShapeDtypeStruct(x.shape, F32),
        compiler_params=pltpu.CompilerParams(vmem_limit_bytes=VMEM_LIMIT),
        name="cache_suffix_sums",
    )(x)


def _attn_sample_kernel(q_ref, kn_ref, vn_ref, ck_ref, cv_ref, d_ref, dt_ref, r_ref, o_ref):
    lane = lax.broadcasted_iota(jnp.int32, (1, PAIR), 1)
    first = lane < HEAD_DIM
    dblk = d_ref[...]
    dlane = lax.broadcasted_iota(jnp.int32, dblk.shape, 1)
    nt = (((1,), (1,)), ((), ()))
    for hp in range(N_PAIRS):
        cols = slice(hp * PAIR, (hp + 1) * PAIR)
        q = q_ref[:, cols]
        zero = jnp.zeros_like(q)
        qh = (jnp.where(first, q, zero), jnp.where(first, zero, q))
        kn = kn_ref[:, cols]
        vn = vn_ref[:, cols]
        ck = ck_ref[:, cols]
        cv = cv_ref[:, cols]
        outs, ls = [], []
        for h in range(2):
            dq = jnp.sum(jnp.where(dlane == 2 * hp + h, dblk, 0.0), axis=1, keepdims=True)
            s1 = lax.dot_general(qh[h], ck, nt, preferred_element_type=F32) + dq + r_ref[hp, h:h + 1, :]
            s2 = lax.dot_general(qh[h], kn, nt, preferred_element_type=F32) + dq - dt_ref[hp, h:h + 1, :]
            row = lax.broadcasted_iota(jnp.int32, s2.shape, 0)
            col = lax.broadcasted_iota(jnp.int32, s2.shape, 1)
            s2 = jnp.where(col <= row, s2, -jnp.inf)
            m = jnp.maximum(jnp.max(s1, axis=1, keepdims=True), jnp.max(s2, axis=1, keepdims=True))
            p1 = jnp.exp(s1 - m)
            p2 = jnp.exp(s2 - m)
            ls.append(jnp.sum(p1, axis=1, keepdims=True) + jnp.sum(p2, axis=1, keepdims=True))
            outs.append(jnp.dot(p1.astype(BF16), cv, preferred_element_type=F32)
                        + jnp.dot(p2.astype(BF16), vn, preferred_element_type=F32))
        o_ref[:, cols] = (jnp.where(first, outs[0], outs[1]) / jnp.where(first, ls[0], ls[1])).astype(o_ref.dtype)


def _attention_sample(q, kb, vb, cache_k, cache_v, d, dt4, r4, t):
    nb, p_len = cache_k.shape[0], cache_k.shape[1]
    row_blk = pl.BlockSpec((t, ATTN_WIDTH), lambda b: (b, 0))
    cache_blk = pl.BlockSpec((None, p_len, ATTN_WIDTH), lambda b: (b, 0, 0))
    return pl.pallas_call(
        _attn_sample_kernel,
        grid=(nb,),
        in_specs=[row_blk, row_blk, row_blk, cache_blk, cache_blk,
                  pl.BlockSpec((t, LANES), lambda b: (b, 0)),
                  pl.BlockSpec((None, N_PAIRS, 2, t), lambda b: (b, 0, 0, 0)),
                  pl.BlockSpec((None, N_PAIRS, 2, p_len), lambda b: (b, 0, 0, 0))],
        out_specs=row_blk,
        out_shape=jax.ShapeDtypeStruct((nb * t, ATTN_WIDTH), BF16),
        compiler_params=_cparams(("arbitrary",)),
        name="attention_sample",
    )(q, kb, vb, cache_k, cache_v, d, dt4, r4)


def _mix_kernel(a_ref, p_ref, hist_ref, x_ref, pw_ref, ps_ref, wo_ref, g2_ref, wrh_ref, wrl_ref, br_ref, tri_ref,
                xg_ref, plan_ref, cnt_ref, *, seg, pos0, pos_stride, zero_first):
    i = pl.program_id(0)
    t = p_ref.shape[0]
    p = p_ref[...]
    hist = hist_ref[...]
    if zero_first:
        hist = jnp.where(i == 0, 0.0, hist)
    ext = seg + HIST_ROWS
    xh = jnp.concatenate([piece for b in range(t // seg)
                          for piece in (hist[b * HIST_ROWS:(b + 1) * HIST_ROWS, :], p[b * seg:(b + 1) * seg, :])], axis=0)

    def own_rows(v):
        return jnp.concatenate([v[b * ext + HIST_ROWS:(b + 1) * ext, :] for b in range(t // seg)], axis=0)

    pos = pos0 + i * pos_stride + lax.broadcasted_iota(jnp.int32, (t, 1), 0) % seg
    ys = []
    for g, w in enumerate(POOL_WINDOWS):
        col = xh[:, g * POOL_GROUP_DIM:(g + 1) * POOL_GROUP_DIM]
        acc = col
        span = 1
        while span < w:
            acc = acc + pltpu.roll(acc, span, axis=0)
            span *= 2
        cnt = jnp.minimum(pos + 1, w).astype(F32)
        dgrp = own_rows(acc) / cnt - own_rows(col)
        ys.append(jnp.dot(dgrp.astype(BF16), pw_ref[g], preferred_element_type=F32))
    pm = (jnp.concatenate(ys, axis=1) * ps_ref[...]).astype(BF16)
    mix = (jnp.dot(a_ref[...], wo_ref[0:ATTN_WIDTH, :], preferred_element_type=F32)
           + jnp.dot(pm, wo_ref[ATTN_WIDTH:, :], preferred_element_type=F32))
    x1 = x_ref[...] + mix
    xg_ref[:, 0:D_MODEL] = x1

    r = lax.rsqrt(jnp.mean(x1 * x1, axis=-1, keepdims=True) + EPS)
    n2 = (x1 * r) * g2_ref[...]

    nh, nl = _split2(n2)
    wrh = wrh_ref[...]
    logits = (jnp.dot(nh, wrh, preferred_element_type=F32) + jnp.dot(nl, wrh, preferred_element_type=F32)
              + jnp.dot(nh, wrl_ref[...], preferred_element_type=F32)) + br_ref[...]
    lane = lax.broadcasted_iota(jnp.int32, logits.shape, 1).astype(F32)
    neg = -jnp.inf
    big = jnp.float32(1 << 20)
    is_g = (lane >= N_EXPERTS) & (lane < N_EXPERTS + N_GROUPS)
    lg = jnp.where(is_g, logits, neg)
    gmax = jnp.max(lg, axis=1, keepdims=True)
    g_lane = jnp.min(jnp.where(lg == gmax, lane, big), axis=1, keepdims=True)
    top_pg = 1.0 / jnp.sum(jnp.exp(lg - gmax), axis=1, keepdims=True)
    g_idx = g_lane - N_EXPERTS
    sel = (lane >= g_idx * EXPERTS_PER_GROUP) & (lane < (g_idx + 1) * EXPERTS_PER_GROUP)
    le = jnp.where(sel, logits, neg)
    m1 = jnp.max(le, axis=1, keepdims=True)
    i1 = jnp.min(jnp.where(le == m1, lane, big), axis=1, keepdims=True)
    le2 = jnp.where(lane == i1, neg, le)
    m2 = jnp.max(le2, axis=1, keepdims=True)
    i2 = jnp.min(jnp.where(le2 == m2, lane, big), axis=1, keepdims=True)
    e2 = jnp.exp(m2 - m1)
    w1 = top_pg / (1.0 + e2)
    w2 = top_pg * e2 / (1.0 + e2)
    gate = jnp.where(lane == i1, w1, 0.0) + jnp.where(lane == i2, w2, 0.0)

    @pl.when(i == 0)
    def _():
        cnt_ref[...] = jnp.zeros_like(cnt_ref)
    e_lo = jnp.minimum(i1, i2) - g_idx * EXPERTS_PER_GROUP
    e_hi = jnp.maximum(i1, i2) - g_idx * EXPERTS_PER_GROUP
    pair = jnp.zeros_like(e_lo)
    for idx, (a, b) in enumerate(PAIR_ORDER):
        pair = jnp.where((e_lo == a) & (e_hi == b), float(idx), pair)
    category = g_idx * PAIRS_PER_GROUP + pair
    member = lane == category
    tri = tri_ref[...]
    blk = tri.shape[0]
    seen = cnt_ref[0:1, :]
    pieces = []
    for r in range(0, t, blk):
        mem = member[r:r + blk, :].astype(BF16)
        pieces.append(jnp.dot(tri, mem, preferred_element_type=F32) + seen)
        seen = seen + jnp.sum(mem.astype(F32), axis=0, keepdims=True)
    before = jnp.concatenate(pieces, axis=0)
    rank = jnp.sum(jnp.where(member, before, 0.0), axis=1, keepdims=True)
    cnt_ref[...] = jnp.broadcast_to(seen, cnt_ref.shape)
    route = jnp.where(lane == ROUTE_CATEGORY_LANE, category, jnp.where(lane == ROUTE_RANK_LANE, rank, gate))
    xg_ref[:, D_MODEL:] = route
    srow = lax.broadcasted_iota(jnp.int32, (SUBLANES, LANES), 0)
    slane = lax.broadcasted_iota(jnp.int32, (SUBLANES, LANES), 1)
    pick = (((srow == 0) & (slane == ROUTE_CATEGORY_LANE)) | ((srow == 1) & (slane == ROUTE_RANK_LANE))).astype(BF16)
    nt = (((1,), (1,)), ((), ()))
    ids = jnp.where(lane >= N_EXPERTS, route, 0.0)
    hi, lo = _split2(ids)
    plan_ref[...] = (lax.dot_general(pick, hi, nt, preferred_element_type=F32)
                     + lax.dot_general(pick, lo, nt, preferred_element_type=F32))


def _mix(a, p, hist_src, hist_map, x2d, t, seg, pos0, pos_stride, zero_first, pw, ps, wo, g2, wrh, wrl, br):
    n = x2d.shape[0]
    blk = min(t, MXU_DIM)
    rows = lax.broadcasted_iota(jnp.int32, (blk, blk), 0)
    cols = lax.broadcasted_iota(jnp.int32, (blk, blk), 1)
    tri = (cols < rows).astype(BF16)
    row_blk = lambda w: pl.BlockSpec((t, w), lambda i: (i, 0))
    const = lambda arr: pl.BlockSpec(arr.shape, lambda i: (0,) * arr.ndim)
    return pl.pallas_call(
        functools.partial(_mix_kernel, seg=seg, pos0=pos0, pos_stride=pos_stride, zero_first=zero_first),
        grid=(n // t,),
        in_specs=[row_blk(ATTN_WIDTH), row_blk(POOL_WIDTH),
                  pl.BlockSpec((t // seg * HIST_ROWS, POOL_WIDTH), hist_map),
                  row_blk(D_MODEL), const(pw), const(ps), const(wo), const(g2), const(wrh), const(wrl), const(br),
                  const(tri)],
        out_specs=[row_blk(ROW_WIDTH), pl.BlockSpec((SUBLANES, t), lambda i: (i, 0)),
                   pl.BlockSpec((SUBLANES, LANES), lambda i: (0, 0))],
        out_shape=(jax.ShapeDtypeStruct((n, ROW_WIDTH), F32), jax.ShapeDtypeStruct((n // t * SUBLANES, t), F32),
                   jax.ShapeDtypeStruct((SUBLANES, LANES), F32)),
        compiler_params=_cparams(("arbitrary",)),
        name="pool_mix_route",
    )(a, p, hist_src, x2d, pw, ps, wo, g2, wrh, wrl, br, tri)


def _issue_row_scatter(dest_ref, x_ref, xs_ref, sem):
    t = x_ref.shape[0]

    def issue(g, carry):
        base = pl.multiple_of(g * SUBLANES, SUBLANES)
        tile_rows = x_ref.at[pl.ds(base, SUBLANES)]
        for u in range(SUBLANES):
            pltpu.make_async_copy(tile_rows.at[pl.ds(u, 1)], xs_ref.at[pl.ds(dest_ref[base + u], 1)],
                                  sem).start(priority=u % 2)
        return carry

    lax.fori_loop(0, t // SUBLANES, issue, 0, unroll=2)
    pltpu.make_async_copy(x_ref, xs_ref.at[pl.ds(0, t)], sem).wait()


def _scatter_rows_kernel(fill_ref, dest_a_ref, dest_b_ref, xa_ref, xb_ref, xs_ref, zero_ref, sem, *, steps_a):
    step = pl.program_id(0)

    @pl.when(step == 0)
    def _():
        zero_ref[...] = jnp.zeros_like(zero_ref)
        fill_rows = zero_ref.shape[0]
        last_start = xs_ref.shape[0] - fill_rows
        for g in range(fill_ref.shape[0]):
            @pl.when(fill_ref[g] < xs_ref.shape[0])
            def _(g=g):
                start = pl.multiple_of(jnp.minimum(fill_ref[g], last_start), SUBLANES)
                fill = pltpu.make_async_copy(zero_ref, xs_ref.at[pl.ds(start, fill_rows)], sem)
                fill.start()
                fill.wait()

    @pl.when(step < steps_a)
    def _():
        _issue_row_scatter(dest_a_ref, xa_ref, xs_ref, sem)

    @pl.when(step >= steps_a)
    def _():
        _issue_row_scatter(dest_b_ref, xb_ref, xs_ref, sem)


def _scatter_rows(fill_start, dest_a, xa, dest_b, xb, m_rows, t, fill_rows):
    w = xa.shape[1]
    steps_a, steps_b = xa.shape[0] // t, xb.shape[0] // t
    in_a = lambda i: jnp.minimum(i, steps_a - 1)
    in_b = lambda i: jnp.maximum(i - steps_a, 0)
    smem = pltpu.SMEM
    return pl.pallas_call(
        functools.partial(_scatter_rows_kernel, steps_a=steps_a),
        grid=(steps_a + steps_b,),
        in_specs=[pl.BlockSpec(fill_start.shape, lambda i: (0,), memory_space=smem),
                  pl.BlockSpec((t,), lambda i: (in_a(i),), memory_space=smem),
                  pl.BlockSpec((t,), lambda i: (in_b(i),), memory_space=smem),
                  pl.BlockSpec((t, w), lambda i: (in_a(i), 0)),
                  pl.BlockSpec((t, w), lambda i: (in_b(i), 0))],
        out_specs=pl.BlockSpec(memory_space=pl.ANY),
        out_shape=jax.ShapeDtypeStruct((m_rows, w), xa.dtype),
        scratch_shapes=[pltpu.VMEM((fill_rows, w), xa.dtype), pltpu.SemaphoreType.DMA],
        compiler_params=_cparams(("arbitrary",)),
        name="moe_scatter_rows",
    )(fill_start, dest_a, dest_b, xa, xb)


def _gather_rows_kernel(dest_ref, ys_ref, y_ref, sem):
    t = y_ref.shape[0]

    def issue(g, carry):
        base = pl.multiple_of(g * SUBLANES, SUBLANES)
        tile_rows = y_ref.at[pl.ds(base, SUBLANES)]
        for u in range(SUBLANES):
            pltpu.make_async_copy(ys_ref.at[pl.ds(dest_ref[base + u], 1)], tile_rows.at[pl.ds(u, 1)],
                                  sem).start(priority=u % 2)
        return carry

    lax.fori_loop(0, t // SUBLANES, issue, 0, unroll=2)
    pltpu.make_async_copy(ys_ref.at[pl.ds(0, t)], y_ref, sem).wait()


def _gather_rows(dest, ys, t):
    n = dest.shape[0]
    w = ys.shape[1]
    return pl.pallas_call(
        _gather_rows_kernel,
        grid=(n // t,),
        in_specs=[pl.BlockSpec((t,), lambda i: (i,), memory_space=pltpu.SMEM),
                  pl.BlockSpec(memory_space=pl.ANY)],
        out_specs=pl.BlockSpec((t, w), lambda i: (i, 0)),
        out_shape=jax.ShapeDtypeStruct((n, w), ys.dtype),
        scratch_shapes=[pltpu.SemaphoreType.DMA],
        compiler_params=_cparams(("arbitrary",)),
        name="moe_gather_rows",
    )(dest, ys)


def _moe_kernel(e1_ref, e2_ref, tv_ref, xs_ref, g2_ref, wg1_ref, wu1_ref, wd1_ref, wg2_ref, wu2_ref, wd2_ref, y_ref):
    m = pl.program_id(0)

    @pl.when(tv_ref[m] > 0)
    def _():
        x1 = xs_ref[:, 0:D_MODEL]
        route = xs_ref[:, D_MODEL:]
        lane = lax.broadcasted_iota(jnp.int32, route.shape, 1)
        r = lax.rsqrt(jnp.mean(x1 * x1, axis=-1, keepdims=True) + EPS)
        n2 = ((x1 * r) * g2_ref[...]).astype(BF16)
        acc = x1
        for e_ref, wg_ref, wu_ref, wd_ref in ((e1_ref, wg1_ref, wu1_ref, wd1_ref), (e2_ref, wg2_ref, wu2_ref, wd2_ref)):
            ge = jnp.sum(jnp.where(lane == e_ref[m], route, 0.0), axis=1, keepdims=True)
            hg = jnp.dot(n2, wg_ref[...], preferred_element_type=F32)
            hu = jnp.dot(n2, wu_ref[...], preferred_element_type=F32)
            h = (hg * jax.nn.sigmoid(hg)) * hu * ge
            acc = acc + jnp.dot(h.astype(BF16), wd_ref[...], preferred_element_type=F32)
        y_ref[...] = acc

    @pl.when(tv_ref[m] == 0)
    def _():
        y_ref[...] = jnp.zeros_like(y_ref)


def _moe_sorted(tile_e1, tile_e2, tile_valid, xs, g2, wg, wu, wd, tm):
    m_pad = tile_valid.shape[0] * tm
    up = lambda e: pl.BlockSpec((None, D_MODEL, D_EXPERT), lambda m, e1, e2, tv: ((e1, e2)[e][m], 0, 0))
    down = lambda e: pl.BlockSpec((None, D_EXPERT, D_MODEL), lambda m, e1, e2, tv: ((e1, e2)[e][m], 0, 0))
    grid_spec = pltpu.PrefetchScalarGridSpec(
        num_scalar_prefetch=3,
        grid=(m_pad // tm,),
        in_specs=[
            pl.BlockSpec((tm, ROW_WIDTH), lambda m, e1, e2, tv: (m, 0)),
            pl.BlockSpec(g2.shape, lambda m, e1, e2, tv: (0, 0)),
            up(0), up(0), down(0), up(1), up(1), down(1),
        ],
        out_specs=pl.BlockSpec((tm, D_MODEL), lambda m, e1, e2, tv: (m, 0)),
    )
    return pl.pallas_call(
        _moe_kernel,
        grid_spec=grid_spec,
        out_shape=jax.ShapeDtypeStruct((m_pad, D_MODEL), F32),
        compiler_params=_cparams(("arbitrary",)),
        name="moe_experts",
    )(tile_e1, tile_e2, tile_valid, xs, g2, wg, wu, wd, wg, wu, wd)


def _moe_plan(plan_p, cnt_p, plan_s, cnt_s, tm):
    def ids(plan):
        blocks = plan.reshape(-1, SUBLANES, plan.shape[1]).astype(jnp.int32)
        return blocks[:, 0, :], blocks[:, 1, :]

    def lookup(table, cat):
        out = jnp.zeros_like(cat)
        for c in range(N_CATEGORIES):
            out = jnp.where(cat == c, table[c], out)
        return out

    (kp, rank_p), (ks, rank_s) = ids(plan_p), ids(plan_s)
    n_total = kp.size + ks.size
    n_tiles = n_total // tm + N_CATEGORIES
    cp = cnt_p[0, :N_CATEGORIES].astype(jnp.int32)
    cs = cnt_s[0, :N_CATEGORIES].astype(jnp.int32)
    total = cp + cs
    tiles = (total + tm - 1) // tm
    tile_end = jnp.cumsum(tiles)
    offset = (tile_end - tiles) * tm
    dest_p = (lookup(offset, kp) + rank_p).reshape(-1)
    dest_s = (lookup(offset + cp, ks) + rank_s).reshape(-1)
    tile_id = jnp.arange(n_tiles)
    tile_cat = jnp.minimum(jnp.sum(tile_id[:, None] >= tile_end[None, :], axis=1), N_CATEGORIES - 1)
    used = jnp.clip(offset[tile_cat] + total[tile_cat] - tile_id * tm, 0, tm)
    tile_valid = jnp.where(tile_id < tile_end[N_CATEGORIES - 1], used, 0).astype(jnp.int32)
    experts = jnp.asarray(CATEGORY_EXPERTS, jnp.int32)
    tile_e1, tile_e2 = experts[tile_cat, 0], experts[tile_cat, 1]
    m_rows = (n_tiles + 1) * tm
    ends = ((offset + total) // SUBLANES) * SUBLANES
    tail = ends[N_CATEGORIES - 1] + tm * jnp.arange(1, m_rows // tm - n_total // tm + 1)
    fill_start = jnp.concatenate([ends, tail]).astype(jnp.int32)
    return dest_p, dest_s, fill_start, tile_e1, tile_e2, tile_valid, m_rows


def kernel(x_prompt, x_sample, cache_k, cache_v, cache_logf, state_pool, norm1_g, w_in, b_f, q_norm_g, k_norm_g,
           pool_w, pool_scale, w_out, norm2_g, w_router_group, b_router_group, w_router_expert, b_router_expert,
           w_gate, w_up, w_down):
    depth = norm1_g.shape[0]
    assert depth == 1, "single-layer step"
    b_p, s_p, _ = x_prompt.shape
    assert b_p == 1, "prompt kernels assume one stream"
    b_s, t_s, _ = x_sample.shape
    p_len = cache_k.shape[2]
    l = 0

    w = w_in[l]
    a3 = 3 * ATTN_WIDTH
    w_all = jnp.concatenate(
        [w[:, :a3], w[:, a3 + N_HEADS:], jnp.pad(w[:, a3:a3 + N_HEADS], ((0, 0), (0, LANES - N_HEADS)))],
        axis=1).astype(BF16)
    bf_pad = jnp.pad(b_f[l], (0, LANES - N_HEADS)).reshape(1, LANES)
    g1 = norm1_g[l].reshape(1, D_MODEL)
    qg = jnp.tile(q_norm_g[l], N_HEADS).reshape(1, ATTN_WIDTH)
    kg = jnp.tile(k_norm_g[l], N_HEADS).reshape(1, ATTN_WIDTH)
    hr = lax.broadcasted_iota(jnp.int32, (MXU_DIM, MXU_DIM), 0) // HEAD_DIM
    hc = lax.broadcasted_iota(jnp.int32, (MXU_DIM, MXU_DIM), 1) // HEAD_DIM
    hm = jnp.where(hr == hc, 1.0 / HEAD_DIM, 0.0).astype(BF16)
    pw = pool_w[l].astype(BF16)
    ps = pool_scale[l].reshape(1, POOL_WIDTH)
    wo = w_out[l].astype(BF16)
    g2 = norm2_g[l].reshape(1, D_MODEL)
    w_r = jnp.pad(jnp.concatenate([w_router_expert[l], w_router_group[l]], axis=1),
                  ((0, 0), (0, LANES - N_EXPERTS - N_GROUPS)))
    wrh = w_r.astype(BF16)
    wrl = (w_r - wrh.astype(F32)).astype(BF16)
    br = jnp.pad(jnp.concatenate([b_router_expert[l], b_router_group[l]]),
                 (0, LANES - N_EXPERTS - N_GROUPS)).reshape(1, LANES)
    wg = w_gate[l].astype(BF16)
    wu = w_up[l].astype(BF16)
    wd = w_down[l].astype(BF16)

    tile = 512
    query_tile = 512
    key_tile = 256
    expert_tile = 256
    move_tile = 1024

    xp = x_prompt.reshape(s_p, D_MODEL)
    q, kf, _, vf, _, ka, vt, pin, logf, _, edge = _in_project(xp, tile, tile, True, key_tile, g1, w_all, bf_pad, qg, kg,
                                                              hm)
    blocks_per_tile = tile // key_tile
    edge = edge.reshape(s_p // tile, SUBLANES, LANES)[:, :2 * blocks_per_tile, :N_HEADS]
    edge = edge.reshape(s_p // key_tile, 2, N_HEADS)
    c_blocks = edge[:, 0, :]
    j0 = _first_key_blocks(c_blocks, edge[:, 1, :], q_norm_g[l], k_norm_g[l], query_tile, key_tile)
    a = _attention_prompt(j0, c_blocks, q, ka, vt, query_tile, key_tile)
    hist_map = lambda i: (jnp.maximum(i * (tile // HIST_ROWS) - 1, 0), 0)
    xg_p, plan_p, cnt_p = _mix(a, pin, pin, hist_map, xp, tile, tile, 0, tile, True, pw, ps, wo, g2, wrh, wrl, br)
    k_prompt = kf.reshape(depth, b_p, s_p, N_HEADS, HEAD_DIM)
    v_prompt = vf.reshape(depth, b_p, s_p, N_HEADS, HEAD_DIM)
    logf_prompt = logf[:, :N_HEADS].reshape(depth, b_p, s_p, N_HEADS)
    pool_prompt = pin[s_p - POOL_PAD:].reshape(depth, b_p, POOL_PAD, POOL_WIDTH)

    n_s = b_s * t_s
    xs = x_sample.reshape(n_s, D_MODEL)
    q, kf, kb, vf, vb, _, _, pin, logf, d, _ = _in_project(xs, tile, t_s, False, key_tile, g1, w_all, bf_pad, qg, kg, hm)
    dt4 = d[:, :N_HEADS].reshape(b_s, t_s, N_PAIRS, 2).transpose(0, 2, 3, 1)
    clf = cache_logf[l].astype(F32).transpose(0, 2, 1).reshape(b_s * N_HEADS, p_len)
    r4 = _suffix_sums(clf).reshape(b_s, N_PAIRS, 2, p_len)
    ck = cache_k[l].reshape(b_s, p_len, ATTN_WIDTH).astype(BF16)
    cv = cache_v[l].reshape(b_s, p_len, ATTN_WIDTH).astype(BF16)
    a = _attention_sample(q, kb, vb, ck, cv, d, dt4, r4, t_s)
    hist_s = jnp.pad(state_pool[l], ((0, 0), (HIST_ROWS - POOL_PAD, 0), (0, 0))).reshape(b_s * HIST_ROWS, POOL_WIDTH)
    xg_s, plan_s, cnt_s = _mix(a, pin, hist_s, lambda i: (i, 0), xs, tile, t_s, p_len, 0, False, pw, ps, wo, g2, wrh,
                               wrl, br)
    k_sample = kf.reshape(depth, b_s, t_s, N_HEADS, HEAD_DIM)
    v_sample = vf.reshape(depth, b_s, t_s, N_HEADS, HEAD_DIM)
    logf_sample = logf[:, :N_HEADS].reshape(depth, b_s, t_s, N_HEADS)
    pool_sample = pin.reshape(b_s, t_s, POOL_WIDTH)[:, t_s - POOL_PAD:].reshape(depth, b_s, POOL_PAD, POOL_WIDTH)

    dest_p, dest_s, fill_start, tile_e1, tile_e2, tile_valid, m_rows = _moe_plan(plan_p, cnt_p, plan_s, cnt_s,
                                                                              expert_tile)
    rows = _scatter_rows(fill_start, dest_p, xg_p, dest_s, xg_s, m_rows, move_tile, expert_tile)
    ys = _moe_sorted(tile_e1, tile_e2, tile_valid, rows, g2, wg, wu, wd, expert_tile)
    y_prompt = _gather_rows(dest_p, ys, 2 * move_tile).reshape(b_p, s_p, D_MODEL)
    y_sample = _gather_rows(dest_s, ys, move_tile).reshape(b_s, t_s, D_MODEL)

    return (y_prompt, y_sample, k_prompt, v_prompt, logf_prompt, pool_prompt,
            k_sample, v_sample, logf_sample, pool_sample)
```

```python
import functools

import jax
import jax.numpy as jnp
from jax import lax
from jax.experimental import pallas as pl
from jax.experimental.pallas import tpu as pltpu

F32 = jnp.float32
BF16 = jnp.bfloat16

D_MODEL = 1024
ATTN_WIDTH = 512
N_HEADS = 8
HEAD_DIM = 64
POOL_WIDTH = 512
POOL_WINDOWS = (2, 4, 8, 16)
POOL_GROUP_DIM = 128
POOL_PAD = 15
HIST_ROWS = 16
N_GROUPS = 4
EXPERTS_PER_GROUP = 4
N_EXPERTS = 16
D_EXPERT = 512
EPS = 1e-6
F32_EXP_UNDERFLOW = 104.0
BF16_SLACK = 1.02
SKIP_MARGIN = 1.0
LANES = 128
SUBLANES = 8
MXU_DIM = 256
PAIR = 2 * HEAD_DIM
N_PAIRS = N_HEADS // 2
ROWSUM_ROW = (HEAD_DIM, 0)
VMEM_LIMIT = 56 * 1024 * 1024
ROW_WIDTH = D_MODEL + LANES
PAIRS_PER_GROUP = 6
N_CATEGORIES = N_GROUPS * PAIRS_PER_GROUP
PAIR_ORDER = ((0, 1), (0, 2), (1, 2), (1, 3), (2, 3), (0, 3))
CATEGORY_EXPERTS = tuple((4 * g + a, 4 * g + b) for g in range(N_GROUPS) for a, b in PAIR_ORDER)
ROUTE_CATEGORY_LANE = N_EXPERTS
ROUTE_RANK_LANE = N_EXPERTS + 1
ROUTE_SLOTS = 32


def _cparams(sem):
    return pltpu.CompilerParams(dimension_semantics=sem, vmem_limit_bytes=VMEM_LIMIT)


def _split3(x):
    hi = x.astype(BF16)
    r1 = x - hi.astype(F32)
    mid = r1.astype(BF16)
    lo = (r1 - mid.astype(F32)).astype(BF16)
    return hi, mid, lo


def _split2(x):
    hi = x.astype(BF16)
    lo = (x - hi.astype(F32)).astype(BF16)
    return hi, lo


def _inproj_kernel(x_ref, g1_ref, w_ref, bf_ref, qg_ref, kg_ref, hm_ref, tri_ref, place_ref,
                   q_ref, kf_ref, kb_ref, vf_ref, vb_ref, ka_ref, vt_ref, p_ref, logf_ref, c_ref, edge_ref, carry_ref, *,
                   carry_rows, key_blk):
    x = x_ref[...]
    r = lax.rsqrt(jnp.mean(x * x, axis=-1, keepdims=True) + EPS)
    n = ((x * r) * g1_ref[...]).astype(BF16)
    z = jnp.dot(n, w_ref[...], preferred_element_type=F32)

    hm = hm_ref[...]

    def head_norm(zz, g):
        hi, lo = _split2(zz * zz)
        half = hm.shape[0]
        ms = jnp.concatenate(
            [jnp.dot(hi[:, r:r + half], hm, preferred_element_type=F32)
             + jnp.dot(lo[:, r:r + half], hm, preferred_element_type=F32) for r in range(0, zz.shape[1], half)], axis=1)
        return (zz * lax.rsqrt(ms + EPS)) * g

    q = head_norm(z[:, 0:ATTN_WIDTH], qg_ref[...])
    q_ref[...] = (q * (HEAD_DIM ** -0.5)).astype(BF16)
    k = head_norm(z[:, ATTN_WIDTH:2 * ATTN_WIDTH], kg_ref[...])
    kf_ref[...] = k
    kb_ref[...] = k.astype(BF16)
    v = z[:, 2 * ATTN_WIDTH:3 * ATTN_WIDTH]
    vf_ref[...] = v
    vb_ref[...] = v.astype(BF16)
    p_ref[...] = z[:, 3 * ATTN_WIDTH:3 * ATTN_WIDTH + POOL_WIDTH]
    prow = lax.broadcasted_iota(jnp.int32, (PAIR, v.shape[0]), 0)
    for hp in range(N_PAIRS):
        vpair_t = v[:, hp * PAIR:(hp + 1) * PAIR].T
        for h in range(2):
            own = (prow < HEAD_DIM) if h == 0 else (prow >= HEAD_DIM)
            marker = jnp.where(prow == ROWSUM_ROW[h], 1.0, 0.0)
            vt_ref[(2 * hp + h) * LANES:(2 * hp + h + 1) * LANES, :] = jnp.where(own, vpair_t, marker).astype(BF16)

    f = z[:, 3 * ATTN_WIDTH + POOL_WIDTH:] + bf_ref[...]
    lane = lax.broadcasted_iota(jnp.int32, f.shape, 1)
    logf = -(jnp.maximum(-f, 0.0) + jnp.log1p(jnp.exp(-jnp.abs(f))))
    logf = jnp.where(lane < N_HEADS, logf, 0.0)
    logf_ref[...] = logf

    tri = tri_ref[...]
    blk = tri.shape[0]
    parts = _split3(logf)
    pieces = []
    for r in range(0, logf.shape[0], blk):
        piece = sum(jnp.dot(tri, part[r:r + blk, :], preferred_element_type=F32) for part in parts)
        if carry_rows and pieces:
            piece = piece + pieces[-1][blk - 1:blk, :]
        pieces.append(piece)
    c = jnp.concatenate(pieces, axis=0)
    if carry_rows:
        @pl.when(pl.program_id(0) == 0)
        def _():
            carry_ref[...] = jnp.zeros_like(carry_ref)
        c = c + carry_ref[0:1, :]
        carry_ref[...] = jnp.broadcast_to(c[c.shape[0] - 1:, :], carry_ref.shape)
    c_ref[...] = c
    edges = [c[r:r + 1, :] for b in range(c.shape[0] // key_blk) for r in (b * key_blk, (b + 1) * key_blk - 1)]
    edge_ref[...] = jnp.concatenate(edges + [jnp.zeros((SUBLANES - len(edges), LANES), F32)], axis=0)

    tm = c.shape[0]
    rel = jnp.concatenate([c[r:r + key_blk, :] - c[r:r + 1, :] for r in range(0, tm, key_blk)], axis=0)
    parts = jnp.concatenate(_split3(rel), axis=1)
    decay = jnp.dot(parts, place_ref[...], preferred_element_type=F32).astype(BF16)
    kb = k.astype(BF16)
    for hp in range(N_PAIRS):
        ka_ref[:, 2 * hp * LANES:(2 * hp + 1) * LANES] = kb[:, hp * PAIR:(hp + 1) * PAIR]
        ka_ref[:, (2 * hp + 1) * LANES:(2 * hp + 2) * LANES] = decay[:, hp * LANES:(hp + 1) * LANES]


def _decay_lane(h, part):
    return 3 * h + part


def _in_project(x2d, tm, seg, carry_rows, key_blk, g1, w_all, bf_pad, qg, kg, hm):
    n = x2d.shape[0]
    blk = min(tm, MXU_DIM)
    assert seg % blk == 0 or blk % seg == 0, "running-sum segments must nest with the triangular block"
    rows = lax.broadcasted_iota(jnp.int32, (blk, blk), 0)
    cols = lax.broadcasted_iota(jnp.int32, (blk, blk), 1)
    tri = ((cols <= rows) & (rows // seg == cols // seg)).astype(BF16)
    src = jnp.arange(3 * LANES)
    part, head = src // LANES, src % LANES
    dst = (head // 2) * LANES + _decay_lane(head % 2, part)
    place = ((jnp.arange(N_PAIRS * LANES)[None, :] == dst[:, None]) & (head < N_HEADS)[:, None]).astype(BF16)
    row_blk = lambda w: pl.BlockSpec((tm, w), lambda i: (i, 0))
    const = lambda a: pl.BlockSpec(a.shape, lambda i: (0,) * a.ndim)
    out_shape = (
        jax.ShapeDtypeStruct((n, ATTN_WIDTH), BF16),
        jax.ShapeDtypeStruct((n, ATTN_WIDTH), F32),
        jax.ShapeDtypeStruct((n, ATTN_WIDTH), BF16),
        jax.ShapeDtypeStruct((n, ATTN_WIDTH), F32),
        jax.ShapeDtypeStruct((n, ATTN_WIDTH), BF16),
        jax.ShapeDtypeStruct((n, N_PAIRS * 2 * LANES), BF16),
        jax.ShapeDtypeStruct((N_HEADS * LANES, n), BF16),
        jax.ShapeDtypeStruct((n, POOL_WIDTH), F32),
        jax.ShapeDtypeStruct((n, LANES), F32),
        jax.ShapeDtypeStruct((n, LANES), F32),
        jax.ShapeDtypeStruct((n // tm * SUBLANES, LANES), F32),
    )
    return pl.pallas_call(
        functools.partial(_inproj_kernel, carry_rows=carry_rows, key_blk=key_blk),
        grid=(n // tm,),
        in_specs=[row_blk(D_MODEL), const(g1), const(w_all), const(bf_pad), const(qg), const(kg), const(hm),
                  const(tri), const(place)],
        out_specs=[row_blk(ATTN_WIDTH)] * 5 + [row_blk(N_PAIRS * 2 * LANES),
                                                pl.BlockSpec((N_HEADS * LANES, tm), lambda i: (0, i)),
                                                row_blk(POOL_WIDTH), row_blk(LANES), row_blk(LANES),
                                                pl.BlockSpec((SUBLANES, LANES), lambda i: (i, 0))],
        out_shape=out_shape,
        scratch_shapes=[pltpu.VMEM((8, LANES), F32)],
        compiler_params=_cparams(("arbitrary",)),
        name="in_project",
    )(x2d, g1, w_all, bf_pad, qg, kg, hm, tri, place)


def _attn_prompt_kernel(j0_ref, cb_ref, q_ref, k_ref, vt_ref, o_ref, acc_ref, m_ref, *, tq, tk):
    hp = pl.program_id(0)
    i = pl.program_id(1)
    nk = tq // tk
    lane = lax.broadcasted_iota(jnp.int32, (1, PAIR), 1)
    first = lane < HEAD_DIM
    q = q_ref[...]
    zero = jnp.zeros_like(q)
    qw = []
    for h in range(2):
        minus = jnp.zeros((1, LANES), F32)
        for part in range(3):
            minus = jnp.where(lane == _decay_lane(h, part), -1.0, minus)
        qw.append(jnp.concatenate([jnp.where(first == (h == 0), q, zero),
                                   jnp.broadcast_to(minus, (tq, LANES)).astype(BF16)], axis=1))
    q_start = i * tq

    acc_ref[...] = jnp.zeros_like(acc_ref)
    m_ref[...] = jnp.full_like(m_ref, -jnp.inf)

    last = (i + 1) * nk - 1

    def scores(h, j, masked=True):
        start = pl.multiple_of(j * tk, tk)
        k = k_ref[pl.ds(start, tk), :]
        s = lax.dot_general(k, qw[h], (((1,), (1,)), ((), ())), preferred_element_type=F32)
        if masked:
            key = lax.broadcasted_iota(jnp.int32, s.shape, 0)
            qry = lax.broadcasted_iota(jnp.int32, s.shape, 1)
            s = jnp.where(key - qry <= q_start - j * tk, s, -jnp.inf)
        return s, jnp.max(s, axis=0, keepdims=True)

    def update(h, j, s, colmax):
        shift = cb_ref[j, 2 * hp + h] - cb_ref[i * nk, 2 * hp + h]
        start = pl.multiple_of(j * tk, tk)
        vt = vt_ref[h * LANES:(h + 1) * LANES, pl.ds(start, tk)]
        m_prev = m_ref[h]
        m_new = jnp.maximum(m_prev, colmax - shift)
        alpha = jnp.exp(m_prev - m_new)
        p = jnp.exp(s - (m_new + shift))
        m_ref[h] = m_new
        acc_ref[h] = acc_ref[h] * alpha + jnp.dot(vt, p.astype(BF16), preferred_element_type=F32)

    j_first = [j0_ref[i, 2 * hp + h] for h in range(2)]
    j_both = jnp.maximum(j_first[0], j_first[1])
    start = [scores(h, j_first[h]) for h in range(2)]
    ahead = []
    for h in range(2):
        def body_one(j, carry, h=h):
            nxt = scores(h, j + 1)
            update(h, j, *carry)
            return nxt

        ahead.append(lax.fori_loop(j_first[h], j_both, body_one, start[h]))

    def body_both(j, carry, masked):
        nxt = (scores(0, j + 1, masked), scores(1, j + 1, masked))
        update(0, j, *carry[0])
        update(1, j, *carry[1])
        return nxt

    j_mask = jnp.maximum(j_both, i * nk - 1)
    pairs = (j_mask - j_both) // 2

    def body_two(t, carry):
        j = j_both + 2 * t
        return body_both(j + 1, body_both(j, carry, False), False)

    carry = lax.fori_loop(0, pairs, body_two, (ahead[0], ahead[1]))
    carry = lax.fori_loop(j_both + 2 * pairs, last, functools.partial(body_both, masked=True), carry)
    update(0, last, *carry[0])
    update(1, last, *carry[1])

    out = [acc_ref[h] / acc_ref[h][ROWSUM_ROW[h]:ROWSUM_ROW[h] + 1, :] for h in range(2)]
    row = lax.broadcasted_iota(jnp.int32, (PAIR, 1), 0)
    o_ref[...] = jnp.where(row < HEAD_DIM, out[0], out[1]).T.astype(o_ref.dtype)


def _attention_prompt(j0, c_blocks, q, ka, vt, tq, tk):
    n = q.shape[0]
    grid_spec = pltpu.PrefetchScalarGridSpec(
        num_scalar_prefetch=2,
        grid=(N_PAIRS, n // tq),
        in_specs=[
            pl.BlockSpec((tq, PAIR), lambda hp, i, j0, cb: (i, hp)),
            pl.BlockSpec((n, 2 * LANES), lambda hp, i, j0, cb: (0, hp)),
            pl.BlockSpec((2 * LANES, n), lambda hp, i, j0, cb: (hp, 0)),
        ],
        out_specs=pl.BlockSpec((tq, PAIR), lambda hp, i, j0, cb: (i, hp)),
        scratch_shapes=[pltpu.VMEM((2, LANES, tq), F32), pltpu.VMEM((2, 1, tq), F32)],
    )
    return pl.pallas_call(
        functools.partial(_attn_prompt_kernel, tq=tq, tk=tk),
        grid_spec=grid_spec,
        out_shape=jax.ShapeDtypeStruct((n, ATTN_WIDTH), BF16),
        compiler_params=_cparams(("arbitrary", "arbitrary")),
        name="attention_prompt",
    )(j0, c_blocks, q, ka, vt)


def _first_key_blocks(c_first, c_last, q_gain, k_gain, tq, tk):
    n_k = c_first.shape[0]
    n_q = n_k * tk // tq
    qk_bound = BF16_SLACK * HEAD_DIM ** 0.5 * jnp.max(jnp.abs(q_gain)) * jnp.max(jnp.abs(k_gain))
    threshold = F32_EXP_UNDERFLOW + 2.0 * qk_bound + SKIP_MARGIN
    c_start = c_first[0::tq // tk]
    gap = c_start[:, None, :] - c_last[None, :, :]
    earlier = (jnp.arange(n_k)[None, :, None] + 1) * tk <= jnp.arange(n_q)[:, None, None] * tq
    return jnp.sum((gap < -threshold) & earlier, axis=1).astype(jnp.int32)


def _suffix_kernel(x_ref, o_ref):
    x = x_ref[...]
    p_len = x.shape[1]
    rows = lax.broadcasted_iota(jnp.int32, (p_len, p_len), 0)
    cols = lax.broadcasted_iota(jnp.int32, (p_len, p_len), 1)
    u = (rows > cols).astype(BF16)
    hi, mid, lo = _split3(x)
    o_ref[...] = (jnp.dot(hi, u, preferred_element_type=F32) + jnp.dot(mid, u, preferred_element_type=F32)
                  + jnp.dot(lo, u, preferred_element_type=F32))


def _suffix_sums(x):
    return pl.pallas_call(
        _suffix_kernel,
        out_shape=jax.ShapeDtypeStruct(x.shape, F32),
        compiler_params=pltpu.CompilerParams(vmem_limit_bytes=VMEM_LIMIT),
        name="cache_suffix_sums",
    )(x)


def _attn_sample_kernel(q_ref, kn_ref, vn_ref, ck_ref, cv_ref, d_ref, dt_ref, r_ref, o_ref):
    lane = lax.broadcasted_iota(jnp.int32, (1, PAIR), 1)
    first = lane < HEAD_DIM
    dblk = d_ref[...]
    dlane = lax.broadcasted_iota(jnp.int32, dblk.shape, 1)
    nt = (((1,), (1,)), ((), ()))
    for hp in range(N_PAIRS):
        cols = slice(hp * PAIR, (hp + 1) * PAIR)
        q = q_ref[:, cols]
        zero = jnp.zeros_like(q)
        qh = (jnp.where(first, q, zero), jnp.where(first, zero, q))
        kn = kn_ref[:, cols]
        vn = vn_ref[:, cols]
        ck = ck_ref[:, cols]
        cv = cv_ref[:, cols]
        outs, ls = [], []
        for h in range(2):
            dq = jnp.sum(jnp.where(dlane == 2 * hp + h, dblk, 0.0), axis=1, keepdims=True)
            s1 = lax.dot_general(qh[h], ck, nt, preferred_element_type=F32) + dq + r_ref[hp, h:h + 1, :]
            s2 = lax.dot_general(qh[h], kn, nt, preferred_element_type=F32) + dq - dt_ref[hp, h:h + 1, :]
            row = lax.broadcasted_iota(jnp.int32, s2.shape, 0)
            col = lax.broadcasted_iota(jnp.int32, s2.shape, 1)
            s2 = jnp.where(col <= row, s2, -jnp.inf)
            m = jnp.maximum(jnp.max(s1, axis=1, keepdims=True), jnp.max(s2, axis=1, keepdims=True))
            p1 = jnp.exp(s1 - m)
            p2 = jnp.exp(s2 - m)
            ls.append(jnp.sum(p1, axis=1, keepdims=True) + jnp.sum(p2, axis=1, keepdims=True))
            outs.append(jnp.dot(p1.astype(BF16), cv, preferred_element_type=F32)
                        + jnp.dot(p2.astype(BF16), vn, preferred_element_type=F32))
        o_ref[:, cols] = (jnp.where(first, outs[0], outs[1]) / jnp.where(first, ls[0], ls[1])).astype(o_ref.dtype)


def _attention_sample(q, kb, vb, cache_k, cache_v, d, dt4, r4, t):
    nb, p_len = cache_k.shape[0], cache_k.shape[1]
    row_blk = pl.BlockSpec((t, ATTN_WIDTH), lambda b: (b, 0))
    cache_blk = pl.BlockSpec((None, p_len, ATTN_WIDTH), lambda b: (b, 0, 0))
    return pl.pallas_call(
        _attn_sample_kernel,
        grid=(nb,),
        in_specs=[row_blk, row_blk, row_blk, cache_blk, cache_blk,
                  pl.BlockSpec((t, LANES), lambda b: (b, 0)),
                  pl.BlockSpec((None, N_PAIRS, 2, t), lambda b: (b, 0, 0, 0)),
                  pl.BlockSpec((None, N_PAIRS, 2, p_len), lambda b: (b, 0, 0, 0))],
        out_specs=row_blk,
        out_shape=jax.ShapeDtypeStruct((nb * t, ATTN_WIDTH), BF16),
        compiler_params=_cparams(("arbitrary",)),
        name="attention_sample",
    )(q, kb, vb, cache_k, cache_v, d, dt4, r4)


def _mix_kernel(a_ref, p_ref, hist_ref, x_ref, pw_ref, ps_ref, wo_ref, g2_ref, wrh_ref, wrl_ref, br_ref, tri_ref,
                xg_ref, plan_ref, cnt_ref, *, seg, pos0, pos_stride, zero_first):
    i = pl.program_id(0)
    t = p_ref.shape[0]
    p = p_ref[...]
    hist = hist_ref[...]
    if zero_first:
        hist = jnp.where(i == 0, 0.0, hist)
    ext = seg + HIST_ROWS
    xh = jnp.concatenate([piece for b in range(t // seg)
                          for piece in (hist[b * HIST_ROWS:(b + 1) * HIST_ROWS, :], p[b * seg:(b + 1) * seg, :])], axis=0)

    def own_rows(v):
        return jnp.concatenate([v[b * ext + HIST_ROWS:(b + 1) * ext, :] for b in range(t // seg)], axis=0)

    pos = pos0 + i * pos_stride + lax.broadcasted_iota(jnp.int32, (t, 1), 0) % seg
    ys = []
    for g, w in enumerate(POOL_WINDOWS):
        col = xh[:, g * POOL_GROUP_DIM:(g + 1) * POOL_GROUP_DIM]
        acc = col
        span = 1
        while span < w:
            acc = acc + pltpu.roll(acc, span, axis=0)
            span *= 2
        cnt = jnp.minimum(pos + 1, w).astype(F32)
        dgrp = own_rows(acc) / cnt - own_rows(col)
        ys.append(jnp.dot(dgrp.astype(BF16), pw_ref[g], preferred_element_type=F32))
    pm = (jnp.concatenate(ys, axis=1) * ps_ref[...]).astype(BF16)
    mix = (jnp.dot(a_ref[...], wo_ref[0:ATTN_WIDTH, :], preferred_element_type=F32)
           + jnp.dot(pm, wo_ref[ATTN_WIDTH:, :], preferred_element_type=F32))
    x1 = x_ref[...] + mix
    xg_ref[:, 0:D_MODEL] = x1

    r = lax.rsqrt(jnp.mean(x1 * x1, axis=-1, keepdims=True) + EPS)
    n2 = (x1 * r) * g2_ref[...]

    nh, nl = _split2(n2)
    wrh = wrh_ref[...]
    logits = (jnp.dot(nh, wrh, preferred_element_type=F32) + jnp.dot(nl, wrh, preferred_element_type=F32)
              + jnp.dot(nh, wrl_ref[...], preferred_element_type=F32)) + br_ref[...]
    z = logits.T[0:ROUTE_SLOTS, :]
    slot = lax.broadcasted_iota(jnp.int32, z.shape, 0).astype(F32)
    neg = -jnp.inf
    big = jnp.float32(1 << 20)
    is_g = (slot >= N_EXPERTS) & (slot < N_EXPERTS + N_GROUPS)
    lg = jnp.where(is_g, z, neg)
    gmax = jnp.max(lg, axis=0, keepdims=True)
    g_slot = jnp.min(jnp.where(lg == gmax, slot, big), axis=0, keepdims=True)
    top_pg = 1.0 / jnp.sum(jnp.exp(lg - gmax), axis=0, keepdims=True)
    g_idx = g_slot - N_EXPERTS
    sel = (slot >= g_idx * EXPERTS_PER_GROUP) & (slot < (g_idx + 1) * EXPERTS_PER_GROUP)
    le = jnp.where(sel, z, neg)
    m1 = jnp.max(le, axis=0, keepdims=True)
    i1 = jnp.min(jnp.where(le == m1, slot, big), axis=0, keepdims=True)
    le2 = jnp.where(slot == i1, neg, le)
    m2 = jnp.max(le2, axis=0, keepdims=True)
    i2 = jnp.min(jnp.where(le2 == m2, slot, big), axis=0, keepdims=True)
    e2 = jnp.exp(m2 - m1)
    w1 = top_pg / (1.0 + e2)
    w2 = top_pg * e2 / (1.0 + e2)
    gate = jnp.where(slot == i1, w1, 0.0) + jnp.where(slot == i2, w2, 0.0)

    e_lo = jnp.minimum(i1, i2) - g_idx * EXPERTS_PER_GROUP
    e_hi = jnp.maximum(i1, i2) - g_idx * EXPERTS_PER_GROUP
    pair = jnp.zeros_like(e_lo)
    for idx, (a, b) in enumerate(PAIR_ORDER):
        pair = jnp.where((e_lo == a) & (e_hi == b), float(idx), pair)
    category = g_idx * PAIRS_PER_GROUP + pair

    @pl.when(i == 0)
    def _():
        cnt_ref[...] = jnp.zeros_like(cnt_ref)
    member = slot == category
    tri = tri_ref[...]
    blk = tri.shape[0]
    nt = (((1,), (1,)), ((), ()))
    seen = cnt_ref[:, 0:1]
    pieces = []
    for r in range(0, t, blk):
        mem = member[:, r:r + blk].astype(BF16)
        pieces.append(lax.dot_general(mem, tri, nt, preferred_element_type=F32) + seen)
        seen = seen + jnp.sum(mem.astype(F32), axis=1, keepdims=True)
    before = jnp.concatenate(pieces, axis=1)
    rank = jnp.sum(jnp.where(member, before, 0.0), axis=0, keepdims=True)
    cnt_ref[...] = jnp.broadcast_to(seen, cnt_ref.shape)
    route = jnp.where(slot == ROUTE_CATEGORY_LANE, category, jnp.where(slot == ROUTE_RANK_LANE, rank, gate))
    route = jnp.concatenate([route, jnp.zeros((LANES - ROUTE_SLOTS, t), F32)], axis=0)
    xg_ref[:, D_MODEL:] = route.T
    srow = lax.broadcasted_iota(jnp.int32, (SUBLANES, t), 0)
    plan_ref[...] = jnp.where(srow == 0, category, jnp.where(srow == 1, rank, 0.0))


def _mix(a, p, hist_src, hist_map, x2d, t, seg, pos0, pos_stride, zero_first, pw, ps, wo, g2, wrh, wrl, br):
    n = x2d.shape[0]
    blk = min(t, MXU_DIM)
    rows = lax.broadcasted_iota(jnp.int32, (blk, blk), 0)
    cols = lax.broadcasted_iota(jnp.int32, (blk, blk), 1)
    tri = (cols < rows).astype(BF16)
    row_blk = lambda w: pl.BlockSpec((t, w), lambda i: (i, 0))
    const = lambda arr: pl.BlockSpec(arr.shape, lambda i: (0,) * arr.ndim)
    return pl.pallas_call(
        functools.partial(_mix_kernel, seg=seg, pos0=pos0, pos_stride=pos_stride, zero_first=zero_first),
        grid=(n // t,),
        in_specs=[row_blk(ATTN_WIDTH), row_blk(POOL_WIDTH),
                  pl.BlockSpec((t // seg * HIST_ROWS, POOL_WIDTH), hist_map),
                  row_blk(D_MODEL), const(pw), const(ps), const(wo), const(g2), const(wrh), const(wrl), const(br),
                  const(tri)],
        out_specs=[row_blk(ROW_WIDTH), pl.BlockSpec((SUBLANES, t), lambda i: (i, 0)),
                   pl.BlockSpec((ROUTE_SLOTS, LANES), lambda i: (0, 0))],
        out_shape=(jax.ShapeDtypeStruct((n, ROW_WIDTH), F32), jax.ShapeDtypeStruct((n // t * SUBLANES, t), F32),
                   jax.ShapeDtypeStruct((ROUTE_SLOTS, LANES), F32)),
        compiler_params=_cparams(("arbitrary",)),
        name="pool_mix_route",
    )(a, p, hist_src, x2d, pw, ps, wo, g2, wrh, wrl, br, tri)


def _issue_row_scatter(dest_ref, x_ref, xs_ref, sem):
    t = x_ref.shape[0]

    def issue(g, carry):
        base = pl.multiple_of(g * SUBLANES, SUBLANES)
        tile_rows = x_ref.at[pl.ds(base, SUBLANES)]
        for u in range(SUBLANES):
            pltpu.make_async_copy(tile_rows.at[pl.ds(u, 1)], xs_ref.at[pl.ds(dest_ref[base + u], 1)],
                                  sem).start(priority=u % 2)
        return carry

    lax.fori_loop(0, t // SUBLANES, issue, 0, unroll=2)
    pltpu.make_async_copy(x_ref, xs_ref.at[pl.ds(0, t)], sem).wait()


def _scatter_rows_kernel(fill_ref, dest_a_ref, dest_b_ref, xa_ref, xb_ref, xs_ref, zero_ref, sem, *, steps_a):
    step = pl.program_id(0)

    @pl.when(step == 0)
    def _():
        zero_ref[...] = jnp.zeros_like(zero_ref)
        fill_rows = zero_ref.shape[0]
        last_start = xs_ref.shape[0] - fill_rows
        for g in range(fill_ref.shape[0]):
            @pl.when(fill_ref[g] < xs_ref.shape[0])
            def _(g=g):
                start = pl.multiple_of(jnp.minimum(fill_ref[g], last_start), SUBLANES)
                fill = pltpu.make_async_copy(zero_ref, xs_ref.at[pl.ds(start, fill_rows)], sem)
                fill.start()
                fill.wait()

    @pl.when(step < steps_a)
    def _():
        _issue_row_scatter(dest_a_ref, xa_ref, xs_ref, sem)

    @pl.when(step >= steps_a)
    def _():
        _issue_row_scatter(dest_b_ref, xb_ref, xs_ref, sem)


def _scatter_rows(fill_start, dest_a, xa, dest_b, xb, m_rows, t, fill_rows):
    w = xa.shape[1]
    steps_a, steps_b = xa.shape[0] // t, xb.shape[0] // t
    in_a = lambda i: jnp.minimum(i, steps_a - 1)
    in_b = lambda i: jnp.maximum(i - steps_a, 0)
    smem = pltpu.SMEM
    return pl.pallas_call(
        functools.partial(_scatter_rows_kernel, steps_a=steps_a),
        grid=(steps_a + steps_b,),
        in_specs=[pl.BlockSpec(fill_start.shape, lambda i: (0,), memory_space=smem),
                  pl.BlockSpec((t,), lambda i: (in_a(i),), memory_space=smem),
                  pl.BlockSpec((t,), lambda i: (in_b(i),), memory_space=smem),
                  pl.BlockSpec((t, w), lambda i: (in_a(i), 0)),
                  pl.BlockSpec((t, w), lambda i: (in_b(i), 0))],
        out_specs=pl.BlockSpec(memory_space=pl.ANY),
        out_shape=jax.ShapeDtypeStruct((m_rows, w), xa.dtype),
        scratch_shapes=[pltpu.VMEM((fill_rows, w), xa.dtype), pltpu.SemaphoreType.DMA],
        compiler_params=_cparams(("arbitrary",)),
        name="moe_scatter_rows",
    )(fill_start, dest_a, dest_b, xa, xb)


def _gather_rows_kernel(dest_ref, ys_ref, y_ref, sem):
    t = y_ref.shape[0]

    def issue(g, carry):
        base = pl.multiple_of(g * SUBLANES, SUBLANES)
        tile_rows = y_ref.at[pl.ds(base, SUBLANES)]
        for u in range(SUBLANES):
            pltpu.make_async_copy(ys_ref.at[pl.ds(dest_ref[base + u], 1)], tile_rows.at[pl.ds(u, 1)],
                                  sem).start(priority=u % 2)
        return carry

    lax.fori_loop(0, t // SUBLANES, issue, 0, unroll=2)
    pltpu.make_async_copy(ys_ref.at[pl.ds(0, t)], y_ref, sem).wait()


def _gather_rows(dest, ys, t):
    n = dest.shape[0]
    w = ys.shape[1]
    return pl.pallas_call(
        _gather_rows_kernel,
        grid=(n // t,),
        in_specs=[pl.BlockSpec((t,), lambda i: (i,), memory_space=pltpu.SMEM),
                  pl.BlockSpec(memory_space=pl.ANY)],
        out_specs=pl.BlockSpec((t, w), lambda i: (i, 0)),
        out_shape=jax.ShapeDtypeStruct((n, w), ys.dtype),
        scratch_shapes=[pltpu.SemaphoreType.DMA],
        compiler_params=_cparams(("arbitrary",)),
        name="moe_gather_rows",
    )(dest, ys)


def _moe_kernel(e1_ref, e2_ref, tv_ref, xs_ref, g2_ref, wg1_ref, wu1_ref, wd1_ref, wg2_ref, wu2_ref, wd2_ref, y_ref):
    m = pl.program_id(0)

    @pl.when(tv_ref[m] > 0)
    def _():
        x1 = xs_ref[:, 0:D_MODEL]
        route = xs_ref[:, D_MODEL:]
        lane = lax.broadcasted_iota(jnp.int32, route.shape, 1)
        r = lax.rsqrt(jnp.mean(x1 * x1, axis=-1, keepdims=True) + EPS)
        n2 = ((x1 * r) * g2_ref[...]).astype(BF16)
        acc = x1
        for e_ref, wg_ref, wu_ref, wd_ref in ((e1_ref, wg1_ref, wu1_ref, wd1_ref), (e2_ref, wg2_ref, wu2_ref, wd2_ref)):
            ge = jnp.sum(jnp.where(lane == e_ref[m], route, 0.0), axis=1, keepdims=True)
            hg = jnp.dot(n2, wg_ref[...], preferred_element_type=F32)
            hu = jnp.dot(n2, wu_ref[...], preferred_element_type=F32)
            h = (hg * jax.nn.sigmoid(hg)) * hu * ge
            acc = acc + jnp.dot(h.astype(BF16), wd_ref[...], preferred_element_type=F32)
        y_ref[...] = acc

    @pl.when(tv_ref[m] == 0)
    def _():
        y_ref[...] = jnp.zeros_like(y_ref)


def _moe_sorted(tile_e1, tile_e2, tile_valid, xs, g2, wg, wu, wd, tm):
    m_pad = tile_valid.shape[0] * tm
    up = lambda e: pl.BlockSpec((None, D_MODEL, D_EXPERT), lambda m, e1, e2, tv: ((e1, e2)[e][m], 0, 0))
    down = lambda e: pl.BlockSpec((None, D_EXPERT, D_MODEL), lambda m, e1, e2, tv: ((e1, e2)[e][m], 0, 0))
    grid_spec = pltpu.PrefetchScalarGridSpec(
        num_scalar_prefetch=3,
        grid=(m_pad // tm,),
        in_specs=[
            pl.BlockSpec((tm, ROW_WIDTH), lambda m, e1, e2, tv: (m, 0)),
            pl.BlockSpec(g2.shape, lambda m, e1, e2, tv: (0, 0)),
            up(0), up(0), down(0), up(1), up(1), down(1),
        ],
        out_specs=pl.BlockSpec((tm, D_MODEL), lambda m, e1, e2, tv: (m, 0)),
    )
    return pl.pallas_call(
        _moe_kernel,
        grid_spec=grid_spec,
        out_shape=jax.ShapeDtypeStruct((m_pad, D_MODEL), F32),
        compiler_params=_cparams(("arbitrary",)),
        name="moe_experts",
    )(tile_e1, tile_e2, tile_valid, xs, g2, wg, wu, wd, wg, wu, wd)


def _moe_plan(plan_p, cnt_p, plan_s, cnt_s, tm):
    def ids(plan):
        blocks = plan.reshape(-1, SUBLANES, plan.shape[1]).astype(jnp.int32)
        return blocks[:, 0, :], blocks[:, 1, :]

    def lookup(table, cat):
        out = jnp.zeros_like(cat)
        for c in range(N_CATEGORIES):
            out = jnp.where(cat == c, table[c], out)
        return out

    (kp, rank_p), (ks, rank_s) = ids(plan_p), ids(plan_s)
    n_total = kp.size + ks.size
    n_tiles = n_total // tm + N_CATEGORIES
    cp = cnt_p[:N_CATEGORIES, 0].astype(jnp.int32)
    cs = cnt_s[:N_CATEGORIES, 0].astype(jnp.int32)
    total = cp + cs
    tiles = (total + tm - 1) // tm
    tile_end = jnp.cumsum(tiles)
    offset = (tile_end - tiles) * tm
    dest_p = (lookup(offset, kp) + rank_p).reshape(-1)
    dest_s = (lookup(offset + cp, ks) + rank_s).reshape(-1)
    tile_id = jnp.arange(n_tiles)
    tile_cat = jnp.minimum(jnp.sum(tile_id[:, None] >= tile_end[None, :], axis=1), N_CATEGORIES - 1)
    used = jnp.clip(offset[tile_cat] + total[tile_cat] - tile_id * tm, 0, tm)
    tile_valid = jnp.where(tile_id < tile_end[N_CATEGORIES - 1], used, 0).astype(jnp.int32)
    experts = jnp.asarray(CATEGORY_EXPERTS, jnp.int32)
    tile_e1, tile_e2 = experts[tile_cat, 0], experts[tile_cat, 1]
    m_rows = (n_tiles + 1) * tm
    ends = ((offset + total) // SUBLANES) * SUBLANES
    tail = ends[N_CATEGORIES - 1] + tm * jnp.arange(1, m_rows // tm - n_total // tm + 1)
    fill_start = jnp.concatenate([ends, tail]).astype(jnp.int32)
    return dest_p, dest_s, fill_start, tile_e1, tile_e2, tile_valid, m_rows


def kernel(x_prompt, x_sample, cache_k, cache_v, cache_logf, state_pool, norm1_g, w_in, b_f, q_norm_g, k_norm_g,
           pool_w, pool_scale, w_out, norm2_g, w_router_group, b_router_group, w_router_expert, b_router_expert,
           w_gate, w_up, w_down):
    depth = norm1_g.shape[0]
    assert depth == 1, "single-layer step"
    b_p, s_p, _ = x_prompt.shape
    assert b_p == 1, "prompt kernels assume one stream"
    b_s, t_s, _ = x_sample.shape
    p_len = cache_k.shape[2]
    l = 0

    w = w_in[l]
    a3 = 3 * ATTN_WIDTH
    w_all = jnp.concatenate(
        [w[:, :a3], w[:, a3 + N_HEADS:], jnp.pad(w[:, a3:a3 + N_HEADS], ((0, 0), (0, LANES - N_HEADS)))],
        axis=1).astype(BF16)
    bf_pad = jnp.pad(b_f[l], (0, LANES - N_HEADS)).reshape(1, LANES)
    g1 = norm1_g[l].reshape(1, D_MODEL)
    qg = jnp.tile(q_norm_g[l], N_HEADS).reshape(1, ATTN_WIDTH)
    kg = jnp.tile(k_norm_g[l], N_HEADS).reshape(1, ATTN_WIDTH)
    hr = lax.broadcasted_iota(jnp.int32, (MXU_DIM, MXU_DIM), 0) // HEAD_DIM
    hc = lax.broadcasted_iota(jnp.int32, (MXU_DIM, MXU_DIM), 1) // HEAD_DIM
    hm = jnp.where(hr == hc, 1.0 / HEAD_DIM, 0.0).astype(BF16)
    pw = pool_w[l].astype(BF16)
    ps = pool_scale[l].reshape(1, POOL_WIDTH)
    wo = w_out[l].astype(BF16)
    g2 = norm2_g[l].reshape(1, D_MODEL)
    w_r = jnp.pad(jnp.concatenate([w_router_expert[l], w_router_group[l]], axis=1),
                  ((0, 0), (0, LANES - N_EXPERTS - N_GROUPS)))
    wrh = w_r.astype(BF16)
    wrl = (w_r - wrh.astype(F32)).astype(BF16)
    br = jnp.pad(jnp.concatenate([b_router_expert[l], b_router_group[l]]),
                 (0, LANES - N_EXPERTS - N_GROUPS)).reshape(1, LANES)
    wg = w_gate[l].astype(BF16)
    wu = w_up[l].astype(BF16)
    wd = w_down[l].astype(BF16)

    tile = 512
    query_tile = 512
    key_tile = 256
    expert_tile = 256
    move_tile = 1024

    xp = x_prompt.reshape(s_p, D_MODEL)
    q, kf, _, vf, _, ka, vt, pin, logf, _, edge = _in_project(xp, tile, tile, True, key_tile, g1, w_all, bf_pad, qg, kg,
                                                              hm)
    blocks_per_tile = tile // key_tile
    edge = edge.reshape(s_p // tile, SUBLANES, LANES)[:, :2 * blocks_per_tile, :N_HEADS]
    edge = edge.reshape(s_p // key_tile, 2, N_HEADS)
    c_blocks = edge[:, 0, :]
    j0 = _first_key_blocks(c_blocks, edge[:, 1, :], q_norm_g[l], k_norm_g[l], query_tile, key_tile)
    a = _attention_prompt(j0, c_blocks, q, ka, vt, query_tile, key_tile)
    hist_map = lambda i: (jnp.maximum(i * (tile // HIST_ROWS) - 1, 0), 0)
    xg_p, plan_p, cnt_p = _mix(a, pin, pin, hist_map, xp, tile, tile, 0, tile, True, pw, ps, wo, g2, wrh, wrl, br)
    k_prompt = kf.reshape(depth, b_p, s_p, N_HEADS, HEAD_DIM)
    v_prompt = vf.reshape(depth, b_p, s_p, N_HEADS, HEAD_DIM)
    logf_prompt = logf[:, :N_HEADS].reshape(depth, b_p, s_p, N_HEADS)
    pool_prompt = pin[s_p - POOL_PAD:].reshape(depth, b_p, POOL_PAD, POOL_WIDTH)

    n_s = b_s * t_s
    xs = x_sample.reshape(n_s, D_MODEL)
    q, kf, kb, vf, vb, _, _, pin, logf, d, _ = _in_project(xs, tile, t_s, False, key_tile, g1, w_all, bf_pad, qg, kg, hm)
    dt4 = d[:, :N_HEADS].reshape(b_s, t_s, N_PAIRS, 2).transpose(0, 2, 3, 1)
    clf = cache_logf[l].astype(F32).transpose(0, 2, 1).reshape(b_s * N_HEADS, p_len)
    r4 = _suffix_sums(clf).reshape(b_s, N_PAIRS, 2, p_len)
    ck = cache_k[l].reshape(b_s, p_len, ATTN_WIDTH).astype(BF16)
    cv = cache_v[l].reshape(b_s, p_len, ATTN_WIDTH).astype(BF16)
    a = _attention_sample(q, kb, vb, ck, cv, d, dt4, r4, t_s)
    hist_s = jnp.pad(state_pool[l], ((0, 0), (HIST_ROWS - POOL_PAD, 0), (0, 0))).reshape(b_s * HIST_ROWS, POOL_WIDTH)
    xg_s, plan_s, cnt_s = _mix(a, pin, hist_s, lambda i: (i, 0), xs, tile, t_s, p_len, 0, False, pw, ps, wo, g2, wrh,
                               wrl, br)
    k_sample = kf.reshape(depth, b_s, t_s, N_HEADS, HEAD_DIM)
    v_sample = vf.reshape(depth, b_s, t_s, N_HEADS, HEAD_DIM)
    logf_sample = logf[:, :N_HEADS].reshape(depth, b_s, t_s, N_HEADS)
    pool_sample = pin.reshape(b_s, t_s, POOL_WIDTH)[:, t_s - POOL_PAD:].reshape(depth, b_s, POOL_PAD, POOL_WIDTH)

    dest_p, dest_s, fill_start, tile_e1, tile_e2, tile_valid, m_rows = _moe_plan(plan_p, cnt_p, plan_s, cnt_s,
                                                                              expert_tile)
    rows = _scatter_rows(fill_start, dest_p, xg_p, dest_s, xg_s, m_rows, move_tile, expert_tile)
    ys = _moe_sorted(tile_e1, tile_e2, tile_valid, rows, g2, wg, wu, wd, expert_tile)
    y_prompt = _gather_rows(dest_p, ys, 2 * move_tile).reshape(b_p, s_p, D_MODEL)
    y_sample = _gather_rows(dest_s, ys, move_tile).reshape(b_s, t_s, D_MODEL)

    return (y_prompt, y_sample, k_prompt, v_prompt, logf_prompt, pool_prompt,
            k_sample, v_sample, logf_sample, pool_sample)
```

```python
import functools

import jax
import jax.numpy as jnp
from jax import lax
from jax.experimental import pallas as pl
from jax.experimental.pallas import tpu as pltpu

F32 = jnp.float32
BF16 = jnp.bfloat16

D_MODEL = 1024
ATTN_WIDTH = 512
N_HEADS = 8
HEAD_DIM = 64
POOL_WIDTH = 512
POOL_WINDOWS = (2, 4, 8, 16)
POOL_GROUP_DIM = 128
POOL_PAD = 15
HIST_ROWS = 16
N_GROUPS = 4
EXPERTS_PER_GROUP = 4
N_EXPERTS = 16
D_EXPERT = 512
EPS = 1e-6
F32_EXP_UNDERFLOW = 104.0
BF16_SLACK = 1.02
SKIP_MARGIN = 1.0
LANES = 128
SUBLANES = 8
MXU_DIM = 256
PAIR = 2 * HEAD_DIM
N_PAIRS = N_HEADS // 2
ROWSUM_ROW = (HEAD_DIM, 0)
VMEM_LIMIT = 56 * 1024 * 1024
ROW_WIDTH = D_MODEL + LANES
PAIRS_PER_GROUP = 6
N_CATEGORIES = N_GROUPS * PAIRS_PER_GROUP
PAIR_ORDER = ((0, 1), (0, 2), (1, 2), (1, 3), (2, 3), (0, 3))
CATEGORY_EXPERTS = tuple((4 * g + a, 4 * g + b) for g in range(N_GROUPS) for a, b in PAIR_ORDER)
ROUTE_CATEGORY_LANE = N_EXPERTS
ROUTE_RANK_LANE = N_EXPERTS + 1
ROUTE_SLOTS = 32


def _cparams(sem):
    return pltpu.CompilerParams(dimension_semantics=sem, vmem_limit_bytes=VMEM_LIMIT)


def _split3(x):
    hi = x.astype(BF16)
    r1 = x - hi.astype(F32)
    mid = r1.astype(BF16)
    lo = (r1 - mid.astype(F32)).astype(BF16)
    return hi, mid, lo


def _split2(x):
    hi = x.astype(BF16)
    lo = (x - hi.astype(F32)).astype(BF16)
    return hi, lo


def _inproj_kernel(x_ref, g1_ref, w_ref, bf_ref, qg_ref, kg_ref, hm_ref, tri_ref,
                   q_ref, kf_ref, kb_ref, vf_ref, vb_ref, ka_ref, vt_ref, p_ref, logf_ref, c_ref, edge_ref, carry_ref, *,
                   carry_rows, key_blk):
    x = x_ref[...]
    r = lax.rsqrt(jnp.mean(x * x, axis=-1, keepdims=True) + EPS)
    n = ((x * r) * g1_ref[...]).astype(BF16)
    z = jnp.dot(n, w_ref[...], preferred_element_type=F32)

    hm = hm_ref[...]

    def head_norm(zz, g):
        hi, lo = _split2(zz * zz)
        half = hm.shape[0]
        ms = jnp.concatenate(
            [jnp.dot(hi[:, r:r + half], hm, preferred_element_type=F32)
             + jnp.dot(lo[:, r:r + half], hm, preferred_element_type=F32) for r in range(0, zz.shape[1], half)], axis=1)
        return (zz * lax.rsqrt(ms + EPS)) * g

    q = head_norm(z[:, 0:ATTN_WIDTH], qg_ref[...])
    q_ref[...] = (q * (HEAD_DIM ** -0.5)).astype(BF16)
    k = head_norm(z[:, ATTN_WIDTH:2 * ATTN_WIDTH], kg_ref[...])
    kf_ref[...] = k
    kb_ref[...] = k.astype(BF16)
    v = z[:, 2 * ATTN_WIDTH:3 * ATTN_WIDTH]
    vf_ref[...] = v
    vb_ref[...] = v.astype(BF16)
    p_ref[...] = z[:, 3 * ATTN_WIDTH:3 * ATTN_WIDTH + POOL_WIDTH]
    prow = lax.broadcasted_iota(jnp.int32, (PAIR, v.shape[0]), 0)
    for hp in range(N_PAIRS):
        vpair_t = v[:, hp * PAIR:(hp + 1) * PAIR].T
        for h in range(2):
            own = (prow < HEAD_DIM) if h == 0 else (prow >= HEAD_DIM)
            marker = jnp.where(prow == ROWSUM_ROW[h], 1.0, 0.0)
            vt_ref[(2 * hp + h) * LANES:(2 * hp + h + 1) * LANES, :] = jnp.where(own, vpair_t, marker).astype(BF16)

    f = z[:, 3 * ATTN_WIDTH + POOL_WIDTH:] + bf_ref[...]
    lane = lax.broadcasted_iota(jnp.int32, f.shape, 1)
    logf = -(jnp.maximum(-f, 0.0) + jnp.log1p(jnp.exp(-jnp.abs(f))))
    logf = jnp.where(lane < N_HEADS, logf, 0.0)
    logf_ref[...] = logf

    tri = tri_ref[...]
    blk = tri.shape[0]
    parts = _split3(logf)
    pieces = []
    for r in range(0, logf.shape[0], blk):
        piece = sum(jnp.dot(tri, part[r:r + blk, :], preferred_element_type=F32) for part in parts)
        if carry_rows and pieces:
            piece = piece + pieces[-1][blk - 1:blk, :]
        pieces.append(piece)
    c = jnp.concatenate(pieces, axis=0)
    if carry_rows:
        @pl.when(pl.program_id(0) == 0)
        def _():
            carry_ref[...] = jnp.zeros_like(carry_ref)
        c = c + carry_ref[0:1, :]
        carry_ref[...] = jnp.broadcast_to(c[c.shape[0] - 1:, :], carry_ref.shape)
    c_ref[...] = c
    edges = [c[r:r + 1, :] for b in range(c.shape[0] // key_blk) for r in (b * key_blk, (b + 1) * key_blk - 1)]
    edge_ref[...] = jnp.concatenate(edges + [jnp.zeros((SUBLANES - len(edges), LANES), F32)], axis=0)

    tm = c.shape[0]
    rel = jnp.concatenate([c[r:r + key_blk, :] - c[r:r + 1, :] for r in range(0, tm, key_blk)], axis=0)
    hi, mid, lo = (part.astype(F32) for part in _split3(rel))
    decay = (hi + pltpu.roll(mid, N_HEADS, axis=1) + pltpu.roll(lo, 2 * N_HEADS, axis=1)).astype(BF16)
    kb = k.astype(BF16)
    for hp in range(N_PAIRS):
        ka_ref[:, 2 * hp * LANES:(2 * hp + 1) * LANES] = kb[:, hp * PAIR:(hp + 1) * PAIR]
        ka_ref[:, (2 * hp + 1) * LANES:(2 * hp + 2) * LANES] = decay


def _decay_lane(head, part):
    return part * N_HEADS + head


def _in_project(x2d, tm, seg, carry_rows, key_blk, g1, w_all, bf_pad, qg, kg, hm):
    n = x2d.shape[0]
    blk = min(tm, MXU_DIM)
    assert seg % blk == 0 or blk % seg == 0, "running-sum segments must nest with the triangular block"
    rows = lax.broadcasted_iota(jnp.int32, (blk, blk), 0)
    cols = lax.broadcasted_iota(jnp.int32, (blk, blk), 1)
    tri = ((cols <= rows) & (rows // seg == cols // seg)).astype(BF16)
    row_blk = lambda w: pl.BlockSpec((tm, w), lambda i: (i, 0))
    const = lambda a: pl.BlockSpec(a.shape, lambda i: (0,) * a.ndim)
    out_shape = (
        jax.ShapeDtypeStruct((n, ATTN_WIDTH), BF16),
        jax.ShapeDtypeStruct((n, ATTN_WIDTH), F32),
        jax.ShapeDtypeStruct((n, ATTN_WIDTH), BF16),
        jax.ShapeDtypeStruct((n, ATTN_WIDTH), F32),
        jax.ShapeDtypeStruct((n, ATTN_WIDTH), BF16),
        jax.ShapeDtypeStruct((n, N_PAIRS * 2 * LANES), BF16),
        jax.ShapeDtypeStruct((N_HEADS * LANES, n), BF16),
        jax.ShapeDtypeStruct((n, POOL_WIDTH), F32),
        jax.ShapeDtypeStruct((n, LANES), F32),
        jax.ShapeDtypeStruct((n, LANES), F32),
        jax.ShapeDtypeStruct((n // tm * SUBLANES, LANES), F32),
    )
    return pl.pallas_call(
        functools.partial(_inproj_kernel, carry_rows=carry_rows, key_blk=key_blk),
        grid=(n // tm,),
        in_specs=[row_blk(D_MODEL), const(g1), const(w_all), const(bf_pad), const(qg), const(kg), const(hm),
                  const(tri)],
        out_specs=[row_blk(ATTN_WIDTH)] * 5 + [row_blk(N_PAIRS * 2 * LANES),
                                                pl.BlockSpec((N_HEADS * LANES, tm), lambda i: (0, i)),
                                                row_blk(POOL_WIDTH), row_blk(LANES), row_blk(LANES),
                                                pl.BlockSpec((SUBLANES, LANES), lambda i: (i, 0))],
        out_shape=out_shape,
        scratch_shapes=[pltpu.VMEM((8, LANES), F32)],
        compiler_params=_cparams(("arbitrary",)),
        name="in_project",
    )(x2d, g1, w_all, bf_pad, qg, kg, hm, tri)


def _attn_prompt_kernel(j0_ref, cb_ref, q_ref, k_ref, vt_ref, o_ref, acc_ref, m_ref, *, tq, tk):
    hp = pl.program_id(0)
    i = pl.program_id(1)
    nk = tq // tk
    lane = lax.broadcasted_iota(jnp.int32, (1, PAIR), 1)
    first = lane < HEAD_DIM
    q = q_ref[...]
    zero = jnp.zeros_like(q)
    qw = []
    for h in range(2):
        minus = jnp.zeros((1, LANES), F32)
        for part in range(3):
            minus = jnp.where(lane == _decay_lane(2 * hp + h, part), -1.0, minus)
        qw.append(jnp.concatenate([jnp.where(first == (h == 0), q, zero),
                                   jnp.broadcast_to(minus, (tq, LANES)).astype(BF16)], axis=1))
    q_start = i * tq

    acc_ref[...] = jnp.zeros_like(acc_ref)
    m_ref[...] = jnp.full_like(m_ref, -jnp.inf)

    last = (i + 1) * nk - 1

    def scores(h, j, masked=True):
        start = pl.multiple_of(j * tk, tk)
        k = k_ref[pl.ds(start, tk), :]
        s = lax.dot_general(k, qw[h], (((1,), (1,)), ((), ())), preferred_element_type=F32)
        if masked:
            key = lax.broadcasted_iota(jnp.int32, s.shape, 0)
            qry = lax.broadcasted_iota(jnp.int32, s.shape, 1)
            s = jnp.where(key - qry <= q_start - j * tk, s, -jnp.inf)
        return s, jnp.max(s, axis=0, keepdims=True)

    def update(h, j, s, colmax):
        shift = cb_ref[j, 2 * hp + h] - cb_ref[i * nk, 2 * hp + h]
        start = pl.multiple_of(j * tk, tk)
        vt = vt_ref[h * LANES:(h + 1) * LANES, pl.ds(start, tk)]
        m_prev = m_ref[h]
        m_new = jnp.maximum(m_prev, colmax - shift)
        alpha = jnp.exp(m_prev - m_new)
        p = jnp.exp(s - (m_new + shift))
        m_ref[h] = m_new
        acc_ref[h] = acc_ref[h] * alpha + jnp.dot(vt, p.astype(BF16), preferred_element_type=F32)

    j_first = [j0_ref[i, 2 * hp + h] for h in range(2)]
    j_both = jnp.maximum(j_first[0], j_first[1])
    start = [scores(h, j_first[h]) for h in range(2)]
    ahead = []
    for h in range(2):
        def body_one(j, carry, h=h):
            nxt = scores(h, j + 1)
            update(h, j, *carry)
            return nxt

        ahead.append(lax.fori_loop(j_first[h], j_both, body_one, start[h]))

    def body_both(j, carry, masked):
        nxt = (scores(0, j + 1, masked), scores(1, j + 1, masked))
        update(0, j, *carry[0])
        update(1, j, *carry[1])
        return nxt

    j_mask = jnp.maximum(j_both, i * nk - 1)
    pairs = (j_mask - j_both) // 2

    def body_two(t, carry):
        j = j_both + 2 * t
        return body_both(j + 1, body_both(j, carry, False), False)

    carry = lax.fori_loop(0, pairs, body_two, (ahead[0], ahead[1]))
    carry = lax.fori_loop(j_both + 2 * pairs, last, functools.partial(body_both, masked=True), carry)
    update(0, last, *carry[0])
    update(1, last, *carry[1])

    out = [acc_ref[h] / acc_ref[h][ROWSUM_ROW[h]:ROWSUM_ROW[h] + 1, :] for h in range(2)]
    row = lax.broadcasted_iota(jnp.int32, (PAIR, 1), 0)
    o_ref[...] = jnp.where(row < HEAD_DIM, out[0], out[1]).T.astype(o_ref.dtype)


def _attention_prompt(j0, c_blocks, q, ka, vt, tq, tk):
    n = q.shape[0]
    grid_spec = pltpu.PrefetchScalarGridSpec(
        num_scalar_prefetch=2,
        grid=(N_PAIRS, n // tq),
        in_specs=[
            pl.BlockSpec((tq, PAIR), lambda hp, i, j0, cb: (i, hp)),
            pl.BlockSpec((n, 2 * LANES), lambda hp, i, j0, cb: (0, hp)),
            pl.BlockSpec((2 * LANES, n), lambda hp, i, j0, cb: (hp, 0)),
        ],
        out_specs=pl.BlockSpec((tq, PAIR), lambda hp, i, j0, cb: (i, hp)),
        scratch_shapes=[pltpu.VMEM((2, LANES, tq), F32), pltpu.VMEM((2, 1, tq), F32)],
    )
    return pl.pallas_call(
        functools.partial(_attn_prompt_kernel, tq=tq, tk=tk),
        grid_spec=grid_spec,
        out_shape=jax.ShapeDtypeStruct((n, ATTN_WIDTH), BF16),
        compiler_params=_cparams(("arbitrary", "arbitrary")),
        name="attention_prompt",
    )(j0, c_blocks, q, ka, vt)


def _first_key_blocks(c_first, c_last, q_gain, k_gain, tq, tk):
    n_k = c_first.shape[0]
    n_q = n_k * tk // tq
    qk_bound = BF16_SLACK * HEAD_DIM ** 0.5 * jnp.max(jnp.abs(q_gain)) * jnp.max(jnp.abs(k_gain))
    threshold = F32_EXP_UNDERFLOW + 2.0 * qk_bound + SKIP_MARGIN
    c_start = c_first[0::tq // tk]
    gap = c_start[:, None, :] - c_last[None, :, :]
    earlier = (jnp.arange(n_k)[None, :, None] + 1) * tk <= jnp.arange(n_q)[:, None, None] * tq
    return jnp.sum((gap < -threshold) & earlier, axis=1).astype(jnp.int32)


def _suffix_kernel(x_ref, o_ref):
    x = x_ref[...]
    p_len = x.shape[1]
    rows = lax.broadcasted_iota(jnp.int32, (p_len, p_len), 0)
    cols = lax.broadcasted_iota(jnp.int32, (p_len, p_len), 1)
    u = (rows > cols).astype(BF16)
    hi, mid, lo = _split3(x)
    o_ref[...] = (jnp.dot(hi, u, preferred_element_type=F32) + jnp.dot(mid, u, preferred_element_type=F32)
                  + jnp.dot(lo, u, preferred_element_type=F32))


def _suffix_sums(x):
    return pl.pallas_call(
        _suffix_kernel,
        out_shape=jax.ShapeDtypeStruct(x.shape, F32),
        compiler_params=pltpu.CompilerParams(vmem_limit_bytes=VMEM_LIMIT),
        name="cache_suffix_sums",
    )(x)


def _attn_sample_kernel(q_ref, kn_ref, vn_ref, ck_ref, cv_ref, d_ref, dt_ref, r_ref, o_ref):
    lane = lax.broadcasted_iota(jnp.int32, (1, PAIR), 1)
    first = lane < HEAD_DIM
    dblk = d_ref[...]
    dlane = lax.broadcasted_iota(jnp.int32, dblk.shape, 1)
    nt = (((1,), (1,)), ((), ()))
    for hp in range(N_PAIRS):
        cols = slice(hp * PAIR, (hp + 1) * PAIR)
        q = q_ref[:, cols]
        zero = jnp.zeros_like(q)
        qh = (jnp.where(first, q, zero), jnp.where(first, zero, q))
        kn = kn_ref[:, cols]
        vn = vn_ref[:, cols]
        ck = ck_ref[:, cols]
        cv = cv_ref[:, cols]
        outs, ls = [], []
        for h in range(2):
            dq = jnp.sum(jnp.where(dlane == 2 * hp + h, dblk, 0.0), axis=1, keepdims=True)
            s1 = lax.dot_general(qh[h], ck, nt, preferred_element_type=F32) + dq + r_ref[hp, h:h + 1, :]
            s2 = lax.dot_general(qh[h], kn, nt, preferred_element_type=F32) + dq - dt_ref[hp, h:h + 1, :]
            row = lax.broadcasted_iota(jnp.int32, s2.shape, 0)
            col = lax.broadcasted_iota(jnp.int32, s2.shape, 1)
            s2 = jnp.where(col <= row, s2, -jnp.inf)
            m = jnp.maximum(jnp.max(s1, axis=1, keepdims=True), jnp.max(s2, axis=1, keepdims=True))
            p1 = jnp.exp(s1 - m)
            p2 = jnp.exp(s2 - m)
            ls.append(jnp.sum(p1, axis=1, keepdims=True) + jnp.sum(p2, axis=1, keepdims=True))
            outs.append(jnp.dot(p1.astype(BF16), cv, preferred_element_type=F32)
                        + jnp.dot(p2.astype(BF16), vn, preferred_element_type=F32))
        o_ref[:, cols] = (jnp.where(first, outs[0], outs[1]) / jnp.where(first, ls[0], ls[1])).astype(o_ref.dtype)


def _attention_sample(q, kb, vb, cache_k, cache_v, d, dt4, r4, t):
    nb, p_len = cache_k.shape[0], cache_k.shape[1]
    row_blk = pl.BlockSpec((t, ATTN_WIDTH), lambda b: (b, 0))
    cache_blk = pl.BlockSpec((None, p_len, ATTN_WIDTH), lambda b: (b, 0, 0))
    return pl.pallas_call(
        _attn_sample_kernel,
        grid=(nb,),
        in_specs=[row_blk, row_blk, row_blk, cache_blk, cache_blk,
                  pl.BlockSpec((t, LANES), lambda b: (b, 0)),
                  pl.BlockSpec((None, N_PAIRS, 2, t), lambda b: (b, 0, 0, 0)),
                  pl.BlockSpec((None, N_PAIRS, 2, p_len), lambda b: (b, 0, 0, 0))],
        out_specs=row_blk,
        out_shape=jax.ShapeDtypeStruct((nb * t, ATTN_WIDTH), BF16),
        compiler_params=_cparams(("arbitrary",)),
        name="attention_sample",
    )(q, kb, vb, cache_k, cache_v, d, dt4, r4)


def _mix_kernel(a_ref, p_ref, hist_ref, x_ref, pw_ref, ps_ref, wo_ref, g2_ref, wrh_ref, wrl_ref, br_ref, tri_ref,
                xg_ref, plan_ref, cnt_ref, *, seg, pos0, pos_stride, zero_first):
    i = pl.program_id(0)
    t = p_ref.shape[0]
    p = p_ref[...]
    hist = hist_ref[...]
    if zero_first:
        hist = jnp.where(i == 0, 0.0, hist)
    ext = seg + HIST_ROWS
    xh = jnp.concatenate([piece for b in range(t // seg)
                          for piece in (hist[b * HIST_ROWS:(b + 1) * HIST_ROWS, :], p[b * seg:(b + 1) * seg, :])], axis=0)

    def own_rows(v):
        return jnp.concatenate([v[b * ext + HIST_ROWS:(b + 1) * ext, :] for b in range(t // seg)], axis=0)

    pos = pos0 + i * pos_stride + lax.broadcasted_iota(jnp.int32, (t, 1), 0) % seg
    ys = []
    for g, w in enumerate(POOL_WINDOWS):
        col = xh[:, g * POOL_GROUP_DIM:(g + 1) * POOL_GROUP_DIM]
        acc = col
        span = 1
        while span < w:
            acc = acc + pltpu.roll(acc, span, axis=0)
            span *= 2
        cnt = jnp.minimum(pos + 1, w).astype(F32)
        dgrp = own_rows(acc) / cnt - own_rows(col)
        ys.append(jnp.dot(dgrp.astype(BF16), pw_ref[g], preferred_element_type=F32))
    pm = (jnp.concatenate(ys, axis=1) * ps_ref[...]).astype(BF16)
    mix = (jnp.dot(a_ref[...], wo_ref[0:ATTN_WIDTH, :], preferred_element_type=F32)
           + jnp.dot(pm, wo_ref[ATTN_WIDTH:, :], preferred_element_type=F32))
    x1 = x_ref[...] + mix
    xg_ref[:, 0:D_MODEL] = x1

    r = lax.rsqrt(jnp.mean(x1 * x1, axis=-1, keepdims=True) + EPS)
    n2 = (x1 * r) * g2_ref[...]

    nh, nl = _split2(n2)
    wrh = wrh_ref[...]
    logits = (jnp.dot(nh, wrh, preferred_element_type=F32) + jnp.dot(nl, wrh, preferred_element_type=F32)
              + jnp.dot(nh, wrl_ref[...], preferred_element_type=F32)) + br_ref[...]
    z = logits.T[0:ROUTE_SLOTS, :]
    slot = lax.broadcasted_iota(jnp.int32, z.shape, 0).astype(F32)
    neg = -jnp.inf
    big = jnp.float32(1 << 20)
    is_g = (slot >= N_EXPERTS) & (slot < N_EXPERTS + N_GROUPS)
    lg = jnp.where(is_g, z, neg)
    gmax = jnp.max(lg, axis=0, keepdims=True)
    g_slot = jnp.min(jnp.where(lg == gmax, slot, big), axis=0, keepdims=True)
    top_pg = 1.0 / jnp.sum(jnp.exp(lg - gmax), axis=0, keepdims=True)
    g_idx = g_slot - N_EXPERTS
    sel = (slot >= g_idx * EXPERTS_PER_GROUP) & (slot < (g_idx + 1) * EXPERTS_PER_GROUP)
    le = jnp.where(sel, z, neg)
    m1 = jnp.max(le, axis=0, keepdims=True)
    i1 = jnp.min(jnp.where(le == m1, slot, big), axis=0, keepdims=True)
    le2 = jnp.where(slot == i1, neg, le)
    m2 = jnp.max(le2, axis=0, keepdims=True)
    i2 = jnp.min(jnp.where(le2 == m2, slot, big), axis=0, keepdims=True)
    e2 = jnp.exp(m2 - m1)
    w1 = top_pg / (1.0 + e2)
    w2 = top_pg * e2 / (1.0 + e2)
    gate = jnp.where(slot == i1, w1, 0.0) + jnp.where(slot == i2, w2, 0.0)

    e_lo = jnp.minimum(i1, i2) - g_idx * EXPERTS_PER_GROUP
    e_hi = jnp.maximum(i1, i2) - g_idx * EXPERTS_PER_GROUP
    pair = jnp.zeros_like(e_lo)
    for idx, (a, b) in enumerate(PAIR_ORDER):
        pair = jnp.where((e_lo == a) & (e_hi == b), float(idx), pair)
    category = g_idx * PAIRS_PER_GROUP + pair

    @pl.when(i == 0)
    def _():
        cnt_ref[...] = jnp.zeros_like(cnt_ref)
    member = slot == category
    tri = tri_ref[...]
    blk = tri.shape[0]
    nt = (((1,), (1,)), ((), ()))
    seen = cnt_ref[:, 0:1]
    pieces = []
    for r in range(0, t, blk):
        mem = member[:, r:r + blk].astype(BF16)
        pieces.append(lax.dot_general(mem, tri, nt, preferred_element_type=F32) + seen)
        seen = seen + jnp.sum(mem.astype(F32), axis=1, keepdims=True)
    before = jnp.concatenate(pieces, axis=1)
    rank = jnp.sum(jnp.where(member, before, 0.0), axis=0, keepdims=True)
    cnt_ref[...] = jnp.broadcast_to(seen, cnt_ref.shape)
    route = jnp.where(slot == ROUTE_CATEGORY_LANE, category, jnp.where(slot == ROUTE_RANK_LANE, rank, gate))
    route = jnp.concatenate([route, jnp.zeros((LANES - ROUTE_SLOTS, t), F32)], axis=0)
    xg_ref[:, D_MODEL:] = route.T
    srow = lax.broadcasted_iota(jnp.int32, (SUBLANES, t), 0)
    plan_ref[...] = jnp.where(srow == 0, category, jnp.where(srow == 1, rank, 0.0))


def _mix(a, p, hist_src, hist_map, x2d, t, seg, pos0, pos_stride, zero_first, pw, ps, wo, g2, wrh, wrl, br):
    n = x2d.shape[0]
    blk = min(t, MXU_DIM)
    rows = lax.broadcasted_iota(jnp.int32, (blk, blk), 0)
    cols = lax.broadcasted_iota(jnp.int32, (blk, blk), 1)
    tri = (cols < rows).astype(BF16)
    row_blk = lambda w: pl.BlockSpec((t, w), lambda i: (i, 0))
    const = lambda arr: pl.BlockSpec(arr.shape, lambda i: (0,) * arr.ndim)
    return pl.pallas_call(
        functools.partial(_mix_kernel, seg=seg, pos0=pos0, pos_stride=pos_stride, zero_first=zero_first),
        grid=(n // t,),
        in_specs=[row_blk(ATTN_WIDTH), row_blk(POOL_WIDTH),
                  pl.BlockSpec((t // seg * HIST_ROWS, POOL_WIDTH), hist_map),
                  row_blk(D_MODEL), const(pw), const(ps), const(wo), const(g2), const(wrh), const(wrl), const(br),
                  const(tri)],
        out_specs=[row_blk(ROW_WIDTH), pl.BlockSpec((SUBLANES, t), lambda i: (i, 0)),
                   pl.BlockSpec((ROUTE_SLOTS, LANES), lambda i: (0, 0))],
        out_shape=(jax.ShapeDtypeStruct((n, ROW_WIDTH), F32), jax.ShapeDtypeStruct((n // t * SUBLANES, t), F32),
                   jax.ShapeDtypeStruct((ROUTE_SLOTS, LANES), F32)),
        compiler_params=_cparams(("arbitrary",)),
        name="pool_mix_route",
    )(a, p, hist_src, x2d, pw, ps, wo, g2, wrh, wrl, br, tri)


def _issue_row_scatter(dest_ref, x_ref, xs_ref, sem):
    t = x_ref.shape[0]

    def issue(g, carry):
        base = pl.multiple_of(g * SUBLANES, SUBLANES)
        tile_rows = x_ref.at[pl.ds(base, SUBLANES)]
        for u in range(SUBLANES):
            pltpu.make_async_copy(tile_rows.at[pl.ds(u, 1)], xs_ref.at[pl.ds(dest_ref[base + u], 1)],
                                  sem).start(priority=u % 2)
        return carry

    lax.fori_loop(0, t // SUBLANES, issue, 0, unroll=2)
    pltpu.make_async_copy(x_ref, xs_ref.at[pl.ds(0, t)], sem).wait()


def _scatter_rows_kernel(fill_ref, dest_a_ref, dest_b_ref, xa_ref, xb_ref, xs_ref, zero_ref, sem, *, steps_a):
    step = pl.program_id(0)

    @pl.when(step == 0)
    def _():
        zero_ref[...] = jnp.zeros_like(zero_ref)
        fill_rows = zero_ref.shape[0]
        last_start = xs_ref.shape[0] - fill_rows
        for g in range(fill_ref.shape[0]):
            @pl.when(fill_ref[g] < xs_ref.shape[0])
            def _(g=g):
                start = pl.multiple_of(jnp.minimum(fill_ref[g], last_start), SUBLANES)
                fill = pltpu.make_async_copy(zero_ref, xs_ref.at[pl.ds(start, fill_rows)], sem)
                fill.start()
                fill.wait()

    @pl.when(step < steps_a)
    def _():
        _issue_row_scatter(dest_a_ref, xa_ref, xs_ref, sem)

    @pl.when(step >= steps_a)
    def _():
        _issue_row_scatter(dest_b_ref, xb_ref, xs_ref, sem)


def _scatter_rows(fill_start, dest_a, xa, dest_b, xb, m_rows, t, fill_rows):
    w = xa.shape[1]
    steps_a, steps_b = xa.shape[0] // t, xb.shape[0] // t
    in_a = lambda i: jnp.minimum(i, steps_a - 1)
    in_b = lambda i: jnp.maximum(i - steps_a, 0)
    smem = pltpu.SMEM
    return pl.pallas_call(
        functools.partial(_scatter_rows_kernel, steps_a=steps_a),
        grid=(steps_a + steps_b,),
        in_specs=[pl.BlockSpec(fill_start.shape, lambda i: (0,), memory_space=smem),
                  pl.BlockSpec((t,), lambda i: (in_a(i),), memory_space=smem),
                  pl.BlockSpec((t,), lambda i: (in_b(i),), memory_space=smem),
                  pl.BlockSpec((t, w), lambda i: (in_a(i), 0)),
                  pl.BlockSpec((t, w), lambda i: (in_b(i), 0))],
        out_specs=pl.BlockSpec(memory_space=pl.ANY),
        out_shape=jax.ShapeDtypeStruct((m_rows, w), xa.dtype),
        scratch_shapes=[pltpu.VMEM((fill_rows, w), xa.dtype), pltpu.SemaphoreType.DMA],
        compiler_params=_cparams(("arbitrary",)),
        name="moe_scatter_rows",
    )(fill_start, dest_a, dest_b, xa, xb)


def _gather_rows_kernel(dest_ref, ys_ref, y_ref, sem):
    t = y_ref.shape[0]

    def issue(g, carry):
        base = pl.multiple_of(g * SUBLANES, SUBLANES)
        tile_rows = y_ref.at[pl.ds(base, SUBLANES)]
        for u in range(SUBLANES):
            pltpu.make_async_copy(ys_ref.at[pl.ds(dest_ref[base + u], 1)], tile_rows.at[pl.ds(u, 1)],
                                  sem).start(priority=u % 2)
        return carry

    lax.fori_loop(0, t // SUBLANES, issue, 0, unroll=2)
    pltpu.make_async_copy(ys_ref.at[pl.ds(0, t)], y_ref, sem).wait()


def _gather_rows(dest, ys, t):
    n = dest.shape[0]
    w = ys.shape[1]
    return pl.pallas_call(
        _gather_rows_kernel,
        grid=(n // t,),
        in_specs=[pl.BlockSpec((t,), lambda i: (i,), memory_space=pltpu.SMEM),
                  pl.BlockSpec(memory_space=pl.ANY)],
        out_specs=pl.BlockSpec((t, w), lambda i: (i, 0)),
        out_shape=jax.ShapeDtypeStruct((n, w), ys.dtype),
        scratch_shapes=[pltpu.SemaphoreType.DMA],
        compiler_params=_cparams(("arbitrary",)),
        name="moe_gather_rows",
    )(dest, ys)


def _moe_kernel(e1_ref, e2_ref, tv_ref, xs_ref, g2_ref, wg1_ref, wu1_ref, wd1_ref, wg2_ref, wu2_ref, wd2_ref, y_ref):
    m = pl.program_id(0)

    @pl.when(tv_ref[m] > 0)
    def _():
        x1 = xs_ref[:, 0:D_MODEL]
        route = xs_ref[:, D_MODEL:]
        lane = lax.broadcasted_iota(jnp.int32, route.shape, 1)
        r = lax.rsqrt(jnp.mean(x1 * x1, axis=-1, keepdims=True) + EPS)
        n2 = ((x1 * r) * g2_ref[...]).astype(BF16)
        acc = x1
        for e_ref, wg_ref, wu_ref, wd_ref in ((e1_ref, wg1_ref, wu1_ref, wd1_ref), (e2_ref, wg2_ref, wu2_ref, wd2_ref)):
            ge = jnp.sum(jnp.where(lane == e_ref[m], route, 0.0), axis=1, keepdims=True)
            hg = jnp.dot(n2, wg_ref[...], preferred_element_type=F32)
            hu = jnp.dot(n2, wu_ref[...], preferred_element_type=F32)
            h = (hg * jax.nn.sigmoid(hg)) * hu * ge
            acc = acc + jnp.dot(h.astype(BF16), wd_ref[...], preferred_element_type=F32)
        y_ref[...] = acc

    @pl.when(tv_ref[m] == 0)
    def _():
        y_ref[...] = jnp.zeros_like(y_ref)


def _moe_sorted(tile_e1, tile_e2, tile_valid, xs, g2, wg, wu, wd, tm):
    m_pad = tile_valid.shape[0] * tm
    up = lambda e: pl.BlockSpec((None, D_MODEL, D_EXPERT), lambda m, e1, e2, tv: ((e1, e2)[e][m], 0, 0))
    down = lambda e: pl.BlockSpec((None, D_EXPERT, D_MODEL), lambda m, e1, e2, tv: ((e1, e2)[e][m], 0, 0))
    grid_spec = pltpu.PrefetchScalarGridSpec(
        num_scalar_prefetch=3,
        grid=(m_pad // tm,),
        in_specs=[
            pl.BlockSpec((tm, ROW_WIDTH), lambda m, e1, e2, tv: (m, 0)),
            pl.BlockSpec(g2.shape, lambda m, e1, e2, tv: (0, 0)),
            up(0), up(0), down(0), up(1), up(1), down(1),
        ],
        out_specs=pl.BlockSpec((tm, D_MODEL), lambda m, e1, e2, tv: (m, 0)),
    )
    return pl.pallas_call(
        _moe_kernel,
        grid_spec=grid_spec,
        out_shape=jax.ShapeDtypeStruct((m_pad, D_MODEL), F32),
        compiler_params=_cparams(("arbitrary",)),
        name="moe_experts",
    )(tile_e1, tile_e2, tile_valid, xs, g2, wg, wu, wd, wg, wu, wd)


def _moe_plan(plan_p, cnt_p, plan_s, cnt_s, tm):
    def ids(plan):
        blocks = plan.reshape(-1, SUBLANES, plan.shape[1]).astype(jnp.int32)
        return blocks[:, 0, :], blocks[:, 1, :]

    def lookup(table, cat):
        out = jnp.zeros_like(cat)
        for c in range(N_CATEGORIES):
            out = jnp.where(cat == c, table[c], out)
        return out

    (kp, rank_p), (ks, rank_s) = ids(plan_p), ids(plan_s)
    n_total = kp.size + ks.size
    n_tiles = n_total // tm + N_CATEGORIES
    cp = cnt_p[:N_CATEGORIES, 0].astype(jnp.int32)
    cs = cnt_s[:N_CATEGORIES, 0].astype(jnp.int32)
    total = cp + cs
    tiles = (total + tm - 1) // tm
    tile_end = jnp.cumsum(tiles)
    offset = (tile_end - tiles) * tm
    dest_p = (lookup(offset, kp) + rank_p).reshape(-1)
    dest_s = (lookup(offset + cp, ks) + rank_s).reshape(-1)
    tile_id = jnp.arange(n_tiles)
    tile_cat = jnp.minimum(jnp.sum(tile_id[:, None] >= tile_end[None, :], axis=1), N_CATEGORIES - 1)
    used = jnp.clip(offset[tile_cat] + total[tile_cat] - tile_id * tm, 0, tm)
    tile_valid = jnp.where(tile_id < tile_end[N_CATEGORIES - 1], used, 0).astype(jnp.int32)
    experts = jnp.asarray(CATEGORY_EXPERTS, jnp.int32)
    tile_e1, tile_e2 = experts[tile_cat, 0], experts[tile_cat, 1]
    m_rows = (n_tiles + 1) * tm
    ends = ((offset + total) // SUBLANES) * SUBLANES
    tail = ends[N_CATEGORIES - 1] + tm * jnp.arange(1, m_rows // tm - n_total // tm + 1)
    fill_start = jnp.concatenate([ends, tail]).astype(jnp.int32)
    return dest_p, dest_s, fill_start, tile_e1, tile_e2, tile_valid, m_rows


def kernel(x_prompt, x_sample, cache_k, cache_v, cache_logf, state_pool, norm1_g, w_in, b_f, q_norm_g, k_norm_g,
           pool_w, pool_scale, w_out, norm2_g, w_router_group, b_router_group, w_router_expert, b_router_expert,
           w_gate, w_up, w_down):
    depth = norm1_g.shape[0]
    assert depth == 1, "single-layer step"
    b_p, s_p, _ = x_prompt.shape
    assert b_p == 1, "prompt kernels assume one stream"
    b_s, t_s, _ = x_sample.shape
    p_len = cache_k.shape[2]
    l = 0

    w = w_in[l]
    a3 = 3 * ATTN_WIDTH
    w_all = jnp.concatenate(
        [w[:, :a3], w[:, a3 + N_HEADS:], jnp.pad(w[:, a3:a3 + N_HEADS], ((0, 0), (0, LANES - N_HEADS)))],
        axis=1).astype(BF16)
    bf_pad = jnp.pad(b_f[l], (0, LANES - N_HEADS)).reshape(1, LANES)
    g1 = norm1_g[l].reshape(1, D_MODEL)
    qg = jnp.tile(q_norm_g[l], N_HEADS).reshape(1, ATTN_WIDTH)
    kg = jnp.tile(k_norm_g[l], N_HEADS).reshape(1, ATTN_WIDTH)
    hr = lax.broadcasted_iota(jnp.int32, (MXU_DIM, MXU_DIM), 0) // HEAD_DIM
    hc = lax.broadcasted_iota(jnp.int32, (MXU_DIM, MXU_DIM), 1) // HEAD_DIM
    hm = jnp.where(hr == hc, 1.0 / HEAD_DIM, 0.0).astype(BF16)
    pw = pool_w[l].astype(BF16)
    ps = pool_scale[l].reshape(1, POOL_WIDTH)
    wo = w_out[l].astype(BF16)
    g2 = norm2_g[l].reshape(1, D_MODEL)
    w_r = jnp.pad(jnp.concatenate([w_router_expert[l], w_router_group[l]], axis=1),
                  ((0, 0), (0, LANES - N_EXPERTS - N_GROUPS)))
    wrh = w_r.astype(BF16)
    wrl = (w_r - wrh.astype(F32)).astype(BF16)
    br = jnp.pad(jnp.concatenate([b_router_expert[l], b_router_group[l]]),
                 (0, LANES - N_EXPERTS - N_GROUPS)).reshape(1, LANES)
    wg = w_gate[l].astype(BF16)
    wu = w_up[l].astype(BF16)
    wd = w_down[l].astype(BF16)

    tile = 512
    query_tile = 512
    key_tile = 256
    expert_tile = 256
    move_tile = 1024

    xp = x_prompt.reshape(s_p, D_MODEL)
    q, kf, _, vf, _, ka, vt, pin, logf, _, edge = _in_project(xp, tile, tile, True, key_tile, g1, w_all, bf_pad, qg, kg,
                                                              hm)
    blocks_per_tile = tile // key_tile
    edge = edge.reshape(s_p // tile, SUBLANES, LANES)[:, :2 * blocks_per_tile, :N_HEADS]
    edge = edge.reshape(s_p // key_tile, 2, N_HEADS)
    c_blocks = edge[:, 0, :]
    j0 = _first_key_blocks(c_blocks, edge[:, 1, :], q_norm_g[l], k_norm_g[l], query_tile, key_tile)
    a = _attention_prompt(j0, c_blocks, q, ka, vt, query_tile, key_tile)
    hist_map = lambda i: (jnp.maximum(i * (tile // HIST_ROWS) - 1, 0), 0)
    xg_p, plan_p, cnt_p = _mix(a, pin, pin, hist_map, xp, tile, tile, 0, tile, True, pw, ps, wo, g2, wrh, wrl, br)
    k_prompt = kf.reshape(depth, b_p, s_p, N_HEADS, HEAD_DIM)
    v_prompt = vf.reshape(depth, b_p, s_p, N_HEADS, HEAD_DIM)
    logf_prompt = logf[:, :N_HEADS].reshape(depth, b_p, s_p, N_HEADS)
    pool_prompt = pin[s_p - POOL_PAD:].reshape(depth, b_p, POOL_PAD, POOL_WIDTH)

    n_s = b_s * t_s
    xs = x_sample.reshape(n_s, D_MODEL)
    q, kf, kb, vf, vb, _, _, pin, logf, d, _ = _in_project(xs, tile, t_s, False, key_tile, g1, w_all, bf_pad, qg, kg, hm)
    dt4 = d[:, :N_HEADS].reshape(b_s, t_s, N_PAIRS, 2).transpose(0, 2, 3, 1)
    clf = cache_logf[l].astype(F32).transpose(0, 2, 1).reshape(b_s * N_HEADS, p_len)
    r4 = _suffix_sums(clf).reshape(b_s, N_PAIRS, 2, p_len)
    ck = cache_k[l].reshape(b_s, p_len, ATTN_WIDTH).astype(BF16)
    cv = cache_v[l].reshape(b_s, p_len, ATTN_WIDTH).astype(BF16)
    a = _attention_sample(q, kb, vb, ck, cv, d, dt4, r4, t_s)
    hist_s = jnp.pad(state_pool[l], ((0, 0), (HIST_ROWS - POOL_PAD, 0), (0, 0))).reshape(b_s * HIST_ROWS, POOL_WIDTH)
    xg_s, plan_s, cnt_s = _mix(a, pin, hist_s, lambda i: (i, 0), xs, tile, t_s, p_len, 0, False, pw, ps, wo, g2, wrh,
                               wrl, br)
    k_sample = kf.reshape(depth, b_s, t_s, N_HEADS, HEAD_DIM)
    v_sample = vf.reshape(depth, b_s, t_s, N_HEADS, HEAD_DIM)
    logf_sample = logf[:, :N_HEADS].reshape(depth, b_s, t_s, N_HEADS)
    pool_sample = pin.reshape(b_s, t_s, POOL_WIDTH)[:, t_s - POOL_PAD:].reshape(depth, b_s, POOL_PAD, POOL_WIDTH)

    dest_p, dest_s, fill_start, tile_e1, tile_e2, tile_valid, m_rows = _moe_plan(plan_p, cnt_p, plan_s, cnt_s,
                                                                              expert_tile)
    rows = _scatter_rows(fill_start, dest_p, xg_p, dest_s, xg_s, m_rows, move_tile, expert_tile)
    ys = _moe_sorted(tile_e1, tile_e2, tile_valid, rows, g2, wg, wu, wd, expert_tile)
    y_prompt = _gather_rows(dest_p, ys, 2 * move_tile).reshape(b_p, s_p, D_MODEL)
    y_sample = _gather_rows(dest_s, ys, move_tile).reshape(b_s, t_s, D_MODEL)

    return (y_prompt, y_sample, k_prompt, v_prompt, logf_prompt, pool_prompt,
            k_sample, v_sample, logf_sample, pool_sample)
```

```python
import functools

import jax
import jax.numpy as jnp
from jax import lax
from jax.experimental import pallas as pl
from jax.experimental.pallas import tpu as pltpu

F32 = jnp.float32
BF16 = jnp.bfloat16

D_MODEL = 1024
ATTN_WIDTH = 512
N_HEADS = 8
HEAD_DIM = 64
POOL_WIDTH = 512
POOL_WINDOWS = (2, 4, 8, 16)
POOL_GROUP_DIM = 128
POOL_PAD = 15
HIST_ROWS = 16
N_GROUPS = 4
EXPERTS_PER_GROUP = 4
N_EXPERTS = 16
D_EXPERT = 512
EPS = 1e-6
F32_EXP_UNDERFLOW = 104.0
BF16_SLACK = 1.02
SKIP_MARGIN = 1.0
LANES = 128
SUBLANES = 8
MXU_DIM = 256
PAIR = 2 * HEAD_DIM
N_PAIRS = N_HEADS // 2
ROWSUM_ROW = (HEAD_DIM, 0)
VMEM_LIMIT = 56 * 1024 * 1024
ROW_WIDTH = D_MODEL + LANES
PAIRS_PER_GROUP = 6
N_CATEGORIES = N_GROUPS * PAIRS_PER_GROUP
PAIR_ORDER = ((0, 1), (0, 2), (1, 2), (1, 3), (2, 3), (0, 3))
CATEGORY_EXPERTS = tuple((4 * g + a, 4 * g + b) for g in range(N_GROUPS) for a, b in PAIR_ORDER)
ROUTE_CATEGORY_LANE = N_EXPERTS
ROUTE_RANK_LANE = N_EXPERTS + 1
ROUTE_SLOTS = 32


def _cparams(sem):
    return pltpu.CompilerParams(dimension_semantics=sem, vmem_limit_bytes=VMEM_LIMIT)


def _split3(x):
    hi = x.astype(BF16)
    r1 = x - hi.astype(F32)
    mid = r1.astype(BF16)
    lo = (r1 - mid.astype(F32)).astype(BF16)
    return hi, mid, lo


def _split2(x):
    hi = x.astype(BF16)
    lo = (x - hi.astype(F32)).astype(BF16)
    return hi, lo


def _inproj_kernel(x_ref, g1_ref, w_ref, bf_ref, qg_ref, kg_ref, hm_ref, tri_ref,
                   q_ref, kf_ref, kb_ref, vf_ref, vb_ref, ka_ref, vt_ref, p_ref, logf_ref, c_ref, edge_ref, carry_ref, *,
                   carry_rows, key_blk):
    x = x_ref[...]
    r = lax.rsqrt(jnp.mean(x * x, axis=-1, keepdims=True) + EPS)
    n = ((x * r) * g1_ref[...]).astype(BF16)
    z = jnp.dot(n, w_ref[...], preferred_element_type=F32)

    hm = hm_ref[...]

    def head_norm(zz, g):
        hi, lo = _split2(zz * zz)
        half = hm.shape[0]
        ms = jnp.concatenate(
            [jnp.dot(hi[:, r:r + half], hm, preferred_element_type=F32)
             + jnp.dot(lo[:, r:r + half], hm, preferred_element_type=F32) for r in range(0, zz.shape[1], half)], axis=1)
        return (zz * lax.rsqrt(ms + EPS)) * g

    q = head_norm(z[:, 0:ATTN_WIDTH], qg_ref[...])
    q_ref[...] = (q * (HEAD_DIM ** -0.5)).astype(BF16)
    k = head_norm(z[:, ATTN_WIDTH:2 * ATTN_WIDTH], kg_ref[...])
    kf_ref[...] = k
    kb_ref[...] = k.astype(BF16)
    v = z[:, 2 * ATTN_WIDTH:3 * ATTN_WIDTH]
    vf_ref[...] = v
    vb_ref[...] = v.astype(BF16)
    p_ref[...] = z[:, 3 * ATTN_WIDTH:3 * ATTN_WIDTH + POOL_WIDTH]
    prow = lax.broadcasted_iota(jnp.int32, (PAIR, v.shape[0]), 0)
    for hp in range(N_PAIRS):
        vpair_t = v[:, hp * PAIR:(hp + 1) * PAIR].T
        for h in range(2):
            own = (prow < HEAD_DIM) if h == 0 else (prow >= HEAD_DIM)
            marker = jnp.where(prow == ROWSUM_ROW[h], 1.0, 0.0)
            vt_ref[(2 * hp + h) * LANES:(2 * hp + h + 1) * LANES, :] = jnp.where(own, vpair_t, marker).astype(BF16)

    f = z[:, 3 * ATTN_WIDTH + POOL_WIDTH:] + bf_ref[...]
    lane = lax.broadcasted_iota(jnp.int32, f.shape, 1)
    logf = -(jnp.maximum(-f, 0.0) + jnp.log1p(jnp.exp(-jnp.abs(f))))
    logf = jnp.where(lane < N_HEADS, logf, 0.0)
    logf_ref[...] = logf

    tri = tri_ref[...]
    blk = tri.shape[0]
    parts = _split3(logf)
    pieces = []
    for r in range(0, logf.shape[0], blk):
        piece = sum(jnp.dot(tri, part[r:r + blk, :], preferred_element_type=F32) for part in parts)
        if carry_rows and pieces:
            piece = piece + pieces[-1][blk - 1:blk, :]
        pieces.append(piece)
    c = jnp.concatenate(pieces, axis=0)
    if carry_rows:
        @pl.when(pl.program_id(0) == 0)
        def _():
            carry_ref[...] = jnp.zeros_like(carry_ref)
        c = c + carry_ref[0:1, :]
        carry_ref[...] = jnp.broadcast_to(c[c.shape[0] - 1:, :], carry_ref.shape)
    c_ref[...] = c
    edges = [c[r:r + 1, :] for b in range(c.shape[0] // key_blk) for r in (b * key_blk, (b + 1) * key_blk - 1)]
    edge_ref[...] = jnp.concatenate(edges + [jnp.zeros((SUBLANES - len(edges), LANES), F32)], axis=0)

    tm = c.shape[0]
    rel = jnp.concatenate([c[r:r + key_blk, :] - c[r:r + 1, :] for r in range(0, tm, key_blk)], axis=0)
    hi, mid, lo = (part.astype(F32) for part in _split3(rel))
    decay = (hi + pltpu.roll(mid, N_HEADS, axis=1) + pltpu.roll(lo, 2 * N_HEADS, axis=1)).astype(BF16)
    kb = k.astype(BF16)
    for hp in range(N_PAIRS):
        ka_ref[:, 2 * hp * LANES:(2 * hp + 1) * LANES] = kb[:, hp * PAIR:(hp + 1) * PAIR]
        ka_ref[:, (2 * hp + 1) * LANES:(2 * hp + 2) * LANES] = decay


def _decay_lane(head, part):
    return part * N_HEADS + head


def _in_project(x2d, tm, seg, carry_rows, key_blk, g1, w_all, bf_pad, qg, kg, hm):
    n = x2d.shape[0]
    blk = min(tm, MXU_DIM)
    assert seg % blk == 0 or blk % seg == 0, "running-sum segments must nest with the triangular block"
    rows = lax.broadcasted_iota(jnp.int32, (blk, blk), 0)
    cols = lax.broadcasted_iota(jnp.int32, (blk, blk), 1)
    tri = ((cols <= rows) & (rows // seg == cols // seg)).astype(BF16)
    row_blk = lambda w: pl.BlockSpec((tm, w), lambda i: (i, 0))
    const = lambda a: pl.BlockSpec(a.shape, lambda i: (0,) * a.ndim)
    out_shape = (
        jax.ShapeDtypeStruct((n, ATTN_WIDTH), BF16),
        jax.ShapeDtypeStruct((n, ATTN_WIDTH), F32),
        jax.ShapeDtypeStruct((n, ATTN_WIDTH), BF16),
        jax.ShapeDtypeStruct((n, ATTN_WIDTH), F32),
        jax.ShapeDtypeStruct((n, ATTN_WIDTH), BF16),
        jax.ShapeDtypeStruct((n, N_PAIRS * 2 * LANES), BF16),
        jax.ShapeDtypeStruct((N_HEADS * LANES, n), BF16),
        jax.ShapeDtypeStruct((n, POOL_WIDTH), F32),
        jax.ShapeDtypeStruct((n, LANES), F32),
        jax.ShapeDtypeStruct((n, LANES), F32),
        jax.ShapeDtypeStruct((n // tm * SUBLANES, LANES), F32),
    )
    return pl.pallas_call(
        functools.partial(_inproj_kernel, carry_rows=carry_rows, key_blk=key_blk),
        grid=(n // tm,),
        in_specs=[row_blk(D_MODEL), const(g1), const(w_all), const(bf_pad), const(qg), const(kg), const(hm),
                  const(tri)],
        out_specs=[row_blk(ATTN_WIDTH)] * 5 + [row_blk(N_PAIRS * 2 * LANES),
                                                pl.BlockSpec((N_HEADS * LANES, tm), lambda i: (0, i)),
                                                row_blk(POOL_WIDTH), row_blk(LANES), row_blk(LANES),
                                                pl.BlockSpec((SUBLANES, LANES), lambda i: (i, 0))],
        out_shape=out_shape,
        scratch_shapes=[pltpu.VMEM((8, LANES), F32)],
        compiler_params=_cparams(("arbitrary",)),
        name="in_project",
    )(x2d, g1, w_all, bf_pad, qg, kg, hm, tri)


def _attn_prompt_kernel(j0_ref, cb_ref, q_ref, k_ref, vt_ref, o_ref, acc_ref, m_ref, *, tq, tk):
    hp = pl.program_id(0)
    i = pl.program_id(1)
    nk = tq // tk
    lane = lax.broadcasted_iota(jnp.int32, (1, PAIR), 1)
    first = lane < HEAD_DIM
    q = q_ref[...]
    zero = jnp.zeros_like(q)
    qw = []
    for h in range(2):
        minus = jnp.zeros((1, LANES), F32)
        for part in range(3):
            minus = jnp.where(lane == _decay_lane(2 * hp + h, part), -1.0, minus)
        qw.append(jnp.concatenate([jnp.where(first == (h == 0), q, zero),
                                   jnp.broadcast_to(minus, (tq, LANES)).astype(BF16)], axis=1))
    q_start = i * tq

    acc_ref[...] = jnp.zeros_like(acc_ref)
    m_ref[...] = jnp.full_like(m_ref, -jnp.inf)

    last = (i + 1) * nk - 1

    def scores(h, j, masked=True):
        start = pl.multiple_of(j * tk, tk)
        k = k_ref[pl.ds(start, tk), :]
        s = lax.dot_general(k, qw[h], (((1,), (1,)), ((), ())), preferred_element_type=F32)
        if masked:
            key = lax.broadcasted_iota(jnp.int32, s.shape, 0)
            qry = lax.broadcasted_iota(jnp.int32, s.shape, 1)
            s = jnp.where(key - qry <= q_start - j * tk, s, -jnp.inf)
        return s, jnp.max(s, axis=0, keepdims=True)

    def update(h, j, s, colmax):
        shift = cb_ref[j, 2 * hp + h] - cb_ref[i * nk, 2 * hp + h]
        start = pl.multiple_of(j * tk, tk)
        vt = vt_ref[h * LANES:(h + 1) * LANES, pl.ds(start, tk)]
        m_prev = m_ref[h]
        m_new = jnp.maximum(m_prev, colmax - shift)
        alpha = jnp.exp(m_prev - m_new)
        p = jnp.exp(s - (m_new + shift))
        m_ref[h] = m_new
        acc_ref[h] = acc_ref[h] * alpha + jnp.dot(vt, p.astype(BF16), preferred_element_type=F32)

    j_first = [j0_ref[i, 2 * hp + h] for h in range(2)]
    j_both = jnp.maximum(j_first[0], j_first[1])
    start = [scores(h, j_first[h]) for h in range(2)]
    ahead = []
    for h in range(2):
        def body_one(j, carry, h=h):
            nxt = scores(h, j + 1)
            update(h, j, *carry)
            return nxt

        ahead.append(lax.fori_loop(j_first[h], j_both, body_one, start[h]))

    def body_both(j, carry, masked):
        nxt = (scores(0, j + 1, masked), scores(1, j + 1, masked))
        update(0, j, *carry[0])
        update(1, j, *carry[1])
        return nxt

    j_mask = jnp.maximum(j_both, i * nk - 1)
    pairs = (j_mask - j_both) // 2

    def body_two(t, carry):
        j = j_both + 2 * t
        return body_both(j + 1, body_both(j, carry, False), False)

    carry = lax.fori_loop(0, pairs, body_two, (ahead[0], ahead[1]))
    carry = lax.fori_loop(j_both + 2 * pairs, last, functools.partial(body_both, masked=True), carry)
    update(0, last, *carry[0])
    update(1, last, *carry[1])

    out = [acc_ref[h] / acc_ref[h][ROWSUM_ROW[h]:ROWSUM_ROW[h] + 1, :] for h in range(2)]
    row = lax.broadcasted_iota(jnp.int32, (PAIR, 1), 0)
    o_ref[...] = jnp.where(row < HEAD_DIM, out[0], out[1]).T.astype(o_ref.dtype)


def _attention_prompt(j0, c_blocks, q, ka, vt, tq, tk):
    n = q.shape[0]
    grid_spec = pltpu.PrefetchScalarGridSpec(
        num_scalar_prefetch=2,
        grid=(N_PAIRS, n // tq),
        in_specs=[
            pl.BlockSpec((tq, PAIR), lambda hp, i, j0, cb: (i, hp)),
            pl.BlockSpec((n, 2 * LANES), lambda hp, i, j0, cb: (0, hp)),
            pl.BlockSpec((2 * LANES, n), lambda hp, i, j0, cb: (hp, 0)),
        ],
        out_specs=pl.BlockSpec((tq, PAIR), lambda hp, i, j0, cb: (i, hp)),
        scratch_shapes=[pltpu.VMEM((2, LANES, tq), F32), pltpu.VMEM((2, 1, tq), F32)],
    )
    return pl.pallas_call(
        functools.partial(_attn_prompt_kernel, tq=tq, tk=tk),
        grid_spec=grid_spec,
        out_shape=jax.ShapeDtypeStruct((n, ATTN_WIDTH), BF16),
        compiler_params=_cparams(("arbitrary", "arbitrary")),
        name="attention_prompt",
    )(j0, c_blocks, q, ka, vt)


def _first_key_blocks(c_first, c_last, q_gain, k_gain, tq, tk):
    n_k = c_first.shape[0]
    n_q = n_k * tk // tq
    qk_bound = BF16_SLACK * HEAD_DIM ** 0.5 * jnp.max(jnp.abs(q_gain)) * jnp.max(jnp.abs(k_gain))
    threshold = F32_EXP_UNDERFLOW + 2.0 * qk_bound + SKIP_MARGIN
    c_start = c_first[0::tq // tk]
    gap = c_start[:, None, :] - c_last[None, :, :]
    earlier = (jnp.arange(n_k)[None, :, None] + 1) * tk <= jnp.arange(n_q)[:, None, None] * tq
    return jnp.sum((gap < -threshold) & earlier, axis=1).astype(jnp.int32)


def _suffix_kernel(x_ref, o_ref):
    x = x_ref[...]
    p_len = x.shape[1]
    rows = lax.broadcasted_iota(jnp.int32, (p_len, p_len), 0)
    cols = lax.broadcasted_iota(jnp.int32, (p_len, p_len), 1)
    u = (rows > cols).astype(BF16)
    hi, mid, lo = _split3(x)
    o_ref[...] = (jnp.dot(hi, u, preferred_element_type=F32) + jnp.dot(mid, u, preferred_element_type=F32)
                  + jnp.dot(lo, u, preferred_element_type=F32))


def _suffix_sums(x):
    return pl.pallas_call(
        _suffix_kernel,
        out_shape=jax.ShapeDtypeStruct(x.shape, F32),
        compiler_params=pltpu.CompilerParams(vmem_limit_bytes=VMEM_LIMIT),
        name="cache_suffix_sums",
    )(x)


def _attn_sample_kernel(q_ref, kn_ref, vn_ref, ck_ref, cv_ref, d_ref, dt_ref, r_ref, o_ref):
    lane = lax.broadcasted_iota(jnp.int32, (1, PAIR), 1)
    first = lane < HEAD_DIM
    dblk = d_ref[...]
    dlane = lax.broadcasted_iota(jnp.int32, dblk.shape, 1)
    nt = (((1,), (1,)), ((), ()))
    for hp in range(N_PAIRS):
        cols = slice(hp * PAIR, (hp + 1) * PAIR)
        q = q_ref[:, cols]
        zero = jnp.zeros_like(q)
        qh = (jnp.where(first, q, zero), jnp.where(first, zero, q))
        kn = kn_ref[:, cols]
        vn = vn_ref[:, cols]
        ck = ck_ref[:, cols]
        cv = cv_ref[:, cols]
        outs, ls = [], []
        for h in range(2):
            dq = jnp.sum(jnp.where(dlane == 2 * hp + h, dblk, 0.0), axis=1, keepdims=True)
            s1 = lax.dot_general(qh[h], ck, nt, preferred_element_type=F32) + dq + r_ref[hp, h:h + 1, :]
            s2 = lax.dot_general(qh[h], kn, nt, preferred_element_type=F32) + dq - dt_ref[hp, h:h + 1, :]
            row = lax.broadcasted_iota(jnp.int32, s2.shape, 0)
            col = lax.broadcasted_iota(jnp.int32, s2.shape, 1)
            s2 = jnp.where(col <= row, s2, -jnp.inf)
            m = jnp.maximum(jnp.max(s1, axis=1, keepdims=True), jnp.max(s2, axis=1, keepdims=True))
            p1 = jnp.exp(s1 - m)
            p2 = jnp.exp(s2 - m)
            ls.append(jnp.sum(p1, axis=1, keepdims=True) + jnp.sum(p2, axis=1, keepdims=True))
            outs.append(jnp.dot(p1.astype(BF16), cv, preferred_element_type=F32)
                        + jnp.dot(p2.astype(BF16), vn, preferred_element_type=F32))
        o_ref[:, cols] = (jnp.where(first, outs[0], outs[1]) / jnp.where(first, ls[0], ls[1])).astype(o_ref.dtype)


def _attention_sample(q, kb, vb, cache_k, cache_v, d, dt4, r4, t):
    nb, p_len = cache_k.shape[0], cache_k.shape[1]
    row_blk = pl.BlockSpec((t, ATTN_WIDTH), lambda b: (b, 0))
    cache_blk = pl.BlockSpec((None, p_len, ATTN_WIDTH), lambda b: (b, 0, 0))
    return pl.pallas_call(
        _attn_sample_kernel,
        grid=(nb,),
        in_specs=[row_blk, row_blk, row_blk, cache_blk, cache_blk,
                  pl.BlockSpec((t, LANES), lambda b: (b, 0)),
                  pl.BlockSpec((None, N_PAIRS, 2, t), lambda b: (b, 0, 0, 0)),
                  pl.BlockSpec((None, N_PAIRS, 2, p_len), lambda b: (b, 0, 0, 0))],
        out_specs=row_blk,
        out_shape=jax.ShapeDtypeStruct((nb * t, ATTN_WIDTH), BF16),
        compiler_params=_cparams(("arbitrary",)),
        name="attention_sample",
    )(q, kb, vb, cache_k, cache_v, d, dt4, r4)


def _mix_kernel(a_ref, p_ref, hist_ref, x_ref, pw_ref, ps_ref, wo_ref, g2_ref, wrh_ref, wrl_ref, br_ref, tri_ref,
                xg_ref, plan_ref, cnt_ref, *, seg, pos0, pos_stride, zero_first):
    i = pl.program_id(0)
    t = p_ref.shape[0]
    p = p_ref[...]
    hist = hist_ref[...]
    if zero_first:
        hist = jnp.where(i == 0, 0.0, hist)
    ext = seg + HIST_ROWS
    xh = jnp.concatenate([piece for b in range(t // seg)
                          for piece in (hist[b * HIST_ROWS:(b + 1) * HIST_ROWS, :], p[b * seg:(b + 1) * seg, :])], axis=0)

    def own_rows(v):
        return jnp.concatenate([v[b * ext + HIST_ROWS:(b + 1) * ext, :] for b in range(t // seg)], axis=0)

    pos = pos0 + i * pos_stride + lax.broadcasted_iota(jnp.int32, (t, 1), 0) % seg
    ys = []
    for g, w in enumerate(POOL_WINDOWS):
        col = xh[:, g * POOL_GROUP_DIM:(g + 1) * POOL_GROUP_DIM]
        acc = col
        span = 1
        while span < w:
            acc = acc + pltpu.roll(acc, span, axis=0)
            span *= 2
        cnt = jnp.minimum(pos + 1, w).astype(F32)
        dgrp = own_rows(acc) / cnt - own_rows(col)
        ys.append(jnp.dot(dgrp.astype(BF16), pw_ref[g], preferred_element_type=F32))
    pm = (jnp.concatenate(ys, axis=1) * ps_ref[...]).astype(BF16)
    mix = (jnp.dot(a_ref[...], wo_ref[0:ATTN_WIDTH, :], preferred_element_type=F32)
           + jnp.dot(pm, wo_ref[ATTN_WIDTH:, :], preferred_element_type=F32))
    x1 = x_ref[...] + mix
    xg_ref[:, 0:D_MODEL] = x1

    r = lax.rsqrt(jnp.mean(x1 * x1, axis=-1, keepdims=True) + EPS)
    n2 = (x1 * r) * g2_ref[...]

    nh, nl = _split2(n2)
    wrh = wrh_ref[...]
    logits = (jnp.dot(nh, wrh, preferred_element_type=F32) + jnp.dot(nl, wrh, preferred_element_type=F32)
              + jnp.dot(nh, wrl_ref[...], preferred_element_type=F32)) + br_ref[...]
    z = logits.T[0:ROUTE_SLOTS, :]
    slot = lax.broadcasted_iota(jnp.int32, z.shape, 0).astype(F32)
    neg = -jnp.inf
    big = jnp.float32(1 << 20)
    is_g = (slot >= N_EXPERTS) & (slot < N_EXPERTS + N_GROUPS)
    lg = jnp.where(is_g, z, neg)
    gmax = jnp.max(lg, axis=0, keepdims=True)
    g_slot = jnp.min(jnp.where(lg == gmax, slot, big), axis=0, keepdims=True)
    top_pg = 1.0 / jnp.sum(jnp.exp(lg - gmax), axis=0, keepdims=True)
    g_idx = g_slot - N_EXPERTS
    sel = (slot >= g_idx * EXPERTS_PER_GROUP) & (slot < (g_idx + 1) * EXPERTS_PER_GROUP)
    le = jnp.where(sel, z, neg)
    m1 = jnp.max(le, axis=0, keepdims=True)
    i1 = jnp.min(jnp.where(le == m1, slot, big), axis=0, keepdims=True)
    le2 = jnp.where(slot == i1, neg, le)
    m2 = jnp.max(le2, axis=0, keepdims=True)
    i2 = jnp.min(jnp.where(le2 == m2, slot, big), axis=0, keepdims=True)
    e2 = jnp.exp(m2 - m1)
    w1 = top_pg / (1.0 + e2)
    w2 = top_pg * e2 / (1.0 + e2)
    gate = jnp.where(slot == i1, w1, 0.0) + jnp.where(slot == i2, w2, 0.0)

    e_lo = jnp.minimum(i1, i2) - g_idx * EXPERTS_PER_GROUP
    e_hi = jnp.maximum(i1, i2) - g_idx * EXPERTS_PER_GROUP
    pair = jnp.zeros_like(e_lo)
    for idx, (a, b) in enumerate(PAIR_ORDER):
        pair = jnp.where((e_lo == a) & (e_hi == b), float(idx), pair)
    category = g_idx * PAIRS_PER_GROUP + pair

    @pl.when(i == 0)
    def _():
        cnt_ref[...] = jnp.zeros_like(cnt_ref)
    member = slot == category
    tri = tri_ref[...]
    blk = tri.shape[0]
    nt = (((1,), (1,)), ((), ()))
    seen = cnt_ref[:, 0:1]
    pieces = []
    for r in range(0, t, blk):
        mem = member[:, r:r + blk].astype(BF16)
        pieces.append(lax.dot_general(mem, tri, nt, preferred_element_type=F32) + seen)
        seen = seen + jnp.sum(mem.astype(F32), axis=1, keepdims=True)
    before = jnp.concatenate(pieces, axis=1)
    rank = jnp.sum(jnp.where(member, before, 0.0), axis=0, keepdims=True)
    cnt_ref[...] = jnp.broadcast_to(seen, cnt_ref.shape)
    route = jnp.where(slot == ROUTE_CATEGORY_LANE, category, jnp.where(slot == ROUTE_RANK_LANE, rank, gate))
    route = jnp.concatenate([route, jnp.zeros((LANES - ROUTE_SLOTS, t), F32)], axis=0)
    xg_ref[:, D_MODEL:] = route.T
    srow = lax.broadcasted_iota(jnp.int32, (SUBLANES, t), 0)
    plan_ref[...] = jnp.where(srow == 0, category, jnp.where(srow == 1, rank, 0.0))


def _mix(a, p, hist_src, hist_map, x2d, t, seg, pos0, pos_stride, zero_first, pw, ps, wo, g2, wrh, wrl, br):
    n = x2d.shape[0]
    blk = min(t, MXU_DIM)
    rows = lax.broadcasted_iota(jnp.int32, (blk, blk), 0)
    cols = lax.broadcasted_iota(jnp.int32, (blk, blk), 1)
    tri = (cols < rows).astype(BF16)
    row_blk = lambda w: pl.BlockSpec((t, w), lambda i: (i, 0))
    const = lambda arr: pl.BlockSpec(arr.shape, lambda i: (0,) * arr.ndim)
    return pl.pallas_call(
        functools.partial(_mix_kernel, seg=seg, pos0=pos0, pos_stride=pos_stride, zero_first=zero_first),
        grid=(n // t,),
        in_specs=[row_blk(ATTN_WIDTH), row_blk(POOL_WIDTH),
                  pl.BlockSpec((t // seg * HIST_ROWS, POOL_WIDTH), hist_map),
                  row_blk(D_MODEL), const(pw), const(ps), const(wo), const(g2), const(wrh), const(wrl), const(br),
                  const(tri)],
        out_specs=[row_blk(ROW_WIDTH), pl.BlockSpec((SUBLANES, t), lambda i: (i, 0)),
                   pl.BlockSpec((ROUTE_SLOTS, LANES), lambda i: (0, 0))],
        out_shape=(jax.ShapeDtypeStruct((n, ROW_WIDTH), F32), jax.ShapeDtypeStruct((n // t * SUBLANES, t), F32),
                   jax.ShapeDtypeStruct((ROUTE_SLOTS, LANES), F32)),
        compiler_params=_cparams(("arbitrary",)),
        name="pool_mix_route",
    )(a, p, hist_src, x2d, pw, ps, wo, g2, wrh, wrl, br, tri)


def _issue_row_scatter(dest_ref, x_ref, xs_ref, sem):
    t = x_ref.shape[0]

    def issue(g, carry):
        base = pl.multiple_of(g * SUBLANES, SUBLANES)
        tile_rows = x_ref.at[pl.ds(base, SUBLANES)]
        for u in range(SUBLANES):
            pltpu.make_async_copy(tile_rows.at[pl.ds(u, 1)], xs_ref.at[pl.ds(dest_ref[base + u], 1)],
                                  sem).start(priority=u % 2)
        return carry

    lax.fori_loop(0, t // SUBLANES, issue, 0, unroll=2)
    pltpu.make_async_copy(x_ref, xs_ref.at[pl.ds(0, t)], sem).wait()


def _scatter_rows_kernel(fill_ref, dest_a_ref, dest_b_ref, xa_ref, xb_ref, xs_ref, zero_ref, sem, *, steps_a):
    step = pl.program_id(0)

    @pl.when(step == 0)
    def _():
        zero_ref[...] = jnp.zeros_like(zero_ref)
        fill_rows = zero_ref.shape[0]
        last_start = xs_ref.shape[0] - fill_rows
        for g in range(fill_ref.shape[0]):
            @pl.when(fill_ref[g] < xs_ref.shape[0])
            def _(g=g):
                start = pl.multiple_of(jnp.minimum(fill_ref[g], last_start), SUBLANES)
                fill = pltpu.make_async_copy(zero_ref, xs_ref.at[pl.ds(start, fill_rows)], sem)
                fill.start()
                fill.wait()

    @pl.when(step < steps_a)
    def _():
        _issue_row_scatter(dest_a_ref, xa_ref, xs_ref, sem)

    @pl.when(step >= steps_a)
    def _():
        _issue_row_scatter(dest_b_ref, xb_ref, xs_ref, sem)


def _scatter_rows(fill_start, dest_a, xa, dest_b, xb, m_rows, t, fill_rows):
    w = xa.shape[1]
    steps_a, steps_b = xa.shape[0] // t, xb.shape[0] // t
    in_a = lambda i: jnp.minimum(i, steps_a - 1)
    in_b = lambda i: jnp.maximum(i - steps_a, 0)
    smem = pltpu.SMEM
    return pl.pallas_call(
        functools.partial(_scatter_rows_kernel, steps_a=steps_a),
        grid=(steps_a + steps_b,),
        in_specs=[pl.BlockSpec(fill_start.shape, lambda i: (0,), memory_space=smem),
                  pl.BlockSpec((t,), lambda i: (in_a(i),), memory_space=smem),
                  pl.BlockSpec((t,), lambda i: (in_b(i),), memory_space=smem),
                  pl.BlockSpec((t, w), lambda i: (in_a(i), 0)),
                  pl.BlockSpec((t, w), lambda i: (in_b(i), 0))],
        out_specs=pl.BlockSpec(memory_space=pl.ANY),
        out_shape=jax.ShapeDtypeStruct((m_rows, w), xa.dtype),
        scratch_shapes=[pltpu.VMEM((fill_rows, w), xa.dtype), pltpu.SemaphoreType.DMA],
        compiler_params=_cparams(("arbitrary",)),
        name="moe_scatter_rows",
    )(fill_start, dest_a, dest_b, xa, xb)


def _gather_rows_kernel(dest_ref, ys_ref, y_ref, sem):
    t = y_ref.shape[0]

    def issue(g, carry):
        base = pl.multiple_of(g * SUBLANES, SUBLANES)
        tile_rows = y_ref.at[pl.ds(base, SUBLANES)]
        for u in range(SUBLANES):
            pltpu.make_async_copy(ys_ref.at[pl.ds(dest_ref[base + u], 1)], tile_rows.at[pl.ds(u, 1)],
                                  sem).start(priority=u % 2)
        return carry

    lax.fori_loop(0, t // SUBLANES, issue, 0, unroll=2)
    pltpu.make_async_copy(ys_ref.at[pl.ds(0, t)], y_ref, sem).wait()


def _gather_rows(dest, ys, t):
    n = dest.shape[0]
    w = ys.shape[1]
    return pl.pallas_call(
        _gather_rows_kernel,
        grid=(n // t,),
        in_specs=[pl.BlockSpec((t,), lambda i: (i,), memory_space=pltpu.SMEM),
                  pl.BlockSpec(memory_space=pl.ANY)],
        out_specs=pl.BlockSpec((t, w), lambda i: (i, 0)),
        out_shape=jax.ShapeDtypeStruct((n, w), ys.dtype),
        scratch_shapes=[pltpu.SemaphoreType.DMA],
        compiler_params=_cparams(("arbitrary",)),
        name="moe_gather_rows",
    )(dest, ys)


def _moe_kernel(e1_ref, e2_ref, tv_ref, xs_ref, g2_ref, wg1_ref, wu1_ref, wd1_ref, wg2_ref, wu2_ref, wd2_ref, y_ref):
    m = pl.program_id(0)

    @pl.when(tv_ref[m] > 0)
    def _():
        x1 = xs_ref[:, 0:D_MODEL]
        route = xs_ref[:, D_MODEL:]
        lane = lax.broadcasted_iota(jnp.int32, route.shape, 1)
        r = lax.rsqrt(jnp.mean(x1 * x1, axis=-1, keepdims=True) + EPS)
        n2 = ((x1 * r) * g2_ref[...]).astype(BF16)
        acc = x1
        for e_ref, wg_ref, wu_ref, wd_ref in ((e1_ref, wg1_ref, wu1_ref, wd1_ref), (e2_ref, wg2_ref, wu2_ref, wd2_ref)):
            ge = jnp.sum(jnp.where(lane == e_ref[m], route, 0.0), axis=1, keepdims=True)
            hg = jnp.dot(n2, wg_ref[...], preferred_element_type=F32)
            hu = jnp.dot(n2, wu_ref[...], preferred_element_type=F32)
            h = (hg * jax.nn.sigmoid(hg)) * hu * ge
            acc = acc + jnp.dot(h.astype(BF16), wd_ref[...], preferred_element_type=F32)
        y_ref[...] = acc

    @pl.when(tv_ref[m] == 0)
    def _():
        y_ref[...] = jnp.zeros_like(y_ref)


def _moe_sorted(tile_e1, tile_e2, tile_valid, xs, g2, wg, wu, wd, tm):
    m_pad = tile_valid.shape[0] * tm
    up = lambda e: pl.BlockSpec((None, D_MODEL, D_EXPERT), lambda m, e1, e2, tv: ((e1, e2)[e][m], 0, 0))
    down = lambda e: pl.BlockSpec((None, D_EXPERT, D_MODEL), lambda m, e1, e2, tv: ((e1, e2)[e][m], 0, 0))
    grid_spec = pltpu.PrefetchScalarGridSpec(
        num_scalar_prefetch=3,
        grid=(m_pad // tm,),
        in_specs=[
            pl.BlockSpec((tm, ROW_WIDTH), lambda m, e1, e2, tv: (jnp.where(tv[m] > 0, m, 0), 0)),
            pl.BlockSpec(g2.shape, lambda m, e1, e2, tv: (0, 0)),
            up(0), up(0), down(0), up(1), up(1), down(1),
        ],
        out_specs=pl.BlockSpec((tm, D_MODEL), lambda m, e1, e2, tv: (m, 0)),
    )
    return pl.pallas_call(
        _moe_kernel,
        grid_spec=grid_spec,
        out_shape=jax.ShapeDtypeStruct((m_pad, D_MODEL), F32),
        compiler_params=_cparams(("arbitrary",)),
        name="moe_experts",
    )(tile_e1, tile_e2, tile_valid, xs, g2, wg, wu, wd, wg, wu, wd)


def _moe_plan(plan_p, cnt_p, plan_s, cnt_s, tm):
    def ids(plan):
        blocks = plan.reshape(-1, SUBLANES, plan.shape[1]).astype(jnp.int32)
        return blocks[:, 0, :], blocks[:, 1, :]

    def lookup(table, cat):
        out = jnp.zeros_like(cat)
        for c in range(N_CATEGORIES):
            out = jnp.where(cat == c, table[c], out)
        return out

    (kp, rank_p), (ks, rank_s) = ids(plan_p), ids(plan_s)
    n_total = kp.size + ks.size
    n_tiles = n_total // tm + N_CATEGORIES
    cp = cnt_p[:N_CATEGORIES, 0].astype(jnp.int32)
    cs = cnt_s[:N_CATEGORIES, 0].astype(jnp.int32)
    total = cp + cs
    tiles = (total + tm - 1) // tm
    tile_end = jnp.cumsum(tiles)
    offset = (tile_end - tiles) * tm
    dest_p = (lookup(offset, kp) + rank_p).reshape(-1)
    dest_s = (lookup(offset + cp, ks) + rank_s).reshape(-1)
    tile_id = jnp.arange(n_tiles)
    tile_cat = jnp.minimum(jnp.sum(tile_id[:, None] >= tile_end[None, :], axis=1), N_CATEGORIES - 1)
    used = jnp.clip(offset[tile_cat] + total[tile_cat] - tile_id * tm, 0, tm)
    tile_valid = jnp.where(tile_id < tile_end[N_CATEGORIES - 1], used, 0).astype(jnp.int32)
    experts = jnp.asarray(CATEGORY_EXPERTS, jnp.int32)
    tile_e1, tile_e2 = experts[tile_cat, 0], experts[tile_cat, 1]
    m_rows = (n_tiles + 1) * tm
    ends = ((offset + total) // SUBLANES) * SUBLANES
    tail = ends[N_CATEGORIES - 1] + tm * jnp.arange(1, m_rows // tm - n_total // tm + 1)
    fill_start = jnp.concatenate([ends, tail]).astype(jnp.int32)
    return dest_p, dest_s, fill_start, tile_e1, tile_e2, tile_valid, m_rows


def kernel(x_prompt, x_sample, cache_k, cache_v, cache_logf, state_pool, norm1_g, w_in, b_f, q_norm_g, k_norm_g,
           pool_w, pool_scale, w_out, norm2_g, w_router_group, b_router_group, w_router_expert, b_router_expert,
           w_gate, w_up, w_down):
    depth = norm1_g.shape[0]
    assert depth == 1, "single-layer step"
    b_p, s_p, _ = x_prompt.shape
    assert b_p == 1, "prompt kernels assume one stream"
    b_s, t_s, _ = x_sample.shape
    p_len = cache_k.shape[2]
    l = 0

    w = w_in[l]
    a3 = 3 * ATTN_WIDTH
    w_all = jnp.concatenate(
        [w[:, :a3], w[:, a3 + N_HEADS:], jnp.pad(w[:, a3:a3 + N_HEADS], ((0, 0), (0, LANES - N_HEADS)))],
        axis=1).astype(BF16)
    bf_pad = jnp.pad(b_f[l], (0, LANES - N_HEADS)).reshape(1, LANES)
    g1 = norm1_g[l].reshape(1, D_MODEL)
    qg = jnp.tile(q_norm_g[l], N_HEADS).reshape(1, ATTN_WIDTH)
    kg = jnp.tile(k_norm_g[l], N_HEADS).reshape(1, ATTN_WIDTH)
    hr = lax.broadcasted_iota(jnp.int32, (MXU_DIM, MXU_DIM), 0) // HEAD_DIM
    hc = lax.broadcasted_iota(jnp.int32, (MXU_DIM, MXU_DIM), 1) // HEAD_DIM
    hm = jnp.where(hr == hc, 1.0 / HEAD_DIM, 0.0).astype(BF16)
    pw = pool_w[l].astype(BF16)
    ps = pool_scale[l].reshape(1, POOL_WIDTH)
    wo = w_out[l].astype(BF16)
    g2 = norm2_g[l].reshape(1, D_MODEL)
    w_r = jnp.pad(jnp.concatenate([w_router_expert[l], w_router_group[l]], axis=1),
                  ((0, 0), (0, LANES - N_EXPERTS - N_GROUPS)))
    wrh = w_r.astype(BF16)
    wrl = (w_r - wrh.astype(F32)).astype(BF16)
    br = jnp.pad(jnp.concatenate([b_router_expert[l], b_router_group[l]]),
                 (0, LANES - N_EXPERTS - N_GROUPS)).reshape(1, LANES)
    wg = w_gate[l].astype(BF16)
    wu = w_up[l].astype(BF16)
    wd = w_down[l].astype(BF16)

    tile = 512
    query_tile = 512
    key_tile = 256
    expert_tile = 256
    move_tile = 1024

    xp = x_prompt.reshape(s_p, D_MODEL)
    q, kf, _, vf, _, ka, vt, pin, logf, _, edge = _in_project(xp, tile, tile, True, key_tile, g1, w_all, bf_pad, qg, kg,
                                                              hm)
    blocks_per_tile = tile // key_tile
    edge = edge.reshape(s_p // tile, SUBLANES, LANES)[:, :2 * blocks_per_tile, :N_HEADS]
    edge = edge.reshape(s_p // key_tile, 2, N_HEADS)
    c_blocks = edge[:, 0, :]
    j0 = _first_key_blocks(c_blocks, edge[:, 1, :], q_norm_g[l], k_norm_g[l], query_tile, key_tile)
    a = _attention_prompt(j0, c_blocks, q, ka, vt, query_tile, key_tile)
    hist_map = lambda i: (jnp.maximum(i * (tile // HIST_ROWS) - 1, 0), 0)
    xg_p, plan_p, cnt_p = _mix(a, pin, pin, hist_map, xp, tile, tile, 0, tile, True, pw, ps, wo, g2, wrh, wrl, br)
    k_prompt = kf.reshape(depth, b_p, s_p, N_HEADS, HEAD_DIM)
    v_prompt = vf.reshape(depth, b_p, s_p, N_HEADS, HEAD_DIM)
    logf_prompt = logf[:, :N_HEADS].reshape(depth, b_p, s_p, N_HEADS)
    pool_prompt = pin[s_p - POOL_PAD:].reshape(depth, b_p, POOL_PAD, POOL_WIDTH)

    n_s = b_s * t_s
    xs = x_sample.reshape(n_s, D_MODEL)
    q, kf, kb, vf, vb, _, _, pin, logf, d, _ = _in_project(xs, tile, t_s, False, key_tile, g1, w_all, bf_pad, qg, kg, hm)
    dt4 = d[:, :N_HEADS].reshape(b_s, t_s, N_PAIRS, 2).transpose(0, 2, 3, 1)
    clf = cache_logf[l].astype(F32).transpose(0, 2, 1).reshape(b_s * N_HEADS, p_len)
    r4 = _suffix_sums(clf).reshape(b_s, N_PAIRS, 2, p_len)
    ck = cache_k[l].reshape(b_s, p_len, ATTN_WIDTH).astype(BF16)
    cv = cache_v[l].reshape(b_s, p_len, ATTN_WIDTH).astype(BF16)
    a = _attention_sample(q, kb, vb, ck, cv, d, dt4, r4, t_s)
    hist_s = jnp.pad(state_pool[l], ((0, 0), (HIST_ROWS - POOL_PAD, 0), (0, 0))).reshape(b_s * HIST_ROWS, POOL_WIDTH)
    xg_s, plan_s, cnt_s = _mix(a, pin, hist_s, lambda i: (i, 0), xs, tile, t_s, p_len, 0, False, pw, ps, wo, g2, wrh,
                               wrl, br)
    k_sample = kf.reshape(depth, b_s, t_s, N_HEADS, HEAD_DIM)
    v_sample = vf.reshape(depth, b_s, t_s, N_HEADS, HEAD_DIM)
    logf_sample = logf[:, :N_HEADS].reshape(depth, b_s, t_s, N_HEADS)
    pool_sample = pin.reshape(b_s, t_s, POOL_WIDTH)[:, t_s - POOL_PAD:].reshape(depth, b_s, POOL_PAD, POOL_WIDTH)

    dest_p, dest_s, fill_start, tile_e1, tile_e2, tile_valid, m_rows = _moe_plan(plan_p, cnt_p, plan_s, cnt_s,
                                                                              expert_tile)
    rows = _scatter_rows(fill_start, dest_p, xg_p, dest_s, xg_s, m_rows, move_tile, expert_tile)
    ys = _moe_sorted(tile_e1, tile_e2, tile_valid, rows, g2, wg, wu, wd, expert_tile)
    y_prompt = _gather_rows(dest_p, ys, 2 * move_tile).reshape(b_p, s_p, D_MODEL)
    y_sample = _gather_rows(dest_s, ys, move_tile).reshape(b_s, t_s, D_MODEL)

    return (y_prompt, y_sample, k_prompt, v_prompt, logf_prompt, pool_prompt,
            k_sample, v_sample, logf_sample, pool_sample)
```

```python
import functools

import jax
import jax.numpy as jnp
from jax import lax
from jax.experimental import pallas as pl
from jax.experimental.pallas import tpu as pltpu

F32 = jnp.float32
BF16 = jnp.bfloat16

D_MODEL = 1024
ATTN_WIDTH = 512
N_HEADS = 8
HEAD_DIM = 64
POOL_WIDTH = 512
POOL_WINDOWS = (2, 4, 8, 16)
POOL_GROUP_DIM = 128
POOL_PAD = 15
HIST_ROWS = 16
N_GROUPS = 4
EXPERTS_PER_GROUP = 4
N_EXPERTS = 16
D_EXPERT = 512
EPS = 1e-6
F32_EXP_UNDERFLOW = 104.0
BF16_SLACK = 1.02
SKIP_MARGIN = 1.0
LANES = 128
SUBLANES = 8
MXU_DIM = 256
PAIR = 2 * HEAD_DIM
N_PAIRS = N_HEADS // 2
ROWSUM_ROW = (HEAD_DIM, 0)
VMEM_LIMIT = 56 * 1024 * 1024
ROW_WIDTH = D_MODEL + LANES
PAIRS_PER_GROUP = 6
N_CATEGORIES = N_GROUPS * PAIRS_PER_GROUP
PAIR_ORDER = ((0, 1), (0, 2), (1, 2), (1, 3), (2, 3), (0, 3))
CATEGORY_EXPERTS = tuple((4 * g + a, 4 * g + b) for g in range(N_GROUPS) for a, b in PAIR_ORDER)
ROUTE_CATEGORY_LANE = N_EXPERTS
ROUTE_RANK_LANE = N_EXPERTS + 1
ROUTE_SLOTS = 32


def _cparams(sem):
    return pltpu.CompilerParams(dimension_semantics=sem, vmem_limit_bytes=VMEM_LIMIT)


def _split3(x):
    hi = x.astype(BF16)
    r1 = x - hi.astype(F32)
    mid = r1.astype(BF16)
    lo = (r1 - mid.astype(F32)).astype(BF16)
    return hi, mid, lo


def _split2(x):
    hi = x.astype(BF16)
    lo = (x - hi.astype(F32)).astype(BF16)
    return hi, lo


def _inproj_kernel(x_ref, g1_ref, w_ref, bf_ref, qg_ref, kg_ref, hm_ref, tri_ref,
                   q_ref, kf_ref, kb_ref, vf_ref, vb_ref, ka_ref, vt_ref, p_ref, logf_ref, c_ref, edge_ref, carry_ref, *,
                   carry_rows, key_blk):
    x = x_ref[...]
    r = lax.rsqrt(jnp.mean(x * x, axis=-1, keepdims=True) + EPS)
    n = ((x * r) * g1_ref[...]).astype(BF16)
    z = jnp.dot(n, w_ref[...], preferred_element_type=F32)

    hm = hm_ref[...]

    def head_norm(zz, g):
        hi, lo = _split2(zz * zz)
        half = hm.shape[0]
        ms = jnp.concatenate(
            [jnp.dot(hi[:, r:r + half], hm, preferred_element_type=F32)
             + jnp.dot(lo[:, r:r + half], hm, preferred_element_type=F32) for r in range(0, zz.shape[1], half)], axis=1)
        return (zz * lax.rsqrt(ms + EPS)) * g

    q = head_norm(z[:, 0:ATTN_WIDTH], qg_ref[...])
    q_ref[...] = (q * (HEAD_DIM ** -0.5)).astype(BF16)
    k = head_norm(z[:, ATTN_WIDTH:2 * ATTN_WIDTH], kg_ref[...])
    kf_ref[...] = k
    kb_ref[...] = k.astype(BF16)
    v = z[:, 2 * ATTN_WIDTH:3 * ATTN_WIDTH]
    vf_ref[...] = v
    vb_ref[...] = v.astype(BF16)
    p_ref[...] = z[:, 3 * ATTN_WIDTH:3 * ATTN_WIDTH + POOL_WIDTH]
    prow = lax.broadcasted_iota(jnp.int32, (PAIR, v.shape[0]), 0)
    for hp in range(N_PAIRS):
        vpair_t = v[:, hp * PAIR:(hp + 1) * PAIR].T
        for h in range(2):
            own = (prow < HEAD_DIM) if h == 0 else (prow >= HEAD_DIM)
            marker = jnp.where(prow == ROWSUM_ROW[h], 1.0, 0.0)
            vt_ref[(2 * hp + h) * LANES:(2 * hp + h + 1) * LANES, :] = jnp.where(own, vpair_t, marker).astype(BF16)

    f = z[:, 3 * ATTN_WIDTH + POOL_WIDTH:] + bf_ref[...]
    lane = lax.broadcasted_iota(jnp.int32, f.shape, 1)
    logf = -(jnp.maximum(-f, 0.0) + jnp.log1p(jnp.exp(-jnp.abs(f))))
    logf = jnp.where(lane < N_HEADS, logf, 0.0)
    logf_ref[...] = logf

    tri = tri_ref[...]
    blk = tri.shape[0]
    parts = _split3(logf)
    pieces = []
    for r in range(0, logf.shape[0], blk):
        piece = sum(jnp.dot(tri, part[r:r + blk, :], preferred_element_type=F32) for part in parts)
        if carry_rows and pieces:
            piece = piece + pieces[-1][blk - 1:blk, :]
        pieces.append(piece)
    c = jnp.concatenate(pieces, axis=0)
    if carry_rows:
        @pl.when(pl.program_id(0) == 0)
        def _():
            carry_ref[...] = jnp.zeros_like(carry_ref)
        c = c + carry_ref[0:1, :]
        carry_ref[...] = jnp.broadcast_to(c[c.shape[0] - 1:, :], carry_ref.shape)
    c_ref[...] = c
    edges = [c[r:r + 1, :] for b in range(c.shape[0] // key_blk) for r in (b * key_blk, (b + 1) * key_blk - 1)]
    edge_ref[...] = jnp.concatenate(edges + [jnp.zeros((SUBLANES - len(edges), LANES), F32)], axis=0)

    tm = c.shape[0]
    rel = jnp.concatenate([c[r:r + key_blk, :] - c[r:r + 1, :] for r in range(0, tm, key_blk)], axis=0)
    hi, mid, lo = (part.astype(F32) for part in _split3(rel))
    decay = (hi + pltpu.roll(mid, N_HEADS, axis=1) + pltpu.roll(lo, 2 * N_HEADS, axis=1)).astype(BF16)
    kb = k.astype(BF16)
    for hp in range(N_PAIRS):
        ka_ref[:, 2 * hp * LANES:(2 * hp + 1) * LANES] = kb[:, hp * PAIR:(hp + 1) * PAIR]
        ka_ref[:, (2 * hp + 1) * LANES:(2 * hp + 2) * LANES] = decay


def _decay_lane(head, part):
    return part * N_HEADS + head


def _in_project(x2d, tm, seg, carry_rows, key_blk, g1, w_all, bf_pad, qg, kg, hm):
    n = x2d.shape[0]
    blk = min(tm, MXU_DIM)
    assert seg % blk == 0 or blk % seg == 0, "running-sum segments must nest with the triangular block"
    rows = lax.broadcasted_iota(jnp.int32, (blk, blk), 0)
    cols = lax.broadcasted_iota(jnp.int32, (blk, blk), 1)
    tri = ((cols <= rows) & (rows // seg == cols // seg)).astype(BF16)
    row_blk = lambda w: pl.BlockSpec((tm, w), lambda i: (i, 0))
    const = lambda a: pl.BlockSpec(a.shape, lambda i: (0,) * a.ndim)
    out_shape = (
        jax.ShapeDtypeStruct((n, ATTN_WIDTH), BF16),
        jax.ShapeDtypeStruct((n, ATTN_WIDTH), F32),
        jax.ShapeDtypeStruct((n, ATTN_WIDTH), BF16),
        jax.ShapeDtypeStruct((n, ATTN_WIDTH), F32),
        jax.ShapeDtypeStruct((n, ATTN_WIDTH), BF16),
        jax.ShapeDtypeStruct((n, N_PAIRS * 2 * LANES), BF16),
        jax.ShapeDtypeStruct((N_HEADS * LANES, n), BF16),
        jax.ShapeDtypeStruct((n, POOL_WIDTH), F32),
        jax.ShapeDtypeStruct((n, LANES), F32),
        jax.ShapeDtypeStruct((n, LANES), F32),
        jax.ShapeDtypeStruct((n // tm * SUBLANES, LANES), F32),
    )
    return pl.pallas_call(
        functools.partial(_inproj_kernel, carry_rows=carry_rows, key_blk=key_blk),
        grid=(n // tm,),
        in_specs=[row_blk(D_MODEL), const(g1), const(w_all), const(bf_pad), const(qg), const(kg), const(hm),
                  const(tri)],
        out_specs=[row_blk(ATTN_WIDTH)] * 5 + [row_blk(N_PAIRS * 2 * LANES),
                                                pl.BlockSpec((N_HEADS * LANES, tm), lambda i: (0, i)),
                                                row_blk(POOL_WIDTH), row_blk(LANES), row_blk(LANES),
                                                pl.BlockSpec((SUBLANES, LANES), lambda i: (i, 0))],
        out_shape=out_shape,
        scratch_shapes=[pltpu.VMEM((8, LANES), F32)],
        compiler_params=_cparams(("arbitrary",)),
        name="in_project",
    )(x2d, g1, w_all, bf_pad, qg, kg, hm, tri)


def _attn_prompt_kernel(j0_ref, cb_ref, q_ref, k_ref, vt_ref, o_ref, acc_ref, m_ref, *, tq, tk):
    hp = pl.program_id(0)
    i = pl.program_id(1)
    nk = tq // tk
    lane = lax.broadcasted_iota(jnp.int32, (1, PAIR), 1)
    first = lane < HEAD_DIM
    q = q_ref[...]
    zero = jnp.zeros_like(q)
    qw = []
    for h in range(2):
        minus = jnp.zeros((1, LANES), F32)
        for part in range(3):
            minus = jnp.where(lane == _decay_lane(2 * hp + h, part), -1.0, minus)
        qw.append(jnp.concatenate([jnp.where(first == (h == 0), q, zero),
                                   jnp.broadcast_to(minus, (tq, LANES)).astype(BF16)], axis=1))
    q_start = i * tq

    acc_ref[...] = jnp.zeros_like(acc_ref)
    m_ref[...] = jnp.full_like(m_ref, -jnp.inf)

    last = (i + 1) * nk - 1

    def scores(h, j, masked=True):
        start = pl.multiple_of(j * tk, tk)
        k = k_ref[pl.ds(start, tk), :]
        s = lax.dot_general(k, qw[h], (((1,), (1,)), ((), ())), preferred_element_type=F32)
        if masked:
            key = lax.broadcasted_iota(jnp.int32, s.shape, 0)
            qry = lax.broadcasted_iota(jnp.int32, s.shape, 1)
            s = jnp.where(key - qry <= q_start - j * tk, s, -jnp.inf)
        return s, jnp.max(s, axis=0, keepdims=True)

    def update(h, j, s, colmax):
        shift = cb_ref[j, 2 * hp + h] - cb_ref[i * nk, 2 * hp + h]
        start = pl.multiple_of(j * tk, tk)
        vt = vt_ref[h * LANES:(h + 1) * LANES, pl.ds(start, tk)]
        m_prev = m_ref[h]
        m_new = jnp.maximum(m_prev, colmax - shift)
        alpha = jnp.exp(m_prev - m_new)
        p = jnp.exp(s - (m_new + shift))
        m_ref[h] = m_new
        acc_ref[h] = acc_ref[h] * alpha + jnp.dot(vt, p.astype(BF16), preferred_element_type=F32)

    j_first = [j0_ref[i, 2 * hp + h] for h in range(2)]
    j_both = jnp.maximum(j_first[0], j_first[1])
    start = [scores(h, j_first[h]) for h in range(2)]
    ahead = []
    for h in range(2):
        def body_one(j, carry, h=h):
            nxt = scores(h, j + 1)
            update(h, j, *carry)
            return nxt

        ahead.append(lax.fori_loop(j_first[h], j_both, body_one, start[h]))

    def body_both(j, carry, masked):
        nxt = (scores(0, j + 1, masked), scores(1, j + 1, masked))
        update(0, j, *carry[0])
        update(1, j, *carry[1])
        return nxt

    j_mask = jnp.maximum(j_both, i * nk - 1)
    pairs = (j_mask - j_both) // 2

    def body_two(t, carry):
        j = j_both + 2 * t
        return body_both(j + 1, body_both(j, carry, False), False)

    carry = lax.fori_loop(0, pairs, body_two, (ahead[0], ahead[1]))
    carry = lax.fori_loop(j_both + 2 * pairs, last, functools.partial(body_both, masked=True), carry)
    update(0, last, *carry[0])
    update(1, last, *carry[1])

    out = [acc_ref[h] / acc_ref[h][ROWSUM_ROW[h]:ROWSUM_ROW[h] + 1, :] for h in range(2)]
    row = lax.broadcasted_iota(jnp.int32, (PAIR, 1), 0)
    o_ref[...] = jnp.where(row < HEAD_DIM, out[0], out[1]).T.astype(o_ref.dtype)


def _attention_prompt(j0, c_blocks, q, ka, vt, tq, tk):
    n = q.shape[0]
    grid_spec = pltpu.PrefetchScalarGridSpec(
        num_scalar_prefetch=2,
        grid=(N_PAIRS, n // tq),
        in_specs=[
            pl.BlockSpec((tq, PAIR), lambda hp, i, j0, cb: (i, hp)),
            pl.BlockSpec((n, 2 * LANES), lambda hp, i, j0, cb: (0, hp)),
            pl.BlockSpec((2 * LANES, n), lambda hp, i, j0, cb: (hp, 0)),
        ],
        out_specs=pl.BlockSpec((tq, PAIR), lambda hp, i, j0, cb: (i, hp)),
        scratch_shapes=[pltpu.VMEM((2, LANES, tq), F32), pltpu.VMEM((2, 1, tq), F32)],
    )
    return pl.pallas_call(
        functools.partial(_attn_prompt_kernel, tq=tq, tk=tk),
        grid_spec=grid_spec,
        out_shape=jax.ShapeDtypeStruct((n, ATTN_WIDTH), BF16),
        compiler_params=_cparams(("arbitrary", "arbitrary")),
        name="attention_prompt",
    )(j0, c_blocks, q, ka, vt)


def _first_key_blocks(c_first, c_last, q_gain, k_gain, tq, tk):
    n_k = c_first.shape[0]
    n_q = n_k * tk // tq
    qk_bound = BF16_SLACK * HEAD_DIM ** 0.5 * jnp.max(jnp.abs(q_gain)) * jnp.max(jnp.abs(k_gain))
    threshold = F32_EXP_UNDERFLOW + 2.0 * qk_bound + SKIP_MARGIN
    c_start = c_first[0::tq // tk]
    gap = c_start[:, None, :] - c_last[None, :, :]
    earlier = (jnp.arange(n_k)[None, :, None] + 1) * tk <= jnp.arange(n_q)[:, None, None] * tq
    return jnp.sum((gap < -threshold) & earlier, axis=1).astype(jnp.int32)


def _suffix_kernel(x_ref, o_ref):
    x = x_ref[...]
    p_len = x.shape[1]
    rows = lax.broadcasted_iota(jnp.int32, (p_len, p_len), 0)
    cols = lax.broadcasted_iota(jnp.int32, (p_len, p_len), 1)
    u = (rows > cols).astype(BF16)
    hi, mid, lo = _split3(x)
    o_ref[...] = (jnp.dot(hi, u, preferred_element_type=F32) + jnp.dot(mid, u, preferred_element_type=F32)
                  + jnp.dot(lo, u, preferred_element_type=F32))


def _suffix_sums(x):
    return pl.pallas_call(
        _suffix_kernel,
        out_shape=jax.ShapeDtypeStruct(x.shape, F32),
        compiler_params=pltpu.CompilerParams(vmem_limit_bytes=VMEM_LIMIT),
        name="cache_suffix_sums",
    )(x)


def _attn_sample_kernel(q_ref, kn_ref, vn_ref, ck_ref, cv_ref, d_ref, dt_ref, r_ref, o_ref):
    lane = lax.broadcasted_iota(jnp.int32, (1, PAIR), 1)
    first = lane < HEAD_DIM
    dblk = d_ref[...]
    dlane = lax.broadcasted_iota(jnp.int32, dblk.shape, 1)
    nt = (((1,), (1,)), ((), ()))
    for hp in range(N_PAIRS):
        cols = slice(hp * PAIR, (hp + 1) * PAIR)
        q = q_ref[:, cols]
        zero = jnp.zeros_like(q)
        qh = (jnp.where(first, q, zero), jnp.where(first, zero, q))
        kn = kn_ref[:, cols]
        vn = vn_ref[:, cols]
        ck = ck_ref[:, cols]
        cv = cv_ref[:, cols]
        outs, ls = [], []
        for h in range(2):
            dq = jnp.sum(jnp.where(dlane == 2 * hp + h, dblk, 0.0), axis=1, keepdims=True)
            s1 = lax.dot_general(qh[h], ck, nt, preferred_element_type=F32) + dq + r_ref[hp, h:h + 1, :]
            s2 = lax.dot_general(qh[h], kn, nt, preferred_element_type=F32) + dq - dt_ref[hp, h:h + 1, :]
            row = lax.broadcasted_iota(jnp.int32, s2.shape, 0)
            col = lax.broadcasted_iota(jnp.int32, s2.shape, 1)
            s2 = jnp.where(col <= row, s2, -jnp.inf)
            m = jnp.maximum(jnp.max(s1, axis=1, keepdims=True), jnp.max(s2, axis=1, keepdims=True))
            p1 = jnp.exp(s1 - m)
            p2 = jnp.exp(s2 - m)
            ls.append(jnp.sum(p1, axis=1, keepdims=True) + jnp.sum(p2, axis=1, keepdims=True))
            outs.append(jnp.dot(p1.astype(BF16), cv, preferred_element_type=F32)
                        + jnp.dot(p2.astype(BF16), vn, preferred_element_type=F32))
        o_ref[:, cols] = (jnp.where(first, outs[0], outs[1]) / jnp.where(first, ls[0], ls[1])).astype(o_ref.dtype)


def _attention_sample(q, kb, vb, cache_k, cache_v, d, dt4, r4, t):
    nb, p_len = cache_k.shape[0], cache_k.shape[1]
    row_blk = pl.BlockSpec((t, ATTN_WIDTH), lambda b: (b, 0))
    cache_blk = pl.BlockSpec((None, p_len, ATTN_WIDTH), lambda b: (b, 0, 0))
    return pl.pallas_call(
        _attn_sample_kernel,
        grid=(nb,),
        in_specs=[row_blk, row_blk, row_blk, cache_blk, cache_blk,
                  pl.BlockSpec((t, LANES), lambda b: (b, 0)),
                  pl.BlockSpec((None, N_PAIRS, 2, t), lambda b: (b, 0, 0, 0)),
                  pl.BlockSpec((None, N_PAIRS, 2, p_len), lambda b: (b, 0, 0, 0))],
        out_specs=row_blk,
        out_shape=jax.ShapeDtypeStruct((nb * t, ATTN_WIDTH), BF16),
        compiler_params=_cparams(("arbitrary",)),
        name="attention_sample",
    )(q, kb, vb, cache_k, cache_v, d, dt4, r4)


def _mix_kernel(a_ref, p_ref, hist_ref, x_ref, pw_ref, ps_ref, wo_ref, g2_ref, wrh_ref, wrl_ref, br_ref, tri_ref,
                xg_ref, plan_ref, cnt_ref, *, seg, pos0, pos_stride, zero_first):
    i = pl.program_id(0)
    t = p_ref.shape[0]
    p = p_ref[...]
    hist = hist_ref[...]
    if zero_first:
        hist = jnp.where(i == 0, 0.0, hist)
    ext = seg + HIST_ROWS
    xh = jnp.concatenate([piece for b in range(t // seg)
                          for piece in (hist[b * HIST_ROWS:(b + 1) * HIST_ROWS, :], p[b * seg:(b + 1) * seg, :])], axis=0)

    def own_rows(v):
        return jnp.concatenate([v[b * ext + HIST_ROWS:(b + 1) * ext, :] for b in range(t // seg)], axis=0)

    pos = pos0 + i * pos_stride + lax.broadcasted_iota(jnp.int32, (t, 1), 0) % seg
    ys = []
    for g, w in enumerate(POOL_WINDOWS):
        col = xh[:, g * POOL_GROUP_DIM:(g + 1) * POOL_GROUP_DIM]
        acc = col
        span = 1
        while span < w:
            acc = acc + pltpu.roll(acc, span, axis=0)
            span *= 2
        cnt = jnp.minimum(pos + 1, w).astype(F32)
        dgrp = own_rows(acc) / cnt - own_rows(col)
        ys.append(jnp.dot(dgrp.astype(BF16), pw_ref[g], preferred_element_type=F32))
    pm = (jnp.concatenate(ys, axis=1) * ps_ref[...]).astype(BF16)
    mix = (jnp.dot(a_ref[...], wo_ref[0:ATTN_WIDTH, :], preferred_element_type=F32)
           + jnp.dot(pm, wo_ref[ATTN_WIDTH:, :], preferred_element_type=F32))
    x1 = x_ref[...] + mix
    xg_ref[:, 0:D_MODEL] = x1

    r = lax.rsqrt(jnp.mean(x1 * x1, axis=-1, keepdims=True) + EPS)
    n2 = (x1 * r) * g2_ref[...]

    nh, nl = _split2(n2)
    wrh = wrh_ref[...]
    logits = (jnp.dot(nh, wrh, preferred_element_type=F32) + jnp.dot(nl, wrh, preferred_element_type=F32)
              + jnp.dot(nh, wrl_ref[...], preferred_element_type=F32)) + br_ref[...]
    z = logits.T[0:ROUTE_SLOTS, :]
    slot = lax.broadcasted_iota(jnp.int32, z.shape, 0).astype(F32)
    neg = -jnp.inf
    big = jnp.float32(1 << 20)
    is_g = (slot >= N_EXPERTS) & (slot < N_EXPERTS + N_GROUPS)
    lg = jnp.where(is_g, z, neg)
    gmax = jnp.max(lg, axis=0, keepdims=True)
    g_slot = jnp.min(jnp.where(lg == gmax, slot, big), axis=0, keepdims=True)
    top_pg = 1.0 / jnp.sum(jnp.exp(lg - gmax), axis=0, keepdims=True)
    g_idx = g_slot - N_EXPERTS
    sel = (slot >= g_idx * EXPERTS_PER_GROUP) & (slot < (g_idx + 1) * EXPERTS_PER_GROUP)
    le = jnp.where(sel, z, neg)
    m1 = jnp.max(le, axis=0, keepdims=True)
    i1 = jnp.min(jnp.where(le == m1, slot, big), axis=0, keepdims=True)
    le2 = jnp.where(slot == i1, neg, le)
    m2 = jnp.max(le2, axis=0, keepdims=True)
    i2 = jnp.min(jnp.where(le2 == m2, slot, big), axis=0, keepdims=True)
    e2 = jnp.exp(m2 - m1)
    w1 = top_pg / (1.0 + e2)
    w2 = top_pg * e2 / (1.0 + e2)
    gate = jnp.where(slot == i1, w1, 0.0) + jnp.where(slot == i2, w2, 0.0)

    e_lo = jnp.minimum(i1, i2) - g_idx * EXPERTS_PER_GROUP
    e_hi = jnp.maximum(i1, i2) - g_idx * EXPERTS_PER_GROUP
    pair = jnp.zeros_like(e_lo)
    for idx, (a, b) in enumerate(PAIR_ORDER):
        pair = jnp.where((e_lo == a) & (e_hi == b), float(idx), pair)
    category = g_idx * PAIRS_PER_GROUP + pair

    @pl.when(i == 0)
    def _():
        cnt_ref[...] = jnp.zeros_like(cnt_ref)
    member = slot == category
    tri = tri_ref[...]
    blk = tri.shape[0]
    nt = (((1,), (1,)), ((), ()))
    seen = cnt_ref[:, 0:1]
    pieces = []
    for r in range(0, t, blk):
        mem = member[:, r:r + blk].astype(BF16)
        pieces.append(lax.dot_general(mem, tri, nt, preferred_element_type=F32) + seen)
        seen = seen + jnp.sum(mem.astype(F32), axis=1, keepdims=True)
    before = jnp.concatenate(pieces, axis=1)
    rank = jnp.sum(jnp.where(member, before, 0.0), axis=0, keepdims=True)
    cnt_ref[...] = jnp.broadcast_to(seen, cnt_ref.shape)
    route = jnp.where(slot == ROUTE_CATEGORY_LANE, category, jnp.where(slot == ROUTE_RANK_LANE, rank, gate))
    route = jnp.concatenate([route, jnp.zeros((LANES - ROUTE_SLOTS, t), F32)], axis=0)
    xg_ref[:, D_MODEL:] = route.T
    srow = lax.broadcasted_iota(jnp.int32, (SUBLANES, t), 0)
    plan_ref[...] = jnp.where(srow == 0, category, jnp.where(srow == 1, rank, 0.0))


def _mix(a, p, hist_src, hist_map, x2d, t, seg, pos0, pos_stride, zero_first, pw, ps, wo, g2, wrh, wrl, br):
    n = x2d.shape[0]
    blk = min(t, MXU_DIM)
    rows = lax.broadcasted_iota(jnp.int32, (blk, blk), 0)
    cols = lax.broadcasted_iota(jnp.int32, (blk, blk), 1)
    tri = (cols < rows).astype(BF16)
    row_blk = lambda w: pl.BlockSpec((t, w), lambda i: (i, 0))
    const = lambda arr: pl.BlockSpec(arr.shape, lambda i: (0,) * arr.ndim)
    return pl.pallas_call(
        functools.partial(_mix_kernel, seg=seg, pos0=pos0, pos_stride=pos_stride, zero_first=zero_first),
        grid=(n // t,),
        in_specs=[row_blk(ATTN_WIDTH), row_blk(POOL_WIDTH),
                  pl.BlockSpec((t // seg * HIST_ROWS, POOL_WIDTH), hist_map),
                  row_blk(D_MODEL), const(pw), const(ps), const(wo), const(g2), const(wrh), const(wrl), const(br),
                  const(tri)],
        out_specs=[row_blk(ROW_WIDTH), pl.BlockSpec((SUBLANES, t), lambda i: (i, 0)),
                   pl.BlockSpec((ROUTE_SLOTS, LANES), lambda i: (0, 0))],
        out_shape=(jax.ShapeDtypeStruct((n, ROW_WIDTH), F32), jax.ShapeDtypeStruct((n // t * SUBLANES, t), F32),
                   jax.ShapeDtypeStruct((ROUTE_SLOTS, LANES), F32)),
        compiler_params=_cparams(("arbitrary",)),
        name="pool_mix_route",
    )(a, p, hist_src, x2d, pw, ps, wo, g2, wrh, wrl, br, tri)


def _issue_row_scatter(dest_ref, x_ref, xs_ref, sem):
    t = x_ref.shape[0]

    def issue(g, carry):
        base = pl.multiple_of(g * SUBLANES, SUBLANES)
        tile_rows = x_ref.at[pl.ds(base, SUBLANES)]
        for u in range(SUBLANES):
            pltpu.make_async_copy(tile_rows.at[pl.ds(u, 1)], xs_ref.at[pl.ds(dest_ref[base + u], 1)],
                                  sem).start(priority=u % 2)
        return carry

    lax.fori_loop(0, t // SUBLANES, issue, 0, unroll=2)
    pltpu.make_async_copy(x_ref, xs_ref.at[pl.ds(0, t)], sem).wait()


def _scatter_rows_kernel(fill_ref, dest_a_ref, dest_b_ref, xa_ref, xb_ref, xs_ref, zero_ref, sem, *, steps_a):
    step = pl.program_id(0)

    @pl.when(step == 0)
    def _():
        zero_ref[...] = jnp.zeros_like(zero_ref)
        fill_rows = zero_ref.shape[0]

        def fill(g):
            start = pl.multiple_of(fill_ref[g], SUBLANES)
            return pltpu.make_async_copy(zero_ref, xs_ref.at[pl.ds(start, fill_rows)], sem)

        for g in range(fill_ref.shape[0]):
            pl.when(fill_ref[g] < xs_ref.shape[0])(lambda g=g: fill(g).start())
        for g in range(fill_ref.shape[0]):
            pl.when(fill_ref[g] < xs_ref.shape[0])(lambda g=g: fill(g).wait())

    @pl.when(step < steps_a)
    def _():
        _issue_row_scatter(dest_a_ref, xa_ref, xs_ref, sem)

    @pl.when(step >= steps_a)
    def _():
        _issue_row_scatter(dest_b_ref, xb_ref, xs_ref, sem)


def _scatter_rows(fill_start, dest_a, xa, dest_b, xb, m_rows, t, fill_rows):
    w = xa.shape[1]
    steps_a, steps_b = xa.shape[0] // t, xb.shape[0] // t
    in_a = lambda i: jnp.minimum(i, steps_a - 1)
    in_b = lambda i: jnp.maximum(i - steps_a, 0)
    smem = pltpu.SMEM
    return pl.pallas_call(
        functools.partial(_scatter_rows_kernel, steps_a=steps_a),
        grid=(steps_a + steps_b,),
        in_specs=[pl.BlockSpec(fill_start.shape, lambda i: (0,), memory_space=smem),
                  pl.BlockSpec((t,), lambda i: (in_a(i),), memory_space=smem),
                  pl.BlockSpec((t,), lambda i: (in_b(i),), memory_space=smem),
                  pl.BlockSpec((t, w), lambda i: (in_a(i), 0)),
                  pl.BlockSpec((t, w), lambda i: (in_b(i), 0))],
        out_specs=pl.BlockSpec(memory_space=pl.ANY),
        out_shape=jax.ShapeDtypeStruct((m_rows, w), xa.dtype),
        scratch_shapes=[pltpu.VMEM((fill_rows, w), xa.dtype), pltpu.SemaphoreType.DMA],
        compiler_params=_cparams(("arbitrary",)),
        name="moe_scatter_rows",
    )(fill_start, dest_a, dest_b, xa, xb)


def _gather_rows_kernel(dest_ref, ys_ref, y_ref, sem):
    t = y_ref.shape[0]

    def issue(g, carry):
        base = pl.multiple_of(g * SUBLANES, SUBLANES)
        tile_rows = y_ref.at[pl.ds(base, SUBLANES)]
        for u in range(SUBLANES):
            pltpu.make_async_copy(ys_ref.at[pl.ds(dest_ref[base + u], 1)], tile_rows.at[pl.ds(u, 1)],
                                  sem).start(priority=u % 2)
        return carry

    lax.fori_loop(0, t // SUBLANES, issue, 0, unroll=2)
    pltpu.make_async_copy(ys_ref.at[pl.ds(0, t)], y_ref, sem).wait()


def _gather_rows(dest, ys, t):
    n = dest.shape[0]
    w = ys.shape[1]
    return pl.pallas_call(
        _gather_rows_kernel,
        grid=(n // t,),
        in_specs=[pl.BlockSpec((t,), lambda i: (i,), memory_space=pltpu.SMEM),
                  pl.BlockSpec(memory_space=pl.ANY)],
        out_specs=pl.BlockSpec((t, w), lambda i: (i, 0)),
        out_shape=jax.ShapeDtypeStruct((n, w), ys.dtype),
        scratch_shapes=[pltpu.SemaphoreType.DMA],
        compiler_params=_cparams(("arbitrary",)),
        name="moe_gather_rows",
    )(dest, ys)


def _moe_kernel(e1_ref, e2_ref, tv_ref, xs_ref, g2_ref, wg1_ref, wu1_ref, wd1_ref, wg2_ref, wu2_ref, wd2_ref, y_ref):
    m = pl.program_id(0)

    @pl.when(tv_ref[m] > 0)
    def _():
        x1 = xs_ref[:, 0:D_MODEL]
        route = xs_ref[:, D_MODEL:]
        lane = lax.broadcasted_iota(jnp.int32, route.shape, 1)
        r = lax.rsqrt(jnp.mean(x1 * x1, axis=-1, keepdims=True) + EPS)
        n2 = ((x1 * r) * g2_ref[...]).astype(BF16)
        acc = x1
        for e_ref, wg_ref, wu_ref, wd_ref in ((e1_ref, wg1_ref, wu1_ref, wd1_ref), (e2_ref, wg2_ref, wu2_ref, wd2_ref)):
            ge = jnp.sum(jnp.where(lane == e_ref[m], route, 0.0), axis=1, keepdims=True)
            hg = jnp.dot(n2, wg_ref[...], preferred_element_type=F32)
            hu = jnp.dot(n2, wu_ref[...], preferred_element_type=F32)
            h = (hg * jax.nn.sigmoid(hg)) * hu * ge
            acc = acc + jnp.dot(h.astype(BF16), wd_ref[...], preferred_element_type=F32)
        y_ref[...] = acc

    @pl.when(tv_ref[m] == 0)
    def _():
        y_ref[...] = jnp.zeros_like(y_ref)


def _moe_sorted(tile_e1, tile_e2, tile_valid, xs, g2, wg, wu, wd, tm):
    m_pad = tile_valid.shape[0] * tm
    up = lambda e: pl.BlockSpec((None, D_MODEL, D_EXPERT), lambda m, e1, e2, tv: ((e1, e2)[e][m], 0, 0))
    down = lambda e: pl.BlockSpec((None, D_EXPERT, D_MODEL), lambda m, e1, e2, tv: ((e1, e2)[e][m], 0, 0))
    grid_spec = pltpu.PrefetchScalarGridSpec(
        num_scalar_prefetch=3,
        grid=(m_pad // tm,),
        in_specs=[
            pl.BlockSpec((tm, ROW_WIDTH), lambda m, e1, e2, tv: (jnp.where(tv[m] > 0, m, 0), 0)),
            pl.BlockSpec(g2.shape, lambda m, e1, e2, tv: (0, 0)),
            up(0), up(0), down(0), up(1), up(1), down(1),
        ],
        out_specs=pl.BlockSpec((tm, D_MODEL), lambda m, e1, e2, tv: (m, 0)),
    )
    return pl.pallas_call(
        _moe_kernel,
        grid_spec=grid_spec,
        out_shape=jax.ShapeDtypeStruct((m_pad, D_MODEL), F32),
        compiler_params=_cparams(("arbitrary",)),
        name="moe_experts",
    )(tile_e1, tile_e2, tile_valid, xs, g2, wg, wu, wd, wg, wu, wd)


def _moe_plan(plan_p, cnt_p, plan_s, cnt_s, tm):
    def ids(plan):
        blocks = plan.reshape(-1, SUBLANES, plan.shape[1]).astype(jnp.int32)
        return blocks[:, 0, :], blocks[:, 1, :]

    def lookup(table, cat):
        out = jnp.zeros_like(cat)
        for c in range(N_CATEGORIES):
            out = jnp.where(cat == c, table[c], out)
        return out

    (kp, rank_p), (ks, rank_s) = ids(plan_p), ids(plan_s)
    n_total = kp.size + ks.size
    n_tiles = n_total // tm + N_CATEGORIES
    cp = cnt_p[:N_CATEGORIES, 0].astype(jnp.int32)
    cs = cnt_s[:N_CATEGORIES, 0].astype(jnp.int32)
    total = cp + cs
    tiles = (total + tm - 1) // tm
    tile_end = jnp.cumsum(tiles)
    offset = (tile_end - tiles) * tm
    dest_p = (lookup(offset, kp) + rank_p).reshape(-1)
    dest_s = (lookup(offset + cp, ks) + rank_s).reshape(-1)
    tile_id = jnp.arange(n_tiles)
    tile_cat = jnp.minimum(jnp.sum(tile_id[:, None] >= tile_end[None, :], axis=1), N_CATEGORIES - 1)
    used = jnp.clip(offset[tile_cat] + total[tile_cat] - tile_id * tm, 0, tm)
    tile_valid = jnp.where(tile_id < tile_end[N_CATEGORIES - 1], used, 0).astype(jnp.int32)
    experts = jnp.asarray(CATEGORY_EXPERTS, jnp.int32)
    tile_e1, tile_e2 = experts[tile_cat, 0], experts[tile_cat, 1]
    m_rows = n_tiles * tm
    last_tile = jnp.where(tiles > 0, (tile_end - 1) * tm, m_rows)
    tail = jnp.minimum((tile_end[N_CATEGORIES - 1] + jnp.arange(n_tiles - n_total // tm)) * tm, m_rows)
    fill_start = jnp.concatenate([last_tile, tail]).astype(jnp.int32)
    return dest_p, dest_s, fill_start, tile_e1, tile_e2, tile_valid, m_rows


def kernel(x_prompt, x_sample, cache_k, cache_v, cache_logf, state_pool, norm1_g, w_in, b_f, q_norm_g, k_norm_g,
           pool_w, pool_scale, w_out, norm2_g, w_router_group, b_router_group, w_router_expert, b_router_expert,
           w_gate, w_up, w_down):
    depth = norm1_g.shape[0]
    assert depth == 1, "single-layer step"
    b_p, s_p, _ = x_prompt.shape
    assert b_p == 1, "prompt kernels assume one stream"
    b_s, t_s, _ = x_sample.shape
    p_len = cache_k.shape[2]
    l = 0

    w = w_in[l]
    a3 = 3 * ATTN_WIDTH
    w_all = jnp.concatenate(
        [w[:, :a3], w[:, a3 + N_HEADS:], jnp.pad(w[:, a3:a3 + N_HEADS], ((0, 0), (0, LANES - N_HEADS)))],
        axis=1).astype(BF16)
    bf_pad = jnp.pad(b_f[l], (0, LANES - N_HEADS)).reshape(1, LANES)
    g1 = norm1_g[l].reshape(1, D_MODEL)
    qg = jnp.tile(q_norm_g[l], N_HEADS).reshape(1, ATTN_WIDTH)
    kg = jnp.tile(k_norm_g[l], N_HEADS).reshape(1, ATTN_WIDTH)
    hr = lax.broadcasted_iota(jnp.int32, (MXU_DIM, MXU_DIM), 0) // HEAD_DIM
    hc = lax.broadcasted_iota(jnp.int32, (MXU_DIM, MXU_DIM), 1) // HEAD_DIM
    hm = jnp.where(hr == hc, 1.0 / HEAD_DIM, 0.0).astype(BF16)
    pw = pool_w[l].astype(BF16)
    ps = pool_scale[l].reshape(1, POOL_WIDTH)
    wo = w_out[l].astype(BF16)
    g2 = norm2_g[l].reshape(1, D_MODEL)
    w_r = jnp.pad(jnp.concatenate([w_router_expert[l], w_router_group[l]], axis=1),
                  ((0, 0), (0, LANES - N_EXPERTS - N_GROUPS)))
    wrh = w_r.astype(BF16)
    wrl = (w_r - wrh.astype(F32)).astype(BF16)
    br = jnp.pad(jnp.concatenate([b_router_expert[l], b_router_group[l]]),
                 (0, LANES - N_EXPERTS - N_GROUPS)).reshape(1, LANES)
    wg = w_gate[l].astype(BF16)
    wu = w_up[l].astype(BF16)
    wd = w_down[l].astype(BF16)

    tile = 512
    query_tile = 512
    key_tile = 256
    expert_tile = 256
    move_tile = 1024

    xp = x_prompt.reshape(s_p, D_MODEL)
    q, kf, _, vf, _, ka, vt, pin, logf, _, edge = _in_project(xp, tile, tile, True, key_tile, g1, w_all, bf_pad, qg, kg,
                                                              hm)
    blocks_per_tile = tile // key_tile
    edge = edge.reshape(s_p // tile, SUBLANES, LANES)[:, :2 * blocks_per_tile, :N_HEADS]
    edge = edge.reshape(s_p // key_tile, 2, N_HEADS)
    c_blocks = edge[:, 0, :]
    j0 = _first_key_blocks(c_blocks, edge[:, 1, :], q_norm_g[l], k_norm_g[l], query_tile, key_tile)
    a = _attention_prompt(j0, c_blocks, q, ka, vt, query_tile, key_tile)
    hist_map = lambda i: (jnp.maximum(i * (tile // HIST_ROWS) - 1, 0), 0)
    xg_p, plan_p, cnt_p = _mix(a, pin, pin, hist_map, xp, tile, tile, 0, tile, True, pw, ps, wo, g2, wrh, wrl, br)
    k_prompt = kf.reshape(depth, b_p, s_p, N_HEADS, HEAD_DIM)
    v_prompt = vf.reshape(depth, b_p, s_p, N_HEADS, HEAD_DIM)
    logf_prompt = logf[:, :N_HEADS].reshape(depth, b_p, s_p, N_HEADS)
    pool_prompt = pin[s_p - POOL_PAD:].reshape(depth, b_p, POOL_PAD, POOL_WIDTH)

    n_s = b_s * t_s
    xs = x_sample.reshape(n_s, D_MODEL)
    q, kf, kb, vf, vb, _, _, pin, logf, d, _ = _in_project(xs, tile, t_s, False, key_tile, g1, w_all, bf_pad, qg, kg, hm)
    dt4 = d[:, :N_HEADS].reshape(b_s, t_s, N_PAIRS, 2).transpose(0, 2, 3, 1)
    clf = cache_logf[l].astype(F32).transpose(0, 2, 1).reshape(b_s * N_HEADS, p_len)
    r4 = _suffix_sums(clf).reshape(b_s, N_PAIRS, 2, p_len)
    ck = cache_k[l].reshape(b_s, p_len, ATTN_WIDTH).astype(BF16)
    cv = cache_v[l].reshape(b_s, p_len, ATTN_WIDTH).astype(BF16)
    a = _attention_sample(q, kb, vb, ck, cv, d, dt4, r4, t_s)
    hist_s = jnp.pad(state_pool[l], ((0, 0), (HIST_ROWS - POOL_PAD, 0), (0, 0))).reshape(b_s * HIST_ROWS, POOL_WIDTH)
    xg_s, plan_s, cnt_s = _mix(a, pin, hist_s, lambda i: (i, 0), xs, tile, t_s, p_len, 0, False, pw, ps, wo, g2, wrh,
                               wrl, br)
    k_sample = kf.reshape(depth, b_s, t_s, N_HEADS, HEAD_DIM)
    v_sample = vf.reshape(depth, b_s, t_s, N_HEADS, HEAD_DIM)
    logf_sample = logf[:, :N_HEADS].reshape(depth, b_s, t_s, N_HEADS)
    pool_sample = pin.reshape(b_s, t_s, POOL_WIDTH)[:, t_s - POOL_PAD:].reshape(depth, b_s, POOL_PAD, POOL_WIDTH)

    dest_p, dest_s, fill_start, tile_e1, tile_e2, tile_valid, m_rows = _moe_plan(plan_p, cnt_p, plan_s, cnt_s,
                                                                              expert_tile)
    rows = _scatter_rows(fill_start, dest_p, xg_p, dest_s, xg_s, m_rows, move_tile, expert_tile)
    ys = _moe_sorted(tile_e1, tile_e2, tile_valid, rows, g2, wg, wu, wd, expert_tile)
    y_prompt = _gather_rows(dest_p, ys, 2 * move_tile).reshape(b_p, s_p, D_MODEL)
    y_sample = _gather_rows(dest_s, ys, move_tile).reshape(b_s, t_s, D_MODEL)

    return (y_prompt, y_sample, k_prompt, v_prompt, logf_prompt, pool_prompt,
            k_sample, v_sample, logf_sample, pool_sample)
```
